```python
import math
import jax
import jax.numpy as jnp
from jax import lax
import numpy as np

D_MODEL = 1024
BATCH = 4
SEQ = 4096
DEPTH = 1
DEC_BATCH = 128
DEC_SEQ = 4
PAST_LEN = 2048
PAGE_SIZE = 128

ATT_WIDTH = D_MODEL // 2
LRU_WIDTH = D_MODEL - ATT_WIDTH
ATT_HEADS = 4
V_HEAD_DIM = ATT_WIDTH // ATT_HEADS
QK_SUB_DIM = V_HEAD_DIM // 2
ROPE_DIM = QK_SUB_DIM // 4
ROPE_THETA = 500000.0
LRU_BLOCKS = 8
LRU_BLOCK_DIM = LRU_WIDTH // LRU_BLOCKS
LRU_C = 8.0
LRU_CONV = 4
D_FF = ((8 * D_MODEL // 3 + 127) // 128) * 128
FFN_CONV = 3
Q_BLOCK = 128
IN_WIDTH = 3 * ATT_WIDTH + 2 * LRU_WIDTH
EPS = 1e-6
F32 = jnp.float32

kernel_name = 'hymba_diffattn_rglru_convffn_step'


def lambda_init(layer):
    return 0.8 - 0.6 * math.exp(-0.3 * layer)


def rmsnorm(x, g):
    xf = x.astype(F32)
    y = xf * lax.rsqrt(jnp.mean(xf * xf, axis=-1, keepdims=True) + EPS)
    return (y * g.astype(F32)).astype(x.dtype)


def partial_rope(t, pos):
    half = ROPE_DIM // 2
    freqs = ROPE_THETA ** (-jnp.arange(half, dtype=F32) * 2.0 / ROPE_DIM)
    ang = pos.astype(F32)[:, None] * freqs[None, :]
    cos = jnp.cos(ang)[None, :, None, None, :]
    sin = jnp.sin(ang)[None, :, None, None, :]
    tf = t.astype(F32)
    r1 = tf[..., :half]
    r2 = tf[..., half:ROPE_DIM]
    out = jnp.concatenate([r1 * cos - r2 * sin, r1 * sin + r2 * cos, tf[..., ROPE_DIM:]], axis=-1)
    return out.astype(t.dtype)


def causal_dwconv(xpad, w, b):
    k_w = w.shape[0]
    t_len = xpad.shape[1] - k_w + 1
    out = xpad[:, 0:t_len] * w[0]
    for k in range(1, k_w):
        out = out + xpad[:, k:k + t_len] * w[k]
    return out + b


def diff_attn_block(q, k, v, mask, lam):
    s = jnp.einsum('bqhcd,bkhcd->bhcqk', q.astype(F32), k.astype(F32)) * (QK_SUB_DIM ** -0.5)
    s = jnp.where(mask, s, jnp.finfo(F32).min)
    p = jax.nn.softmax(s, axis=-1)
    p_diff = p[:, :, 0] - lam * p[:, :, 1]
    return jnp.einsum('bhqk,bkhd->bqhd', p_diff, v.astype(F32))


def diff_attn_prompt(q, k, v, lam):
    bsz, s_len = q.shape[0], q.shape[1]
    n_blk = s_len // Q_BLOCK
    kf = k.astype(F32)
    vf = v.astype(F32)
    qb = q.astype(F32).reshape(bsz, n_blk, Q_BLOCK, ATT_HEADS, 2, QK_SUB_DIM).swapaxes(0, 1)
    k_pos = jnp.arange(s_len)

    def one_block(args):
        q_i, i = args
        q_pos = i * Q_BLOCK + jnp.arange(Q_BLOCK)
        return diff_attn_block(q_i, kf, vf, k_pos[None, :] <= q_pos[:, None], lam)

    ob = lax.map(one_block, (qb, jnp.arange(n_blk)))
    return ob.swapaxes(0, 1).reshape(bsz, s_len, ATT_HEADS, V_HEAD_DIM)


def diff_attn_sample(q, k_new, v_new, k_past, v_past, lam):
    past = k_past.shape[1]
    t_len = q.shape[1]
    k_all = jnp.concatenate([k_past, k_new.astype(k_past.dtype)], axis=1)
    v_all = jnp.concatenate([v_past, v_new.astype(v_past.dtype)], axis=1)
    q_pos = past + jnp.arange(t_len)
    k_pos = jnp.arange(past + t_len)
    return diff_attn_block(q, k_all, v_all, k_pos[None, :] <= q_pos[:, None], lam)


def lru_combine(left, right):
    a1, b1 = left
    a2, b2 = right
    return a1 * a2, a2 * b1 + b2


def rg_lru(xc, h0, w_r, b_r, w_i, b_i, lru_lambda):
    bsz, t_len = xc.shape[0], xc.shape[1]
    xf = xc.astype(F32)
    xb = xf.reshape(bsz, t_len, LRU_BLOCKS, LRU_BLOCK_DIM)
    r = jax.nn.sigmoid(jnp.einsum('btni,nij->btnj', xb, w_r.astype(F32)) + b_r.astype(F32)).reshape(bsz, t_len, LRU_WIDTH)
    ig = jax.nn.sigmoid(jnp.einsum('btni,nij->btnj', xb, w_i.astype(F32)) + b_i.astype(F32)).reshape(bsz, t_len, LRU_WIDTH)
    log_a = -LRU_C * r * jax.nn.softplus(-lru_lambda.astype(F32))
    a = jnp.exp(log_a)
    gated_x = jnp.sqrt(-jnp.expm1(2.0 * log_a)) * (ig * xf)
    gated_x = gated_x.at[:, 0].add(a[:, 0] * h0.astype(F32))
    _, hs = lax.associative_scan(lru_combine, (a, gated_x), axis=1)
    return hs, hs[:, -1]


def layer_forward(x, c, pos, attn_past, lru_buf, lru_h0, ffn_buf, p, lam_init):
    dt = x.dtype
    bsz, t_len = x.shape[0], x.shape[1]
    mod = jax.nn.silu(c.astype(F32)) @ p['w_ada'].astype(F32) + p['b_ada'].astype(F32)
    mod = mod.astype(dt)[:, None, :]
    sh1, sc1, gt1, sh2, sc2, gt2 = jnp.split(mod, 6, axis=-1)

    u = rmsnorm(x, p['g_norm1']) * (1 + sc1) + sh1
    proj = u @ p['w_in']
    q, k, v, lx, lg = jnp.split(proj, [ATT_WIDTH, 2 * ATT_WIDTH, 3 * ATT_WIDTH, 3 * ATT_WIDTH + LRU_WIDTH], axis=-1)
    q = partial_rope(rmsnorm(q.reshape(bsz, t_len, ATT_HEADS, 2, QK_SUB_DIM), p['g_q']), pos)
    k = partial_rope(rmsnorm(k.reshape(bsz, t_len, ATT_HEADS, 2, QK_SUB_DIM), p['g_k']), pos)
    v = v.reshape(bsz, t_len, ATT_HEADS, V_HEAD_DIM)
    lam = (jnp.exp(jnp.sum(p['lam_q1'].astype(F32) * p['lam_k1'].astype(F32)))
           - jnp.exp(jnp.sum(p['lam_q2'].astype(F32) * p['lam_k2'].astype(F32))) + lam_init)
    if attn_past is None:
        o = diff_attn_prompt(q, k, v, lam)
    else:
        o = diff_attn_sample(q, k, v, attn_past[0], attn_past[1], lam)
    o = rmsnorm(o, p['g_subln']) * (1.0 - lam_init)
    attn_out = o.reshape(bsz, t_len, ATT_WIDTH).astype(dt)

    lx_pad = jnp.concatenate([lru_buf.astype(dt), lx], axis=1)
    xc = causal_dwconv(lx_pad, p['conv_lru_w'], p['conv_lru_b'])
    new_lru_buf = lx_pad[:, -(LRU_CONV - 1):]
    hs, h_last = rg_lru(xc, lru_h0, p['w_rgate'], p['b_rgate'], p['w_igate'], p['b_igate'], p['lru_lambda'])
    lru_out = (hs * jax.nn.gelu(lg.astype(F32), approximate=True)).astype(dt)

    m = jnp.concatenate([attn_out, lru_out], axis=-1) @ p['w_out']
    x = x + gt1 * m

    u2 = rmsnorm(x, p['g_norm2']) * (1 + sc2) + sh2
    up = u2 @ p['w_up']
    up_pad = jnp.concatenate([ffn_buf.astype(dt), up], axis=1)
    hc = causal_dwconv(up_pad, p['conv_ffn_w'], p['conv_ffn_b'])
    new_ffn_buf = up_pad[:, -(FFN_CONV - 1):]
    g, val = jnp.split(hc, 2, axis=-1)
    x = x + gt2 * ((jax.nn.silu(g) * val) @ p['w_down'])
    return x, (k, v, new_lru_buf, h_last.astype(dt), new_ffn_buf)


def setup_inputs(seed: int = 0) -> dict:
    key = jax.random.key(seed)
    ks = jax.random.split(key, 40)
    n_pages = PAST_LEN // PAGE_SIZE
    n_used = DEC_BATCH * n_pages
    n_pool = n_used + max(1, n_used // 4)
    L = DEPTH

    def nrm(k, shape, s):
        return s * jax.random.normal(k, shape, F32)

    page_table = jax.random.permutation(ks[4], n_pool)[:n_used].reshape(DEC_BATCH, n_pages).astype(jnp.int32)
    u = jax.random.uniform(ks[30], (L, LRU_WIDTH), F32, 0.9, 0.999)
    a = u ** (1.0 / LRU_C)
    lru_lambda = jnp.log(a) - jnp.log1p(-a)
    return {
        'x_prompt': nrm(ks[0], (BATCH, SEQ, D_MODEL), 1.0),
        'x_sample': nrm(ks[1], (DEC_BATCH, DEC_SEQ, D_MODEL), 1.0),
        'cache_k': nrm(ks[2], (L, n_pool, PAGE_SIZE, ATT_HEADS, 2, QK_SUB_DIM), 1.0),
        'cache_v': nrm(ks[3], (L, n_pool, PAGE_SIZE, ATT_HEADS, V_HEAD_DIM), 1.0),
        'page_table': page_table,
        'state_lru_conv': nrm(ks[5], (L, DEC_BATCH, LRU_CONV - 1, LRU_WIDTH), 1.0),
        'state_lru_h': nrm(ks[6], (L, DEC_BATCH, LRU_WIDTH), 0.5),
        'state_ffn_conv': nrm(ks[7], (L, DEC_BATCH, FFN_CONV - 1, 2 * D_FF), 1.0),
        'c_prompt': nrm(ks[8], (BATCH, D_MODEL), 1.0),
        'c_sample': nrm(ks[9], (DEC_BATCH, D_MODEL), 1.0),
        'g_norm1': 1.0 + nrm(ks[10], (L, D_MODEL), 0.02),
        'g_norm2': 1.0 + nrm(ks[11], (L, D_MODEL), 0.02),
        'w_ada': nrm(ks[12], (L, D_MODEL, 6 * D_MODEL), 0.5 * D_MODEL ** -0.5),
        'b_ada': nrm(ks[13], (L, 6 * D_MODEL), 0.02),
        'w_in': nrm(ks[14], (L, D_MODEL, IN_WIDTH), D_MODEL ** -0.5),
        'g_q': 1.0 + nrm(ks[15], (L, QK_SUB_DIM), 0.02),
        'g_k': 1.0 + nrm(ks[16], (L, QK_SUB_DIM), 0.02),
        'lam_q1': nrm(ks[17], (L, QK_SUB_DIM), 0.1),
        'lam_k1': nrm(ks[18], (L, QK_SUB_DIM), 0.1),
        'lam_q2': nrm(ks[19], (L, QK_SUB_DIM), 0.1),
        'lam_k2': nrm(ks[20], (L, QK_SUB_DIM), 0.1),
        'g_subln': 1.0 + nrm(ks[21], (L, V_HEAD_DIM), 0.02),
        'w_out': nrm(ks[22], (L, ATT_WIDTH + LRU_WIDTH, D_MODEL), (ATT_WIDTH + LRU_WIDTH) ** -0.5),
        'conv_lru_w': nrm(ks[23], (L, LRU_CONV, LRU_WIDTH), LRU_CONV ** -0.5),
        'conv_lru_b': nrm(ks[24], (L, LRU_WIDTH), 0.02),
        'w_rgate': nrm(ks[25], (L, LRU_BLOCKS, LRU_BLOCK_DIM, LRU_BLOCK_DIM), LRU_BLOCK_DIM ** -0.5),
        'b_rgate': nrm(ks[26], (L, LRU_BLOCKS, LRU_BLOCK_DIM), 0.02),
        'w_igate': nrm(ks[27], (L, LRU_BLOCKS, LRU_BLOCK_DIM, LRU_BLOCK_DIM), LRU_BLOCK_DIM ** -0.5),
        'b_igate': nrm(ks[28], (L, LRU_BLOCKS, LRU_BLOCK_DIM), 0.02),
        'lru_lambda': lru_lambda,
        'w_up': nrm(ks[31], (L, D_MODEL, 2 * D_FF), D_MODEL ** -0.5),
        'conv_ffn_w': nrm(ks[32], (L, FFN_CONV, 2 * D_FF), FFN_CONV ** -0.5),
        'conv_ffn_b': nrm(ks[33], (L, 2 * D_FF), 0.02),
        'w_down': nrm(ks[34], (L, D_FF, D_MODEL), D_FF ** -0.5),
    }


def reference(x_prompt, x_sample, cache_k, cache_v, page_table, state_lru_conv, state_lru_h, state_ffn_conv,
              c_prompt, c_sample, g_norm1, g_norm2, w_ada, b_ada, w_in, g_q, g_k, lam_q1, lam_k1, lam_q2, lam_k2,
              g_subln, w_out, conv_lru_w, conv_lru_b, w_rgate, b_rgate, w_igate, b_igate, lru_lambda,
              w_up, conv_ffn_w, conv_ffn_b, w_down):
    bsz, s_len = x_prompt.shape[0], x_prompt.shape[1]
    dbsz, t_len = x_sample.shape[0], x_sample.shape[1]
    dt = x_prompt.dtype
    past_len = page_table.shape[1] * cache_k.shape[2]
    pos_p = jnp.arange(s_len)
    pos_s = past_len + jnp.arange(t_len)
    yp, ys = x_prompt, x_sample
    st_p, st_s = [], []
    for l in range(DEPTH):
        p = {'g_norm1': g_norm1[l], 'g_norm2': g_norm2[l], 'w_ada': w_ada[l], 'b_ada': b_ada[l],
             'w_in': w_in[l], 'g_q': g_q[l], 'g_k': g_k[l], 'lam_q1': lam_q1[l], 'lam_k1': lam_k1[l],
             'lam_q2': lam_q2[l], 'lam_k2': lam_k2[l], 'g_subln': g_subln[l], 'w_out': w_out[l],
             'conv_lru_w': conv_lru_w[l], 'conv_lru_b': conv_lru_b[l], 'w_rgate': w_rgate[l],
             'b_rgate': b_rgate[l], 'w_igate': w_igate[l], 'b_igate': b_igate[l],
             'lru_lambda': lru_lambda[l], 'w_up': w_up[l], 'conv_ffn_w': conv_ffn_w[l],
             'conv_ffn_b': conv_ffn_b[l], 'w_down': w_down[l]}
        lam0 = lambda_init(l)
        yp, sp = layer_forward(yp, c_prompt, pos_p, None,
                               jnp.zeros((bsz, LRU_CONV - 1, LRU_WIDTH), dt),
                               jnp.zeros((bsz, LRU_WIDTH), F32),
                               jnp.zeros((bsz, FFN_CONV - 1, 2 * D_FF), dt), p, lam0)
        k_past = cache_k[l][page_table].reshape(dbsz, past_len, ATT_HEADS, 2, QK_SUB_DIM)
        v_past = cache_v[l][page_table].reshape(dbsz, past_len, ATT_HEADS, V_HEAD_DIM)
        ys, ss = layer_forward(ys, c_sample, pos_s, (k_past, v_past), state_lru_conv[l], state_lru_h[l],
                               state_ffn_conv[l], p, lam0)
        st_p.append(sp)
        st_s.append(ss)
    k_p = jnp.stack([s[0] for s in st_p])
    v_p = jnp.stack([s[1] for s in st_p])
    lc_p = jnp.stack([s[2] for s in st_p])
    h_p = jnp.stack([s[3] for s in st_p])
    fc_p = jnp.stack([s[4] for s in st_p])
    k_s = jnp.stack([s[0] for s in st_s])
    v_s = jnp.stack([s[1] for s in st_s])
    lc_s = jnp.stack([s[2] for s in st_s])
    h_s = jnp.stack([s[3] for s in st_s])
    fc_s = jnp.stack([s[4] for s in st_s])
    return (yp, ys, k_p, v_p, lc_p, h_p, fc_p, k_s, v_s, lc_s, h_s, fc_s)
```

```python
import functools
import math

import jax
import jax.numpy as jnp
from jax import lax
from jax.experimental import pallas as pl
from jax.experimental.pallas import tpu as pltpu

F32 = jnp.float32
BF16 = jnp.bfloat16

D_MODEL = 1024
ATT_WIDTH = 512
LRU_WIDTH = 512
ATT_HEADS = 4
V_HEAD_DIM = 128
QK_SUB_DIM = 64
ROPE_DIM = 16
ROPE_THETA = 500000.0
LRU_BLOCKS = 8
LRU_C = 8.0
LRU_CONV = 4
FFN_CONV = 3
EPS = 1e-6
LAM_INIT = 0.8 - 0.6 * math.exp(-0.3 * 0)
QK_SCALE = QK_SUB_DIM ** -0.5

LANES = 128
SUBLANES = 8
VMEM_LIMIT = 56 * 1024 * 1024

TM_IN = 512
TQ = 256
TK = 256
TM_FFN = 256
FF_CHUNK = 256
NEG = -1e30


def _dot(a, b):
    return jnp.dot(a, b, preferred_element_type=F32)


def _dot_nt(a, b):
    return lax.dot_general(a, b, (((1,), (1,)), ((), ())), preferred_element_type=F32)


def _const_spec(shape):
    nd = len(shape)
    return pl.BlockSpec(shape, lambda *_: (0,) * nd, pipeline_mode=pl.Buffered(1))


def _modulated_norm(x, g, sc, sh):
    xn = x * lax.rsqrt(jnp.mean(x * x, axis=-1, keepdims=True) + EPS) * g
    return xn * (1.0 + sc) + sh


def _group_norm_rope(t, g_tiled, gmat, cos, sin_a, sin_b):
    ms = _dot((t * t).astype(BF16), gmat)
    tn = t * lax.rsqrt(ms + EPS) * g_tiled
    outs = []
    for h in range(ATT_WIDTH // LANES):
        th = tn[:, h * LANES:(h + 1) * LANES]
        outs.append(th * cos + pltpu.roll(th, LANES - 8, 1) * sin_a + pltpu.roll(th, 8, 1) * sin_b)
    return jnp.concatenate(outs, axis=1)


def _lru_gates(xc, wr_ref, br, wi_ref, bi, neg_c_softplus):
    half = LRU_WIDTH // 2
    xb = xc.astype(BF16)
    lo, hi = xb[:, :half], xb[:, half:]
    r = jax.nn.sigmoid(jnp.concatenate([_dot(lo, wr_ref[0]), _dot(hi, wr_ref[1])], axis=1) + br)
    ig = jax.nn.sigmoid(jnp.concatenate([_dot(lo, wi_ref[0]), _dot(hi, wi_ref[1])], axis=1) + bi)
    log_a = neg_c_softplus * r
    a = jnp.exp(log_a)
    one_minus_a2 = -jnp.tanh(log_a) * (a * a + 1.0)
    return a, jnp.sqrt(one_minus_a2) * (ig * xc)


def _neg_c_softplus(lam):
    z = -lam
    return -LRU_C * (jnp.maximum(z, 0.0) + jnp.log1p(jnp.exp(-jnp.abs(z))))


def _diff_lambda(lq1, lk1, lq2, lk2):
    s1 = jnp.sum(lq1 * lk1, axis=-1, keepdims=True)
    s2 = jnp.sum(lq2 * lk2, axis=-1, keepdims=True)
    return jnp.exp(s1) - jnp.exp(s2) + LAM_INIT


def _subln(o, g):
    return o * lax.rsqrt(jnp.mean(o * o, axis=-1, keepdims=True) + EPS) * g * (1.0 - LAM_INIT)


def _ada_kernel(c_ref, w_ref, b_ref, o_ref):
    c = c_ref[...]
    s = (c * jax.nn.sigmoid(c)).astype(BF16)
    o_ref[...] = _dot(s, w_ref[...].astype(BF16)) + b_ref[...]


def _ada(c_all, w_ada, b_ada):
    m = c_all.shape[0]
    n = w_ada.shape[1]
    tn = 1024
    return pl.pallas_call(
        _ada_kernel,
        grid=(n // tn,),
        in_specs=[pl.BlockSpec((m, D_MODEL), lambda j: (0, 0)),
                  pl.BlockSpec((D_MODEL, tn), lambda j: (0, j)),
                  pl.BlockSpec((1, tn), lambda j: (0, j))],
        out_specs=pl.BlockSpec((m, tn), lambda j: (0, j)),
        out_shape=jax.ShapeDtypeStruct((m, n), F32),
        compiler_params=pltpu.CompilerParams(dimension_semantics=("arbitrary",), vmem_limit_bytes=VMEM_LIMIT),
        name="ada_mod",
    )(c_all, w_ada, b_ada)


def _rope_table_kernel(freq_ref, ma_ref, mb_ref, c_ref, sa_ref, sb_ref):
    tm = c_ref.shape[0]
    pos = (pl.program_id(0) * tm + lax.broadcasted_iota(jnp.int32, (tm, LANES), 0)).astype(F32)
    ang = pos * freq_ref[...]
    s = jnp.sin(ang)
    c_ref[...] = jnp.cos(ang)
    sa_ref[...] = -s * ma_ref[...]
    sb_ref[...] = s * mb_ref[...]


def _rope_tables(n_pos):
    half = ROPE_DIM // 2
    freqs = ROPE_THETA ** (-jnp.arange(half, dtype=F32) * 2.0 / ROPE_DIM)
    d = jnp.arange(LANES) % QK_SUB_DIM
    freq_lane = jnp.where(d < ROPE_DIM, freqs[d % half], 0.0).astype(F32)[None, :]
    mask_a = (d < half).astype(F32)[None, :]
    mask_b = ((d >= half) & (d < ROPE_DIM)).astype(F32)[None, :]
    tm = 512
    row = pl.BlockSpec((1, LANES), lambda i: (0, 0))
    tab = pl.BlockSpec((tm, LANES), lambda i: (i, 0))
    shp = jax.ShapeDtypeStruct((n_pos, LANES), F32)
    return pl.pallas_call(
        _rope_table_kernel,
        grid=(n_pos // tm,),
        in_specs=[row, row, row],
        out_specs=[tab, tab, tab],
        out_shape=[shp, shp, shp],
        compiler_params=pltpu.CompilerParams(dimension_semantics=("arbitrary",)),
        name="rope_tables",
    )(freq_lane, mask_a, mask_b)


def _inproj_prompt_kernel(x_ref, sh_ref, sc_ref, g1_ref, win_ref, gq_ref, gk_ref, gmat_ref,
                          cos_ref, sa_ref, sb_ref, cw_ref, cb_ref, wr_ref, br_ref, wi_ref, bi_ref, lam_ref,
                          q_ref, kf_ref, kb_ref, vf_ref, vb_ref, lo_ref, lc_ref, lh_ref,
                          buf_ref, hcar_ref, acum_ref, bcum_ref, hs_ref):
    tm = x_ref.shape[0]

    @pl.when(pl.program_id(1) == 0)
    def _():
        buf_ref[0:SUBLANES, :] = jnp.zeros((SUBLANES, LRU_WIDTH), F32)
        hcar_ref[...] = jnp.zeros((SUBLANES, LRU_WIDTH), F32)

    u = _modulated_norm(x_ref[...], g1_ref[...], sc_ref[...], sh_ref[...]).astype(BF16)
    cos, sa, sb = cos_ref[...], sa_ref[...], sb_ref[...]
    gmat = gmat_ref[...]

    q = _group_norm_rope(_dot(u, win_ref[:, 0:ATT_WIDTH]), gq_ref[...], gmat, cos, sa, sb)
    q_ref[...] = (q * QK_SCALE).astype(BF16)
    k = _group_norm_rope(_dot(u, win_ref[:, ATT_WIDTH:2 * ATT_WIDTH]), gk_ref[...], gmat, cos, sa, sb)
    kf_ref[...] = k
    kb_ref[...] = k.astype(BF16)
    v = _dot(u, win_ref[:, 2 * ATT_WIDTH:3 * ATT_WIDTH])
    vf_ref[...] = v
    vb_ref[...] = v.astype(BF16)

    c0 = 3 * ATT_WIDTH
    lx = _dot(u, win_ref[:, c0:c0 + LRU_WIDTH])
    buf_ref[SUBLANES:SUBLANES + tm, :] = lx
    cw = cw_ref[...]
    xc = buf_ref[SUBLANES - 3:SUBLANES - 3 + tm, :] * cw[0:1]
    xc = xc + buf_ref[SUBLANES - 2:SUBLANES - 2 + tm, :] * cw[1:2]
    xc = xc + buf_ref[SUBLANES - 1:SUBLANES - 1 + tm, :] * cw[2:3]
    xc = xc + lx * cw[3:4] + cb_ref[...]
    lc_ref[...] = buf_ref[SUBLANES + tm - (LRU_CONV - 1):SUBLANES + tm, :]
    buf_ref[0:SUBLANES, :] = buf_ref[tm:tm + SUBLANES, :]

    a, gx = _lru_gates(xc, wr_ref, br_ref[...], wi_ref, bi_ref[...], _neg_c_softplus(lam_ref[...]))

    row_in_group = lax.broadcasted_iota(jnp.int32, (tm, LRU_WIDTH), 0) & (SUBLANES - 1)
    for d in (1, 2, 4):
        keep = row_in_group >= d
        a_prev = jnp.where(keep, pltpu.roll(a, d, 0), 1.0)
        g_prev = jnp.where(keep, pltpu.roll(gx, d, 0), 0.0)
        gx = a * g_prev + gx
        a = a * a_prev
    acum_ref[...] = a
    bcum_ref[...] = gx

    def group_step(j, h):
        r0 = pl.multiple_of(j * SUBLANES, SUBLANES)
        hb = acum_ref[pl.ds(r0, SUBLANES), :] * h + bcum_ref[pl.ds(r0, SUBLANES), :]
        hs_ref[pl.ds(r0, SUBLANES), :] = hb
        return jnp.broadcast_to(hb[SUBLANES - 1:SUBLANES, :], (SUBLANES, LRU_WIDTH))

    h = lax.fori_loop(0, tm // SUBLANES, group_step, hcar_ref[...], unroll=8)
    hcar_ref[...] = h
    lh_ref[...] = h[0:1, :]

    lg = _dot(u, win_ref[:, c0 + LRU_WIDTH:c0 + 2 * LRU_WIDTH])
    lo_ref[...] = (hs_ref[...] * jax.nn.gelu(lg, approximate=True)).astype(BF16)


def _inproj_prompt(x, mod3, g1, w_in, gq, gk, gmat, tabs, cw, cb, wr, br, wi, bi, lam):
    bsz, s_len, _ = x.shape
    tm = TM_IN
    row_spec = lambda w: pl.BlockSpec((None, tm, w), lambda b, i: (b, i, 0))
    tab_spec = pl.BlockSpec((tm, LANES), lambda b, i: (i, 0))
    mod_spec = lambda k: pl.BlockSpec((None, 1, D_MODEL), lambda b, i: (b * 6 + k, 0, 0))
    act = lambda dt: jax.ShapeDtypeStruct((bsz, s_len, ATT_WIDTH), dt)
    return pl.pallas_call(
        _inproj_prompt_kernel,
        grid=(bsz, s_len // tm),
        in_specs=[row_spec(D_MODEL), mod_spec(0), mod_spec(1), _const_spec(g1.shape), _const_spec(w_in.shape),
                  _const_spec(gq.shape), _const_spec(gk.shape), _const_spec(gmat.shape),
                  tab_spec, tab_spec, tab_spec,
                  _const_spec(cw.shape), _const_spec(cb.shape), _const_spec(wr.shape), _const_spec(br.shape),
                  _const_spec(wi.shape), _const_spec(bi.shape), _const_spec(lam.shape)],
        out_specs=[row_spec(ATT_WIDTH)] * 6 + [
            pl.BlockSpec((None, LRU_CONV - 1, LRU_WIDTH), lambda b, i: (b, 0, 0)),
            pl.BlockSpec((None, 1, LRU_WIDTH), lambda b, i: (b, 0, 0))],
        out_shape=[act(BF16), act(F32), act(BF16), act(F32), act(BF16), act(BF16),
                   jax.ShapeDtypeStruct((bsz, LRU_CONV - 1, LRU_WIDTH), F32),
                   jax.ShapeDtypeStruct((bsz, 1, LRU_WIDTH), F32)],
        scratch_shapes=[pltpu.VMEM((tm + SUBLANES, LRU_WIDTH), F32),
                        pltpu.VMEM((SUBLANES, LRU_WIDTH), F32),
                        pltpu.VMEM((tm, LRU_WIDTH), F32),
                        pltpu.VMEM((tm, LRU_WIDTH), F32),
                        pltpu.VMEM((tm, LRU_WIDTH), F32)],
        compiler_params=pltpu.CompilerParams(dimension_semantics=("arbitrary", "arbitrary"),
                                             vmem_limit_bytes=VMEM_LIMIT),
        name="inproj_prompt",
    )(x, mod3, mod3, g1, w_in, gq, gk, gmat, *tabs, cw, cb, wr, br, wi, bi, lam)


def _attn_prompt_kernel(q_ref, k_ref, v_ref, lq1_ref, lk1_ref, lq2_ref, lk2_ref, gs_ref, o_ref):
    tq = q_ref.shape[0]
    qi = pl.program_id(2)
    q = q_ref[...].astype(F32)
    lane = lax.broadcasted_iota(jnp.int32, (tq, LANES), 1)
    qc = (jnp.where(lane < QK_SUB_DIM, q, 0.0).astype(BF16), jnp.where(lane >= QK_SUB_DIM, q, 0.0).astype(BF16))

    def tile(j, carry, masked):
        k0 = pl.multiple_of(j * TK, TK)
        ks = k_ref[pl.ds(k0, TK), :]
        vs = v_ref[pl.ds(k0, TK), :]
        out = []
        for c in range(2):
            m, l, acc = carry[c]
            s = _dot_nt(qc[c], ks)
            if masked:
                rows = lax.broadcasted_iota(jnp.int32, (tq, TK), 0)
                cols = lax.broadcasted_iota(jnp.int32, (tq, TK), 1)
                s = jnp.where(cols <= rows, s, NEG)
            m_new = jnp.maximum(m, jnp.max(s, axis=1, keepdims=True))
            alpha = jnp.exp(m - m_new)
            p = jnp.exp(s - m_new)
            l = alpha * l + jnp.sum(p, axis=1, keepdims=True)
            acc = alpha * acc + _dot(p.astype(BF16), vs)
            out.append((m_new, l, acc))
        return tuple(out)

    init = tuple((jnp.full((tq, 1), NEG, F32), jnp.zeros((tq, 1), F32), jnp.zeros((tq, V_HEAD_DIM), F32))
                 for _ in range(2))
    carry = lax.fori_loop(0, qi, lambda j, c: tile(j, c, False), init)
    (_, l0, a0), (_, l1, a1) = tile(qi, carry, True)
    lam = _diff_lambda(lq1_ref[...], lk1_ref[...], lq2_ref[...], lk2_ref[...])
    o = a0 / l0 - lam * (a1 / l1)
    o_ref[...] = _subln(o, gs_ref[...]).astype(o_ref.dtype)


def _attn_prompt(q, k, v, lq1, lk1, lq2, lk2, gs):
    bsz, s_len, _ = q.shape
    q_spec = pl.BlockSpec((None, TQ, LANES), lambda b, h, i: (b, i, h))
    kv_spec = pl.BlockSpec((None, s_len, LANES), lambda b, h, i: (b, 0, h))
    small = lambda a: pl.BlockSpec(a.shape, lambda b, h, i: (0, 0))
    return pl.pallas_call(
        _attn_prompt_kernel,
        grid=(bsz, ATT_HEADS, s_len // TQ),
        in_specs=[q_spec, kv_spec, kv_spec, small(lq1), small(lk1), small(lq2), small(lk2), small(gs)],
        out_specs=q_spec,
        out_shape=jax.ShapeDtypeStruct((bsz, s_len, ATT_WIDTH), BF16),
        compiler_params=pltpu.CompilerParams(dimension_semantics=("arbitrary", "arbitrary", "arbitrary"),
                                             vmem_limit_bytes=VMEM_LIMIT),
        name="attn_prompt",
    )(q, k, v, lq1, lk1, lq2, lk2, gs)


def _ffn_prompt_kernel(x_ref, at_ref, lo_ref, gt1_ref, sh2_ref, sc2_ref, gt2_ref, g2_ref,
                       wout_ref, wup_ref, cfw_ref, cfb_ref, wdn_ref,
                       y_ref, fc_ref, upbuf_ref):
    tm = x_ref.shape[0]
    d_ff = wdn_ref.shape[0]

    @pl.when(pl.program_id(1) == 0)
    def _():
        upbuf_ref[0:SUBLANES, :] = jnp.zeros((SUBLANES, 2 * d_ff), F32)

    mix = jnp.concatenate([at_ref[...], lo_ref[...]], axis=1)
    x1 = x_ref[...] + gt1_ref[...] * _dot(mix, wout_ref[...])
    u2 = _modulated_norm(x1, g2_ref[...], sc2_ref[...], sh2_ref[...]).astype(BF16)

    acc = jnp.zeros((tm, D_MODEL), F32)
    for j in range(d_ff // FF_CHUNK):
        halves = []
        for base in (0, d_ff):
            c0 = base + j * FF_CHUNK
            cols = slice(c0, c0 + FF_CHUNK)
            up = _dot(u2, wup_ref[:, cols])
            upbuf_ref[SUBLANES:SUBLANES + tm, cols] = up
            w = cfw_ref[:, cols]
            hc = upbuf_ref[SUBLANES - 2:SUBLANES - 2 + tm, cols] * w[0:1]
            hc = hc + upbuf_ref[SUBLANES - 1:SUBLANES - 1 + tm, cols] * w[1:2]
            hc = hc + up * w[2:3] + cfb_ref[:, cols]
            halves.append(hc)
        g, val = halves
        hmid = (g * jax.nn.sigmoid(g) * val).astype(BF16)
        acc = acc + _dot(hmid, wdn_ref[j * FF_CHUNK:(j + 1) * FF_CHUNK, :])
    y_ref[...] = x1 + gt2_ref[...] * acc
    fc_ref[...] = upbuf_ref[SUBLANES + tm - (FFN_CONV - 1):SUBLANES + tm, :]
    upbuf_ref[0:SUBLANES, :] = upbuf_ref[tm:tm + SUBLANES, :]


def _ffn_prompt(x, attn, lru, mod3, g2, w_out, w_up, cfw, cfb, w_dn):
    bsz, s_len, _ = x.shape
    tm = TM_FFN
    d_ff = w_dn.shape[0]
    row_spec = lambda w: pl.BlockSpec((None, tm, w), lambda b, i: (b, i, 0))
    mod_spec = lambda k: pl.BlockSpec((None, 1, D_MODEL), lambda b, i: (b * 6 + k, 0, 0))
    return pl.pallas_call(
        _ffn_prompt_kernel,
        grid=(bsz, s_len // tm),
        in_specs=[row_spec(D_MODEL), row_spec(ATT_WIDTH), row_spec(LRU_WIDTH),
                  mod_spec(2), mod_spec(3), mod_spec(4), mod_spec(5), _const_spec(g2.shape),
                  _const_spec(w_out.shape), _const_spec(w_up.shape), _const_spec(cfw.shape),
                  _const_spec(cfb.shape), _const_spec(w_dn.shape)],
        out_specs=[row_spec(D_MODEL), pl.BlockSpec((None, FFN_CONV - 1, 2 * d_ff), lambda b, i: (b, 0, 0))],
        out_shape=[jax.ShapeDtypeStruct((bsz, s_len, D_MODEL), F32),
                   jax.ShapeDtypeStruct((bsz, FFN_CONV - 1, 2 * d_ff), F32)],
        scratch_shapes=[pltpu.VMEM((tm + SUBLANES, 2 * d_ff), F32)],
        compiler_params=pltpu.CompilerParams(dimension_semantics=("arbitrary", "arbitrary"),
                                             vmem_limit_bytes=VMEM_LIMIT),
        name="ffn_prompt",
    )(x, attn, lru, mod3, mod3, mod3, mod3, g2, w_out, w_up, cfw, cfb, w_dn)


def _inproj_sample_kernel(x_ref, mod_ref, g1_ref, win_ref, gq_ref, gk_ref, gmat_ref,
                          cos_ref, sa_ref, sb_ref, cw_ref, cb_ref, wr_ref, br_ref, wi_ref, bi_ref, lam_ref,
                          st_ref, h0_ref,
                          q_ref, k_ref, v_ref, lo_ref, lc_ref, lh_ref):
    nb = h0_ref.shape[0]
    nt = x_ref.shape[0] // nb
    rep = lambda a: jnp.concatenate([a] * nt, axis=0)
    sh1 = rep(mod_ref[:, 0:D_MODEL])
    sc1 = rep(mod_ref[:, D_MODEL:2 * D_MODEL])
    u = _modulated_norm(x_ref[...], g1_ref[...], sc1, sh1).astype(BF16)
    per_t = lambda r: jnp.concatenate([jnp.broadcast_to(r[t:t + 1, :], (nb, LANES)) for t in range(nt)], axis=0)
    cos, sa, sb = per_t(cos_ref[...]), per_t(sa_ref[...]), per_t(sb_ref[...])
    gmat = gmat_ref[...]

    q = _group_norm_rope(_dot(u, win_ref[:, 0:ATT_WIDTH]), gq_ref[...], gmat, cos, sa, sb)
    q_ref[...] = q * QK_SCALE
    k_ref[...] = _group_norm_rope(_dot(u, win_ref[:, ATT_WIDTH:2 * ATT_WIDTH]), gk_ref[...], gmat, cos, sa, sb)
    v_ref[...] = _dot(u, win_ref[:, 2 * ATT_WIDTH:3 * ATT_WIDTH])

    c0 = 3 * ATT_WIDTH
    lx = _dot(u, win_ref[:, c0:c0 + LRU_WIDTH])
    lg = _dot(u, win_ref[:, c0 + LRU_WIDTH:c0 + 2 * LRU_WIDTH])
    pad = [st_ref[i] for i in range(LRU_CONV - 1)] + [lx[t * nb:(t + 1) * nb, :] for t in range(nt)]
    for i in range(LRU_CONV - 1):
        lc_ref[i] = pad[nt + i]
    cw = cw_ref[...]
    xcs = []
    for t in range(nt):
        xc = pad[t] * cw[0:1]
        for kk in range(1, LRU_CONV):
            xc = xc + pad[t + kk] * cw[kk:kk + 1]
        xcs.append(xc + cb_ref[...])
    a, gx = _lru_gates(jnp.concatenate(xcs, axis=0), wr_ref, br_ref[...], wi_ref, bi_ref[...],
                       _neg_c_softplus(lam_ref[...]))
    h = h0_ref[...]
    hs = []
    for t in range(nt):
        h = a[t * nb:(t + 1) * nb, :] * h + gx[t * nb:(t + 1) * nb, :]
        hs.append(h)
    lh_ref[...] = h
    lo_ref[...] = (jnp.concatenate(hs, axis=0) * jax.nn.gelu(lg, approximate=True)).astype(BF16)


def _inproj_sample(x_tb, mod_s, g1, w_in, gq, gk, gmat, tabs_s, cw, cb, wr, br, wi, bi, lam, st, h0):
    m = x_tb.shape[0]
    nb = h0.shape[0]
    act = jax.ShapeDtypeStruct((m, ATT_WIDTH), F32)
    return pl.pallas_call(
        _inproj_sample_kernel,
        out_shape=[act, act, act, jax.ShapeDtypeStruct((m, LRU_WIDTH), BF16),
                   jax.ShapeDtypeStruct((LRU_CONV - 1, nb, LRU_WIDTH), F32),
                   jax.ShapeDtypeStruct((nb, LRU_WIDTH), F32)],
        compiler_params=pltpu.CompilerParams(vmem_limit_bytes=VMEM_LIMIT),
        name="inproj_sample",
    )(x_tb, mod_s, g1, w_in, gq, gk, gmat, *tabs_s, cw, cb, wr, br, wi, bi, lam, st, h0)


def _attn_sample_kernel(n_pages, pt_ref, q_ref, kn_ref, vn_ref, lq1_ref, lk1_ref, lq2_ref, lk2_ref, gs_ref, *refs):
    k_refs = refs[:n_pages]
    v_refs = refs[n_pages:2 * n_pages]
    o_ref = refs[2 * n_pages]
    page = k_refs[0].shape[0]
    nq = q_ref.shape[0]
    nrow = 2 * ATT_HEADS * nq

    r = lax.broadcasted_iota(jnp.int32, (nrow, ATT_WIDTH), 0)
    ln = lax.broadcasted_iota(jnp.int32, (nrow, ATT_WIDTH), 1)
    own = ((ln // V_HEAD_DIM) == ((r // nq) % ATT_HEADS)) & (((ln % V_HEAD_DIM) // QK_SUB_DIM) == (r // (ATT_HEADS * nq)))
    qm = jnp.where(own, jnp.concatenate([q_ref[...]] * (2 * ATT_HEADS), axis=0), 0.0).astype(BF16)

    s_tiles = [_dot_nt(qm, k_refs[j][...].astype(BF16)) for j in range(n_pages)]
    k_new = jnp.concatenate([kn_ref[...], jnp.zeros((page - nq, ATT_WIDTH), F32)], axis=0).astype(BF16)
    v_new = jnp.concatenate([vn_ref[...], jnp.zeros((page - nq, ATT_WIDTH), F32)], axis=0).astype(BF16)
    rt = lax.broadcasted_iota(jnp.int32, (nrow, page), 0) % nq
    ct = lax.broadcasted_iota(jnp.int32, (nrow, page), 1)
    s_new = jnp.where((ct <= rt) & (ct < nq), _dot_nt(qm, k_new), NEG)
    s_tiles.append(s_new)

    m = s_tiles[0]
    for s in s_tiles[1:]:
        m = jnp.maximum(m, s)
    m = jnp.max(m, axis=1, keepdims=True)
    l = jnp.zeros((nrow, 1), F32)
    acc = jnp.zeros((nrow, ATT_WIDTH), F32)
    for j, s in enumerate(s_tiles):
        p = jnp.exp(s - m)
        l = l + jnp.sum(p, axis=1, keepdims=True)
        vj = v_refs[j][...].astype(BF16) if j < n_pages else v_new
        acc = acc + _dot(p.astype(BF16), vj)
    o = acc / l
    half = nrow // 2
    lam = _diff_lambda(lq1_ref[...], lk1_ref[...], lq2_ref[...], lk2_ref[...])
    diff = o[0:half, :] - lam * o[half:nrow, :]
    head_of_lane = lax.broadcasted_iota(jnp.int32, (nq, ATT_WIDTH), 1) // V_HEAD_DIM
    out = jnp.zeros((nq, ATT_WIDTH), F32)
    for h in range(ATT_HEADS):
        out = out + jnp.where(head_of_lane == h, diff[h * nq:(h + 1) * nq, :], 0.0)
    gs = gs_ref[...]
    o_ref[...] = jnp.concatenate(
        [_subln(out[:, h * V_HEAD_DIM:(h + 1) * V_HEAD_DIM], gs) for h in range(ATT_HEADS)], axis=1)


def _attn_sample(page_table, q8, kn8, vn8, cache_k, cache_v, lq1, lk1, lq2, lk2, gs):
    nb, n_pages = page_table.shape
    nq = q8.shape[1]
    page = cache_k.shape[1]
    new_spec = pl.BlockSpec((None, nq, ATT_WIDTH), lambda b, pt: (b, 0, 0))
    small = lambda a: pl.BlockSpec(a.shape, lambda b, pt: (0, 0))
    page_spec = lambda j: pl.BlockSpec((None, page, ATT_WIDTH), lambda b, pt: (pt[b, j], 0, 0))
    grid_spec = pltpu.PrefetchScalarGridSpec(
        num_scalar_prefetch=1,
        grid=(nb,),
        in_specs=[new_spec, new_spec, new_spec, small(lq1), small(lk1), small(lq2), small(lk2), small(gs)]
        + [page_spec(j) for j in range(n_pages)] * 2,
        out_specs=new_spec,
    )
    return pl.pallas_call(
        functools.partial(_attn_sample_kernel, n_pages),
        grid_spec=grid_spec,
        out_shape=jax.ShapeDtypeStruct((nb, nq, ATT_WIDTH), F32),
        compiler_params=pltpu.CompilerParams(dimension_semantics=("arbitrary",), vmem_limit_bytes=VMEM_LIMIT),
        name="attn_sample",
    )(page_table, q8, kn8, vn8, lq1, lk1, lq2, lk2, gs, *([cache_k] * n_pages), *([cache_v] * n_pages))


def _ffn_sample_kernel(x_ref, at_ref, lo_ref, mod_ref, g2_ref, wout_ref, wup_ref, cfw_ref, cfb_ref, wdn_ref, st_ref,
                       y_ref, fc_ref):
    nb = st_ref.shape[1]
    nt = x_ref.shape[0] // nb
    d_ff = wdn_ref.shape[0]
    rep = lambda a: jnp.concatenate([a] * nt, axis=0)
    gt1 = rep(mod_ref[:, 2 * D_MODEL:3 * D_MODEL])
    sh2 = rep(mod_ref[:, 3 * D_MODEL:4 * D_MODEL])
    sc2 = rep(mod_ref[:, 4 * D_MODEL:5 * D_MODEL])
    gt2 = rep(mod_ref[:, 5 * D_MODEL:6 * D_MODEL])

    mix = jnp.concatenate([at_ref[...].astype(BF16), lo_ref[...]], axis=1)
    x1 = x_ref[...] + gt1 * _dot(mix, wout_ref[...])
    u2 = _modulated_norm(x1, g2_ref[...], sc2, sh2).astype(BF16)

    acc = jnp.zeros((nt * nb, D_MODEL), F32)
    for j in range(d_ff // FF_CHUNK):
        halves = []
        for base in (0, d_ff):
            c0 = base + j * FF_CHUNK
            cols = slice(c0, c0 + FF_CHUNK)
            up = _dot(u2, wup_ref[:, cols])
            pad = [st_ref[i, :, cols] for i in range(FFN_CONV - 1)] + [up[t * nb:(t + 1) * nb, :] for t in range(nt)]
            for i in range(FFN_CONV - 1):
                fc_ref[i, :, cols] = pad[nt + i]
            w = cfw_ref[:, cols]
            b = cfb_ref[:, cols]
            hcs = []
            for t in range(nt):
                hc = pad[t] * w[0:1]
                for kk in range(1, FFN_CONV):
                    hc = hc + pad[t + kk] * w[kk:kk + 1]
                hcs.append(hc + b)
            halves.append(jnp.concatenate(hcs, axis=0))
        g, val = halves
        hmid = (g * jax.nn.sigmoid(g) * val).astype(BF16)
        acc = acc + _dot(hmid, wdn_ref[j * FF_CHUNK:(j + 1) * FF_CHUNK, :])
    y_ref[...] = x1 + gt2 * acc


def _ffn_sample(x_tb, attn_tb, lru_tb, mod_s, g2, w_out, w_up, cfw, cfb, w_dn, st):
    m = x_tb.shape[0]
    return pl.pallas_call(
        _ffn_sample_kernel,
        out_shape=[jax.ShapeDtypeStruct((m, D_MODEL), F32), jax.ShapeDtypeStruct(st.shape, F32)],
        compiler_params=pltpu.CompilerParams(vmem_limit_bytes=VMEM_LIMIT),
        name="ffn_sample",
    )(x_tb, attn_tb, lru_tb, mod_s, g2, w_out, w_up, cfw, cfb, w_dn, st)


def _block_diag_halves(w):
    n, bd, _ = w.shape
    eye = jnp.eye(n // 2, dtype=w.dtype)
    halves = [jnp.einsum('nij,nm->nimj', w[s * (n // 2):(s + 1) * (n // 2)], eye).reshape(n // 2 * bd, n // 2 * bd)
              for s in range(2)]
    return jnp.stack(halves).astype(BF16)


def kernel(x_prompt, x_sample, cache_k, cache_v, page_table, state_lru_conv, state_lru_h, state_ffn_conv, c_prompt, c_sample, g_norm1, g_norm2, w_ada, b_ada, w_in, g_q, g_k, lam_q1, lam_k1, lam_q2, lam_k2, g_subln, w_out, conv_lru_w, conv_lru_b, w_rgate, b_rgate, w_igate, b_igate, lru_lambda, w_up, conv_ffn_w, conv_ffn_b, w_down):
    depth = w_in.shape[0]
    assert depth == 1, "single-layer step"
    bsz, s_len, _ = x_prompt.shape
    nb, nt, _ = x_sample.shape
    n_pages, page = page_table.shape[1], cache_k.shape[2]
    past_len = n_pages * page
    d_ff = w_down.shape[1]

    w_in_b = w_in[0].astype(BF16)
    w_out_b = w_out[0].astype(BF16)
    w_up_b = w_up[0].astype(BF16)
    w_dn_b = w_down[0].astype(BF16)
    wr = _block_diag_halves(w_rgate[0])
    wi = _block_diag_halves(w_igate[0])
    br = b_rgate[0].reshape(1, LRU_WIDTH)
    bi = b_igate[0].reshape(1, LRU_WIDTH)
    n_grp = ATT_WIDTH // QK_SUB_DIM
    gq = jnp.tile(g_q[0], n_grp)[None, :]
    gk = jnp.tile(g_k[0], n_grp)[None, :]
    grp = jnp.arange(ATT_WIDTH) // QK_SUB_DIM
    gmat = jnp.where(grp[:, None] == grp[None, :], 1.0 / QK_SUB_DIM, 0.0).astype(BF16)
    g1, g2 = g_norm1, g_norm2
    cw, cb = conv_lru_w[0], conv_lru_b
    cfw, cfb = conv_ffn_w[0], conv_ffn_b
    lam = lru_lambda
    lams = (lam_q1, lam_k1, lam_q2, lam_k2)
    gs = g_subln

    n_c = bsz + nb
    n_pad = -n_c % SUBLANES
    c_all = jnp.concatenate([c_prompt, c_sample, jnp.zeros((n_pad, D_MODEL), F32)], axis=0)
    mod = _ada(c_all, w_ada[0], b_ada)
    mod3 = mod[:bsz].reshape(bsz * 6, 1, D_MODEL)
    mod_s = mod[bsz:bsz + nb]

    tabs = _rope_tables(s_len)
    pad_rows = SUBLANES - nt
    tabs_s = tuple(t[past_len:past_len + SUBLANES] for t in tabs)

    q_p, kf_p, kb_p, vf_p, vb_p, lo_p, lc_p, lh_p = _inproj_prompt(
        x_prompt, mod3, g1, w_in_b, gq, gk, gmat, tabs, cw, cb, wr, br, wi, bi, lam)
    at_p = _attn_prompt(q_p, kb_p, vb_p, *lams, gs)
    y_p, fc_p = _ffn_prompt(x_prompt, at_p, lo_p, mod3, g2, w_out_b, w_up_b, cfw, cfb, w_dn_b)

    x_tb = x_sample.transpose(1, 0, 2).reshape(nt * nb, D_MODEL)
    st_lru = state_lru_conv[0].transpose(1, 0, 2)
    q_s, k_s, v_s, lo_s, lc_s, lh_s = _inproj_sample(
        x_tb, mod_s, g1, w_in_b, gq, gk, gmat, tabs_s, cw, cb, wr, br, wi, bi, lam, st_lru, state_lru_h[0])
    to_bt = lambda a: a.reshape(nt, nb, -1).transpose(1, 0, 2)
    pad_t = lambda a: jnp.pad(a, ((0, 0), (0, pad_rows), (0, 0)))
    k_bt, v_bt = to_bt(k_s), to_bt(v_s)
    at_s = _attn_sample(page_table, pad_t(to_bt(q_s)), pad_t(k_bt), pad_t(v_bt),
                        cache_k[0].reshape(-1, page, ATT_WIDTH), cache_v[0].reshape(-1, page, ATT_WIDTH), *lams, gs)
    at_tb = at_s[:, :nt].transpose(1, 0, 2).reshape(nt * nb, ATT_WIDTH)
    y_tb, fc_s = _ffn_sample(x_tb, at_tb, lo_s, mod_s, g2, w_out_b, w_up_b, cfw, cfb, w_dn_b,
                             state_ffn_conv[0].transpose(1, 0, 2))

    hd = (ATT_HEADS, 2, QK_SUB_DIM)
    return (y_p, to_bt(y_tb),
            kf_p.reshape(1, bsz, s_len, *hd), vf_p.reshape(1, bsz, s_len, ATT_HEADS, V_HEAD_DIM),
            lc_p[None], lh_p.reshape(1, bsz, LRU_WIDTH), fc_p[None],
            k_bt.reshape(1, nb, nt, *hd), v_bt.reshape(1, nb, nt, ATT_HEADS, V_HEAD_DIM),
            lc_s.transpose(1, 0, 2)[None], lh_s[None], fc_s.transpose(1, 0, 2)[None])
```

```python
import functools
import math

import jax
import jax.numpy as jnp
from jax import lax
from jax.experimental import pallas as pl
from jax.experimental.pallas import tpu as pltpu

F32 = jnp.float32
BF16 = jnp.bfloat16

D_MODEL = 1024
ATT_WIDTH = 512
LRU_WIDTH = 512
ATT_HEADS = 4
V_HEAD_DIM = 128
QK_SUB_DIM = 64
ROPE_DIM = 16
ROPE_THETA = 500000.0
LRU_BLOCKS = 8
LRU_C = 8.0
LRU_CONV = 4
FFN_CONV = 3
EPS = 1e-6
LAM_INIT = 0.8 - 0.6 * math.exp(-0.3 * 0)
QK_SCALE = QK_SUB_DIM ** -0.5

LANES = 128
SUBLANES = 8
VMEM_LIMIT = 56 * 1024 * 1024

TM_IN = 512
TQ = 512
TK = 512
TM_FFN = 256
FF_CHUNK = 256
NEG = -1e30


def _dot(a, b):
    return jnp.dot(a, b, preferred_element_type=F32)


def _dot_nt(a, b):
    return lax.dot_general(a, b, (((1,), (1,)), ((), ())), preferred_element_type=F32)


def _const_spec(shape):
    nd = len(shape)
    return pl.BlockSpec(shape, lambda *_: (0,) * nd, pipeline_mode=pl.Buffered(1))


def _modulated_norm(x, g, sc, sh):
    xn = x * lax.rsqrt(jnp.mean(x * x, axis=-1, keepdims=True) + EPS) * g
    return xn * (1.0 + sc) + sh


def _group_norm_rope(t, g_tiled, gmat, cos, sin_a, sin_b):
    ms = _dot((t * t).astype(BF16), gmat)
    tn = t * lax.rsqrt(ms + EPS) * g_tiled
    outs = []
    for h in range(ATT_WIDTH // LANES):
        th = tn[:, h * LANES:(h + 1) * LANES]
        outs.append(th * cos + pltpu.roll(th, LANES - 8, 1) * sin_a + pltpu.roll(th, 8, 1) * sin_b)
    return jnp.concatenate(outs, axis=1)


def _lru_gates(xc, wr_ref, br, wi_ref, bi, neg_c_softplus):
    half = LRU_WIDTH // 2
    xb = xc.astype(BF16)
    lo, hi = xb[:, :half], xb[:, half:]
    r = jax.nn.sigmoid(jnp.concatenate([_dot(lo, wr_ref[0]), _dot(hi, wr_ref[1])], axis=1) + br)
    ig = jax.nn.sigmoid(jnp.concatenate([_dot(lo, wi_ref[0]), _dot(hi, wi_ref[1])], axis=1) + bi)
    log_a = neg_c_softplus * r
    a = jnp.exp(log_a)
    one_minus_a2 = -jnp.tanh(log_a) * (a * a + 1.0)
    return a, jnp.sqrt(one_minus_a2) * (ig * xc)


def _neg_c_softplus(lam):
    z = -lam
    return -LRU_C * (jnp.maximum(z, 0.0) + jnp.log1p(jnp.exp(-jnp.abs(z))))


def _diff_lambda(lq1, lk1, lq2, lk2):
    s1 = jnp.sum(lq1 * lk1, axis=-1, keepdims=True)
    s2 = jnp.sum(lq2 * lk2, axis=-1, keepdims=True)
    return jnp.exp(s1) - jnp.exp(s2) + LAM_INIT


def _subln(o, g):
    return o * lax.rsqrt(jnp.mean(o * o, axis=-1, keepdims=True) + EPS) * g * (1.0 - LAM_INIT)


def _ada_kernel(c_ref, w_ref, b_ref, o_ref):
    c = c_ref[...]
    s = (c * jax.nn.sigmoid(c)).astype(BF16)
    o_ref[...] = _dot(s, w_ref[...].astype(BF16)) + b_ref[...]


def _ada(c_all, w_ada, b_ada):
    m = c_all.shape[0]
    n = w_ada.shape[1]
    tn = 1024
    return pl.pallas_call(
        _ada_kernel,
        grid=(n // tn,),
        in_specs=[pl.BlockSpec((m, D_MODEL), lambda j: (0, 0)),
                  pl.BlockSpec((D_MODEL, tn), lambda j: (0, j)),
                  pl.BlockSpec((1, tn), lambda j: (0, j))],
        out_specs=pl.BlockSpec((m, tn), lambda j: (0, j)),
        out_shape=jax.ShapeDtypeStruct((m, n), F32),
        compiler_params=pltpu.CompilerParams(dimension_semantics=("arbitrary",), vmem_limit_bytes=VMEM_LIMIT),
        name="ada_mod",
    )(c_all, w_ada, b_ada)


def _rope_table_kernel(freq_ref, ma_ref, mb_ref, c_ref, sa_ref, sb_ref):
    tm = c_ref.shape[0]
    pos = (pl.program_id(0) * tm + lax.broadcasted_iota(jnp.int32, (tm, LANES), 0)).astype(F32)
    ang = pos * freq_ref[...]
    s = jnp.sin(ang)
    c_ref[...] = jnp.cos(ang)
    sa_ref[...] = -s * ma_ref[...]
    sb_ref[...] = s * mb_ref[...]


def _rope_tables(n_pos):
    half = ROPE_DIM // 2
    freqs = ROPE_THETA ** (-jnp.arange(half, dtype=F32) * 2.0 / ROPE_DIM)
    d = jnp.arange(LANES) % QK_SUB_DIM
    freq_lane = jnp.where(d < ROPE_DIM, freqs[d % half], 0.0).astype(F32)[None, :]
    mask_a = (d < half).astype(F32)[None, :]
    mask_b = ((d >= half) & (d < ROPE_DIM)).astype(F32)[None, :]
    tm = 512
    row = pl.BlockSpec((1, LANES), lambda i: (0, 0))
    tab = pl.BlockSpec((tm, LANES), lambda i: (i, 0))
    shp = jax.ShapeDtypeStruct((n_pos, LANES), F32)
    return pl.pallas_call(
        _rope_table_kernel,
        grid=(n_pos // tm,),
        in_specs=[row, row, row],
        out_specs=[tab, tab, tab],
        out_shape=[shp, shp, shp],
        compiler_params=pltpu.CompilerParams(dimension_semantics=("arbitrary",)),
        name="rope_tables",
    )(freq_lane, mask_a, mask_b)


def _inproj_prompt_kernel(x_ref, sh_ref, sc_ref, g1_ref, win_ref, gq_ref, gk_ref, gmat_ref,
                          cos_ref, sa_ref, sb_ref, cw_ref, cb_ref, wr_ref, br_ref, wi_ref, bi_ref, lam_ref,
                          q_ref, kf_ref, kb_ref, vf_ref, vb_ref, lo_ref, lc_ref, lh_ref,
                          buf_ref, hcar_ref, acum_ref, bcum_ref, hs_ref):
    tm = x_ref.shape[0]

    @pl.when(pl.program_id(1) == 0)
    def _():
        buf_ref[0:SUBLANES, :] = jnp.zeros((SUBLANES, LRU_WIDTH), F32)
        hcar_ref[...] = jnp.zeros((SUBLANES, LRU_WIDTH), F32)

    u = _modulated_norm(x_ref[...], g1_ref[...], sc_ref[...], sh_ref[...]).astype(BF16)
    cos, sa, sb = cos_ref[...], sa_ref[...], sb_ref[...]
    gmat = gmat_ref[...]

    q = _group_norm_rope(_dot(u, win_ref[:, 0:ATT_WIDTH]), gq_ref[...], gmat, cos, sa, sb)
    q_ref[...] = (q * QK_SCALE).astype(BF16)
    k = _group_norm_rope(_dot(u, win_ref[:, ATT_WIDTH:2 * ATT_WIDTH]), gk_ref[...], gmat, cos, sa, sb)
    kf_ref[...] = k.T
    kb_ref[...] = k.astype(BF16)
    v = _dot(u, win_ref[:, 2 * ATT_WIDTH:3 * ATT_WIDTH])
    for h in range(ATT_HEADS):
        vf_ref[:, h, :] = v[:, h * V_HEAD_DIM:(h + 1) * V_HEAD_DIM]
    vb_ref[...] = v.astype(BF16)

    c0 = 3 * ATT_WIDTH
    lx = _dot(u, win_ref[:, c0:c0 + LRU_WIDTH])
    buf_ref[SUBLANES:SUBLANES + tm, :] = lx
    cw = cw_ref[...]
    xc = buf_ref[SUBLANES - 3:SUBLANES - 3 + tm, :] * cw[0:1]
    xc = xc + buf_ref[SUBLANES - 2:SUBLANES - 2 + tm, :] * cw[1:2]
    xc = xc + buf_ref[SUBLANES - 1:SUBLANES - 1 + tm, :] * cw[2:3]
    xc = xc + lx * cw[3:4] + cb_ref[...]
    lc_ref[...] = buf_ref[SUBLANES + tm - (LRU_CONV - 1):SUBLANES + tm, :]
    buf_ref[0:SUBLANES, :] = buf_ref[tm:tm + SUBLANES, :]

    a, gx = _lru_gates(xc, wr_ref, br_ref[...], wi_ref, bi_ref[...], _neg_c_softplus(lam_ref[...]))

    row_in_group = lax.broadcasted_iota(jnp.int32, (tm, LRU_WIDTH), 0) & (SUBLANES - 1)
    for d in (1, 2, 4):
        keep = row_in_group >= d
        a_prev = jnp.where(keep, pltpu.roll(a, d, 0), 1.0)
        g_prev = jnp.where(keep, pltpu.roll(gx, d, 0), 0.0)
        gx = a * g_prev + gx
        a = a * a_prev
    acum_ref[...] = a
    bcum_ref[...] = gx

    def group_step(j, h):
        r0 = pl.multiple_of(j * SUBLANES, SUBLANES)
        hb = acum_ref[pl.ds(r0, SUBLANES), :] * h + bcum_ref[pl.ds(r0, SUBLANES), :]
        hs_ref[pl.ds(r0, SUBLANES), :] = hb
        return jnp.broadcast_to(hb[SUBLANES - 1:SUBLANES, :], (SUBLANES, LRU_WIDTH))

    h = lax.fori_loop(0, tm // SUBLANES, group_step, hcar_ref[...], unroll=8)
    hcar_ref[...] = h
    lh_ref[...] = h[0:1, :]

    lg = _dot(u, win_ref[:, c0 + LRU_WIDTH:c0 + 2 * LRU_WIDTH])
    lo_ref[...] = (hs_ref[...] * jax.nn.gelu(lg, approximate=True)).astype(BF16)


def _inproj_prompt(x, mod3, g1, w_in, gq, gk, gmat, tabs, cw, cb, wr, br, wi, bi, lam):
    bsz, s_len, _ = x.shape
    tm = TM_IN
    row_spec = lambda w: pl.BlockSpec((None, tm, w), lambda b, i: (b, i, 0))
    tab_spec = pl.BlockSpec((tm, LANES), lambda b, i: (i, 0))
    mod_spec = lambda k: pl.BlockSpec((None, 1, D_MODEL), lambda b, i: (b * 6 + k, 0, 0))
    act = lambda dt: jax.ShapeDtypeStruct((bsz, s_len, ATT_WIDTH), dt)
    return pl.pallas_call(
        _inproj_prompt_kernel,
        grid=(bsz, s_len // tm),
        in_specs=[row_spec(D_MODEL), mod_spec(0), mod_spec(1), _const_spec(g1.shape), _const_spec(w_in.shape),
                  _const_spec(gq.shape), _const_spec(gk.shape), _const_spec(gmat.shape),
                  tab_spec, tab_spec, tab_spec,
                  _const_spec(cw.shape), _const_spec(cb.shape), _const_spec(wr.shape), _const_spec(br.shape),
                  _const_spec(wi.shape), _const_spec(bi.shape), _const_spec(lam.shape)],
        out_specs=[row_spec(ATT_WIDTH),
                   pl.BlockSpec((None, ATT_WIDTH, tm), lambda b, i: (b, 0, i)),
                   row_spec(ATT_WIDTH),
                   pl.BlockSpec((None, tm, ATT_HEADS, V_HEAD_DIM), lambda b, i: (b, i, 0, 0)),
                   row_spec(ATT_WIDTH), row_spec(LRU_WIDTH),
                   pl.BlockSpec((None, LRU_CONV - 1, LRU_WIDTH), lambda b, i: (b, 0, 0)),
                   pl.BlockSpec((None, 1, LRU_WIDTH), lambda b, i: (b, 0, 0))],
        out_shape=[act(BF16), jax.ShapeDtypeStruct((bsz, ATT_WIDTH, s_len), F32), act(BF16),
                   jax.ShapeDtypeStruct((bsz, s_len, ATT_HEADS, V_HEAD_DIM), F32), act(BF16), act(BF16),
                   jax.ShapeDtypeStruct((bsz, LRU_CONV - 1, LRU_WIDTH), F32),
                   jax.ShapeDtypeStruct((bsz, 1, LRU_WIDTH), F32)],
        scratch_shapes=[pltpu.VMEM((tm + SUBLANES, LRU_WIDTH), F32),
                        pltpu.VMEM((SUBLANES, LRU_WIDTH), F32),
                        pltpu.VMEM((tm, LRU_WIDTH), F32),
                        pltpu.VMEM((tm, LRU_WIDTH), F32),
                        pltpu.VMEM((tm, LRU_WIDTH), F32)],
        compiler_params=pltpu.CompilerParams(dimension_semantics=("arbitrary", "arbitrary"),
                                             vmem_limit_bytes=VMEM_LIMIT),
        name="inproj_prompt",
    )(x, mod3, mod3, g1, w_in, gq, gk, gmat, *tabs, cw, cb, wr, br, wi, bi, lam)


def _attn_prompt_kernel(q_ref, k_ref, v_ref, lq1_ref, lk1_ref, lq2_ref, lk2_ref, gs_ref, o_ref):
    tq = q_ref.shape[0]
    qi = pl.program_id(2)
    q = q_ref[...].astype(F32)
    lane = lax.broadcasted_iota(jnp.int32, (tq, LANES), 1)
    qc = (jnp.where(lane < QK_SUB_DIM, q, 0.0).astype(BF16), jnp.where(lane >= QK_SUB_DIM, q, 0.0).astype(BF16))

    def tile(j, carry, masked):
        k0 = pl.multiple_of(j * TK, TK)
        ks = k_ref[pl.ds(k0, TK), :]
        vs = v_ref[pl.ds(k0, TK), :]
        out = []
        for c in range(2):
            m, l, acc = carry[c]
            s = _dot_nt(qc[c], ks)
            if masked:
                rows = lax.broadcasted_iota(jnp.int32, (tq, TK), 0)
                cols = lax.broadcasted_iota(jnp.int32, (tq, TK), 1)
                s = jnp.where(cols <= rows, s, NEG)
            m_new = jnp.maximum(m, jnp.max(s, axis=1, keepdims=True))
            alpha = jnp.exp(m - m_new)
            p = jnp.exp(s - m_new)
            l = alpha * l + jnp.sum(p, axis=1, keepdims=True)
            acc = alpha * acc + _dot(p.astype(BF16), vs)
            out.append((m_new, l, acc))
        return tuple(out)

    init = tuple((jnp.full((tq, 1), NEG, F32), jnp.zeros((tq, 1), F32), jnp.zeros((tq, V_HEAD_DIM), F32))
                 for _ in range(2))
    carry = lax.fori_loop(0, qi, lambda j, c: tile(j, c, False), init)
    (_, l0, a0), (_, l1, a1) = tile(qi, carry, True)
    lam = _diff_lambda(lq1_ref[...], lk1_ref[...], lq2_ref[...], lk2_ref[...])
    o = a0 / l0 - lam * (a1 / l1)
    o_ref[...] = _subln(o, gs_ref[...]).astype(o_ref.dtype)


def _attn_prompt(q, k, v, lq1, lk1, lq2, lk2, gs):
    bsz, s_len, _ = q.shape
    q_spec = pl.BlockSpec((None, TQ, LANES), lambda b, h, i: (b, i, h))
    kv_spec = pl.BlockSpec((None, s_len, LANES), lambda b, h, i: (b, 0, h))
    small = lambda a: pl.BlockSpec(a.shape, lambda b, h, i: (0, 0))
    return pl.pallas_call(
        _attn_prompt_kernel,
        grid=(bsz, ATT_HEADS, s_len // TQ),
        in_specs=[q_spec, kv_spec, kv_spec, small(lq1), small(lk1), small(lq2), small(lk2), small(gs)],
        out_specs=q_spec,
        out_shape=jax.ShapeDtypeStruct((bsz, s_len, ATT_WIDTH), BF16),
        compiler_params=pltpu.CompilerParams(dimension_semantics=("arbitrary", "arbitrary", "arbitrary"),
                                             vmem_limit_bytes=VMEM_LIMIT),
        name="attn_prompt",
    )(q, k, v, lq1, lk1, lq2, lk2, gs)


def _ffn_prompt_kernel(x_ref, at_ref, lo_ref, gt1_ref, sh2_ref, sc2_ref, gt2_ref, g2_ref,
                       wout_ref, wup_ref, cfw_ref, cfb_ref, wdn_ref,
                       y_ref, fc_ref, upbuf_ref):
    tm = x_ref.shape[0]
    d_ff = wdn_ref.shape[0]

    @pl.when(pl.program_id(1) == 0)
    def _():
        upbuf_ref[0:SUBLANES, :] = jnp.zeros((SUBLANES, 2 * d_ff), F32)

    mix = jnp.concatenate([at_ref[...], lo_ref[...]], axis=1)
    x1 = x_ref[...] + gt1_ref[...] * _dot(mix, wout_ref[...])
    u2 = _modulated_norm(x1, g2_ref[...], sc2_ref[...], sh2_ref[...]).astype(BF16)

    acc = jnp.zeros((tm, D_MODEL), F32)
    for j in range(d_ff // FF_CHUNK):
        halves = []
        for base in (0, d_ff):
            c0 = base + j * FF_CHUNK
            cols = slice(c0, c0 + FF_CHUNK)
            up = _dot(u2, wup_ref[:, cols])
            upbuf_ref[SUBLANES:SUBLANES + tm, cols] = up
            w = cfw_ref[:, cols]
            hc = upbuf_ref[SUBLANES - 2:SUBLANES - 2 + tm, cols] * w[0:1]
            hc = hc + upbuf_ref[SUBLANES - 1:SUBLANES - 1 + tm, cols] * w[1:2]
            hc = hc + up * w[2:3] + cfb_ref[:, cols]
            halves.append(hc)
        g, val = halves
        hmid = (g * jax.nn.sigmoid(g) * val).astype(BF16)
        acc = acc + _dot(hmid, wdn_ref[j * FF_CHUNK:(j + 1) * FF_CHUNK, :])
    y_ref[...] = x1 + gt2_ref[...] * acc
    fc_ref[...] = upbuf_ref[SUBLANES + tm - (FFN_CONV - 1):SUBLANES + tm, :]
    upbuf_ref[0:SUBLANES, :] = upbuf_ref[tm:tm + SUBLANES, :]


def _ffn_prompt(x, attn, lru, mod3, g2, w_out, w_up, cfw, cfb, w_dn):
    bsz, s_len, _ = x.shape
    tm = TM_FFN
    d_ff = w_dn.shape[0]
    row_spec = lambda w: pl.BlockSpec((None, tm, w), lambda b, i: (b, i, 0))
    mod_spec = lambda k: pl.BlockSpec((None, 1, D_MODEL), lambda b, i: (b * 6 + k, 0, 0))
    return pl.pallas_call(
        _ffn_prompt_kernel,
        grid=(bsz, s_len // tm),
        in_specs=[row_spec(D_MODEL), row_spec(ATT_WIDTH), row_spec(LRU_WIDTH),
                  mod_spec(2), mod_spec(3), mod_spec(4), mod_spec(5), _const_spec(g2.shape),
                  _const_spec(w_out.shape), _const_spec(w_up.shape), _const_spec(cfw.shape),
                  _const_spec(cfb.shape), _const_spec(w_dn.shape)],
        out_specs=[row_spec(D_MODEL), pl.BlockSpec((None, FFN_CONV - 1, 2 * d_ff), lambda b, i: (b, 0, 0))],
        out_shape=[jax.ShapeDtypeStruct((bsz, s_len, D_MODEL), F32),
                   jax.ShapeDtypeStruct((bsz, FFN_CONV - 1, 2 * d_ff), F32)],
        scratch_shapes=[pltpu.VMEM((tm + SUBLANES, 2 * d_ff), F32)],
        compiler_params=pltpu.CompilerParams(dimension_semantics=("arbitrary", "arbitrary"),
                                             vmem_limit_bytes=VMEM_LIMIT),
        name="ffn_prompt",
    )(x, attn, lru, mod3, mod3, mod3, mod3, g2, w_out, w_up, cfw, cfb, w_dn)


def _inproj_sample_kernel(x_ref, mod_ref, g1_ref, win_ref, gq_ref, gk_ref, gmat_ref,
                          cos_ref, sa_ref, sb_ref, cw_ref, cb_ref, wr_ref, br_ref, wi_ref, bi_ref, lam_ref,
                          st_ref, h0_ref,
                          q_ref, k_ref, v_ref, lo_ref, lc_ref, lh_ref):
    nb = h0_ref.shape[0]
    nt = x_ref.shape[0] // nb
    rep = lambda a: jnp.concatenate([a] * nt, axis=0)
    sh1 = rep(mod_ref[:, 0:D_MODEL])
    sc1 = rep(mod_ref[:, D_MODEL:2 * D_MODEL])
    u = _modulated_norm(x_ref[...], g1_ref[...], sc1, sh1).astype(BF16)
    per_t = lambda r: jnp.concatenate([jnp.broadcast_to(r[t:t + 1, :], (nb, LANES)) for t in range(nt)], axis=0)
    cos, sa, sb = per_t(cos_ref[...]), per_t(sa_ref[...]), per_t(sb_ref[...])
    gmat = gmat_ref[...]

    q = _group_norm_rope(_dot(u, win_ref[:, 0:ATT_WIDTH]), gq_ref[...], gmat, cos, sa, sb)
    q_ref[...] = q * QK_SCALE
    k_ref[...] = _group_norm_rope(_dot(u, win_ref[:, ATT_WIDTH:2 * ATT_WIDTH]), gk_ref[...], gmat, cos, sa, sb)
    v_ref[...] = _dot(u, win_ref[:, 2 * ATT_WIDTH:3 * ATT_WIDTH])

    c0 = 3 * ATT_WIDTH
    lx = _dot(u, win_ref[:, c0:c0 + LRU_WIDTH])
    lg = _dot(u, win_ref[:, c0 + LRU_WIDTH:c0 + 2 * LRU_WIDTH])
    pad = [st_ref[i] for i in range(LRU_CONV - 1)] + [lx[t * nb:(t + 1) * nb, :] for t in range(nt)]
    for i in range(LRU_CONV - 1):
        lc_ref[i] = pad[nt + i]
    cw = cw_ref[...]
    xcs = []
    for t in range(nt):
        xc = pad[t] * cw[0:1]
        for kk in range(1, LRU_CONV):
            xc = xc + pad[t + kk] * cw[kk:kk + 1]
        xcs.append(xc + cb_ref[...])
    a, gx = _lru_gates(jnp.concatenate(xcs, axis=0), wr_ref, br_ref[...], wi_ref, bi_ref[...],
                       _neg_c_softplus(lam_ref[...]))
    h = h0_ref[...]
    hs = []
    for t in range(nt):
        h = a[t * nb:(t + 1) * nb, :] * h + gx[t * nb:(t + 1) * nb, :]
        hs.append(h)
    lh_ref[...] = h
    lo_ref[...] = (jnp.concatenate(hs, axis=0) * jax.nn.gelu(lg, approximate=True)).astype(BF16)


def _inproj_sample(x_tb, mod_s, g1, w_in, gq, gk, gmat, tabs_s, cw, cb, wr, br, wi, bi, lam, st, h0):
    m = x_tb.shape[0]
    nb = h0.shape[0]
    act = jax.ShapeDtypeStruct((m, ATT_WIDTH), F32)
    return pl.pallas_call(
        _inproj_sample_kernel,
        out_shape=[act, act, act, jax.ShapeDtypeStruct((m, LRU_WIDTH), BF16),
                   jax.ShapeDtypeStruct((LRU_CONV - 1, nb, LRU_WIDTH), F32),
                   jax.ShapeDtypeStruct((nb, LRU_WIDTH), F32)],
        compiler_params=pltpu.CompilerParams(vmem_limit_bytes=VMEM_LIMIT),
        name="inproj_sample",
    )(x_tb, mod_s, g1, w_in, gq, gk, gmat, *tabs_s, cw, cb, wr, br, wi, bi, lam, st, h0)


def _attn_sample_kernel(n_pages, pt_ref, q_ref, kn_ref, vn_ref, lq1_ref, lk1_ref, lq2_ref, lk2_ref, gs_ref, *refs):
    kt_refs = refs[:n_pages]
    v_refs = refs[n_pages:2 * n_pages]
    o_ref = refs[2 * n_pages]
    page = kt_refs[0].shape[1]
    nq = q_ref.shape[0]
    hrows = 2 * nq

    lane = lax.broadcasted_iota(jnp.int32, (nq, V_HEAD_DIM), 1)
    qh = []
    for h in range(ATT_HEADS):
        q8 = q_ref[:, h * V_HEAD_DIM:(h + 1) * V_HEAD_DIM]
        qh.append(jnp.concatenate([jnp.where(lane < QK_SUB_DIM, q8, 0.0), jnp.where(lane >= QK_SUB_DIM, q8, 0.0)],
                                  axis=0).astype(BF16))

    s_tiles = []
    for j in range(n_pages):
        kt = kt_refs[j][...].astype(BF16)
        s_tiles.append(jnp.concatenate(
            [_dot(qh[h], kt[h * V_HEAD_DIM:(h + 1) * V_HEAD_DIM, :]) for h in range(ATT_HEADS)], axis=0))
    zpad = jnp.zeros((page - nq, V_HEAD_DIM), F32)
    k_new = [jnp.concatenate([kn_ref[:, h * V_HEAD_DIM:(h + 1) * V_HEAD_DIM], zpad], axis=0).astype(BF16)
             for h in range(ATT_HEADS)]
    v_new = [jnp.concatenate([vn_ref[:, h * V_HEAD_DIM:(h + 1) * V_HEAD_DIM], zpad], axis=0).astype(BF16)
             for h in range(ATT_HEADS)]
    nrow = ATT_HEADS * hrows
    rt = lax.broadcasted_iota(jnp.int32, (nrow, page), 0) % nq
    ct = lax.broadcasted_iota(jnp.int32, (nrow, page), 1)
    s_new = jnp.concatenate([_dot_nt(qh[h], k_new[h]) for h in range(ATT_HEADS)], axis=0)
    s_tiles.append(jnp.where(ct <= rt, s_new, NEG))

    m = s_tiles[0]
    for s in s_tiles[1:]:
        m = jnp.maximum(m, s)
    m = jnp.max(m, axis=1, keepdims=True)
    l = jnp.zeros((nrow, 1), F32)
    acc = [jnp.zeros((hrows, V_HEAD_DIM), F32) for _ in range(ATT_HEADS)]
    for j, s in enumerate(s_tiles):
        p = jnp.exp(s - m)
        l = l + jnp.sum(p, axis=1, keepdims=True)
        pb = p.astype(BF16)
        for h in range(ATT_HEADS):
            vj = v_refs[j][pl.ds(h, page, stride=ATT_HEADS), :].astype(BF16) if j < n_pages else v_new[h]
            acc[h] = acc[h] + _dot(pb[h * hrows:(h + 1) * hrows, :], vj)
    lam = _diff_lambda(lq1_ref[...], lk1_ref[...], lq2_ref[...], lk2_ref[...])
    gs = gs_ref[...]
    outs = []
    for h in range(ATT_HEADS):
        o = acc[h] / l[h * hrows:(h + 1) * hrows, :]
        outs.append(_subln(o[0:nq, :] - lam * o[nq:hrows, :], gs))
    o_ref[...] = jnp.concatenate(outs, axis=1)


def _attn_sample(page_table, q8, kn8, vn8, cache_kt, cache_v, lq1, lk1, lq2, lk2, gs):
    nb, n_pages = page_table.shape
    nq = q8.shape[1]
    page = cache_kt.shape[2]
    new_spec = pl.BlockSpec((None, nq, ATT_WIDTH), lambda b, pt: (b, 0, 0))
    small = lambda a: pl.BlockSpec(a.shape, lambda b, pt: (0, 0))
    kt_spec = lambda j: pl.BlockSpec((None, ATT_WIDTH, page), lambda b, pt: (pt[b, j], 0, 0))
    v_spec = lambda j: pl.BlockSpec((None, page * ATT_HEADS, V_HEAD_DIM), lambda b, pt: (pt[b, j], 0, 0))
    grid_spec = pltpu.PrefetchScalarGridSpec(
        num_scalar_prefetch=1,
        grid=(nb,),
        in_specs=[new_spec, new_spec, new_spec, small(lq1), small(lk1), small(lq2), small(lk2), small(gs)]
        + [kt_spec(j) for j in range(n_pages)] + [v_spec(j) for j in range(n_pages)],
        out_specs=new_spec,
    )
    return pl.pallas_call(
        functools.partial(_attn_sample_kernel, n_pages),
        grid_spec=grid_spec,
        out_shape=jax.ShapeDtypeStruct((nb, nq, ATT_WIDTH), F32),
        compiler_params=pltpu.CompilerParams(dimension_semantics=("arbitrary",), vmem_limit_bytes=VMEM_LIMIT),
        name="attn_sample",
    )(page_table, q8, kn8, vn8, lq1, lk1, lq2, lk2, gs, *([cache_kt] * n_pages), *([cache_v] * n_pages))


def _ffn_sample_kernel(x_ref, at_ref, lo_ref, mod_ref, g2_ref, wout_ref, wup_ref, cfw_ref, cfb_ref, wdn_ref, st_ref,
                       y_ref, fc_ref):
    nb = st_ref.shape[1]
    nt = x_ref.shape[0] // nb
    d_ff = wdn_ref.shape[0]
    rep = lambda a: jnp.concatenate([a] * nt, axis=0)
    gt1 = rep(mod_ref[:, 2 * D_MODEL:3 * D_MODEL])
    sh2 = rep(mod_ref[:, 3 * D_MODEL:4 * D_MODEL])
    sc2 = rep(mod_ref[:, 4 * D_MODEL:5 * D_MODEL])
    gt2 = rep(mod_ref[:, 5 * D_MODEL:6 * D_MODEL])

    mix = jnp.concatenate([at_ref[...].astype(BF16), lo_ref[...]], axis=1)
    x1 = x_ref[...] + gt1 * _dot(mix, wout_ref[...])
    u2 = _modulated_norm(x1, g2_ref[...], sc2, sh2).astype(BF16)

    acc = jnp.zeros((nt * nb, D_MODEL), F32)
    for j in range(d_ff // FF_CHUNK):
        halves = []
        for base in (0, d_ff):
            c0 = base + j * FF_CHUNK
            cols = slice(c0, c0 + FF_CHUNK)
            up = _dot(u2, wup_ref[:, cols])
            pad = [st_ref[i, :, cols] for i in range(FFN_CONV - 1)] + [up[t * nb:(t + 1) * nb, :] for t in range(nt)]
            for i in range(FFN_CONV - 1):
                fc_ref[i, :, cols] = pad[nt + i]
            w = cfw_ref[:, cols]
            b = cfb_ref[:, cols]
            hcs = []
            for t in range(nt):
                hc = pad[t] * w[0:1]
                for kk in range(1, FFN_CONV):
                    hc = hc + pad[t + kk] * w[kk:kk + 1]
                hcs.append(hc + b)
            halves.append(jnp.concatenate(hcs, axis=0))
        g, val = halves
        hmid = (g * jax.nn.sigmoid(g) * val).astype(BF16)
        acc = acc + _dot(hmid, wdn_ref[j * FF_CHUNK:(j + 1) * FF_CHUNK, :])
    y_ref[...] = x1 + gt2 * acc


def _ffn_sample(x_tb, attn_tb, lru_tb, mod_s, g2, w_out, w_up, cfw, cfb, w_dn, st):
    m = x_tb.shape[0]
    return pl.pallas_call(
        _ffn_sample_kernel,
        out_shape=[jax.ShapeDtypeStruct((m, D_MODEL), F32), jax.ShapeDtypeStruct(st.shape, F32)],
        compiler_params=pltpu.CompilerParams(vmem_limit_bytes=VMEM_LIMIT),
        name="ffn_sample",
    )(x_tb, attn_tb, lru_tb, mod_s, g2, w_out, w_up, cfw, cfb, w_dn, st)


def _block_diag_halves(w):
    n, bd, _ = w.shape
    eye = jnp.eye(n // 2, dtype=w.dtype)
    halves = [jnp.einsum('nij,nm->nimj', w[s * (n // 2):(s + 1) * (n // 2)], eye).reshape(n // 2 * bd, n // 2 * bd)
              for s in range(2)]
    return jnp.stack(halves).astype(BF16)


def kernel(x_prompt, x_sample, cache_k, cache_v, page_table, state_lru_conv, state_lru_h, state_ffn_conv, c_prompt, c_sample, g_norm1, g_norm2, w_ada, b_ada, w_in, g_q, g_k, lam_q1, lam_k1, lam_q2, lam_k2, g_subln, w_out, conv_lru_w, conv_lru_b, w_rgate, b_rgate, w_igate, b_igate, lru_lambda, w_up, conv_ffn_w, conv_ffn_b, w_down):
    depth = w_in.shape[0]
    assert depth == 1, "single-layer step"
    bsz, s_len, _ = x_prompt.shape
    nb, nt, _ = x_sample.shape
    n_pages, page = page_table.shape[1], cache_k.shape[2]
    past_len = n_pages * page
    d_ff = w_down.shape[1]

    w_in_b = w_in[0].astype(BF16)
    w_out_b = w_out[0].astype(BF16)
    w_up_b = w_up[0].astype(BF16)
    w_dn_b = w_down[0].astype(BF16)
    wr = _block_diag_halves(w_rgate[0])
    wi = _block_diag_halves(w_igate[0])
    br = b_rgate[0].reshape(1, LRU_WIDTH)
    bi = b_igate[0].reshape(1, LRU_WIDTH)
    n_grp = ATT_WIDTH // QK_SUB_DIM
    gq = jnp.tile(g_q[0], n_grp)[None, :]
    gk = jnp.tile(g_k[0], n_grp)[None, :]
    grp = jnp.arange(ATT_WIDTH) // QK_SUB_DIM
    gmat = jnp.where(grp[:, None] == grp[None, :], 1.0 / QK_SUB_DIM, 0.0).astype(BF16)
    g1, g2 = g_norm1, g_norm2
    cw, cb = conv_lru_w[0], conv_lru_b
    cfw, cfb = conv_ffn_w[0], conv_ffn_b
    lam = lru_lambda
    lams = (lam_q1, lam_k1, lam_q2, lam_k2)
    gs = g_subln

    n_c = bsz + nb
    n_pad = -n_c % SUBLANES
    c_all = jnp.concatenate([c_prompt, c_sample, jnp.zeros((n_pad, D_MODEL), F32)], axis=0)
    mod = _ada(c_all, w_ada[0], b_ada)
    mod3 = mod[:bsz].reshape(bsz * 6, 1, D_MODEL)
    mod_s = mod[bsz:bsz + nb]

    tabs = _rope_tables(s_len)
    pad_rows = SUBLANES - nt
    tabs_s = tuple(t[past_len:past_len + SUBLANES] for t in tabs)

    q_p, kf_p, kb_p, vf_p, vb_p, lo_p, lc_p, lh_p = _inproj_prompt(
        x_prompt, mod3, g1, w_in_b, gq, gk, gmat, tabs, cw, cb, wr, br, wi, bi, lam)
    at_p = _attn_prompt(q_p, kb_p, vb_p, *lams, gs)
    y_p, fc_p = _ffn_prompt(x_prompt, at_p, lo_p, mod3, g2, w_out_b, w_up_b, cfw, cfb, w_dn_b)

    x_tb = x_sample.transpose(1, 0, 2).reshape(nt * nb, D_MODEL)
    st_lru = state_lru_conv[0].transpose(1, 0, 2)
    q_s, k_s, v_s, lo_s, lc_s, lh_s = _inproj_sample(
        x_tb, mod_s, g1, w_in_b, gq, gk, gmat, tabs_s, cw, cb, wr, br, wi, bi, lam, st_lru, state_lru_h[0])
    to_bt = lambda a: a.reshape(nt, nb, -1).transpose(1, 0, 2)
    pad_t = lambda a: jnp.pad(a, ((0, 0), (0, pad_rows), (0, 0)))
    k_bt, v_bt = to_bt(k_s), to_bt(v_s)
    cache_kt = cache_k[0].transpose(0, 2, 3, 4, 1).reshape(-1, ATT_WIDTH, page)
    at_s = _attn_sample(page_table, pad_t(to_bt(q_s)), pad_t(k_bt), pad_t(v_bt), cache_kt,
                        cache_v[0].reshape(-1, page * ATT_HEADS, V_HEAD_DIM), *lams, gs)
    at_tb = at_s[:, :nt].transpose(1, 0, 2).reshape(nt * nb, ATT_WIDTH)
    y_tb, fc_s = _ffn_sample(x_tb, at_tb, lo_s, mod_s, g2, w_out_b, w_up_b, cfw, cfb, w_dn_b,
                             state_ffn_conv[0].transpose(1, 0, 2))

    hd = (ATT_HEADS, 2, QK_SUB_DIM)
    return (y_p, to_bt(y_tb),
            kf_p.reshape(bsz, *hd, s_len).transpose(0, 4, 1, 2, 3)[None], vf_p[None],
            lc_p[None], lh_p.reshape(1, bsz, LRU_WIDTH), fc_p[None],
            k_bt.reshape(1, nb, nt, *hd), v_bt.reshape(1, nb, nt, ATT_HEADS, V_HEAD_DIM),
            lc_s.transpose(1, 0, 2)[None], lh_s[None], fc_s.transpose(1, 0, 2)[None])
```

```python
import functools
import math

import jax
import jax.numpy as jnp
from jax import lax
from jax.experimental import pallas as pl
from jax.experimental.pallas import tpu as pltpu

F32 = jnp.float32
BF16 = jnp.bfloat16

D_MODEL = 1024
ATT_WIDTH = 512
LRU_WIDTH = 512
ATT_HEADS = 4
V_HEAD_DIM = 128
QK_SUB_DIM = 64
ROPE_DIM = 16
ROPE_THETA = 500000.0
LRU_BLOCKS = 8
LRU_C = 8.0
LRU_CONV = 4
FFN_CONV = 3
EPS = 1e-6
LAM_INIT = 0.8 - 0.6 * math.exp(-0.3 * 0)
QK_SCALE = QK_SUB_DIM ** -0.5
QK_SCALE_LOG2 = QK_SCALE * math.log2(math.e)
SCORE_LOG2_LIMIT = 64.0
QK_NORM_BOUND = 1.01 * math.sqrt(QK_SUB_DIM)

LANES = 128
SUBLANES = 8
VMEM_LIMIT = 56 * 1024 * 1024

TM_IN = 512
TQ = 512
TK = 512
TM_FFN = 256
FF_CHUNK = 256
NEG = -1e30


def _dot(a, b):
    return jnp.dot(a, b, preferred_element_type=F32)


def _dot_nt(a, b):
    return lax.dot_general(a, b, (((1,), (1,)), ((), ())), preferred_element_type=F32)


def _const_spec(shape):
    nd = len(shape)
    return pl.BlockSpec(shape, lambda *_: (0,) * nd, pipeline_mode=pl.Buffered(1))


def _modulated_norm(x, g, sc, sh):
    xn = x * lax.rsqrt(jnp.mean(x * x, axis=-1, keepdims=True) + EPS) * g
    return xn * (1.0 + sc) + sh


def _group_norm_rope(t, g_tiled, gmat, cos, sin_a, sin_b):
    ms = _dot((t * t).astype(BF16), gmat)
    tn = t * lax.rsqrt(ms + EPS) * g_tiled
    outs = []
    for h in range(ATT_WIDTH // LANES):
        th = tn[:, h * LANES:(h + 1) * LANES]
        outs.append(th * cos + pltpu.roll(th, LANES - 8, 1) * sin_a + pltpu.roll(th, 8, 1) * sin_b)
    return jnp.concatenate(outs, axis=1)


def _lru_gates(xc, wr_ref, br, wi_ref, bi, neg_c_softplus):
    half = LRU_WIDTH // 2
    xb = xc.astype(BF16)
    lo, hi = xb[:, :half], xb[:, half:]
    r = jax.nn.sigmoid(jnp.concatenate([_dot(lo, wr_ref[0]), _dot(hi, wr_ref[1])], axis=1) + br)
    ig = jax.nn.sigmoid(jnp.concatenate([_dot(lo, wi_ref[0]), _dot(hi, wi_ref[1])], axis=1) + bi)
    log_a = neg_c_softplus * r
    a = jnp.exp(log_a)
    one_minus_a2 = -jnp.tanh(log_a) * (a * a + 1.0)
    return a, jnp.sqrt(one_minus_a2) * (ig * xc)


def _neg_c_softplus(lam):
    z = -lam
    return -LRU_C * (jnp.maximum(z, 0.0) + jnp.log1p(jnp.exp(-jnp.abs(z))))


def _diff_lambda(lq1, lk1, lq2, lk2):
    s1 = jnp.sum(lq1 * lk1, axis=-1, keepdims=True)
    s2 = jnp.sum(lq2 * lk2, axis=-1, keepdims=True)
    return jnp.exp(s1) - jnp.exp(s2) + LAM_INIT


def _subln(o, g):
    return o * lax.rsqrt(jnp.mean(o * o, axis=-1, keepdims=True) + EPS) * g * (1.0 - LAM_INIT)


def _ada_kernel(c_ref, w_ref, b_ref, o_ref):
    c = c_ref[...]
    s = (c * jax.nn.sigmoid(c)).astype(BF16)
    o_ref[...] = _dot(s, w_ref[...].astype(BF16)) + b_ref[...]


def _ada(c_all, w_ada, b_ada):
    m = c_all.shape[0]
    n = w_ada.shape[1]
    tn = 1024
    return pl.pallas_call(
        _ada_kernel,
        grid=(n // tn,),
        in_specs=[pl.BlockSpec((m, D_MODEL), lambda j: (0, 0)),
                  pl.BlockSpec((D_MODEL, tn), lambda j: (0, j)),
                  pl.BlockSpec((1, tn), lambda j: (0, j))],
        out_specs=pl.BlockSpec((m, tn), lambda j: (0, j)),
        out_shape=jax.ShapeDtypeStruct((m, n), F32),
        compiler_params=pltpu.CompilerParams(dimension_semantics=("arbitrary",), vmem_limit_bytes=VMEM_LIMIT),
        name="ada_mod",
    )(c_all, w_ada, b_ada)


def _rope_table_kernel(freq_ref, ma_ref, mb_ref, c_ref, sa_ref, sb_ref):
    tm = c_ref.shape[0]
    pos = (pl.program_id(0) * tm + lax.broadcasted_iota(jnp.int32, (tm, LANES), 0)).astype(F32)
    ang = pos * freq_ref[...]
    s = jnp.sin(ang)
    c_ref[...] = jnp.cos(ang)
    sa_ref[...] = -s * ma_ref[...]
    sb_ref[...] = s * mb_ref[...]


def _rope_tables(n_pos):
    half = ROPE_DIM // 2
    freqs = ROPE_THETA ** (-jnp.arange(half, dtype=F32) * 2.0 / ROPE_DIM)
    d = jnp.arange(LANES) % QK_SUB_DIM
    freq_lane = jnp.where(d < ROPE_DIM, freqs[d % half], 0.0).astype(F32)[None, :]
    mask_a = (d < half).astype(F32)[None, :]
    mask_b = ((d >= half) & (d < ROPE_DIM)).astype(F32)[None, :]
    tm = 512
    row = pl.BlockSpec((1, LANES), lambda i: (0, 0))
    tab = pl.BlockSpec((tm, LANES), lambda i: (i, 0))
    shp = jax.ShapeDtypeStruct((n_pos, LANES), F32)
    return pl.pallas_call(
        _rope_table_kernel,
        grid=(n_pos // tm,),
        in_specs=[row, row, row],
        out_specs=[tab, tab, tab],
        out_shape=[shp, shp, shp],
        compiler_params=pltpu.CompilerParams(dimension_semantics=("arbitrary",)),
        name="rope_tables",
    )(freq_lane, mask_a, mask_b)


def _inproj_prompt_kernel(x_ref, sh_ref, sc_ref, g1_ref, win_ref, gq_ref, gk_ref, gmat_ref,
                          cos_ref, sa_ref, sb_ref, cw_ref, cb_ref, wr_ref, br_ref, wi_ref, bi_ref, lam_ref,
                          q_ref, kf_ref, kb_ref, vf_ref, vb_ref, lo_ref, lc_ref, lh_ref,
                          buf_ref, hcar_ref, acum_ref, bcum_ref, hs_ref):
    tm = x_ref.shape[0]

    @pl.when(pl.program_id(1) == 0)
    def _():
        buf_ref[0:SUBLANES, :] = jnp.zeros((SUBLANES, LRU_WIDTH), F32)
        hcar_ref[...] = jnp.zeros((SUBLANES, LRU_WIDTH), F32)

    u = _modulated_norm(x_ref[...], g1_ref[...], sc_ref[...], sh_ref[...]).astype(BF16)
    cos, sa, sb = cos_ref[...], sa_ref[...], sb_ref[...]
    gmat = gmat_ref[...]

    q = _group_norm_rope(_dot(u, win_ref[:, 0:ATT_WIDTH]), gq_ref[...], gmat, cos, sa, sb)
    q_ref[...] = (q * QK_SCALE_LOG2).T.astype(BF16)
    k = _group_norm_rope(_dot(u, win_ref[:, ATT_WIDTH:2 * ATT_WIDTH]), gk_ref[...], gmat, cos, sa, sb)
    kf_ref[...] = k.T
    kb_ref[...] = k.astype(BF16)
    v = _dot(u, win_ref[:, 2 * ATT_WIDTH:3 * ATT_WIDTH])
    for h in range(ATT_HEADS):
        vf_ref[:, h, :] = v[:, h * V_HEAD_DIM:(h + 1) * V_HEAD_DIM]
    vb_ref[...] = v.T.astype(BF16)

    c0 = 3 * ATT_WIDTH
    lx = _dot(u, win_ref[:, c0:c0 + LRU_WIDTH])
    buf_ref[SUBLANES:SUBLANES + tm, :] = lx
    cw = cw_ref[...]
    xc = buf_ref[SUBLANES - 3:SUBLANES - 3 + tm, :] * cw[0:1]
    xc = xc + buf_ref[SUBLANES - 2:SUBLANES - 2 + tm, :] * cw[1:2]
    xc = xc + buf_ref[SUBLANES - 1:SUBLANES - 1 + tm, :] * cw[2:3]
    xc = xc + lx * cw[3:4] + cb_ref[...]
    lc_ref[...] = buf_ref[SUBLANES + tm - (LRU_CONV - 1):SUBLANES + tm, :]
    buf_ref[0:SUBLANES, :] = buf_ref[tm:tm + SUBLANES, :]

    a, gx = _lru_gates(xc, wr_ref, br_ref[...], wi_ref, bi_ref[...], _neg_c_softplus(lam_ref[...]))

    row_in_group = lax.broadcasted_iota(jnp.int32, (tm, LRU_WIDTH), 0) & (SUBLANES - 1)
    for d in (1, 2, 4):
        keep = row_in_group >= d
        a_prev = jnp.where(keep, pltpu.roll(a, d, 0), 1.0)
        g_prev = jnp.where(keep, pltpu.roll(gx, d, 0), 0.0)
        gx = a * g_prev + gx
        a = a * a_prev
    acum_ref[...] = a
    bcum_ref[...] = gx

    def group_step(j, h):
        r0 = pl.multiple_of(j * SUBLANES, SUBLANES)
        hb = acum_ref[pl.ds(r0, SUBLANES), :] * h + bcum_ref[pl.ds(r0, SUBLANES), :]
        hs_ref[pl.ds(r0, SUBLANES), :] = hb
        return jnp.broadcast_to(hb[SUBLANES - 1:SUBLANES, :], (SUBLANES, LRU_WIDTH))

    h = lax.fori_loop(0, tm // SUBLANES, group_step, hcar_ref[...], unroll=8)
    hcar_ref[...] = h
    lh_ref[...] = h[0:1, :]

    lg = _dot(u, win_ref[:, c0 + LRU_WIDTH:c0 + 2 * LRU_WIDTH])
    lo_ref[...] = (hs_ref[...] * jax.nn.gelu(lg, approximate=True)).astype(BF16)


def _inproj_prompt(x, mod3, g1, w_in, gq, gk, gmat, tabs, cw, cb, wr, br, wi, bi, lam):
    bsz, s_len, _ = x.shape
    tm = TM_IN
    row_spec = lambda w: pl.BlockSpec((None, tm, w), lambda b, i: (b, i, 0))
    tab_spec = pl.BlockSpec((tm, LANES), lambda b, i: (i, 0))
    mod_spec = lambda k: pl.BlockSpec((None, 1, D_MODEL), lambda b, i: (b * 6 + k, 0, 0))
    act = lambda dt: jax.ShapeDtypeStruct((bsz, s_len, ATT_WIDTH), dt)
    act_t = lambda dt: jax.ShapeDtypeStruct((bsz, ATT_WIDTH, s_len), dt)
    col_spec = pl.BlockSpec((None, ATT_WIDTH, tm), lambda b, i: (b, 0, i))
    return pl.pallas_call(
        _inproj_prompt_kernel,
        grid=(bsz, s_len // tm),
        in_specs=[row_spec(D_MODEL), mod_spec(0), mod_spec(1), _const_spec(g1.shape), _const_spec(w_in.shape),
                  _const_spec(gq.shape), _const_spec(gk.shape), _const_spec(gmat.shape),
                  tab_spec, tab_spec, tab_spec,
                  _const_spec(cw.shape), _const_spec(cb.shape), _const_spec(wr.shape), _const_spec(br.shape),
                  _const_spec(wi.shape), _const_spec(bi.shape), _const_spec(lam.shape)],
        out_specs=[col_spec, col_spec, row_spec(ATT_WIDTH),
                   pl.BlockSpec((None, tm, ATT_HEADS, V_HEAD_DIM), lambda b, i: (b, i, 0, 0)),
                   col_spec, row_spec(LRU_WIDTH),
                   pl.BlockSpec((None, LRU_CONV - 1, LRU_WIDTH), lambda b, i: (b, 0, 0)),
                   pl.BlockSpec((None, 1, LRU_WIDTH), lambda b, i: (b, 0, 0))],
        out_shape=[act_t(BF16), act_t(F32), act(BF16),
                   jax.ShapeDtypeStruct((bsz, s_len, ATT_HEADS, V_HEAD_DIM), F32), act_t(BF16), act(BF16),
                   jax.ShapeDtypeStruct((bsz, LRU_CONV - 1, LRU_WIDTH), F32),
                   jax.ShapeDtypeStruct((bsz, 1, LRU_WIDTH), F32)],
        scratch_shapes=[pltpu.VMEM((tm + SUBLANES, LRU_WIDTH), F32),
                        pltpu.VMEM((SUBLANES, LRU_WIDTH), F32),
                        pltpu.VMEM((tm, LRU_WIDTH), F32),
                        pltpu.VMEM((tm, LRU_WIDTH), F32),
                        pltpu.VMEM((tm, LRU_WIDTH), F32)],
        compiler_params=pltpu.CompilerParams(dimension_semantics=("arbitrary", "arbitrary"),
                                             vmem_limit_bytes=VMEM_LIMIT),
        name="inproj_prompt",
    )(x, mod3, mod3, g1, w_in, gq, gk, gmat, *tabs, cw, cb, wr, br, wi, bi, lam)


def _attn_prompt_kernel(bounded, qt_ref, k_ref, vt_ref, lq1_ref, lk1_ref, lq2_ref, lk2_ref, gs_ref, o_ref):
    tq = qt_ref.shape[1]
    qi = pl.program_id(2)
    qt = qt_ref[...].astype(F32)
    feat = lax.broadcasted_iota(jnp.int32, (V_HEAD_DIM, tq), 0)
    qc = (jnp.where(feat < QK_SUB_DIM, qt, 0.0).astype(BF16), jnp.where(feat >= QK_SUB_DIM, qt, 0.0).astype(BF16))

    def tile(j, carry, masked):
        k0 = pl.multiple_of(j * TK, TK)
        ks = k_ref[pl.ds(k0, TK), :]
        vts = vt_ref[:, pl.ds(k0, TK)]
        ss = [_dot(ks, qc[c]) for c in range(2)]
        out = []
        for c in range(2):
            m, l, acc = carry[c]
            s = ss[c]
            if masked:
                keys = lax.broadcasted_iota(jnp.int32, (TK, tq), 0)
                qs = lax.broadcasted_iota(jnp.int32, (TK, tq), 1)
                s = jnp.where(keys <= qs, s, NEG)
            if bounded:
                p = jnp.exp2(s)
                l = l + jnp.sum(p, axis=0, keepdims=True)
                acc = acc + _dot(vts, p.astype(BF16))
            else:
                m_new = jnp.maximum(m, jnp.max(s, axis=0, keepdims=True))
                alpha = jnp.exp2(m - m_new)
                p = jnp.exp2(s - m_new)
                l = alpha * l + jnp.sum(p, axis=0, keepdims=True)
                acc = alpha * acc + _dot(vts, p.astype(BF16))
                m = m_new
            out.append((m, l, acc))
        return tuple(out)

    init = tuple((jnp.full((1, tq), NEG, F32), jnp.zeros((1, tq), F32), jnp.zeros((V_HEAD_DIM, tq), F32))
                 for _ in range(2))
    carry = lax.fori_loop(0, qi, lambda j, c: tile(j, c, False), init)
    (_, l0, a0), (_, l1, a1) = tile(qi, carry, True)
    lam = _diff_lambda(lq1_ref[...], lk1_ref[...], lq2_ref[...], lk2_ref[...])
    ot = a0 / l0 - lam * (a1 / l1)
    o_ref[...] = _subln(ot.T, gs_ref[...]).astype(o_ref.dtype)


def _attn_prompt(bounded, qt, k, vt, lq1, lk1, lq2, lk2, gs):
    bsz, s_len, _ = k.shape
    qt_spec = pl.BlockSpec((None, V_HEAD_DIM, TQ), lambda b, h, i: (b, h, i))
    k_spec = pl.BlockSpec((None, s_len, V_HEAD_DIM), lambda b, h, i: (b, 0, h))
    vt_spec = pl.BlockSpec((None, V_HEAD_DIM, s_len), lambda b, h, i: (b, h, 0))
    o_spec = pl.BlockSpec((None, TQ, V_HEAD_DIM), lambda b, h, i: (b, i, h))
    small = lambda a: pl.BlockSpec(a.shape, lambda b, h, i: (0, 0))
    return pl.pallas_call(
        functools.partial(_attn_prompt_kernel, bounded),
        grid=(bsz, ATT_HEADS, s_len // TQ),
        in_specs=[qt_spec, k_spec, vt_spec, small(lq1), small(lk1), small(lq2), small(lk2), small(gs)],
        out_specs=o_spec,
        out_shape=jax.ShapeDtypeStruct((bsz, s_len, ATT_WIDTH), BF16),
        compiler_params=pltpu.CompilerParams(dimension_semantics=("arbitrary", "arbitrary", "arbitrary"),
                                             vmem_limit_bytes=VMEM_LIMIT),
        name="attn_prompt_bounded" if bounded else "attn_prompt_general",
    )(qt, k, vt, lq1, lk1, lq2, lk2, gs)


def _ffn_prompt_kernel(x_ref, at_ref, lo_ref, gt1_ref, sh2_ref, sc2_ref, gt2_ref, g2_ref,
                       wout_ref, wup_ref, cfw_ref, cfb_ref, wdn_ref,
                       y_ref, fc_ref, upbuf_ref):
    tm = x_ref.shape[0]
    d_ff = wdn_ref.shape[0]

    @pl.when(pl.program_id(1) == 0)
    def _():
        upbuf_ref[0:SUBLANES, :] = jnp.zeros((SUBLANES, 2 * d_ff), F32)

    mix = jnp.concatenate([at_ref[...], lo_ref[...]], axis=1)
    x1 = x_ref[...] + gt1_ref[...] * _dot(mix, wout_ref[...])
    u2 = _modulated_norm(x1, g2_ref[...], sc2_ref[...], sh2_ref[...]).astype(BF16)

    n_chunks = d_ff // FF_CHUNK
    chunk_cols = lambda j: [slice(base + j * FF_CHUNK, base + (j + 1) * FF_CHUNK) for base in (0, d_ff)]

    def up_project(j):
        ups = []
        for cols in chunk_cols(j):
            up = _dot(u2, wup_ref[:, cols])
            upbuf_ref[SUBLANES:SUBLANES + tm, cols] = up
            ups.append(up)
        return ups

    def hidden(j, ups):
        halves = []
        for cols, up in zip(chunk_cols(j), ups):
            w = cfw_ref[:, cols]
            hc = upbuf_ref[SUBLANES - 2:SUBLANES - 2 + tm, cols] * w[0:1]
            hc = hc + upbuf_ref[SUBLANES - 1:SUBLANES - 1 + tm, cols] * w[1:2]
            halves.append(hc + up * w[2:3] + cfb_ref[:, cols])
        g, val = halves
        return (g * jax.nn.sigmoid(g) * val).astype(BF16)

    acc = jnp.zeros((tm, D_MODEL), F32)
    ups = up_project(0)
    for j in range(n_chunks):
        ups_next = up_project(j + 1) if j + 1 < n_chunks else None
        acc = acc + _dot(hidden(j, ups), wdn_ref[j * FF_CHUNK:(j + 1) * FF_CHUNK, :])
        ups = ups_next
    y_ref[...] = x1 + gt2_ref[...] * acc
    fc_ref[...] = upbuf_ref[SUBLANES + tm - (FFN_CONV - 1):SUBLANES + tm, :]
    upbuf_ref[0:SUBLANES, :] = upbuf_ref[tm:tm + SUBLANES, :]


def _ffn_prompt(x, attn, lru, mod3, g2, w_out, w_up, cfw, cfb, w_dn):
    bsz, s_len, _ = x.shape
    tm = TM_FFN
    d_ff = w_dn.shape[0]
    row_spec = lambda w: pl.BlockSpec((None, tm, w), lambda b, i: (b, i, 0))
    mod_spec = lambda k: pl.BlockSpec((None, 1, D_MODEL), lambda b, i: (b * 6 + k, 0, 0))
    return pl.pallas_call(
        _ffn_prompt_kernel,
        grid=(bsz, s_len // tm),
        in_specs=[row_spec(D_MODEL), row_spec(ATT_WIDTH), row_spec(LRU_WIDTH),
                  mod_spec(2), mod_spec(3), mod_spec(4), mod_spec(5), _const_spec(g2.shape),
                  _const_spec(w_out.shape), _const_spec(w_up.shape), _const_spec(cfw.shape),
                  _const_spec(cfb.shape), _const_spec(w_dn.shape)],
        out_specs=[row_spec(D_MODEL), pl.BlockSpec((None, FFN_CONV - 1, 2 * d_ff), lambda b, i: (b, 0, 0))],
        out_shape=[jax.ShapeDtypeStruct((bsz, s_len, D_MODEL), F32),
                   jax.ShapeDtypeStruct((bsz, FFN_CONV - 1, 2 * d_ff), F32)],
        scratch_shapes=[pltpu.VMEM((tm + SUBLANES, 2 * d_ff), F32)],
        compiler_params=pltpu.CompilerParams(dimension_semantics=("arbitrary", "arbitrary"),
                                             vmem_limit_bytes=VMEM_LIMIT),
        name="ffn_prompt",
    )(x, attn, lru, mod3, mod3, mod3, mod3, g2, w_out, w_up, cfw, cfb, w_dn)


def _inproj_sample_kernel(x_ref, mod_ref, g1_ref, win_ref, gq_ref, gk_ref, gmat_ref,
                          cos_ref, sa_ref, sb_ref, cw_ref, cb_ref, wr_ref, br_ref, wi_ref, bi_ref, lam_ref,
                          st_ref, h0_ref,
                          q_ref, k_ref, v_ref, lo_ref, lc_ref, lh_ref):
    nb = h0_ref.shape[0]
    nt = x_ref.shape[0] // nb
    rep = lambda a: jnp.concatenate([a] * nt, axis=0)
    sh1 = rep(mod_ref[:, 0:D_MODEL])
    sc1 = rep(mod_ref[:, D_MODEL:2 * D_MODEL])
    u = _modulated_norm(x_ref[...], g1_ref[...], sc1, sh1).astype(BF16)
    per_t = lambda r: jnp.concatenate([jnp.broadcast_to(r[t:t + 1, :], (nb, LANES)) for t in range(nt)], axis=0)
    cos, sa, sb = per_t(cos_ref[...]), per_t(sa_ref[...]), per_t(sb_ref[...])
    gmat = gmat_ref[...]

    q = _group_norm_rope(_dot(u, win_ref[:, 0:ATT_WIDTH]), gq_ref[...], gmat, cos, sa, sb)
    q_ref[...] = q * QK_SCALE
    k_ref[...] = _group_norm_rope(_dot(u, win_ref[:, ATT_WIDTH:2 * ATT_WIDTH]), gk_ref[...], gmat, cos, sa, sb)
    v_ref[...] = _dot(u, win_ref[:, 2 * ATT_WIDTH:3 * ATT_WIDTH])

    c0 = 3 * ATT_WIDTH
    lx = _dot(u, win_ref[:, c0:c0 + LRU_WIDTH])
    lg = _dot(u, win_ref[:, c0 + LRU_WIDTH:c0 + 2 * LRU_WIDTH])
    pad = [st_ref[i] for i in range(LRU_CONV - 1)] + [lx[t * nb:(t + 1) * nb, :] for t in range(nt)]
    for i in range(LRU_CONV - 1):
        lc_ref[i] = pad[nt + i]
    cw = cw_ref[...]
    xcs = []
    for t in range(nt):
        xc = pad[t] * cw[0:1]
        for kk in range(1, LRU_CONV):
            xc = xc + pad[t + kk] * cw[kk:kk + 1]
        xcs.append(xc + cb_ref[...])
    a, gx = _lru_gates(jnp.concatenate(xcs, axis=0), wr_ref, br_ref[...], wi_ref, bi_ref[...],
                       _neg_c_softplus(lam_ref[...]))
    h = h0_ref[...]
    hs = []
    for t in range(nt):
        h = a[t * nb:(t + 1) * nb, :] * h + gx[t * nb:(t + 1) * nb, :]
        hs.append(h)
    lh_ref[...] = h
    lo_ref[...] = (jnp.concatenate(hs, axis=0) * jax.nn.gelu(lg, approximate=True)).astype(BF16)


def _inproj_sample(x_tb, mod_s, g1, w_in, gq, gk, gmat, tabs_s, cw, cb, wr, br, wi, bi, lam, st, h0):
    m = x_tb.shape[0]
    nb = h0.shape[0]
    act = jax.ShapeDtypeStruct((m, ATT_WIDTH), F32)
    return pl.pallas_call(
        _inproj_sample_kernel,
        out_shape=[act, act, act, jax.ShapeDtypeStruct((m, LRU_WIDTH), BF16),
                   jax.ShapeDtypeStruct((LRU_CONV - 1, nb, LRU_WIDTH), F32),
                   jax.ShapeDtypeStruct((nb, LRU_WIDTH), F32)],
        compiler_params=pltpu.CompilerParams(vmem_limit_bytes=VMEM_LIMIT),
        name="inproj_sample",
    )(x_tb, mod_s, g1, w_in, gq, gk, gmat, *tabs_s, cw, cb, wr, br, wi, bi, lam, st, h0)


def _attn_sample_kernel(n_pages, pt_ref, q_ref, kn_ref, vn_ref, lq1_ref, lk1_ref, lq2_ref, lk2_ref, gs_ref, *refs):
    kt_refs = refs[:n_pages]
    v_refs = refs[n_pages:2 * n_pages]
    o_ref = refs[2 * n_pages]
    page = kt_refs[0].shape[1]
    nq = q_ref.shape[0]
    hrows = 2 * nq

    lane = lax.broadcasted_iota(jnp.int32, (nq, V_HEAD_DIM), 1)
    qh = []
    for h in range(ATT_HEADS):
        q8 = q_ref[:, h * V_HEAD_DIM:(h + 1) * V_HEAD_DIM]
        qh.append(jnp.concatenate([jnp.where(lane < QK_SUB_DIM, q8, 0.0), jnp.where(lane >= QK_SUB_DIM, q8, 0.0)],
                                  axis=0).astype(BF16))

    s_tiles = []
    for j in range(n_pages):
        kt = kt_refs[j][...].astype(BF16)
        s_tiles.append(jnp.concatenate(
            [_dot(qh[h], kt[h * V_HEAD_DIM:(h + 1) * V_HEAD_DIM, :]) for h in range(ATT_HEADS)], axis=0))
    zpad = jnp.zeros((page - nq, V_HEAD_DIM), F32)
    k_new = [jnp.concatenate([kn_ref[:, h * V_HEAD_DIM:(h + 1) * V_HEAD_DIM], zpad], axis=0).astype(BF16)
             for h in range(ATT_HEADS)]
    v_new = [jnp.concatenate([vn_ref[:, h * V_HEAD_DIM:(h + 1) * V_HEAD_DIM], zpad], axis=0).astype(BF16)
             for h in range(ATT_HEADS)]
    nrow = ATT_HEADS * hrows
    rt = lax.broadcasted_iota(jnp.int32, (nrow, page), 0) % nq
    ct = lax.broadcasted_iota(jnp.int32, (nrow, page), 1)
    s_new = jnp.concatenate([_dot_nt(qh[h], k_new[h]) for h in range(ATT_HEADS)], axis=0)
    s_tiles.append(jnp.where(ct <= rt, s_new, NEG))

    m = s_tiles[0]
    for s in s_tiles[1:]:
        m = jnp.maximum(m, s)
    m = jnp.max(m, axis=1, keepdims=True)
    l = jnp.zeros((nrow, 1), F32)
    acc = [jnp.zeros((hrows, V_HEAD_DIM), F32) for _ in range(ATT_HEADS)]
    for j, s in enumerate(s_tiles):
        p = jnp.exp(s - m)
        l = l + jnp.sum(p, axis=1, keepdims=True)
        pb = p.astype(BF16)
        for h in range(ATT_HEADS):
            vj = v_refs[j][pl.ds(h, page, stride=ATT_HEADS), :].astype(BF16) if j < n_pages else v_new[h]
            acc[h] = acc[h] + _dot(pb[h * hrows:(h + 1) * hrows, :], vj)
    lam = _diff_lambda(lq1_ref[...], lk1_ref[...], lq2_ref[...], lk2_ref[...])
    gs = gs_ref[...]
    outs = []
    for h in range(ATT_HEADS):
        o = acc[h] / l[h * hrows:(h + 1) * hrows, :]
        outs.append(_subln(o[0:nq, :] - lam * o[nq:hrows, :], gs))
    o_ref[...] = jnp.concatenate(outs, axis=1)


def _attn_sample(page_table, q8, kn8, vn8, cache_kt, cache_v, lq1, lk1, lq2, lk2, gs):
    nb, n_pages = page_table.shape
    nq = q8.shape[1]
    page = cache_kt.shape[2]
    new_spec = pl.BlockSpec((None, nq, ATT_WIDTH), lambda b, pt: (b, 0, 0))
    small = lambda a: pl.BlockSpec(a.shape, lambda b, pt: (0, 0))
    kt_spec = lambda j: pl.BlockSpec((None, ATT_WIDTH, page), lambda b, pt: (pt[b, j], 0, 0))
    v_spec = lambda j: pl.BlockSpec((None, page * ATT_HEADS, V_HEAD_DIM), lambda b, pt: (pt[b, j], 0, 0))
    grid_spec = pltpu.PrefetchScalarGridSpec(
        num_scalar_prefetch=1,
        grid=(nb,),
        in_specs=[new_spec, new_spec, new_spec, small(lq1), small(lk1), small(lq2), small(lk2), small(gs)]
        + [kt_spec(j) for j in range(n_pages)] + [v_spec(j) for j in range(n_pages)],
        out_specs=new_spec,
    )
    return pl.pallas_call(
        functools.partial(_attn_sample_kernel, n_pages),
        grid_spec=grid_spec,
        out_shape=jax.ShapeDtypeStruct((nb, nq, ATT_WIDTH), F32),
        compiler_params=pltpu.CompilerParams(dimension_semantics=("arbitrary",), vmem_limit_bytes=VMEM_LIMIT),
        name="attn_sample",
    )(page_table, q8, kn8, vn8, lq1, lk1, lq2, lk2, gs, *([cache_kt] * n_pages), *([cache_v] * n_pages))


def _ffn_sample_kernel(x_ref, at_ref, lo_ref, mod_ref, g2_ref, wout_ref, wup_ref, cfw_ref, cfb_ref, wdn_ref, st_ref,
                       y_ref, fc_ref):
    nb = st_ref.shape[1]
    nt = x_ref.shape[0] // nb
    d_ff = wdn_ref.shape[0]
    rep = lambda a: jnp.concatenate([a] * nt, axis=0)
    gt1 = rep(mod_ref[:, 2 * D_MODEL:3 * D_MODEL])
    sh2 = rep(mod_ref[:, 3 * D_MODEL:4 * D_MODEL])
    sc2 = rep(mod_ref[:, 4 * D_MODEL:5 * D_MODEL])
    gt2 = rep(mod_ref[:, 5 * D_MODEL:6 * D_MODEL])

    mix = jnp.concatenate([at_ref[...].astype(BF16), lo_ref[...]], axis=1)
    x1 = x_ref[...] + gt1 * _dot(mix, wout_ref[...])
    u2 = _modulated_norm(x1, g2_ref[...], sc2, sh2).astype(BF16)

    acc = jnp.zeros((nt * nb, D_MODEL), F32)
    for j in range(d_ff // FF_CHUNK):
        halves = []
        for base in (0, d_ff):
            c0 = base + j * FF_CHUNK
            cols = slice(c0, c0 + FF_CHUNK)
            up = _dot(u2, wup_ref[:, cols])
            pad = [st_ref[i, :, cols] for i in range(FFN_CONV - 1)] + [up[t * nb:(t + 1) * nb, :] for t in range(nt)]
            for i in range(FFN_CONV - 1):
                fc_ref[i, :, cols] = pad[nt + i]
            w = cfw_ref[:, cols]
            b = cfb_ref[:, cols]
            hcs = []
            for t in range(nt):
                hc = pad[t] * w[0:1]
                for kk in range(1, FFN_CONV):
                    hc = hc + pad[t + kk] * w[kk:kk + 1]
                hcs.append(hc + b)
            halves.append(jnp.concatenate(hcs, axis=0))
        g, val = halves
        hmid = (g * jax.nn.sigmoid(g) * val).astype(BF16)
        acc = acc + _dot(hmid, wdn_ref[j * FF_CHUNK:(j + 1) * FF_CHUNK, :])
    y_ref[...] = x1 + gt2 * acc


def _ffn_sample(x_tb, attn_tb, lru_tb, mod_s, g2, w_out, w_up, cfw, cfb, w_dn, st):
    m = x_tb.shape[0]
    return pl.pallas_call(
        _ffn_sample_kernel,
        out_shape=[jax.ShapeDtypeStruct((m, D_MODEL), F32), jax.ShapeDtypeStruct(st.shape, F32)],
        compiler_params=pltpu.CompilerParams(vmem_limit_bytes=VMEM_LIMIT),
        name="ffn_sample",
    )(x_tb, attn_tb, lru_tb, mod_s, g2, w_out, w_up, cfw, cfb, w_dn, st)


def _block_diag_halves(w):
    n, bd, _ = w.shape
    eye = jnp.eye(n // 2, dtype=w.dtype)
    halves = [jnp.einsum('nij,nm->nimj', w[s * (n // 2):(s + 1) * (n // 2)], eye).reshape(n // 2 * bd, n // 2 * bd)
              for s in range(2)]
    return jnp.stack(halves).astype(BF16)


def kernel(x_prompt, x_sample, cache_k, cache_v, page_table, state_lru_conv, state_lru_h, state_ffn_conv, c_prompt, c_sample, g_norm1, g_norm2, w_ada, b_ada, w_in, g_q, g_k, lam_q1, lam_k1, lam_q2, lam_k2, g_subln, w_out, conv_lru_w, conv_lru_b, w_rgate, b_rgate, w_igate, b_igate, lru_lambda, w_up, conv_ffn_w, conv_ffn_b, w_down):
    depth = w_in.shape[0]
    assert depth == 1, "single-layer step"
    bsz, s_len, _ = x_prompt.shape
    nb, nt, _ = x_sample.shape
    n_pages, page = page_table.shape[1], cache_k.shape[2]
    past_len = n_pages * page
    d_ff = w_down.shape[1]

    w_in_b = w_in[0].astype(BF16)
    w_out_b = w_out[0].astype(BF16)
    w_up_b = w_up[0].astype(BF16)
    w_dn_b = w_down[0].astype(BF16)
    wr = _block_diag_halves(w_rgate[0])
    wi = _block_diag_halves(w_igate[0])
    br = b_rgate[0].reshape(1, LRU_WIDTH)
    bi = b_igate[0].reshape(1, LRU_WIDTH)
    n_grp = ATT_WIDTH // QK_SUB_DIM
    gq = jnp.tile(g_q[0], n_grp)[None, :]
    gk = jnp.tile(g_k[0], n_grp)[None, :]
    grp = jnp.arange(ATT_WIDTH) // QK_SUB_DIM
    gmat = jnp.where(grp[:, None] == grp[None, :], 1.0 / QK_SUB_DIM, 0.0).astype(BF16)
    g1, g2 = g_norm1, g_norm2
    cw, cb = conv_lru_w[0], conv_lru_b
    cfw, cfb = conv_ffn_w[0], conv_ffn_b
    lam = lru_lambda
    lams = (lam_q1, lam_k1, lam_q2, lam_k2)
    gs = g_subln

    n_c = bsz + nb
    n_pad = -n_c % SUBLANES
    c_all = jnp.concatenate([c_prompt, c_sample, jnp.zeros((n_pad, D_MODEL), F32)], axis=0)
    mod = _ada(c_all, w_ada[0], b_ada)
    mod3 = mod[:bsz].reshape(bsz * 6, 1, D_MODEL)
    mod_s = mod[bsz:bsz + nb]

    tabs = _rope_tables(s_len)
    pad_rows = SUBLANES - nt
    tabs_s = tuple(t[past_len:past_len + SUBLANES] for t in tabs)

    q_p, kf_p, kb_p, vf_p, vb_p, lo_p, lc_p, lh_p = _inproj_prompt(
        x_prompt, mod3, g1, w_in_b, gq, gk, gmat, tabs, cw, cb, wr, br, wi, bi, lam)
    score_bound = QK_NORM_BOUND ** 2 * QK_SCALE_LOG2 * jnp.max(jnp.abs(g_q)) * jnp.max(jnp.abs(g_k))
    at_p = lax.cond(score_bound <= SCORE_LOG2_LIMIT,
                    functools.partial(_attn_prompt, True), functools.partial(_attn_prompt, False),
                    q_p, kb_p, vb_p, *lams, gs)
    y_p, fc_p = _ffn_prompt(x_prompt, at_p, lo_p, mod3, g2, w_out_b, w_up_b, cfw, cfb, w_dn_b)

    x_tb = x_sample.transpose(1, 0, 2).reshape(nt * nb, D_MODEL)
    st_lru = state_lru_conv[0].transpose(1, 0, 2)
    q_s, k_s, v_s, lo_s, lc_s, lh_s = _inproj_sample(
        x_tb, mod_s, g1, w_in_b, gq, gk, gmat, tabs_s, cw, cb, wr, br, wi, bi, lam, st_lru, state_lru_h[0])
    to_bt = lambda a: a.reshape(nt, nb, -1).transpose(1, 0, 2)
    pad_t = lambda a: jnp.pad(a, ((0, 0), (0, pad_rows), (0, 0)))
    k_bt, v_bt = to_bt(k_s), to_bt(v_s)
    cache_kt = cache_k[0].transpose(0, 2, 3, 4, 1).reshape(-1, ATT_WIDTH, page)
    at_s = _attn_sample(page_table, pad_t(to_bt(q_s)), pad_t(k_bt), pad_t(v_bt), cache_kt,
                        cache_v[0].reshape(-1, page * ATT_HEADS, V_HEAD_DIM), *lams, gs)
    at_tb = at_s[:, :nt].transpose(1, 0, 2).reshape(nt * nb, ATT_WIDTH)
    y_tb, fc_s = _ffn_sample(x_tb, at_tb, lo_s, mod_s, g2, w_out_b, w_up_b, cfw, cfb, w_dn_b,
                             state_ffn_conv[0].transpose(1, 0, 2))

    hd = (ATT_HEADS, 2, QK_SUB_DIM)
    return (y_p, to_bt(y_tb),
            kf_p.reshape(bsz, *hd, s_len).transpose(0, 4, 1, 2, 3)[None], vf_p[None],
            lc_p[None], lh_p.reshape(1, bsz, LRU_WIDTH), fc_p[None],
            k_bt.reshape(1, nb, nt, *hd), v_bt.reshape(1, nb, nt, ATT_HEADS, V_HEAD_DIM),
            lc_s.transpose(1, 0, 2)[None], lh_s[None], fc_s.transpose(1, 0, 2)[None])
```

```python
import functools
import math

import jax
import jax.numpy as jnp
from jax import lax
from jax.experimental import pallas as pl
from jax.experimental.pallas import tpu as pltpu

F32 = jnp.float32
BF16 = jnp.bfloat16

D_MODEL = 1024
ATT_WIDTH = 512
LRU_WIDTH = 512
ATT_HEADS = 4
V_HEAD_DIM = 128
QK_SUB_DIM = 64
ROPE_DIM = 16
ROPE_THETA = 500000.0
LRU_BLOCKS = 8
LRU_C = 8.0
LRU_CONV = 4
FFN_CONV = 3
EPS = 1e-6
LAM_INIT = 0.8 - 0.6 * math.exp(-0.3 * 0)
QK_SCALE = QK_SUB_DIM ** -0.5
QK_SCALE_LOG2 = QK_SCALE * math.log2(math.e)
SCORE_LOG2_LIMIT = 64.0
QK_NORM_BOUND = 1.01 * math.sqrt(QK_SUB_DIM)

LANES = 128
SUBLANES = 8
VMEM_LIMIT = 56 * 1024 * 1024

TM_IN = 512
TQ = 512
TK = 512
TM_FFN = 256
FF_CHUNK = 256
NEG = -1e30


def _dot(a, b):
    return jnp.dot(a, b, preferred_element_type=F32)


def _dot_nt(a, b):
    return lax.dot_general(a, b, (((1,), (1,)), ((), ())), preferred_element_type=F32)


def _const_spec(shape):
    nd = len(shape)
    return pl.BlockSpec(shape, lambda *_: (0,) * nd, pipeline_mode=pl.Buffered(1))


def _modulated_norm(x, g, sc, sh):
    xn = x * lax.rsqrt(jnp.mean(x * x, axis=-1, keepdims=True) + EPS) * g
    return xn * (1.0 + sc) + sh


def _to_segment_major(val, buf_ref):
    tm, w = val.shape
    n = tm // SUBLANES
    for c in range(w // LANES):
        for s in range(SUBLANES):
            for r0 in range(0, n, SUBLANES):
                t0 = s * n + r0
                buf_ref[c, pl.ds(r0 * SUBLANES + s, SUBLANES, stride=SUBLANES), :] = (
                    val[t0:t0 + SUBLANES, c * LANES:(c + 1) * LANES])
    return jnp.concatenate([buf_ref[c] for c in range(w // LANES)], axis=1)


def _to_time_major(val, buf_ref):
    tm, w = val.shape
    n = tm // SUBLANES
    for c in range(w // LANES):
        buf_ref[c] = val[:, c * LANES:(c + 1) * LANES]
    rows = []
    for s in range(SUBLANES):
        for r0 in range(0, n, SUBLANES):
            rows.append(jnp.concatenate(
                [buf_ref[c, pl.ds(r0 * SUBLANES + s, SUBLANES, stride=SUBLANES), :] for c in range(w // LANES)], axis=1))
    return jnp.concatenate(rows, axis=0)


def _tail_groups(cur, k):
    tm = cur.shape[0]
    return [cur[tm - (k - i) * SUBLANES:tm - (k - i - 1) * SUBLANES, :] for i in range(k)]


def _delayed(cur, prev_tail):
    k = len(prev_tail)
    tm = cur.shape[0]
    first = lax.broadcasted_iota(jnp.int32, prev_tail[0].shape, 0) == 0
    heads = [jnp.where(first, pltpu.roll(p, 1, 0), pltpu.roll(c, 1, 0))
             for p, c in zip(prev_tail, _tail_groups(cur, k))]
    return [jnp.concatenate(heads[k - d:] + [cur[:tm - d * SUBLANES, :]], axis=0) for d in range(k, 0, -1)]


def _group_norm_rope(t, g_tiled, gmat, cos, sin_a, sin_b):
    ms = _dot((t * t).astype(BF16), gmat)
    tn = t * lax.rsqrt(ms + EPS) * g_tiled
    outs = []
    for h in range(ATT_WIDTH // LANES):
        th = tn[:, h * LANES:(h + 1) * LANES]
        outs.append(th * cos + pltpu.roll(th, LANES - 8, 1) * sin_a + pltpu.roll(th, 8, 1) * sin_b)
    return jnp.concatenate(outs, axis=1)


def _lru_gates(xc, wr_ref, br, wi_ref, bi, neg_c_softplus):
    half = LRU_WIDTH // 2
    xb = xc.astype(BF16)
    lo, hi = xb[:, :half], xb[:, half:]
    r = jax.nn.sigmoid(jnp.concatenate([_dot(lo, wr_ref[0]), _dot(hi, wr_ref[1])], axis=1) + br)
    ig = jax.nn.sigmoid(jnp.concatenate([_dot(lo, wi_ref[0]), _dot(hi, wi_ref[1])], axis=1) + bi)
    log_a = neg_c_softplus * r
    a = jnp.exp(log_a)
    one_minus_a2 = -jnp.tanh(log_a) * (a * a + 1.0)
    return a, jnp.sqrt(one_minus_a2) * (ig * xc)


def _neg_c_softplus(lam):
    z = -lam
    return -LRU_C * (jnp.maximum(z, 0.0) + jnp.log1p(jnp.exp(-jnp.abs(z))))


def _diff_lambda(lq1, lk1, lq2, lk2):
    s1 = jnp.sum(lq1 * lk1, axis=-1, keepdims=True)
    s2 = jnp.sum(lq2 * lk2, axis=-1, keepdims=True)
    return jnp.exp(s1) - jnp.exp(s2) + LAM_INIT


def _subln(o, g):
    return o * lax.rsqrt(jnp.mean(o * o, axis=-1, keepdims=True) + EPS) * g * (1.0 - LAM_INIT)


def _ada_kernel(c_ref, w_ref, b_ref, o_ref):
    c = c_ref[...]
    s = (c * jax.nn.sigmoid(c)).astype(BF16)
    o_ref[...] = _dot(s, w_ref[...].astype(BF16)) + b_ref[...]


def _ada(c_all, w_ada, b_ada):
    m = c_all.shape[0]
    n = w_ada.shape[1]
    tn = 1024
    return pl.pallas_call(
        _ada_kernel,
        grid=(n // tn,),
        in_specs=[pl.BlockSpec((m, D_MODEL), lambda j: (0, 0)),
                  pl.BlockSpec((D_MODEL, tn), lambda j: (0, j)),
                  pl.BlockSpec((1, tn), lambda j: (0, j))],
        out_specs=pl.BlockSpec((m, tn), lambda j: (0, j)),
        out_shape=jax.ShapeDtypeStruct((m, n), F32),
        compiler_params=pltpu.CompilerParams(dimension_semantics=("arbitrary",), vmem_limit_bytes=VMEM_LIMIT),
        name="ada_mod",
    )(c_all, w_ada, b_ada)


def _rope_table_kernel(freq_ref, ma_ref, mb_ref, c_ref, sa_ref, sb_ref):
    tm = c_ref.shape[0]
    pos = (pl.program_id(0) * tm + lax.broadcasted_iota(jnp.int32, (tm, LANES), 0)).astype(F32)
    ang = pos * freq_ref[...]
    s = jnp.sin(ang)
    c_ref[...] = jnp.cos(ang)
    sa_ref[...] = -s * ma_ref[...]
    sb_ref[...] = s * mb_ref[...]


def _rope_tables(n_pos):
    half = ROPE_DIM // 2
    freqs = ROPE_THETA ** (-jnp.arange(half, dtype=F32) * 2.0 / ROPE_DIM)
    d = jnp.arange(LANES) % QK_SUB_DIM
    freq_lane = jnp.where(d < ROPE_DIM, freqs[d % half], 0.0).astype(F32)[None, :]
    mask_a = (d < half).astype(F32)[None, :]
    mask_b = ((d >= half) & (d < ROPE_DIM)).astype(F32)[None, :]
    tm = 512
    row = pl.BlockSpec((1, LANES), lambda i: (0, 0))
    tab = pl.BlockSpec((tm, LANES), lambda i: (i, 0))
    shp = jax.ShapeDtypeStruct((n_pos, LANES), F32)
    return pl.pallas_call(
        _rope_table_kernel,
        grid=(n_pos // tm,),
        in_specs=[row, row, row],
        out_specs=[tab, tab, tab],
        out_shape=[shp, shp, shp],
        compiler_params=pltpu.CompilerParams(dimension_semantics=("arbitrary",)),
        name="rope_tables",
    )(freq_lane, mask_a, mask_b)


def _inproj_prompt_kernel(x_ref, sh_ref, sc_ref, g1_ref, win_ref, gq_ref, gk_ref, gmat_ref,
                          cos_ref, sa_ref, sb_ref, cw_ref, cb_ref, wr_ref, br_ref, wi_ref, bi_ref, lam_ref,
                          q_ref, kf_ref, kb_ref, vf_ref, vb_ref, lo_ref, lc_ref, lh_ref,
                          lbuf_ref, hbuf_ref, tail_ref, hcar_ref):
    tm = x_ref.shape[0]

    @pl.when(pl.program_id(1) == 0)
    def _():
        tail_ref[...] = jnp.zeros(tail_ref.shape, F32)
        hcar_ref[...] = jnp.zeros(hcar_ref.shape, F32)

    u = _modulated_norm(x_ref[...], g1_ref[...], sc_ref[...], sh_ref[...]).astype(BF16)
    cos, sa, sb = cos_ref[...], sa_ref[...], sb_ref[...]
    gmat = gmat_ref[...]

    q = _group_norm_rope(_dot(u, win_ref[:, 0:ATT_WIDTH]), gq_ref[...], gmat, cos, sa, sb)
    q_ref[...] = (q * QK_SCALE_LOG2).T.astype(BF16)
    k = _group_norm_rope(_dot(u, win_ref[:, ATT_WIDTH:2 * ATT_WIDTH]), gk_ref[...], gmat, cos, sa, sb)
    kf_ref[...] = k.T
    kb_ref[...] = k.astype(BF16)
    v = _dot(u, win_ref[:, 2 * ATT_WIDTH:3 * ATT_WIDTH])
    for h in range(ATT_HEADS):
        vf_ref[:, h, :] = v[:, h * V_HEAD_DIM:(h + 1) * V_HEAD_DIM]
    vb_ref[...] = v.T.astype(BF16)

    c0 = 3 * ATT_WIDTH
    n_tail = LRU_CONV - 1
    lx = _to_segment_major(_dot(u, win_ref[:, c0:c0 + LRU_WIDTH]), lbuf_ref)
    cw = cw_ref[...]
    d3, d2, d1 = _delayed(lx, [tail_ref[i] for i in range(n_tail)])
    xc = d3 * cw[0:1] + d2 * cw[1:2] + d1 * cw[2:3] + lx * cw[3:4] + cb_ref[...]
    for i, grp in enumerate(_tail_groups(lx, n_tail)):
        tail_ref[i] = grp
        lc_ref[i:i + 1, :] = grp[SUBLANES - 1:SUBLANES, :]

    a, gx = _lru_gates(xc, wr_ref, br_ref[...], wi_ref, bi_ref[...], _neg_c_softplus(lam_ref[...]))

    n = tm // SUBLANES
    grp = lambda arr, r: arr[r * SUBLANES:(r + 1) * SUBLANES, :]
    ps, hs = [grp(a, 0)], [grp(gx, 0)]
    for r in range(1, n):
        ar = grp(a, r)
        ps.append(ar * ps[-1])
        hs.append(ar * hs[-1] + grp(gx, r))
    p_end, h_end = ps[-1], hs[-1]
    entering = [hcar_ref[...]]
    for s in range(1, SUBLANES):
        entering.append(p_end[s - 1:s, :] * entering[-1] + h_end[s - 1:s, :])
    h_last = p_end[SUBLANES - 1:SUBLANES, :] * entering[-1] + h_end[SUBLANES - 1:SUBLANES, :]
    hcar_ref[...] = h_last
    lh_ref[...] = h_last
    enter = jnp.concatenate(entering, axis=0)
    states = jnp.concatenate([hs[r] + ps[r] * enter for r in range(n)], axis=0)

    lg = _dot(u, win_ref[:, c0 + LRU_WIDTH:c0 + 2 * LRU_WIDTH])
    lo_ref[...] = (_to_time_major(states, hbuf_ref) * jax.nn.gelu(lg, approximate=True)).astype(BF16)


def _inproj_prompt(x, mod3, g1, w_in, gq, gk, gmat, tabs, cw, cb, wr, br, wi, bi, lam):
    bsz, s_len, _ = x.shape
    tm = TM_IN
    row_spec = lambda w: pl.BlockSpec((None, tm, w), lambda b, i: (b, i, 0))
    tab_spec = pl.BlockSpec((tm, LANES), lambda b, i: (i, 0))
    mod_spec = lambda k: pl.BlockSpec((None, 1, D_MODEL), lambda b, i: (b * 6 + k, 0, 0))
    act = lambda dt: jax.ShapeDtypeStruct((bsz, s_len, ATT_WIDTH), dt)
    act_t = lambda dt: jax.ShapeDtypeStruct((bsz, ATT_WIDTH, s_len), dt)
    col_spec = pl.BlockSpec((None, ATT_WIDTH, tm), lambda b, i: (b, 0, i))
    return pl.pallas_call(
        _inproj_prompt_kernel,
        grid=(bsz, s_len // tm),
        in_specs=[row_spec(D_MODEL), mod_spec(0), mod_spec(1), _const_spec(g1.shape), _const_spec(w_in.shape),
                  _const_spec(gq.shape), _const_spec(gk.shape), _const_spec(gmat.shape),
                  tab_spec, tab_spec, tab_spec,
                  _const_spec(cw.shape), _const_spec(cb.shape), _const_spec(wr.shape), _const_spec(br.shape),
                  _const_spec(wi.shape), _const_spec(bi.shape), _const_spec(lam.shape)],
        out_specs=[col_spec, col_spec, row_spec(ATT_WIDTH),
                   pl.BlockSpec((None, tm, ATT_HEADS, V_HEAD_DIM), lambda b, i: (b, i, 0, 0)),
                   col_spec, row_spec(LRU_WIDTH),
                   pl.BlockSpec((None, LRU_CONV - 1, LRU_WIDTH), lambda b, i: (b, 0, 0)),
                   pl.BlockSpec((None, 1, LRU_WIDTH), lambda b, i: (b, 0, 0))],
        out_shape=[act_t(BF16), act_t(F32), act(BF16),
                   jax.ShapeDtypeStruct((bsz, s_len, ATT_HEADS, V_HEAD_DIM), F32), act_t(BF16), act(BF16),
                   jax.ShapeDtypeStruct((bsz, LRU_CONV - 1, LRU_WIDTH), F32),
                   jax.ShapeDtypeStruct((bsz, 1, LRU_WIDTH), F32)],
        scratch_shapes=[pltpu.VMEM((LRU_WIDTH // LANES, tm, LANES), F32),
                        pltpu.VMEM((LRU_WIDTH // LANES, tm, LANES), F32),
                        pltpu.VMEM((LRU_CONV - 1, SUBLANES, LRU_WIDTH), F32),
                        pltpu.VMEM((1, LRU_WIDTH), F32)],
        compiler_params=pltpu.CompilerParams(dimension_semantics=("arbitrary", "arbitrary"),
                                             vmem_limit_bytes=VMEM_LIMIT),
        name="inproj_prompt",
    )(x, mod3, mod3, g1, w_in, gq, gk, gmat, *tabs, cw, cb, wr, br, wi, bi, lam)


def _attn_prompt_kernel(bounded, qt_ref, k_ref, vt_ref, lq1_ref, lk1_ref, lq2_ref, lk2_ref, gs_ref, o_ref):
    tq = qt_ref.shape[1]
    qi = pl.program_id(2)
    qt = qt_ref[...].astype(F32)
    feat = lax.broadcasted_iota(jnp.int32, (V_HEAD_DIM, tq), 0)
    qc = (jnp.where(feat < QK_SUB_DIM, qt, 0.0).astype(BF16), jnp.where(feat >= QK_SUB_DIM, qt, 0.0).astype(BF16))

    def tile(j, carry, masked):
        k0 = pl.multiple_of(j * TK, TK)
        ks = k_ref[pl.ds(k0, TK), :]
        vts = vt_ref[:, pl.ds(k0, TK)]
        ss = [_dot(ks, qc[c]) for c in range(2)]
        out = []
        for c in range(2):
            m, l, acc = carry[c]
            s = ss[c]
            if masked:
                keys = lax.broadcasted_iota(jnp.int32, (TK, tq), 0)
                qs = lax.broadcasted_iota(jnp.int32, (TK, tq), 1)
                s = jnp.where(keys <= qs, s, NEG)
            if bounded:
                p = jnp.exp2(s)
                l = l + jnp.sum(p, axis=0, keepdims=True)
                acc = acc + _dot(vts, p.astype(BF16))
            else:
                m_new = jnp.maximum(m, jnp.max(s, axis=0, keepdims=True))
                alpha = jnp.exp2(m - m_new)
                p = jnp.exp2(s - m_new)
                l = alpha * l + jnp.sum(p, axis=0, keepdims=True)
                acc = alpha * acc + _dot(vts, p.astype(BF16))
                m = m_new
            out.append((m, l, acc))
        return tuple(out)

    init = tuple((jnp.full((1, tq), NEG, F32), jnp.zeros((1, tq), F32), jnp.zeros((V_HEAD_DIM, tq), F32))
                 for _ in range(2))
    carry = lax.fori_loop(0, qi, lambda j, c: tile(j, c, False), init)
    (_, l0, a0), (_, l1, a1) = tile(qi, carry, True)
    lam = _diff_lambda(lq1_ref[...], lk1_ref[...], lq2_ref[...], lk2_ref[...])
    ot = a0 / l0 - lam * (a1 / l1)
    o_ref[...] = _subln(ot.T, gs_ref[...]).astype(o_ref.dtype)


def _attn_prompt(bounded, qt, k, vt, lq1, lk1, lq2, lk2, gs):
    bsz, s_len, _ = k.shape
    qt_spec = pl.BlockSpec((None, V_HEAD_DIM, TQ), lambda b, h, i: (b, h, i))
    k_spec = pl.BlockSpec((None, s_len, V_HEAD_DIM), lambda b, h, i: (b, 0, h))
    vt_spec = pl.BlockSpec((None, V_HEAD_DIM, s_len), lambda b, h, i: (b, h, 0))
    o_spec = pl.BlockSpec((None, TQ, V_HEAD_DIM), lambda b, h, i: (b, i, h))
    small = lambda a: pl.BlockSpec(a.shape, lambda b, h, i: (0, 0))
    return pl.pallas_call(
        functools.partial(_attn_prompt_kernel, bounded),
        grid=(bsz, ATT_HEADS, s_len // TQ),
        in_specs=[qt_spec, k_spec, vt_spec, small(lq1), small(lk1), small(lq2), small(lk2), small(gs)],
        out_specs=o_spec,
        out_shape=jax.ShapeDtypeStruct((bsz, s_len, ATT_WIDTH), BF16),
        compiler_params=pltpu.CompilerParams(dimension_semantics=("arbitrary", "arbitrary", "arbitrary"),
                                             vmem_limit_bytes=VMEM_LIMIT),
        name="attn_prompt_bounded" if bounded else "attn_prompt_general",
    )(qt, k, vt, lq1, lk1, lq2, lk2, gs)


def _ffn_prompt_kernel(x_ref, at_ref, lo_ref, gt1_ref, sh2_ref, sc2_ref, gt2_ref, g2_ref,
                       wout_ref, wup_ref, cfw_ref, cfb_ref, wdn_ref,
                       y_ref, fc_ref, ubuf_ref, abuf_ref, tail_ref):
    tm = x_ref.shape[0]
    d_ff = wdn_ref.shape[0]
    n_tail = FFN_CONV - 1

    @pl.when(pl.program_id(1) == 0)
    def _():
        tail_ref[...] = jnp.zeros(tail_ref.shape, F32)

    mix = jnp.concatenate([at_ref[...], lo_ref[...]], axis=1)
    x1 = x_ref[...] + gt1_ref[...] * _dot(mix, wout_ref[...])
    u2 = _to_segment_major(_modulated_norm(x1, g2_ref[...], sc2_ref[...], sh2_ref[...]), ubuf_ref).astype(BF16)

    n_chunks = d_ff // FF_CHUNK
    chunk_cols = lambda j: [slice(base + j * FF_CHUNK, base + (j + 1) * FF_CHUNK) for base in (0, d_ff)]
    up_project = lambda j: [_dot(u2, wup_ref[:, cols]) for cols in chunk_cols(j)]

    def hidden(j, ups):
        halves = []
        for cols, up in zip(chunk_cols(j), ups):
            w = cfw_ref[:, cols]
            d2, d1 = _delayed(up, [tail_ref[i, :, cols] for i in range(n_tail)])
            halves.append(d2 * w[0:1] + d1 * w[1:2] + up * w[2:3] + cfb_ref[:, cols])
            for i, grp in enumerate(_tail_groups(up, n_tail)):
                tail_ref[i, :, cols] = grp
                fc_ref[i:i + 1, cols] = grp[SUBLANES - 1:SUBLANES, :]
        g, val = halves
        return (g * jax.nn.sigmoid(g) * val).astype(BF16)

    acc = jnp.zeros((tm, D_MODEL), F32)
    ups = up_project(0)
    for j in range(n_chunks):
        ups_next = up_project(j + 1) if j + 1 < n_chunks else None
        acc = acc + _dot(hidden(j, ups), wdn_ref[j * FF_CHUNK:(j + 1) * FF_CHUNK, :])
        ups = ups_next
    y_ref[...] = x1 + gt2_ref[...] * _to_time_major(acc, abuf_ref)


def _ffn_prompt(x, attn, lru, mod3, g2, w_out, w_up, cfw, cfb, w_dn):
    bsz, s_len, _ = x.shape
    tm = TM_FFN
    d_ff = w_dn.shape[0]
    row_spec = lambda w: pl.BlockSpec((None, tm, w), lambda b, i: (b, i, 0))
    mod_spec = lambda k: pl.BlockSpec((None, 1, D_MODEL), lambda b, i: (b * 6 + k, 0, 0))
    return pl.pallas_call(
        _ffn_prompt_kernel,
        grid=(bsz, s_len // tm),
        in_specs=[row_spec(D_MODEL), row_spec(ATT_WIDTH), row_spec(LRU_WIDTH),
                  mod_spec(2), mod_spec(3), mod_spec(4), mod_spec(5), _const_spec(g2.shape),
                  _const_spec(w_out.shape), _const_spec(w_up.shape), _const_spec(cfw.shape),
                  _const_spec(cfb.shape), _const_spec(w_dn.shape)],
        out_specs=[row_spec(D_MODEL), pl.BlockSpec((None, FFN_CONV - 1, 2 * d_ff), lambda b, i: (b, 0, 0))],
        out_shape=[jax.ShapeDtypeStruct((bsz, s_len, D_MODEL), F32),
                   jax.ShapeDtypeStruct((bsz, FFN_CONV - 1, 2 * d_ff), F32)],
        scratch_shapes=[pltpu.VMEM((D_MODEL // LANES, tm, LANES), F32),
                        pltpu.VMEM((D_MODEL // LANES, tm, LANES), F32),
                        pltpu.VMEM((FFN_CONV - 1, SUBLANES, 2 * d_ff), F32)],
        compiler_params=pltpu.CompilerParams(dimension_semantics=("arbitrary", "arbitrary"),
                                             vmem_limit_bytes=VMEM_LIMIT),
        name="ffn_prompt",
    )(x, attn, lru, mod3, mod3, mod3, mod3, g2, w_out, w_up, cfw, cfb, w_dn)


def _inproj_sample_kernel(x_ref, mod_ref, g1_ref, win_ref, gq_ref, gk_ref, gmat_ref,
                          cos_ref, sa_ref, sb_ref, cw_ref, cb_ref, wr_ref, br_ref, wi_ref, bi_ref, lam_ref,
                          st_ref, h0_ref,
                          q_ref, k_ref, v_ref, lo_ref, lc_ref, lh_ref):
    nb = h0_ref.shape[0]
    nt = x_ref.shape[0] // nb
    rep = lambda a: jnp.concatenate([a] * nt, axis=0)
    sh1 = rep(mod_ref[:, 0:D_MODEL])
    sc1 = rep(mod_ref[:, D_MODEL:2 * D_MODEL])
    u = _modulated_norm(x_ref[...], g1_ref[...], sc1, sh1).astype(BF16)
    per_t = lambda r: jnp.concatenate([jnp.broadcast_to(r[t:t + 1, :], (nb, LANES)) for t in range(nt)], axis=0)
    cos, sa, sb = per_t(cos_ref[...]), per_t(sa_ref[...]), per_t(sb_ref[...])
    gmat = gmat_ref[...]

    q = _group_norm_rope(_dot(u, win_ref[:, 0:ATT_WIDTH]), gq_ref[...], gmat, cos, sa, sb)
    q_ref[...] = q * QK_SCALE
    k_ref[...] = _group_norm_rope(_dot(u, win_ref[:, ATT_WIDTH:2 * ATT_WIDTH]), gk_ref[...], gmat, cos, sa, sb)
    v_ref[...] = _dot(u, win_ref[:, 2 * ATT_WIDTH:3 * ATT_WIDTH])

    c0 = 3 * ATT_WIDTH
    lx = _dot(u, win_ref[:, c0:c0 + LRU_WIDTH])
    lg = _dot(u, win_ref[:, c0 + LRU_WIDTH:c0 + 2 * LRU_WIDTH])
    pad = [st_ref[i] for i in range(LRU_CONV - 1)] + [lx[t * nb:(t + 1) * nb, :] for t in range(nt)]
    for i in range(LRU_CONV - 1):
        lc_ref[i] = pad[nt + i]
    cw = cw_ref[...]
    xcs = []
    for t in range(nt):
        xc = pad[t] * cw[0:1]
        for kk in range(1, LRU_CONV):
            xc = xc + pad[t + kk] * cw[kk:kk + 1]
        xcs.append(xc + cb_ref[...])
    a, gx = _lru_gates(jnp.concatenate(xcs, axis=0), wr_ref, br_ref[...], wi_ref, bi_ref[...],
                       _neg_c_softplus(lam_ref[...]))
    h = h0_ref[...]
    hs = []
    for t in range(nt):
        h = a[t * nb:(t + 1) * nb, :] * h + gx[t * nb:(t + 1) * nb, :]
        hs.append(h)
    lh_ref[...] = h
    lo_ref[...] = (jnp.concatenate(hs, axis=0) * jax.nn.gelu(lg, approximate=True)).astype(BF16)


def _inproj_sample(x_tb, mod_s, g1, w_in, gq, gk, gmat, tabs_s, cw, cb, wr, br, wi, bi, lam, st, h0):
    m = x_tb.shape[0]
    nb = h0.shape[0]
    act = jax.ShapeDtypeStruct((m, ATT_WIDTH), F32)
    return pl.pallas_call(
        _inproj_sample_kernel,
        out_shape=[act, act, act, jax.ShapeDtypeStruct((m, LRU_WIDTH), BF16),
                   jax.ShapeDtypeStruct((LRU_CONV - 1, nb, LRU_WIDTH), F32),
                   jax.ShapeDtypeStruct((nb, LRU_WIDTH), F32)],
        compiler_params=pltpu.CompilerParams(vmem_limit_bytes=VMEM_LIMIT),
        name="inproj_sample",
    )(x_tb, mod_s, g1, w_in, gq, gk, gmat, *tabs_s, cw, cb, wr, br, wi, bi, lam, st, h0)


def _attn_sample_kernel(n_pages, pt_ref, q_ref, kn_ref, vn_ref, lq1_ref, lk1_ref, lq2_ref, lk2_ref, gs_ref, *refs):
    kt_refs = refs[:n_pages]
    v_refs = refs[n_pages:2 * n_pages]
    o_ref = refs[2 * n_pages]
    page = kt_refs[0].shape[1]
    nq = q_ref.shape[0]
    hrows = 2 * nq

    lane = lax.broadcasted_iota(jnp.int32, (nq, V_HEAD_DIM), 1)
    qh = []
    for h in range(ATT_HEADS):
        q8 = q_ref[:, h * V_HEAD_DIM:(h + 1) * V_HEAD_DIM]
        qh.append(jnp.concatenate([jnp.where(lane < QK_SUB_DIM, q8, 0.0), jnp.where(lane >= QK_SUB_DIM, q8, 0.0)],
                                  axis=0).astype(BF16))

    s_tiles = []
    for j in range(n_pages):
        kt = kt_refs[j][...].astype(BF16)
        s_tiles.append(jnp.concatenate(
            [_dot(qh[h], kt[h * V_HEAD_DIM:(h + 1) * V_HEAD_DIM, :]) for h in range(ATT_HEADS)], axis=0))
    zpad = jnp.zeros((page - nq, V_HEAD_DIM), F32)
    k_new = [jnp.concatenate([kn_ref[:, h * V_HEAD_DIM:(h + 1) * V_HEAD_DIM], zpad], axis=0).astype(BF16)
             for h in range(ATT_HEADS)]
    v_new = [jnp.concatenate([vn_ref[:, h * V_HEAD_DIM:(h + 1) * V_HEAD_DIM], zpad], axis=0).astype(BF16)
             for h in range(ATT_HEADS)]
    nrow = ATT_HEADS * hrows
    rt = lax.broadcasted_iota(jnp.int32, (nrow, page), 0) % nq
    ct = lax.broadcasted_iota(jnp.int32, (nrow, page), 1)
    s_new = jnp.concatenate([_dot_nt(qh[h], k_new[h]) for h in range(ATT_HEADS)], axis=0)
    s_tiles.append(jnp.where(ct <= rt, s_new, NEG))

    m = s_tiles[0]
    for s in s_tiles[1:]:
        m = jnp.maximum(m, s)
    m = jnp.max(m, axis=1, keepdims=True)
    l = jnp.zeros((nrow, 1), F32)
    acc = [jnp.zeros((hrows, V_HEAD_DIM), F32) for _ in range(ATT_HEADS)]
    for j, s in enumerate(s_tiles):
        p = jnp.exp(s - m)
        l = l + jnp.sum(p, axis=1, keepdims=True)
        pb = p.astype(BF16)
        for h in range(ATT_HEADS):
            vj = v_refs[j][pl.ds(h, page, stride=ATT_HEADS), :].astype(BF16) if j < n_pages else v_new[h]
            acc[h] = acc[h] + _dot(pb[h * hrows:(h + 1) * hrows, :], vj)
    lam = _diff_lambda(lq1_ref[...], lk1_ref[...], lq2_ref[...], lk2_ref[...])
    gs = gs_ref[...]
    outs = []
    for h in range(ATT_HEADS):
        o = acc[h] / l[h * hrows:(h + 1) * hrows, :]
        outs.append(_subln(o[0:nq, :] - lam * o[nq:hrows, :], gs))
    o_ref[...] = jnp.concatenate(outs, axis=1)


def _attn_sample(page_table, q8, kn8, vn8, cache_kt, cache_v, lq1, lk1, lq2, lk2, gs):
    nb, n_pages = page_table.shape
    nq = q8.shape[1]
    page = cache_kt.shape[2]
    new_spec = pl.BlockSpec((None, nq, ATT_WIDTH), lambda b, pt: (b, 0, 0))
    small = lambda a: pl.BlockSpec(a.shape, lambda b, pt: (0, 0))
    kt_spec = lambda j: pl.BlockSpec((None, ATT_WIDTH, page), lambda b, pt: (pt[b, j], 0, 0))
    v_spec = lambda j: pl.BlockSpec((None, page * ATT_HEADS, V_HEAD_DIM), lambda b, pt: (pt[b, j], 0, 0))
    grid_spec = pltpu.PrefetchScalarGridSpec(
        num_scalar_prefetch=1,
        grid=(nb,),
        in_specs=[new_spec, new_spec, new_spec, small(lq1), small(lk1), small(lq2), small(lk2), small(gs)]
        + [kt_spec(j) for j in range(n_pages)] + [v_spec(j) for j in range(n_pages)],
        out_specs=new_spec,
    )
    return pl.pallas_call(
        functools.partial(_attn_sample_kernel, n_pages),
        grid_spec=grid_spec,
        out_shape=jax.ShapeDtypeStruct((nb, nq, ATT_WIDTH), F32),
        compiler_params=pltpu.CompilerParams(dimension_semantics=("arbitrary",), vmem_limit_bytes=VMEM_LIMIT),
        name="attn_sample",
    )(page_table, q8, kn8, vn8, lq1, lk1, lq2, lk2, gs, *([cache_kt] * n_pages), *([cache_v] * n_pages))


def _ffn_sample_kernel(x_ref, at_ref, lo_ref, mod_ref, g2_ref, wout_ref, wup_ref, cfw_ref, cfb_ref, wdn_ref, st_ref,
                       y_ref, fc_ref):
    nb = st_ref.shape[1]
    nt = x_ref.shape[0] // nb
    d_ff = wdn_ref.shape[0]
    rep = lambda a: jnp.concatenate([a] * nt, axis=0)
    gt1 = rep(mod_ref[:, 2 * D_MODEL:3 * D_MODEL])
    sh2 = rep(mod_ref[:, 3 * D_MODEL:4 * D_MODEL])
    sc2 = rep(mod_ref[:, 4 * D_MODEL:5 * D_MODEL])
    gt2 = rep(mod_ref[:, 5 * D_MODEL:6 * D_MODEL])

    mix = jnp.concatenate([at_ref[...].astype(BF16), lo_ref[...]], axis=1)
    x1 = x_ref[...] + gt1 * _dot(mix, wout_ref[...])
    u2 = _modulated_norm(x1, g2_ref[...], sc2, sh2).astype(BF16)

    acc = jnp.zeros((nt * nb, D_MODEL), F32)
    for j in range(d_ff // FF_CHUNK):
        halves = []
        for base in (0, d_ff):
            c0 = base + j * FF_CHUNK
            cols = slice(c0, c0 + FF_CHUNK)
            up = _dot(u2, wup_ref[:, cols])
            pad = [st_ref[i, :, cols] for i in range(FFN_CONV - 1)] + [up[t * nb:(t + 1) * nb, :] for t in range(nt)]
            for i in range(FFN_CONV - 1):
                fc_ref[i, :, cols] = pad[nt + i]
            w = cfw_ref[:, cols]
            b = cfb_ref[:, cols]
            hcs = []
            for t in range(nt):
                hc = pad[t] * w[0:1]
                for kk in range(1, FFN_CONV):
                    hc = hc + pad[t + kk] * w[kk:kk + 1]
                hcs.append(hc + b)
            halves.append(jnp.concatenate(hcs, axis=0))
        g, val = halves
        hmid = (g * jax.nn.sigmoid(g) * val).astype(BF16)
        acc = acc + _dot(hmid, wdn_ref[j * FF_CHUNK:(j + 1) * FF_CHUNK, :])
    y_ref[...] = x1 + gt2 * acc


def _ffn_sample(x_tb, attn_tb, lru_tb, mod_s, g2, w_out, w_up, cfw, cfb, w_dn, st):
    m = x_tb.shape[0]
    return pl.pallas_call(
        _ffn_sample_kernel,
        out_shape=[jax.ShapeDtypeStruct((m, D_MODEL), F32), jax.ShapeDtypeStruct(st.shape, F32)],
        compiler_params=pltpu.CompilerParams(vmem_limit_bytes=VMEM_LIMIT),
        name="ffn_sample",
    )(x_tb, attn_tb, lru_tb, mod_s, g2, w_out, w_up, cfw, cfb, w_dn, st)


def _block_diag_halves(w):
    n, bd, _ = w.shape
    eye = jnp.eye(n // 2, dtype=w.dtype)
    halves = [jnp.einsum('nij,nm->nimj', w[s * (n // 2):(s + 1) * (n // 2)], eye).reshape(n // 2 * bd, n // 2 * bd)
              for s in range(2)]
    return jnp.stack(halves).astype(BF16)


def kernel(x_prompt, x_sample, cache_k, cache_v, page_table, state_lru_conv, state_lru_h, state_ffn_conv, c_prompt, c_sample, g_norm1, g_norm2, w_ada, b_ada, w_in, g_q, g_k, lam_q1, lam_k1, lam_q2, lam_k2, g_subln, w_out, conv_lru_w, conv_lru_b, w_rgate, b_rgate, w_igate, b_igate, lru_lambda, w_up, conv_ffn_w, conv_ffn_b, w_down):
    depth = w_in.shape[0]
    assert depth == 1, "single-layer step"
    bsz, s_len, _ = x_prompt.shape
    nb, nt, _ = x_sample.shape
    n_pages, page = page_table.shape[1], cache_k.shape[2]
    past_len = n_pages * page
    d_ff = w_down.shape[1]

    w_in_b = w_in[0].astype(BF16)
    w_out_b = w_out[0].astype(BF16)
    w_up_b = w_up[0].astype(BF16)
    w_dn_b = w_down[0].astype(BF16)
    wr = _block_diag_halves(w_rgate[0])
    wi = _block_diag_halves(w_igate[0])
    br = b_rgate[0].reshape(1, LRU_WIDTH)
    bi = b_igate[0].reshape(1, LRU_WIDTH)
    n_grp = ATT_WIDTH // QK_SUB_DIM
    gq = jnp.tile(g_q[0], n_grp)[None, :]
    gk = jnp.tile(g_k[0], n_grp)[None, :]
    grp = jnp.arange(ATT_WIDTH) // QK_SUB_DIM
    gmat = jnp.where(grp[:, None] == grp[None, :], 1.0 / QK_SUB_DIM, 0.0).astype(BF16)
    g1, g2 = g_norm1, g_norm2
    cw, cb = conv_lru_w[0], conv_lru_b
    cfw, cfb = conv_ffn_w[0], conv_ffn_b
    lam = lru_lambda
    lams = (lam_q1, lam_k1, lam_q2, lam_k2)
    gs = g_subln

    n_c = bsz + nb
    n_pad = -n_c % SUBLANES
    c_all = jnp.concatenate([c_prompt, c_sample, jnp.zeros((n_pad, D_MODEL), F32)], axis=0)
    mod = _ada(c_all, w_ada[0], b_ada)
    mod3 = mod[:bsz].reshape(bsz * 6, 1, D_MODEL)
    mod_s = mod[bsz:bsz + nb]

    tabs = _rope_tables(s_len)
    pad_rows = SUBLANES - nt
    tabs_s = tuple(t[past_len:past_len + SUBLANES] for t in tabs)

    q_p, kf_p, kb_p, vf_p, vb_p, lo_p, lc_p, lh_p = _inproj_prompt(
        x_prompt, mod3, g1, w_in_b, gq, gk, gmat, tabs, cw, cb, wr, br, wi, bi, lam)
    score_bound = QK_NORM_BOUND ** 2 * QK_SCALE_LOG2 * jnp.max(jnp.abs(g_q)) * jnp.max(jnp.abs(g_k))
    at_p = lax.cond(score_bound <= SCORE_LOG2_LIMIT,
                    functools.partial(_attn_prompt, True), functools.partial(_attn_prompt, False),
                    q_p, kb_p, vb_p, *lams, gs)
    y_p, fc_p = _ffn_prompt(x_prompt, at_p, lo_p, mod3, g2, w_out_b, w_up_b, cfw, cfb, w_dn_b)

    x_tb = x_sample.transpose(1, 0, 2).reshape(nt * nb, D_MODEL)
    st_lru = state_lru_conv[0].transpose(1, 0, 2)
    q_s, k_s, v_s, lo_s, lc_s, lh_s = _inproj_sample(
        x_tb, mod_s, g1, w_in_b, gq, gk, gmat, tabs_s, cw, cb, wr, br, wi, bi, lam, st_lru, state_lru_h[0])
    to_bt = lambda a: a.reshape(nt, nb, -1).transpose(1, 0, 2)
    pad_t = lambda a: jnp.pad(a, ((0, 0), (0, pad_rows), (0, 0)))
    k_bt, v_bt = to_bt(k_s), to_bt(v_s)
    cache_kt = cache_k[0].transpose(0, 2, 3, 4, 1).reshape(-1, ATT_WIDTH, page)
    at_s = _attn_sample(page_table, pad_t(to_bt(q_s)), pad_t(k_bt), pad_t(v_bt), cache_kt,
                        cache_v[0].reshape(-1, page * ATT_HEADS, V_HEAD_DIM), *lams, gs)
    at_tb = at_s[:, :nt].transpose(1, 0, 2).reshape(nt * nb, ATT_WIDTH)
    y_tb, fc_s = _ffn_sample(x_tb, at_tb, lo_s, mod_s, g2, w_out_b, w_up_b, cfw, cfb, w_dn_b,
                             state_ffn_conv[0].transpose(1, 0, 2))

    hd = (ATT_HEADS, 2, QK_SUB_DIM)
    return (y_p, to_bt(y_tb),
            kf_p.reshape(bsz, *hd, s_len).transpose(0, 4, 1, 2, 3)[None], vf_p[None],
            lc_p[None], lh_p.reshape(1, bsz, LRU_WIDTH), fc_p[None],
            k_bt.reshape(1, nb, nt, *hd), v_bt.reshape(1, nb, nt, ATT_HEADS, V_HEAD_DIM),
            lc_s.transpose(1, 0, 2)[None], lh_s[None], fc_s.transpose(1, 0, 2)[None])
```

```python
import functools
import math

import jax
import jax.numpy as jnp
from jax import lax
from jax.experimental import pallas as pl
from jax.experimental.pallas import tpu as pltpu

F32 = jnp.float32
BF16 = jnp.bfloat16

D_MODEL = 1024
ATT_WIDTH = 512
LRU_WIDTH = 512
ATT_HEADS = 4
V_HEAD_DIM = 128
QK_SUB_DIM = 64
ROPE_DIM = 16
ROPE_THETA = 500000.0
LRU_BLOCKS = 8
LRU_C = 8.0
LRU_CONV = 4
FFN_CONV = 3
EPS = 1e-6
LAM_INIT = 0.8 - 0.6 * math.exp(-0.3 * 0)
QK_SCALE = QK_SUB_DIM ** -0.5
QK_SCALE_LOG2 = QK_SCALE * math.log2(math.e)
SCORE_LOG2_LIMIT = 64.0
QK_NORM_BOUND = 1.01 * math.sqrt(QK_SUB_DIM)

LANES = 128
SUBLANES = 8
VMEM_LIMIT = 56 * 1024 * 1024

TM_IN = 512
TQ = 512
TK = 512
TM_FFN = 256
FF_CHUNK = 256
NEG = -1e30


def _dot(a, b):
    return jnp.dot(a, b, preferred_element_type=F32)


def _dot_nt(a, b):
    return lax.dot_general(a, b, (((1,), (1,)), ((), ())), preferred_element_type=F32)


def _const_spec(shape):
    nd = len(shape)
    return pl.BlockSpec(shape, lambda *_: (0,) * nd, pipeline_mode=pl.Buffered(1))


def _modulated_norm(x, g, sc, sh):
    xn = x * lax.rsqrt(jnp.mean(x * x, axis=-1, keepdims=True) + EPS) * g
    return xn * (1.0 + sc) + sh


def _to_segment_major(val, buf_ref):
    tm, w = val.shape
    n = tm // SUBLANES
    for c in range(w // LANES):
        for s in range(SUBLANES):
            for r0 in range(0, n, SUBLANES):
                t0 = s * n + r0
                buf_ref[c, pl.ds(r0 * SUBLANES + s, SUBLANES, stride=SUBLANES), :] = (
                    val[t0:t0 + SUBLANES, c * LANES:(c + 1) * LANES])
    return jnp.concatenate([buf_ref[c] for c in range(w // LANES)], axis=1)


def _to_time_major(val, buf_ref):
    tm, w = val.shape
    n = tm // SUBLANES
    for c in range(w // LANES):
        buf_ref[c] = val[:, c * LANES:(c + 1) * LANES]
    rows = []
    for s in range(SUBLANES):
        for r0 in range(0, n, SUBLANES):
            rows.append(jnp.concatenate(
                [buf_ref[c, pl.ds(r0 * SUBLANES + s, SUBLANES, stride=SUBLANES), :] for c in range(w // LANES)], axis=1))
    return jnp.concatenate(rows, axis=0)


def _tail_groups(cur, k):
    tm = cur.shape[0]
    return [cur[tm - (k - i) * SUBLANES:tm - (k - i - 1) * SUBLANES, :] for i in range(k)]


def _delayed(cur, prev_tail):
    k = len(prev_tail)
    tm = cur.shape[0]
    first = lax.broadcasted_iota(jnp.int32, prev_tail[0].shape, 0) == 0
    heads = [jnp.where(first, pltpu.roll(p, 1, 0), pltpu.roll(c, 1, 0))
             for p, c in zip(prev_tail, _tail_groups(cur, k))]
    return [jnp.concatenate(heads[k - d:] + [cur[:tm - d * SUBLANES, :]], axis=0) for d in range(k, 0, -1)]


def _group_norm_rope(t, g_tiled, gmat, cos, sin_a, sin_b):
    ms = _dot((t * t).astype(BF16), gmat)
    tn = t * lax.rsqrt(ms + EPS) * g_tiled
    outs = []
    for h in range(ATT_WIDTH // LANES):
        th = tn[:, h * LANES:(h + 1) * LANES]
        outs.append(th * cos + pltpu.roll(th, LANES - 8, 1) * sin_a + pltpu.roll(th, 8, 1) * sin_b)
    return jnp.concatenate(outs, axis=1)


def _lru_gates(xc, wr_ref, br, wi_ref, bi, neg_c_softplus):
    half = LRU_WIDTH // 2
    xb = xc.astype(BF16)
    lo, hi = xb[:, :half], xb[:, half:]
    r = jax.nn.sigmoid(jnp.concatenate([_dot(lo, wr_ref[0]), _dot(hi, wr_ref[1])], axis=1) + br)
    ig = jax.nn.sigmoid(jnp.concatenate([_dot(lo, wi_ref[0]), _dot(hi, wi_ref[1])], axis=1) + bi)
    log_a = neg_c_softplus * r
    a = jnp.exp(log_a)
    one_minus_a2 = -jnp.tanh(log_a) * (a * a + 1.0)
    return a, jnp.sqrt(one_minus_a2) * (ig * xc)


def _neg_c_softplus(lam):
    z = -lam
    return -LRU_C * (jnp.maximum(z, 0.0) + jnp.log1p(jnp.exp(-jnp.abs(z))))


def _diff_lambda(lq1, lk1, lq2, lk2):
    s1 = jnp.sum(lq1 * lk1, axis=-1, keepdims=True)
    s2 = jnp.sum(lq2 * lk2, axis=-1, keepdims=True)
    return jnp.exp(s1) - jnp.exp(s2) + LAM_INIT


def _subln(o, g):
    return o * lax.rsqrt(jnp.mean(o * o, axis=-1, keepdims=True) + EPS) * g * (1.0 - LAM_INIT)


def _ada_kernel(c_ref, w_ref, b_ref, o_ref):
    c = c_ref[...]
    s = (c * jax.nn.sigmoid(c)).astype(BF16)
    o_ref[...] = _dot(s, w_ref[...].astype(BF16)) + b_ref[...]


def _ada(c_all, w_ada, b_ada):
    m = c_all.shape[0]
    n = w_ada.shape[1]
    tn = 1024
    return pl.pallas_call(
        _ada_kernel,
        grid=(n // tn,),
        in_specs=[pl.BlockSpec((m, D_MODEL), lambda j: (0, 0)),
                  pl.BlockSpec((D_MODEL, tn), lambda j: (0, j)),
                  pl.BlockSpec((1, tn), lambda j: (0, j))],
        out_specs=pl.BlockSpec((m, tn), lambda j: (0, j)),
        out_shape=jax.ShapeDtypeStruct((m, n), F32),
        compiler_params=pltpu.CompilerParams(dimension_semantics=("arbitrary",), vmem_limit_bytes=VMEM_LIMIT),
        name="ada_mod",
    )(c_all, w_ada, b_ada)


def _rope_table_kernel(freq_ref, ma_ref, mb_ref, c_ref, sa_ref, sb_ref):
    tm = c_ref.shape[0]
    pos = (pl.program_id(0) * tm + lax.broadcasted_iota(jnp.int32, (tm, LANES), 0)).astype(F32)
    ang = pos * freq_ref[...]
    s = jnp.sin(ang)
    c_ref[...] = jnp.cos(ang)
    sa_ref[...] = -s * ma_ref[...]
    sb_ref[...] = s * mb_ref[...]


def _rope_tables(n_pos):
    half = ROPE_DIM // 2
    freqs = ROPE_THETA ** (-jnp.arange(half, dtype=F32) * 2.0 / ROPE_DIM)
    d = jnp.arange(LANES) % QK_SUB_DIM
    freq_lane = jnp.where(d < ROPE_DIM, freqs[d % half], 0.0).astype(F32)[None, :]
    mask_a = (d < half).astype(F32)[None, :]
    mask_b = ((d >= half) & (d < ROPE_DIM)).astype(F32)[None, :]
    tm = 512
    row = pl.BlockSpec((1, LANES), lambda i: (0, 0))
    tab = pl.BlockSpec((tm, LANES), lambda i: (i, 0))
    shp = jax.ShapeDtypeStruct((n_pos, LANES), F32)
    return pl.pallas_call(
        _rope_table_kernel,
        grid=(n_pos // tm,),
        in_specs=[row, row, row],
        out_specs=[tab, tab, tab],
        out_shape=[shp, shp, shp],
        compiler_params=pltpu.CompilerParams(dimension_semantics=("arbitrary",)),
        name="rope_tables",
    )(freq_lane, mask_a, mask_b)


def _inproj_prompt_kernel(x_ref, sh_ref, sc_ref, g1_ref, win_ref, gq_ref, gk_ref, gmat_ref,
                          cos_ref, sa_ref, sb_ref, cw_ref, cb_ref, wr_ref, br_ref, wi_ref, bi_ref, lam_ref,
                          q_ref, kf_ref, kb_ref, vf_ref, vb_ref, lo_ref, lc_ref, lh_ref,
                          lbuf_ref, hbuf_ref, tail_ref, hcar_ref):
    tm = x_ref.shape[0]

    @pl.when(pl.program_id(1) == 0)
    def _():
        tail_ref[...] = jnp.zeros(tail_ref.shape, F32)
        hcar_ref[...] = jnp.zeros(hcar_ref.shape, F32)

    u = _modulated_norm(x_ref[...], g1_ref[...], sc_ref[...], sh_ref[...]).astype(BF16)
    cos, sa, sb = cos_ref[...], sa_ref[...], sb_ref[...]
    gmat = gmat_ref[...]

    q = _group_norm_rope(_dot(u, win_ref[:, 0:ATT_WIDTH]), gq_ref[...], gmat, cos, sa, sb)
    q_ref[...] = (q * QK_SCALE_LOG2).T.astype(BF16)
    k = _group_norm_rope(_dot(u, win_ref[:, ATT_WIDTH:2 * ATT_WIDTH]), gk_ref[...], gmat, cos, sa, sb)
    kf_ref[...] = k.T
    kb_ref[...] = k.astype(BF16)
    v = _dot(u, win_ref[:, 2 * ATT_WIDTH:3 * ATT_WIDTH])
    for h in range(ATT_HEADS):
        vf_ref[:, h, :] = v[:, h * V_HEAD_DIM:(h + 1) * V_HEAD_DIM]
    vb_ref[...] = v.T.astype(BF16)

    c0 = 3 * ATT_WIDTH
    n_tail = LRU_CONV - 1
    lx = _to_segment_major(_dot(u, win_ref[:, c0:c0 + LRU_WIDTH]), lbuf_ref)
    cw = cw_ref[...]
    d3, d2, d1 = _delayed(lx, [tail_ref[i] for i in range(n_tail)])
    xc = d3 * cw[0:1] + d2 * cw[1:2] + d1 * cw[2:3] + lx * cw[3:4] + cb_ref[...]
    for i, grp in enumerate(_tail_groups(lx, n_tail)):
        tail_ref[i] = grp
        lc_ref[i:i + 1, :] = grp[SUBLANES - 1:SUBLANES, :]

    a, gx = _lru_gates(xc, wr_ref, br_ref[...], wi_ref, bi_ref[...], _neg_c_softplus(lam_ref[...]))

    n = tm // SUBLANES
    grp = lambda arr, r: arr[r * SUBLANES:(r + 1) * SUBLANES, :]
    ps, hs = [grp(a, 0)], [grp(gx, 0)]
    for r in range(1, n):
        ar = grp(a, r)
        ps.append(ar * ps[-1])
        hs.append(ar * hs[-1] + grp(gx, r))
    p_end, h_end = ps[-1], hs[-1]
    entering = [hcar_ref[...]]
    for s in range(1, SUBLANES):
        entering.append(p_end[s - 1:s, :] * entering[-1] + h_end[s - 1:s, :])
    h_last = p_end[SUBLANES - 1:SUBLANES, :] * entering[-1] + h_end[SUBLANES - 1:SUBLANES, :]
    hcar_ref[...] = h_last
    lh_ref[...] = h_last
    enter = jnp.concatenate(entering, axis=0)
    states = jnp.concatenate([hs[r] + ps[r] * enter for r in range(n)], axis=0)

    lg = _dot(u, win_ref[:, c0 + LRU_WIDTH:c0 + 2 * LRU_WIDTH])
    lo_ref[...] = (_to_time_major(states, hbuf_ref) * jax.nn.gelu(lg, approximate=True)).astype(BF16)


def _inproj_prompt(x, mod3, g1, w_in, gq, gk, gmat, tabs, cw, cb, wr, br, wi, bi, lam):
    bsz, s_len, _ = x.shape
    tm = TM_IN
    row_spec = lambda w: pl.BlockSpec((None, tm, w), lambda b, i: (b, i, 0))
    tab_spec = pl.BlockSpec((tm, LANES), lambda b, i: (i, 0))
    mod_spec = lambda k: pl.BlockSpec((None, 1, D_MODEL), lambda b, i: (b * 6 + k, 0, 0))
    act = lambda dt: jax.ShapeDtypeStruct((bsz, s_len, ATT_WIDTH), dt)
    act_t = lambda dt: jax.ShapeDtypeStruct((bsz, ATT_WIDTH, s_len), dt)
    col_spec = pl.BlockSpec((None, ATT_WIDTH, tm), lambda b, i: (b, 0, i))
    return pl.pallas_call(
        _inproj_prompt_kernel,
        grid=(bsz, s_len // tm),
        in_specs=[row_spec(D_MODEL), mod_spec(0), mod_spec(1), _const_spec(g1.shape), _const_spec(w_in.shape),
                  _const_spec(gq.shape), _const_spec(gk.shape), _const_spec(gmat.shape),
                  tab_spec, tab_spec, tab_spec,
                  _const_spec(cw.shape), _const_spec(cb.shape), _const_spec(wr.shape), _const_spec(br.shape),
                  _const_spec(wi.shape), _const_spec(bi.shape), _const_spec(lam.shape)],
        out_specs=[col_spec, col_spec, row_spec(ATT_WIDTH),
                   pl.BlockSpec((None, tm, ATT_HEADS, V_HEAD_DIM), lambda b, i: (b, i, 0, 0)),
                   col_spec, row_spec(LRU_WIDTH),
                   pl.BlockSpec((None, LRU_CONV - 1, LRU_WIDTH), lambda b, i: (b, 0, 0)),
                   pl.BlockSpec((None, 1, LRU_WIDTH), lambda b, i: (b, 0, 0))],
        out_shape=[act_t(BF16), act_t(F32), act(BF16),
                   jax.ShapeDtypeStruct((bsz, s_len, ATT_HEADS, V_HEAD_DIM), F32), act_t(BF16), act(BF16),
                   jax.ShapeDtypeStruct((bsz, LRU_CONV - 1, LRU_WIDTH), F32),
                   jax.ShapeDtypeStruct((bsz, 1, LRU_WIDTH), F32)],
        scratch_shapes=[pltpu.VMEM((LRU_WIDTH // LANES, tm, LANES), F32),
                        pltpu.VMEM((LRU_WIDTH // LANES, tm, LANES), F32),
                        pltpu.VMEM((LRU_CONV - 1, SUBLANES, LRU_WIDTH), F32),
                        pltpu.VMEM((1, LRU_WIDTH), F32)],
        compiler_params=pltpu.CompilerParams(dimension_semantics=("arbitrary", "arbitrary"),
                                             vmem_limit_bytes=VMEM_LIMIT),
        name="inproj_prompt",
    )(x, mod3, mod3, g1, w_in, gq, gk, gmat, *tabs, cw, cb, wr, br, wi, bi, lam)


def _attn_kernel(bounded, n_pages, pt_ref, qt_ref, k_ref, vt_ref, lq1_ref, lk1_ref, lq2_ref, lk2_ref, gs_ref,
                 qs_ref, kn_ref, vn_ref, *refs):
    kt_refs, v_refs = refs[:n_pages], refs[n_pages:2 * n_pages]
    o_ref, os_ref = refs[2 * n_pages:]
    tq = qt_ref.shape[1]
    qi = pl.program_id(2)
    qt = qt_ref[...].astype(F32)
    feat = lax.broadcasted_iota(jnp.int32, (V_HEAD_DIM, tq), 0)
    qc = (jnp.where(feat < QK_SUB_DIM, qt, 0.0).astype(BF16), jnp.where(feat >= QK_SUB_DIM, qt, 0.0).astype(BF16))

    def tile(j, carry, masked):
        k0 = pl.multiple_of(j * TK, TK)
        ks = k_ref[pl.ds(k0, TK), :]
        vts = vt_ref[:, pl.ds(k0, TK)]
        ss = [_dot(ks, qc[c]) for c in range(2)]
        out = []
        for c in range(2):
            m, l, acc = carry[c]
            s = ss[c]
            if masked:
                keys = lax.broadcasted_iota(jnp.int32, (TK, tq), 0)
                qs = lax.broadcasted_iota(jnp.int32, (TK, tq), 1)
                s = jnp.where(keys <= qs, s, NEG)
            if bounded:
                p = jnp.exp2(s)
                l = l + jnp.sum(p, axis=0, keepdims=True)
                acc = acc + _dot(vts, p.astype(BF16))
            else:
                m_new = jnp.maximum(m, jnp.max(s, axis=0, keepdims=True))
                alpha = jnp.exp2(m - m_new)
                p = jnp.exp2(s - m_new)
                l = alpha * l + jnp.sum(p, axis=0, keepdims=True)
                acc = alpha * acc + _dot(vts, p.astype(BF16))
                m = m_new
            out.append((m, l, acc))
        return tuple(out)

    init = tuple((jnp.full((1, tq), NEG, F32), jnp.zeros((1, tq), F32), jnp.zeros((V_HEAD_DIM, tq), F32))
                 for _ in range(2))
    carry = lax.fori_loop(0, qi, lambda j, c: tile(j, c, False), init)
    (_, l0, a0), (_, l1, a1) = tile(qi, carry, True)
    lam = _diff_lambda(lq1_ref[...], lk1_ref[...], lq2_ref[...], lk2_ref[...])
    ot = a0 / l0 - lam * (a1 / l1)
    o_ref[...] = _subln(ot.T, gs_ref[...]).astype(o_ref.dtype)
    os_ref[...] = _sample_attention(bounded, qs_ref, kn_ref, vn_ref, lam, gs_ref[...], kt_refs, v_refs)


def _attn(bounded, page_table, qt, k, vt, q8, kn8, vn8, cache_kt, cache_v, lq1, lk1, lq2, lk2, gs):
    bsz, s_len, _ = k.shape
    nb, n_pages = page_table.shape
    nq = q8.shape[1]
    page = cache_kt.shape[2]
    n_qt = s_len // TQ
    assert nb == bsz * ATT_HEADS * n_qt, "one sample sequence per grid step"
    seq = lambda b, h, i: (b * ATT_HEADS + h) * n_qt + i
    qt_spec = pl.BlockSpec((None, V_HEAD_DIM, TQ), lambda b, h, i, pt: (b, h, i))
    k_spec = pl.BlockSpec((None, s_len, V_HEAD_DIM), lambda b, h, i, pt: (b, 0, h))
    vt_spec = pl.BlockSpec((None, V_HEAD_DIM, s_len), lambda b, h, i, pt: (b, h, 0))
    o_spec = pl.BlockSpec((None, TQ, V_HEAD_DIM), lambda b, h, i, pt: (b, i, h))
    small = lambda a: pl.BlockSpec(a.shape, lambda b, h, i, pt: (0, 0))
    new_spec = pl.BlockSpec((None, nq, ATT_WIDTH), lambda b, h, i, pt: (seq(b, h, i), 0, 0))
    kt_spec = lambda j: pl.BlockSpec((None, ATT_WIDTH, page), lambda b, h, i, pt: (pt[seq(b, h, i), j], 0, 0))
    v_spec = lambda j: pl.BlockSpec((None, page * ATT_HEADS, V_HEAD_DIM),
                                    lambda b, h, i, pt: (pt[seq(b, h, i), j], 0, 0))
    grid_spec = pltpu.PrefetchScalarGridSpec(
        num_scalar_prefetch=1,
        grid=(bsz, ATT_HEADS, n_qt),
        in_specs=[qt_spec, k_spec, vt_spec, small(lq1), small(lk1), small(lq2), small(lk2), small(gs),
                  new_spec, new_spec, new_spec]
        + [kt_spec(j) for j in range(n_pages)] + [v_spec(j) for j in range(n_pages)],
        out_specs=[o_spec, new_spec],
    )
    return pl.pallas_call(
        functools.partial(_attn_kernel, bounded, n_pages),
        grid_spec=grid_spec,
        out_shape=[jax.ShapeDtypeStruct((bsz, s_len, ATT_WIDTH), BF16),
                   jax.ShapeDtypeStruct((nb, nq, ATT_WIDTH), F32)],
        compiler_params=pltpu.CompilerParams(dimension_semantics=("arbitrary", "arbitrary", "arbitrary"),
                                             vmem_limit_bytes=VMEM_LIMIT),
        name="attn_bounded" if bounded else "attn_general",
    )(page_table, qt, k, vt, lq1, lk1, lq2, lk2, gs, q8, kn8, vn8,
      *([cache_kt] * n_pages), *([cache_v] * n_pages))


def _ffn_prompt_kernel(x_ref, at_ref, lo_ref, gt1_ref, sh2_ref, sc2_ref, gt2_ref, g2_ref,
                       wout_ref, wup_ref, cfw_ref, cfb_ref, wdn_ref,
                       y_ref, fc_ref, ubuf_ref, abuf_ref, tail_ref):
    tm = x_ref.shape[0]
    d_ff = wdn_ref.shape[0]
    n_tail = FFN_CONV - 1

    @pl.when(pl.program_id(1) == 0)
    def _():
        tail_ref[...] = jnp.zeros(tail_ref.shape, F32)

    mix = jnp.concatenate([at_ref[...], lo_ref[...]], axis=1)
    x1 = x_ref[...] + gt1_ref[...] * _dot(mix, wout_ref[...])
    u2 = _to_segment_major(_modulated_norm(x1, g2_ref[...], sc2_ref[...], sh2_ref[...]), ubuf_ref).astype(BF16)

    n_chunks = d_ff // FF_CHUNK
    chunk_cols = lambda j: [slice(base + j * FF_CHUNK, base + (j + 1) * FF_CHUNK) for base in (0, d_ff)]
    up_project = lambda j: [_dot(u2, wup_ref[:, cols]) for cols in chunk_cols(j)]

    def hidden(j, ups):
        halves = []
        for cols, up in zip(chunk_cols(j), ups):
            w = cfw_ref[:, cols]
            d2, d1 = _delayed(up, [tail_ref[i, :, cols] for i in range(n_tail)])
            halves.append(d2 * w[0:1] + d1 * w[1:2] + up * w[2:3] + cfb_ref[:, cols])
            for i, grp in enumerate(_tail_groups(up, n_tail)):
                tail_ref[i, :, cols] = grp
                fc_ref[i:i + 1, cols] = grp[SUBLANES - 1:SUBLANES, :]
        g, val = halves
        return (g * jax.nn.sigmoid(g) * val).astype(BF16)

    acc = jnp.zeros((tm, D_MODEL), F32)
    ups = up_project(0)
    for j in range(n_chunks):
        ups_next = up_project(j + 1) if j + 1 < n_chunks else None
        acc = acc + _dot(hidden(j, ups), wdn_ref[j * FF_CHUNK:(j + 1) * FF_CHUNK, :])
        ups = ups_next
    y_ref[...] = x1 + gt2_ref[...] * _to_time_major(acc, abuf_ref)


def _ffn_prompt(x, attn, lru, mod3, g2, w_out, w_up, cfw, cfb, w_dn):
    bsz, s_len, _ = x.shape
    tm = TM_FFN
    d_ff = w_dn.shape[0]
    row_spec = lambda w: pl.BlockSpec((None, tm, w), lambda b, i: (b, i, 0))
    mod_spec = lambda k: pl.BlockSpec((None, 1, D_MODEL), lambda b, i: (b * 6 + k, 0, 0))
    return pl.pallas_call(
        _ffn_prompt_kernel,
        grid=(bsz, s_len // tm),
        in_specs=[row_spec(D_MODEL), row_spec(ATT_WIDTH), row_spec(LRU_WIDTH),
                  mod_spec(2), mod_spec(3), mod_spec(4), mod_spec(5), _const_spec(g2.shape),
                  _const_spec(w_out.shape), _const_spec(w_up.shape), _const_spec(cfw.shape),
                  _const_spec(cfb.shape), _const_spec(w_dn.shape)],
        out_specs=[row_spec(D_MODEL), pl.BlockSpec((None, FFN_CONV - 1, 2 * d_ff), lambda b, i: (b, 0, 0))],
        out_shape=[jax.ShapeDtypeStruct((bsz, s_len, D_MODEL), F32),
                   jax.ShapeDtypeStruct((bsz, FFN_CONV - 1, 2 * d_ff), F32)],
        scratch_shapes=[pltpu.VMEM((D_MODEL // LANES, tm, LANES), F32),
                        pltpu.VMEM((D_MODEL // LANES, tm, LANES), F32),
                        pltpu.VMEM((FFN_CONV - 1, SUBLANES, 2 * d_ff), F32)],
        compiler_params=pltpu.CompilerParams(dimension_semantics=("arbitrary", "arbitrary"),
                                             vmem_limit_bytes=VMEM_LIMIT),
        name="ffn_prompt",
    )(x, attn, lru, mod3, mod3, mod3, mod3, g2, w_out, w_up, cfw, cfb, w_dn)


def _inproj_sample_kernel(x_ref, mod_ref, g1_ref, win_ref, gq_ref, gk_ref, gmat_ref,
                          cos_ref, sa_ref, sb_ref, cw_ref, cb_ref, wr_ref, br_ref, wi_ref, bi_ref, lam_ref,
                          st_ref, h0_ref,
                          q_ref, k_ref, v_ref, lo_ref, lc_ref, lh_ref):
    nb = h0_ref.shape[0]
    nt = x_ref.shape[0] // nb
    rep = lambda a: jnp.concatenate([a] * nt, axis=0)
    sh1 = rep(mod_ref[:, 0:D_MODEL])
    sc1 = rep(mod_ref[:, D_MODEL:2 * D_MODEL])
    u = _modulated_norm(x_ref[...], g1_ref[...], sc1, sh1).astype(BF16)
    per_t = lambda r: jnp.concatenate([jnp.broadcast_to(r[t:t + 1, :], (nb, LANES)) for t in range(nt)], axis=0)
    cos, sa, sb = per_t(cos_ref[...]), per_t(sa_ref[...]), per_t(sb_ref[...])
    gmat = gmat_ref[...]

    q = _group_norm_rope(_dot(u, win_ref[:, 0:ATT_WIDTH]), gq_ref[...], gmat, cos, sa, sb)
    q_ref[...] = q * QK_SCALE_LOG2
    k_ref[...] = _group_norm_rope(_dot(u, win_ref[:, ATT_WIDTH:2 * ATT_WIDTH]), gk_ref[...], gmat, cos, sa, sb)
    v_ref[...] = _dot(u, win_ref[:, 2 * ATT_WIDTH:3 * ATT_WIDTH])

    c0 = 3 * ATT_WIDTH
    lx = _dot(u, win_ref[:, c0:c0 + LRU_WIDTH])
    lg = _dot(u, win_ref[:, c0 + LRU_WIDTH:c0 + 2 * LRU_WIDTH])
    pad = [st_ref[i] for i in range(LRU_CONV - 1)] + [lx[t * nb:(t + 1) * nb, :] for t in range(nt)]
    for i in range(LRU_CONV - 1):
        lc_ref[i] = pad[nt + i]
    cw = cw_ref[...]
    xcs = []
    for t in range(nt):
        xc = pad[t] * cw[0:1]
        for kk in range(1, LRU_CONV):
            xc = xc + pad[t + kk] * cw[kk:kk + 1]
        xcs.append(xc + cb_ref[...])
    a, gx = _lru_gates(jnp.concatenate(xcs, axis=0), wr_ref, br_ref[...], wi_ref, bi_ref[...],
                       _neg_c_softplus(lam_ref[...]))
    h = h0_ref[...]
    hs = []
    for t in range(nt):
        h = a[t * nb:(t + 1) * nb, :] * h + gx[t * nb:(t + 1) * nb, :]
        hs.append(h)
    lh_ref[...] = h
    lo_ref[...] = (jnp.concatenate(hs, axis=0) * jax.nn.gelu(lg, approximate=True)).astype(BF16)


def _inproj_sample(x_tb, mod_s, g1, w_in, gq, gk, gmat, tabs_s, cw, cb, wr, br, wi, bi, lam, st, h0):
    m = x_tb.shape[0]
    nb = h0.shape[0]
    act = jax.ShapeDtypeStruct((m, ATT_WIDTH), F32)
    return pl.pallas_call(
        _inproj_sample_kernel,
        out_shape=[act, act, act, jax.ShapeDtypeStruct((m, LRU_WIDTH), BF16),
                   jax.ShapeDtypeStruct((LRU_CONV - 1, nb, LRU_WIDTH), F32),
                   jax.ShapeDtypeStruct((nb, LRU_WIDTH), F32)],
        compiler_params=pltpu.CompilerParams(vmem_limit_bytes=VMEM_LIMIT),
        name="inproj_sample",
    )(x_tb, mod_s, g1, w_in, gq, gk, gmat, *tabs_s, cw, cb, wr, br, wi, bi, lam, st, h0)


def _sample_attention(bounded, q_ref, kn_ref, vn_ref, lam, gs, kt_refs, v_refs):
    n_pages = len(kt_refs)
    page = kt_refs[0].shape[1]
    nq = q_ref.shape[0]
    hrows = 2 * nq

    lane = lax.broadcasted_iota(jnp.int32, (nq, V_HEAD_DIM), 1)
    qh = []
    for h in range(ATT_HEADS):
        q8 = q_ref[:, h * V_HEAD_DIM:(h + 1) * V_HEAD_DIM]
        qh.append(jnp.concatenate([jnp.where(lane < QK_SUB_DIM, q8, 0.0), jnp.where(lane >= QK_SUB_DIM, q8, 0.0)],
                                  axis=0).astype(BF16))

    zpad = jnp.zeros((page - nq, V_HEAD_DIM), F32)
    rt = lax.broadcasted_iota(jnp.int32, (hrows, page), 0) % nq
    ct = lax.broadcasted_iota(jnp.int32, (hrows, page), 1)
    outs = []
    for h in range(ATT_HEADS):
        rows = slice(h * V_HEAD_DIM, (h + 1) * V_HEAD_DIM)
        kt_all = jnp.concatenate([kt_refs[j][rows, :] for j in range(n_pages)], axis=1).astype(BF16)
        k_new = jnp.concatenate([kn_ref[:, rows], zpad], axis=0).astype(BF16)
        s_new = jnp.where(ct <= rt, _dot_nt(qh[h], k_new), NEG)
        s = jnp.concatenate([_dot(qh[h], kt_all), s_new], axis=1)
        if not bounded:
            s = s - jnp.max(s, axis=1, keepdims=True)
        p = jnp.exp2(s)
        l = jnp.sum(p, axis=1, keepdims=True)
        v_all = jnp.concatenate(
            [v_refs[j][pl.ds(h, page, stride=ATT_HEADS), :] for j in range(n_pages)]
            + [vn_ref[:, rows], zpad], axis=0).astype(BF16)
        o = _dot(p.astype(BF16), v_all) / l
        outs.append(_subln(o[0:nq, :] - lam * o[nq:hrows, :], gs))
    return jnp.concatenate(outs, axis=1)


def _ffn_sample_kernel(x_ref, at_ref, lo_ref, mod_ref, g2_ref, wout_ref, wup_ref, cfw_ref, cfb_ref, wdn_ref, st_ref,
                       y_ref, fc_ref):
    nb = st_ref.shape[1]
    nt = x_ref.shape[0] // nb
    d_ff = wdn_ref.shape[0]
    rep = lambda a: jnp.concatenate([a] * nt, axis=0)
    gt1 = rep(mod_ref[:, 2 * D_MODEL:3 * D_MODEL])
    sh2 = rep(mod_ref[:, 3 * D_MODEL:4 * D_MODEL])
    sc2 = rep(mod_ref[:, 4 * D_MODEL:5 * D_MODEL])
    gt2 = rep(mod_ref[:, 5 * D_MODEL:6 * D_MODEL])

    mix = jnp.concatenate([at_ref[...].astype(BF16), lo_ref[...]], axis=1)
    x1 = x_ref[...] + gt1 * _dot(mix, wout_ref[...])
    u2 = _modulated_norm(x1, g2_ref[...], sc2, sh2).astype(BF16)

    acc = jnp.zeros((nt * nb, D_MODEL), F32)
    for j in range(d_ff // FF_CHUNK):
        halves = []
        for base in (0, d_ff):
            c0 = base + j * FF_CHUNK
            cols = slice(c0, c0 + FF_CHUNK)
            up = _dot(u2, wup_ref[:, cols])
            pad = [st_ref[i, :, cols] for i in range(FFN_CONV - 1)] + [up[t * nb:(t + 1) * nb, :] for t in range(nt)]
            for i in range(FFN_CONV - 1):
                fc_ref[i, :, cols] = pad[nt + i]
            w = cfw_ref[:, cols]
            b = cfb_ref[:, cols]
            hcs = []
            for t in range(nt):
                hc = pad[t] * w[0:1]
                for kk in range(1, FFN_CONV):
                    hc = hc + pad[t + kk] * w[kk:kk + 1]
                hcs.append(hc + b)
            halves.append(jnp.concatenate(hcs, axis=0))
        g, val = halves
        hmid = (g * jax.nn.sigmoid(g) * val).astype(BF16)
        acc = acc + _dot(hmid, wdn_ref[j * FF_CHUNK:(j + 1) * FF_CHUNK, :])
    y_ref[...] = x1 + gt2 * acc


def _ffn_sample(x_tb, attn_tb, lru_tb, mod_s, g2, w_out, w_up, cfw, cfb, w_dn, st):
    m = x_tb.shape[0]
    return pl.pallas_call(
        _ffn_sample_kernel,
        out_shape=[jax.ShapeDtypeStruct((m, D_MODEL), F32), jax.ShapeDtypeStruct(st.shape, F32)],
        compiler_params=pltpu.CompilerParams(vmem_limit_bytes=VMEM_LIMIT),
        name="ffn_sample",
    )(x_tb, attn_tb, lru_tb, mod_s, g2, w_out, w_up, cfw, cfb, w_dn, st)


def _block_diag_halves(w):
    n, bd, _ = w.shape
    eye = jnp.eye(n // 2, dtype=w.dtype)
    halves = [jnp.einsum('nij,nm->nimj', w[s * (n // 2):(s + 1) * (n // 2)], eye).reshape(n // 2 * bd, n // 2 * bd)
              for s in range(2)]
    return jnp.stack(halves).astype(BF16)


def kernel(x_prompt, x_sample, cache_k, cache_v, page_table, state_lru_conv, state_lru_h, state_ffn_conv, c_prompt, c_sample, g_norm1, g_norm2, w_ada, b_ada, w_in, g_q, g_k, lam_q1, lam_k1, lam_q2, lam_k2, g_subln, w_out, conv_lru_w, conv_lru_b, w_rgate, b_rgate, w_igate, b_igate, lru_lambda, w_up, conv_ffn_w, conv_ffn_b, w_down):
    depth = w_in.shape[0]
    assert depth == 1, "single-layer step"
    bsz, s_len, _ = x_prompt.shape
    nb, nt, _ = x_sample.shape
    n_pages, page = page_table.shape[1], cache_k.shape[2]
    past_len = n_pages * page
    d_ff = w_down.shape[1]

    w_in_b = w_in[0].astype(BF16)
    w_out_b = w_out[0].astype(BF16)
    w_up_b = w_up[0].astype(BF16)
    w_dn_b = w_down[0].astype(BF16)
    wr = _block_diag_halves(w_rgate[0])
    wi = _block_diag_halves(w_igate[0])
    br = b_rgate[0].reshape(1, LRU_WIDTH)
    bi = b_igate[0].reshape(1, LRU_WIDTH)
    n_grp = ATT_WIDTH // QK_SUB_DIM
    gq = jnp.tile(g_q[0], n_grp)[None, :]
    gk = jnp.tile(g_k[0], n_grp)[None, :]
    grp = jnp.arange(ATT_WIDTH) // QK_SUB_DIM
    gmat = jnp.where(grp[:, None] == grp[None, :], 1.0 / QK_SUB_DIM, 0.0).astype(BF16)
    g1, g2 = g_norm1, g_norm2
    cw, cb = conv_lru_w[0], conv_lru_b
    cfw, cfb = conv_ffn_w[0], conv_ffn_b
    lam = lru_lambda
    lams = (lam_q1, lam_k1, lam_q2, lam_k2)
    gs = g_subln

    n_c = bsz + nb
    n_pad = -n_c % SUBLANES
    c_all = jnp.concatenate([c_prompt, c_sample, jnp.zeros((n_pad, D_MODEL), F32)], axis=0)
    mod = _ada(c_all, w_ada[0], b_ada)
    mod3 = mod[:bsz].reshape(bsz * 6, 1, D_MODEL)
    mod_s = mod[bsz:bsz + nb]

    tabs = _rope_tables(s_len)
    pad_rows = SUBLANES - nt
    tabs_s = tuple(t[past_len:past_len + SUBLANES] for t in tabs)

    q_p, kf_p, kb_p, vf_p, vb_p, lo_p, lc_p, lh_p = _inproj_prompt(
        x_prompt, mod3, g1, w_in_b, gq, gk, gmat, tabs, cw, cb, wr, br, wi, bi, lam)
    x_tb = x_sample.transpose(1, 0, 2).reshape(nt * nb, D_MODEL)
    st_lru = state_lru_conv[0].transpose(1, 0, 2)
    q_s, k_s, v_s, lo_s, lc_s, lh_s = _inproj_sample(
        x_tb, mod_s, g1, w_in_b, gq, gk, gmat, tabs_s, cw, cb, wr, br, wi, bi, lam, st_lru, state_lru_h[0])
    to_bt = lambda a: a.reshape(nt, nb, -1).transpose(1, 0, 2)
    pad_t = lambda a: jnp.pad(a, ((0, 0), (0, pad_rows), (0, 0)))
    k_bt, v_bt = to_bt(k_s), to_bt(v_s)

    score_bound = QK_NORM_BOUND ** 2 * QK_SCALE_LOG2 * jnp.max(jnp.abs(g_q)) * jnp.max(jnp.abs(g_k))
    cache_kt = cache_k[0].transpose(0, 2, 3, 4, 1).reshape(-1, ATT_WIDTH, page)
    cache_vr = cache_v[0].reshape(-1, page * ATT_HEADS, V_HEAD_DIM)
    at_p, at_s = lax.cond(score_bound <= SCORE_LOG2_LIMIT,
                          functools.partial(_attn, True), functools.partial(_attn, False),
                          page_table, q_p, kb_p, vb_p, pad_t(to_bt(q_s)), pad_t(k_bt), pad_t(v_bt),
                          cache_kt, cache_vr, *lams, gs)

    y_p, fc_p = _ffn_prompt(x_prompt, at_p, lo_p, mod3, g2, w_out_b, w_up_b, cfw, cfb, w_dn_b)
    at_tb = at_s[:, :nt].transpose(1, 0, 2).reshape(nt * nb, ATT_WIDTH)
    y_tb, fc_s = _ffn_sample(x_tb, at_tb, lo_s, mod_s, g2, w_out_b, w_up_b, cfw, cfb, w_dn_b,
                             state_ffn_conv[0].transpose(1, 0, 2))

    hd = (ATT_HEADS, 2, QK_SUB_DIM)
    return (y_p, to_bt(y_tb),
            kf_p.reshape(bsz, *hd, s_len).transpose(0, 4, 1, 2, 3)[None], vf_p[None],
            lc_p[None], lh_p.reshape(1, bsz, LRU_WIDTH), fc_p[None],
            k_bt.reshape(1, nb, nt, *hd), v_bt.reshape(1, nb, nt, ATT_HEADS, V_HEAD_DIM),
            lc_s.transpose(1, 0, 2)[None], lh_s[None], fc_s.transpose(1, 0, 2)[None])
```

```python
import functools
import math

import jax
import jax.numpy as jnp
from jax import lax
from jax.experimental import pallas as pl
from jax.experimental.pallas import tpu as pltpu

F32 = jnp.float32
BF16 = jnp.bfloat16

D_MODEL = 1024
ATT_WIDTH = 512
LRU_WIDTH = 512
ATT_HEADS = 4
V_HEAD_DIM = 128
QK_SUB_DIM = 64
ROPE_DIM = 16
ROPE_THETA = 500000.0
LRU_BLOCKS = 8
LRU_C = 8.0
LRU_CONV = 4
FFN_CONV = 3
EPS = 1e-6
LAM_INIT = 0.8 - 0.6 * math.exp(-0.3 * 0)
QK_SCALE = QK_SUB_DIM ** -0.5
QK_SCALE_LOG2 = QK_SCALE * math.log2(math.e)
SCORE_LOG2_LIMIT = 64.0
QK_NORM_BOUND = 1.01 * math.sqrt(QK_SUB_DIM)

LANES = 128
SUBLANES = 8
VMEM_LIMIT = 56 * 1024 * 1024

TM_IN = 512
TQ = 512
TK = 512
TM_FFN = 256
FF_CHUNK = 256
NEG = -1e30


def _dot(a, b):
    return jnp.dot(a, b, preferred_element_type=F32)


def _dot_nt(a, b):
    return lax.dot_general(a, b, (((1,), (1,)), ((), ())), preferred_element_type=F32)


def _const_spec(shape):
    nd = len(shape)
    return pl.BlockSpec(shape, lambda *_: (0,) * nd, pipeline_mode=pl.Buffered(1))


def _modulated_norm(x, g, sc, sh):
    xn = x * lax.rsqrt(jnp.mean(x * x, axis=-1, keepdims=True) + EPS) * g
    return xn * (1.0 + sc) + sh


def _to_segment_major(val, buf_ref):
    tm, w = val.shape
    n = tm // SUBLANES
    for c in range(w // LANES):
        for s in range(SUBLANES):
            for r0 in range(0, n, SUBLANES):
                t0 = s * n + r0
                buf_ref[c, pl.ds(r0 * SUBLANES + s, SUBLANES, stride=SUBLANES), :] = (
                    val[t0:t0 + SUBLANES, c * LANES:(c + 1) * LANES])
    return jnp.concatenate([buf_ref[c] for c in range(w // LANES)], axis=1)


def _to_time_major(val, buf_ref):
    tm, w = val.shape
    n = tm // SUBLANES
    for c in range(w // LANES):
        buf_ref[c] = val[:, c * LANES:(c + 1) * LANES]
    rows = []
    for s in range(SUBLANES):
        for r0 in range(0, n, SUBLANES):
            rows.append(jnp.concatenate(
                [buf_ref[c, pl.ds(r0 * SUBLANES + s, SUBLANES, stride=SUBLANES), :] for c in range(w // LANES)], axis=1))
    return jnp.concatenate(rows, axis=0)


def _tail_groups(cur, k):
    tm = cur.shape[0]
    return [cur[tm - (k - i) * SUBLANES:tm - (k - i - 1) * SUBLANES, :] for i in range(k)]


def _delayed(cur, prev_tail):
    k = len(prev_tail)
    tm = cur.shape[0]
    first = lax.broadcasted_iota(jnp.int32, prev_tail[0].shape, 0) == 0
    heads = [jnp.where(first, pltpu.roll(p, 1, 0), pltpu.roll(c, 1, 0))
             for p, c in zip(prev_tail, _tail_groups(cur, k))]
    return [jnp.concatenate(heads[k - d:] + [cur[:tm - d * SUBLANES, :]], axis=0) for d in range(k, 0, -1)]


def _group_norm_rope(t, g_tiled, gmat, cos, sin_a, sin_b):
    ms = _dot((t * t).astype(BF16), gmat)
    tn = t * lax.rsqrt(ms + EPS) * g_tiled
    outs = []
    for h in range(ATT_WIDTH // LANES):
        th = tn[:, h * LANES:(h + 1) * LANES]
        outs.append(th * cos + pltpu.roll(th, LANES - 8, 1) * sin_a + pltpu.roll(th, 8, 1) * sin_b)
    return jnp.concatenate(outs, axis=1)


def _lru_gates(xc, wr_ref, br, wi_ref, bi, neg_c_softplus):
    half = LRU_WIDTH // 2
    xb = xc.astype(BF16)
    lo, hi = xb[:, :half], xb[:, half:]
    r = jax.nn.sigmoid(jnp.concatenate([_dot(lo, wr_ref[0]), _dot(hi, wr_ref[1])], axis=1) + br)
    ig = jax.nn.sigmoid(jnp.concatenate([_dot(lo, wi_ref[0]), _dot(hi, wi_ref[1])], axis=1) + bi)
    log_a = neg_c_softplus * r
    a = jnp.exp(log_a)
    one_minus_a2 = -jnp.tanh(log_a) * (a * a + 1.0)
    return a, jnp.sqrt(one_minus_a2) * (ig * xc)


def _neg_c_softplus(lam):
    z = -lam
    return -LRU_C * (jnp.maximum(z, 0.0) + jnp.log1p(jnp.exp(-jnp.abs(z))))


def _diff_lambda(lq1, lk1, lq2, lk2):
    s1 = jnp.sum(lq1 * lk1, axis=-1, keepdims=True)
    s2 = jnp.sum(lq2 * lk2, axis=-1, keepdims=True)
    return jnp.exp(s1) - jnp.exp(s2) + LAM_INIT


def _subln(o, g):
    return o * lax.rsqrt(jnp.mean(o * o, axis=-1, keepdims=True) + EPS) * g * (1.0 - LAM_INIT)


def _ada_kernel(c_ref, w_ref, b_ref, o_ref):
    c = c_ref[...]
    s = (c * jax.nn.sigmoid(c)).astype(BF16)
    o_ref[...] = _dot(s, w_ref[...].astype(BF16)) + b_ref[...]


def _ada(c_all, w_ada, b_ada):
    m = c_all.shape[0]
    n = w_ada.shape[1]
    tn = 1024
    return pl.pallas_call(
        _ada_kernel,
        grid=(n // tn,),
        in_specs=[pl.BlockSpec((m, D_MODEL), lambda j: (0, 0)),
                  pl.BlockSpec((D_MODEL, tn), lambda j: (0, j)),
                  pl.BlockSpec((1, tn), lambda j: (0, j))],
        out_specs=pl.BlockSpec((m, tn), lambda j: (0, j)),
        out_shape=jax.ShapeDtypeStruct((m, n), F32),
        compiler_params=pltpu.CompilerParams(dimension_semantics=("arbitrary",), vmem_limit_bytes=VMEM_LIMIT),
        name="ada_mod",
    )(c_all, w_ada, b_ada)


def _rope_table_kernel(freq_ref, ma_ref, mb_ref, c_ref, sa_ref, sb_ref):
    tm = c_ref.shape[0]
    pos = (pl.program_id(0) * tm + lax.broadcasted_iota(jnp.int32, (tm, LANES), 0)).astype(F32)
    ang = pos * freq_ref[...]
    s = jnp.sin(ang)
    c_ref[...] = jnp.cos(ang)
    sa_ref[...] = -s * ma_ref[...]
    sb_ref[...] = s * mb_ref[...]


def _rope_tables(n_pos):
    half = ROPE_DIM // 2
    freqs = ROPE_THETA ** (-jnp.arange(half, dtype=F32) * 2.0 / ROPE_DIM)
    d = jnp.arange(LANES) % QK_SUB_DIM
    freq_lane = jnp.where(d < ROPE_DIM, freqs[d % half], 0.0).astype(F32)[None, :]
    mask_a = (d < half).astype(F32)[None, :]
    mask_b = ((d >= half) & (d < ROPE_DIM)).astype(F32)[None, :]
    tm = 512
    row = pl.BlockSpec((1, LANES), lambda i: (0, 0))
    tab = pl.BlockSpec((tm, LANES), lambda i: (i, 0))
    shp = jax.ShapeDtypeStruct((n_pos, LANES), F32)
    return pl.pallas_call(
        _rope_table_kernel,
        grid=(n_pos // tm,),
        in_specs=[row, row, row],
        out_specs=[tab, tab, tab],
        out_shape=[shp, shp, shp],
        compiler_params=pltpu.CompilerParams(dimension_semantics=("arbitrary",)),
        name="rope_tables",
    )(freq_lane, mask_a, mask_b)


def _inproj_prompt_kernel(x_ref, sh_ref, sc_ref, g1_ref, win_ref, gq_ref, gk_ref, gmat_ref,
                          cos_ref, sa_ref, sb_ref, cw_ref, cb_ref, wr_ref, br_ref, wi_ref, bi_ref, lam_ref,
                          q_ref, kf_ref, kb_ref, vf_ref, vb_ref, lo_ref, lc_ref, lh_ref,
                          lbuf_ref, hbuf_ref, tail_ref, hcar_ref):
    tm = x_ref.shape[0]

    @pl.when(pl.program_id(1) == 0)
    def _():
        tail_ref[...] = jnp.zeros(tail_ref.shape, F32)
        hcar_ref[...] = jnp.zeros(hcar_ref.shape, F32)

    u = _modulated_norm(x_ref[...], g1_ref[...], sc_ref[...], sh_ref[...]).astype(BF16)
    cos, sa, sb = cos_ref[...], sa_ref[...], sb_ref[...]
    gmat = gmat_ref[...]

    q = _group_norm_rope(_dot(u, win_ref[:, 0:ATT_WIDTH]), gq_ref[...], gmat, cos, sa, sb)
    q_ref[...] = (q * QK_SCALE_LOG2).T.astype(BF16)
    k = _group_norm_rope(_dot(u, win_ref[:, ATT_WIDTH:2 * ATT_WIDTH]), gk_ref[...], gmat, cos, sa, sb)
    kf_ref[...] = k.T
    kb_ref[...] = k.astype(BF16)
    v = _dot(u, win_ref[:, 2 * ATT_WIDTH:3 * ATT_WIDTH])
    for h in range(ATT_HEADS):
        vf_ref[:, h, :] = v[:, h * V_HEAD_DIM:(h + 1) * V_HEAD_DIM]
    vb_ref[...] = v.T.astype(BF16)

    c0 = 3 * ATT_WIDTH
    n_tail = LRU_CONV - 1
    lx = _to_segment_major(_dot(u, win_ref[:, c0:c0 + LRU_WIDTH]), lbuf_ref)
    cw = cw_ref[...]
    d3, d2, d1 = _delayed(lx, [tail_ref[i] for i in range(n_tail)])
    xc = d3 * cw[0:1] + d2 * cw[1:2] + d1 * cw[2:3] + lx * cw[3:4] + cb_ref[...]
    for i, grp in enumerate(_tail_groups(lx, n_tail)):
        tail_ref[i] = grp
        lc_ref[i:i + 1, :] = grp[SUBLANES - 1:SUBLANES, :]

    a, gx = _lru_gates(xc, wr_ref, br_ref[...], wi_ref, bi_ref[...], _neg_c_softplus(lam_ref[...]))

    n = tm // SUBLANES
    grp = lambda arr, r: arr[r * SUBLANES:(r + 1) * SUBLANES, :]
    ps, hs = [grp(a, 0)], [grp(gx, 0)]
    for r in range(1, n):
        ar = grp(a, r)
        ps.append(ar * ps[-1])
        hs.append(ar * hs[-1] + grp(gx, r))
    p_end, h_end = ps[-1], hs[-1]
    entering = [hcar_ref[...]]
    for s in range(1, SUBLANES):
        entering.append(p_end[s - 1:s, :] * entering[-1] + h_end[s - 1:s, :])
    h_last = p_end[SUBLANES - 1:SUBLANES, :] * entering[-1] + h_end[SUBLANES - 1:SUBLANES, :]
    hcar_ref[...] = h_last
    lh_ref[...] = h_last
    enter = jnp.concatenate(entering, axis=0)
    states = jnp.concatenate([hs[r] + ps[r] * enter for r in range(n)], axis=0)

    lg = _dot(u, win_ref[:, c0 + LRU_WIDTH:c0 + 2 * LRU_WIDTH])
    lo_ref[...] = (_to_time_major(states, hbuf_ref) * jax.nn.gelu(lg, approximate=True)).astype(BF16)


def _inproj_prompt(x, mod3, g1, w_in, gq, gk, gmat, tabs, cw, cb, wr, br, wi, bi, lam):
    bsz, s_len, _ = x.shape
    tm = TM_IN
    row_spec = lambda w: pl.BlockSpec((None, tm, w), lambda b, i: (b, i, 0))
    tab_spec = pl.BlockSpec((tm, LANES), lambda b, i: (i, 0))
    mod_spec = lambda k: pl.BlockSpec((None, 1, D_MODEL), lambda b, i: (b * 6 + k, 0, 0))
    act = lambda dt: jax.ShapeDtypeStruct((bsz, s_len, ATT_WIDTH), dt)
    act_t = lambda dt: jax.ShapeDtypeStruct((bsz, ATT_WIDTH, s_len), dt)
    col_spec = pl.BlockSpec((None, ATT_WIDTH, tm), lambda b, i: (b, 0, i))
    return pl.pallas_call(
        _inproj_prompt_kernel,
        grid=(bsz, s_len // tm),
        in_specs=[row_spec(D_MODEL), mod_spec(0), mod_spec(1), _const_spec(g1.shape), _const_spec(w_in.shape),
                  _const_spec(gq.shape), _const_spec(gk.shape), _const_spec(gmat.shape),
                  tab_spec, tab_spec, tab_spec,
                  _const_spec(cw.shape), _const_spec(cb.shape), _const_spec(wr.shape), _const_spec(br.shape),
                  _const_spec(wi.shape), _const_spec(bi.shape), _const_spec(lam.shape)],
        out_specs=[col_spec, col_spec, row_spec(ATT_WIDTH),
                   pl.BlockSpec((None, tm, ATT_HEADS, V_HEAD_DIM), lambda b, i: (b, i, 0, 0)),
                   col_spec, row_spec(LRU_WIDTH),
                   pl.BlockSpec((None, LRU_CONV - 1, LRU_WIDTH), lambda b, i: (b, 0, 0)),
                   pl.BlockSpec((None, 1, LRU_WIDTH), lambda b, i: (b, 0, 0))],
        out_shape=[act_t(BF16), act_t(F32), act(BF16),
                   jax.ShapeDtypeStruct((bsz, s_len, ATT_HEADS, V_HEAD_DIM), F32), act_t(BF16), act(BF16),
                   jax.ShapeDtypeStruct((bsz, LRU_CONV - 1, LRU_WIDTH), F32),
                   jax.ShapeDtypeStruct((bsz, 1, LRU_WIDTH), F32)],
        scratch_shapes=[pltpu.VMEM((LRU_WIDTH // LANES, tm, LANES), F32),
                        pltpu.VMEM((LRU_WIDTH // LANES, tm, LANES), F32),
                        pltpu.VMEM((LRU_CONV - 1, SUBLANES, LRU_WIDTH), F32),
                        pltpu.VMEM((1, LRU_WIDTH), F32)],
        compiler_params=pltpu.CompilerParams(dimension_semantics=("arbitrary", "arbitrary"),
                                             vmem_limit_bytes=VMEM_LIMIT),
        name="inproj_prompt",
    )(x, mod3, mod3, g1, w_in, gq, gk, gmat, *tabs, cw, cb, wr, br, wi, bi, lam)


def _page_copies(pt_ref, ckt_hbm, cv_hbm, kbuf_ref, vbuf_ref, sem_ref, seq, slot):
    copies = []
    for j in range(kbuf_ref.shape[1]):
        pg = pt_ref[seq, j]
        copies.append(pltpu.make_async_copy(ckt_hbm.at[pg], kbuf_ref.at[slot, j], sem_ref.at[slot]))
        copies.append(pltpu.make_async_copy(cv_hbm.at[pg], vbuf_ref.at[slot, j], sem_ref.at[slot]))
    return copies


def _attn_kernel(bounded, pt_ref, qt_ref, k_ref, vt_ref, lq1_ref, lk1_ref, lq2_ref, lk2_ref, gs_ref,
                 qs_ref, kn_ref, vn_ref, ckt_hbm, cv_hbm, o_ref, os_ref, kbuf_ref, vbuf_ref, sem_ref):
    n_pages = kbuf_ref.shape[1]
    step = (pl.program_id(0) * pl.num_programs(1) + pl.program_id(1)) * pl.num_programs(2) + pl.program_id(2)
    n_steps = pl.num_programs(0) * pl.num_programs(1) * pl.num_programs(2)
    slot = step % 2

    @pl.when(step == 0)
    def _():
        for cp in _page_copies(pt_ref, ckt_hbm, cv_hbm, kbuf_ref, vbuf_ref, sem_ref, step, slot):
            cp.start()

    @pl.when(step + 1 < n_steps)
    def _():
        for cp in _page_copies(pt_ref, ckt_hbm, cv_hbm, kbuf_ref, vbuf_ref, sem_ref, step + 1, 1 - slot):
            cp.start()

    tq = qt_ref.shape[1]
    qi = pl.program_id(2)
    qt = qt_ref[...].astype(F32)
    feat = lax.broadcasted_iota(jnp.int32, (V_HEAD_DIM, tq), 0)
    qc = (jnp.where(feat < QK_SUB_DIM, qt, 0.0).astype(BF16), jnp.where(feat >= QK_SUB_DIM, qt, 0.0).astype(BF16))

    def tile(j, carry, masked):
        k0 = pl.multiple_of(j * TK, TK)
        ks = k_ref[pl.ds(k0, TK), :]
        vts = vt_ref[:, pl.ds(k0, TK)]
        ss = [_dot(ks, qc[c]) for c in range(2)]
        out = []
        for c in range(2):
            m, l, acc = carry[c]
            s = ss[c]
            if masked:
                keys = lax.broadcasted_iota(jnp.int32, (TK, tq), 0)
                qs = lax.broadcasted_iota(jnp.int32, (TK, tq), 1)
                s = jnp.where(keys <= qs, s, NEG)
            if bounded:
                p = jnp.exp2(s)
                l = l + jnp.sum(p, axis=0, keepdims=True)
                acc = acc + _dot(vts, p.astype(BF16))
            else:
                m_new = jnp.maximum(m, jnp.max(s, axis=0, keepdims=True))
                alpha = jnp.exp2(m - m_new)
                p = jnp.exp2(s - m_new)
                l = alpha * l + jnp.sum(p, axis=0, keepdims=True)
                acc = alpha * acc + _dot(vts, p.astype(BF16))
                m = m_new
            out.append((m, l, acc))
        return tuple(out)

    init = tuple((jnp.full((1, tq), NEG, F32), jnp.zeros((1, tq), F32), jnp.zeros((V_HEAD_DIM, tq), F32))
                 for _ in range(2))
    carry = lax.fori_loop(0, qi, lambda j, c: tile(j, c, False), init)
    (_, l0, a0), (_, l1, a1) = tile(qi, carry, True)
    lam = _diff_lambda(lq1_ref[...], lk1_ref[...], lq2_ref[...], lk2_ref[...])
    ot = a0 / l0 - lam * (a1 / l1)
    o_ref[...] = _subln(ot.T, gs_ref[...]).astype(o_ref.dtype)
    for cp in _page_copies(pt_ref, ckt_hbm, cv_hbm, kbuf_ref, vbuf_ref, sem_ref, step, slot):
        cp.wait()
    kt_refs = [kbuf_ref.at[slot, j] for j in range(n_pages)]
    v_refs = [vbuf_ref.at[slot, j] for j in range(n_pages)]
    os_ref[...] = _sample_attention(bounded, qs_ref, kn_ref, vn_ref, lam, gs_ref[...], kt_refs, v_refs)


def _attn(bounded, page_table, qt, k, vt, q8, kn8, vn8, cache_kt, cache_v, lq1, lk1, lq2, lk2, gs):
    bsz, s_len, _ = k.shape
    nb, n_pages = page_table.shape
    nq = q8.shape[1]
    page = cache_kt.shape[2]
    n_qt = s_len // TQ
    assert nb == bsz * ATT_HEADS * n_qt, "one sample sequence per grid step"
    seq = lambda b, h, i: (b * ATT_HEADS + h) * n_qt + i
    qt_spec = pl.BlockSpec((None, V_HEAD_DIM, TQ), lambda b, h, i, pt: (b, h, i))
    k_spec = pl.BlockSpec((None, s_len, V_HEAD_DIM), lambda b, h, i, pt: (b, 0, h))
    vt_spec = pl.BlockSpec((None, V_HEAD_DIM, s_len), lambda b, h, i, pt: (b, h, 0))
    o_spec = pl.BlockSpec((None, TQ, V_HEAD_DIM), lambda b, h, i, pt: (b, i, h))
    small = lambda a: pl.BlockSpec(a.shape, lambda b, h, i, pt: (0, 0))
    new_spec = pl.BlockSpec((None, nq, ATT_WIDTH), lambda b, h, i, pt: (seq(b, h, i), 0, 0))
    hbm = pl.BlockSpec(memory_space=pl.ANY)
    grid_spec = pltpu.PrefetchScalarGridSpec(
        num_scalar_prefetch=1,
        grid=(bsz, ATT_HEADS, n_qt),
        in_specs=[qt_spec, k_spec, vt_spec, small(lq1), small(lk1), small(lq2), small(lk2), small(gs),
                  new_spec, new_spec, new_spec, hbm, hbm],
        out_specs=[o_spec, new_spec],
        scratch_shapes=[pltpu.VMEM((2, n_pages, ATT_WIDTH, page), F32),
                        pltpu.VMEM((2, n_pages, page * ATT_HEADS, V_HEAD_DIM), F32),
                        pltpu.SemaphoreType.DMA((2,))],
    )
    return pl.pallas_call(
        functools.partial(_attn_kernel, bounded),
        grid_spec=grid_spec,
        out_shape=[jax.ShapeDtypeStruct((bsz, s_len, ATT_WIDTH), BF16),
                   jax.ShapeDtypeStruct((nb, nq, ATT_WIDTH), F32)],
        compiler_params=pltpu.CompilerParams(dimension_semantics=("arbitrary", "arbitrary", "arbitrary"),
                                             vmem_limit_bytes=VMEM_LIMIT),
        name="attn_bounded" if bounded else "attn_general",
    )(page_table, qt, k, vt, lq1, lk1, lq2, lk2, gs, q8, kn8, vn8, cache_kt, cache_v)


def _ffn_prompt_kernel(x_ref, at_ref, lo_ref, gt1_ref, sh2_ref, sc2_ref, gt2_ref, g2_ref,
                       wout_ref, wup_ref, cfw_ref, cfb_ref, wdn_ref,
                       y_ref, fc_ref, ubuf_ref, abuf_ref, tail_ref):
    tm = x_ref.shape[0]
    d_ff = wdn_ref.shape[0]
    n_tail = FFN_CONV - 1

    @pl.when(pl.program_id(1) == 0)
    def _():
        tail_ref[...] = jnp.zeros(tail_ref.shape, F32)

    mix = jnp.concatenate([at_ref[...], lo_ref[...]], axis=1)
    x1 = x_ref[...] + gt1_ref[...] * _dot(mix, wout_ref[...])
    u2 = _to_segment_major(_modulated_norm(x1, g2_ref[...], sc2_ref[...], sh2_ref[...]), ubuf_ref).astype(BF16)

    n_chunks = d_ff // FF_CHUNK
    chunk_cols = lambda j: [slice(base + j * FF_CHUNK, base + (j + 1) * FF_CHUNK) for base in (0, d_ff)]
    up_project = lambda j: [_dot(u2, wup_ref[:, cols]) for cols in chunk_cols(j)]

    def hidden(j, ups):
        halves = []
        for cols, up in zip(chunk_cols(j), ups):
            w = cfw_ref[:, cols]
            d2, d1 = _delayed(up, [tail_ref[i, :, cols] for i in range(n_tail)])
            halves.append(d2 * w[0:1] + d1 * w[1:2] + up * w[2:3] + cfb_ref[:, cols])
            for i, grp in enumerate(_tail_groups(up, n_tail)):
                tail_ref[i, :, cols] = grp
                fc_ref[i:i + 1, cols] = grp[SUBLANES - 1:SUBLANES, :]
        g, val = halves
        return (g * jax.nn.sigmoid(g) * val).astype(BF16)

    acc = jnp.zeros((tm, D_MODEL), F32)
    ups = up_project(0)
    for j in range(n_chunks):
        ups_next = up_project(j + 1) if j + 1 < n_chunks else None
        acc = acc + _dot(hidden(j, ups), wdn_ref[j * FF_CHUNK:(j + 1) * FF_CHUNK, :])
        ups = ups_next
    y_ref[...] = x1 + gt2_ref[...] * _to_time_major(acc, abuf_ref)


def _ffn_prompt(x, attn, lru, mod3, g2, w_out, w_up, cfw, cfb, w_dn):
    bsz, s_len, _ = x.shape
    tm = TM_FFN
    d_ff = w_dn.shape[0]
    row_spec = lambda w: pl.BlockSpec((None, tm, w), lambda b, i: (b, i, 0))
    mod_spec = lambda k: pl.BlockSpec((None, 1, D_MODEL), lambda b, i: (b * 6 + k, 0, 0))
    return pl.pallas_call(
        _ffn_prompt_kernel,
        grid=(bsz, s_len // tm),
        in_specs=[row_spec(D_MODEL), row_spec(ATT_WIDTH), row_spec(LRU_WIDTH),
                  mod_spec(2), mod_spec(3), mod_spec(4), mod_spec(5), _const_spec(g2.shape),
                  _const_spec(w_out.shape), _const_spec(w_up.shape), _const_spec(cfw.shape),
                  _const_spec(cfb.shape), _const_spec(w_dn.shape)],
        out_specs=[row_spec(D_MODEL), pl.BlockSpec((None, FFN_CONV - 1, 2 * d_ff), lambda b, i: (b, 0, 0))],
        out_shape=[jax.ShapeDtypeStruct((bsz, s_len, D_MODEL), F32),
                   jax.ShapeDtypeStruct((bsz, FFN_CONV - 1, 2 * d_ff), F32)],
        scratch_shapes=[pltpu.VMEM((D_MODEL // LANES, tm, LANES), F32),
                        pltpu.VMEM((D_MODEL // LANES, tm, LANES), F32),
                        pltpu.VMEM((FFN_CONV - 1, SUBLANES, 2 * d_ff), F32)],
        compiler_params=pltpu.CompilerParams(dimension_semantics=("arbitrary", "arbitrary"),
                                             vmem_limit_bytes=VMEM_LIMIT),
        name="ffn_prompt",
    )(x, attn, lru, mod3, mod3, mod3, mod3, g2, w_out, w_up, cfw, cfb, w_dn)


def _inproj_sample_kernel(x_ref, mod_ref, g1_ref, win_ref, gq_ref, gk_ref, gmat_ref,
                          cos_ref, sa_ref, sb_ref, cw_ref, cb_ref, wr_ref, br_ref, wi_ref, bi_ref, lam_ref,
                          st_ref, h0_ref,
                          q_ref, k_ref, v_ref, lo_ref, lc_ref, lh_ref):
    nb = h0_ref.shape[0]
    nt = x_ref.shape[0] // nb
    rep = lambda a: jnp.concatenate([a] * nt, axis=0)
    sh1 = rep(mod_ref[:, 0:D_MODEL])
    sc1 = rep(mod_ref[:, D_MODEL:2 * D_MODEL])
    u = _modulated_norm(x_ref[...], g1_ref[...], sc1, sh1).astype(BF16)
    per_t = lambda r: jnp.concatenate([jnp.broadcast_to(r[t:t + 1, :], (nb, LANES)) for t in range(nt)], axis=0)
    cos, sa, sb = per_t(cos_ref[...]), per_t(sa_ref[...]), per_t(sb_ref[...])
    gmat = gmat_ref[...]

    q = _group_norm_rope(_dot(u, win_ref[:, 0:ATT_WIDTH]), gq_ref[...], gmat, cos, sa, sb)
    q_ref[...] = q * QK_SCALE_LOG2
    k_ref[...] = _group_norm_rope(_dot(u, win_ref[:, ATT_WIDTH:2 * ATT_WIDTH]), gk_ref[...], gmat, cos, sa, sb)
    v_ref[...] = _dot(u, win_ref[:, 2 * ATT_WIDTH:3 * ATT_WIDTH])

    c0 = 3 * ATT_WIDTH
    lx = _dot(u, win_ref[:, c0:c0 + LRU_WIDTH])
    lg = _dot(u, win_ref[:, c0 + LRU_WIDTH:c0 + 2 * LRU_WIDTH])
    pad = [st_ref[i] for i in range(LRU_CONV - 1)] + [lx[t * nb:(t + 1) * nb, :] for t in range(nt)]
    for i in range(LRU_CONV - 1):
        lc_ref[i] = pad[nt + i]
    cw = cw_ref[...]
    xcs = []
    for t in range(nt):
        xc = pad[t] * cw[0:1]
        for kk in range(1, LRU_CONV):
            xc = xc + pad[t + kk] * cw[kk:kk + 1]
        xcs.append(xc + cb_ref[...])
    a, gx = _lru_gates(jnp.concatenate(xcs, axis=0), wr_ref, br_ref[...], wi_ref, bi_ref[...],
                       _neg_c_softplus(lam_ref[...]))
    h = h0_ref[...]
    hs = []
    for t in range(nt):
        h = a[t * nb:(t + 1) * nb, :] * h + gx[t * nb:(t + 1) * nb, :]
        hs.append(h)
    lh_ref[...] = h
    lo_ref[...] = (jnp.concatenate(hs, axis=0) * jax.nn.gelu(lg, approximate=True)).astype(BF16)


def _inproj_sample(x_tb, mod_s, g1, w_in, gq, gk, gmat, tabs_s, cw, cb, wr, br, wi, bi, lam, st, h0):
    m = x_tb.shape[0]
    nb = h0.shape[0]
    act = jax.ShapeDtypeStruct((m, ATT_WIDTH), F32)
    return pl.pallas_call(
        _inproj_sample_kernel,
        out_shape=[act, act, act, jax.ShapeDtypeStruct((m, LRU_WIDTH), BF16),
                   jax.ShapeDtypeStruct((LRU_CONV - 1, nb, LRU_WIDTH), F32),
                   jax.ShapeDtypeStruct((nb, LRU_WIDTH), F32)],
        compiler_params=pltpu.CompilerParams(vmem_limit_bytes=VMEM_LIMIT),
        name="inproj_sample",
    )(x_tb, mod_s, g1, w_in, gq, gk, gmat, *tabs_s, cw, cb, wr, br, wi, bi, lam, st, h0)


def _sample_attention(bounded, q_ref, kn_ref, vn_ref, lam, gs, kt_refs, v_refs):
    n_pages = len(kt_refs)
    page = kt_refs[0].shape[1]
    nq = q_ref.shape[0]
    hrows = 2 * nq

    lane = lax.broadcasted_iota(jnp.int32, (nq, V_HEAD_DIM), 1)
    qh = []
    for h in range(ATT_HEADS):
        q8 = q_ref[:, h * V_HEAD_DIM:(h + 1) * V_HEAD_DIM]
        qh.append(jnp.concatenate([jnp.where(lane < QK_SUB_DIM, q8, 0.0), jnp.where(lane >= QK_SUB_DIM, q8, 0.0)],
                                  axis=0).astype(BF16))

    zpad = jnp.zeros((page - nq, V_HEAD_DIM), F32)
    rt = lax.broadcasted_iota(jnp.int32, (hrows, page), 0) % nq
    ct = lax.broadcasted_iota(jnp.int32, (hrows, page), 1)
    outs = []
    for h in range(ATT_HEADS):
        rows = slice(h * V_HEAD_DIM, (h + 1) * V_HEAD_DIM)
        kt_all = jnp.concatenate([kt_refs[j][rows, :] for j in range(n_pages)], axis=1).astype(BF16)
        k_new = jnp.concatenate([kn_ref[:, rows], zpad], axis=0).astype(BF16)
        s_new = jnp.where(ct <= rt, _dot_nt(qh[h], k_new), NEG)
        s = jnp.concatenate([_dot(qh[h], kt_all), s_new], axis=1)
        if not bounded:
            s = s - jnp.max(s, axis=1, keepdims=True)
        p = jnp.exp2(s)
        l = jnp.sum(p, axis=1, keepdims=True)
        v_all = jnp.concatenate(
            [v_refs[j][pl.ds(h, page, stride=ATT_HEADS), :] for j in range(n_pages)]
            + [vn_ref[:, rows], zpad], axis=0).astype(BF16)
        o = _dot(p.astype(BF16), v_all) / l
        outs.append(_subln(o[0:nq, :] - lam * o[nq:hrows, :], gs))
    return jnp.concatenate(outs, axis=1)


def _ffn_sample_kernel(x_ref, at_ref, lo_ref, mod_ref, g2_ref, wout_ref, wup_ref, cfw_ref, cfb_ref, wdn_ref, st_ref,
                       y_ref, fc_ref):
    nb = st_ref.shape[1]
    nt = x_ref.shape[0] // nb
    d_ff = wdn_ref.shape[0]
    rep = lambda a: jnp.concatenate([a] * nt, axis=0)
    gt1 = rep(mod_ref[:, 2 * D_MODEL:3 * D_MODEL])
    sh2 = rep(mod_ref[:, 3 * D_MODEL:4 * D_MODEL])
    sc2 = rep(mod_ref[:, 4 * D_MODEL:5 * D_MODEL])
    gt2 = rep(mod_ref[:, 5 * D_MODEL:6 * D_MODEL])

    mix = jnp.concatenate([at_ref[...].astype(BF16), lo_ref[...]], axis=1)
    x1 = x_ref[...] + gt1 * _dot(mix, wout_ref[...])
    u2 = _modulated_norm(x1, g2_ref[...], sc2, sh2).astype(BF16)

    acc = jnp.zeros((nt * nb, D_MODEL), F32)
    for j in range(d_ff // FF_CHUNK):
        halves = []
        for base in (0, d_ff):
            c0 = base + j * FF_CHUNK
            cols = slice(c0, c0 + FF_CHUNK)
            up = _dot(u2, wup_ref[:, cols])
            pad = [st_ref[i, :, cols] for i in range(FFN_CONV - 1)] + [up[t * nb:(t + 1) * nb, :] for t in range(nt)]
            for i in range(FFN_CONV - 1):
                fc_ref[i, :, cols] = pad[nt + i]
            w = cfw_ref[:, cols]
            b = cfb_ref[:, cols]
            hcs = []
            for t in range(nt):
                hc = pad[t] * w[0:1]
                for kk in range(1, FFN_CONV):
                    hc = hc + pad[t + kk] * w[kk:kk + 1]
                hcs.append(hc + b)
            halves.append(jnp.concatenate(hcs, axis=0))
        g, val = halves
        hmid = (g * jax.nn.sigmoid(g) * val).astype(BF16)
        acc = acc + _dot(hmid, wdn_ref[j * FF_CHUNK:(j + 1) * FF_CHUNK, :])
    y_ref[...] = x1 + gt2 * acc


def _ffn_sample(x_tb, attn_tb, lru_tb, mod_s, g2, w_out, w_up, cfw, cfb, w_dn, st):
    m = x_tb.shape[0]
    return pl.pallas_call(
        _ffn_sample_kernel,
        out_shape=[jax.ShapeDtypeStruct((m, D_MODEL), F32), jax.ShapeDtypeStruct(st.shape, F32)],
        compiler_params=pltpu.CompilerParams(vmem_limit_bytes=VMEM_LIMIT),
        name="ffn_sample",
    )(x_tb, attn_tb, lru_tb, mod_s, g2, w_out, w_up, cfw, cfb, w_dn, st)


def _block_diag_halves(w):
    n, bd, _ = w.shape
    eye = jnp.eye(n // 2, dtype=w.dtype)
    halves = [jnp.einsum('nij,nm->nimj', w[s * (n // 2):(s + 1) * (n // 2)], eye).reshape(n // 2 * bd, n // 2 * bd)
              for s in range(2)]
    return jnp.stack(halves).astype(BF16)


def kernel(x_prompt, x_sample, cache_k, cache_v, page_table, state_lru_conv, state_lru_h, state_ffn_conv, c_prompt, c_sample, g_norm1, g_norm2, w_ada, b_ada, w_in, g_q, g_k, lam_q1, lam_k1, lam_q2, lam_k2, g_subln, w_out, conv_lru_w, conv_lru_b, w_rgate, b_rgate, w_igate, b_igate, lru_lambda, w_up, conv_ffn_w, conv_ffn_b, w_down):
    depth = w_in.shape[0]
    assert depth == 1, "single-layer step"
    bsz, s_len, _ = x_prompt.shape
    nb, nt, _ = x_sample.shape
    n_pages, page = page_table.shape[1], cache_k.shape[2]
    past_len = n_pages * page
    d_ff = w_down.shape[1]

    w_in_b = w_in[0].astype(BF16)
    w_out_b = w_out[0].astype(BF16)
    w_up_b = w_up[0].astype(BF16)
    w_dn_b = w_down[0].astype(BF16)
    wr = _block_diag_halves(w_rgate[0])
    wi = _block_diag_halves(w_igate[0])
    br = b_rgate[0].reshape(1, LRU_WIDTH)
    bi = b_igate[0].reshape(1, LRU_WIDTH)
    n_grp = ATT_WIDTH // QK_SUB_DIM
    gq = jnp.tile(g_q[0], n_grp)[None, :]
    gk = jnp.tile(g_k[0], n_grp)[None, :]
    grp = jnp.arange(ATT_WIDTH) // QK_SUB_DIM
    gmat = jnp.where(grp[:, None] == grp[None, :], 1.0 / QK_SUB_DIM, 0.0).astype(BF16)
    g1, g2 = g_norm1, g_norm2
    cw, cb = conv_lru_w[0], conv_lru_b
    cfw, cfb = conv_ffn_w[0], conv_ffn_b
    lam = lru_lambda
    lams = (lam_q1, lam_k1, lam_q2, lam_k2)
    gs = g_subln

    n_c = bsz + nb
    n_pad = -n_c % SUBLANES
    c_all = jnp.concatenate([c_prompt, c_sample, jnp.zeros((n_pad, D_MODEL), F32)], axis=0)
    mod = _ada(c_all, w_ada[0], b_ada)
    mod3 = mod[:bsz].reshape(bsz * 6, 1, D_MODEL)
    mod_s = mod[bsz:bsz + nb]

    tabs = _rope_tables(s_len)
    pad_rows = SUBLANES - nt
    tabs_s = tuple(t[past_len:past_len + SUBLANES] for t in tabs)

    q_p, kf_p, kb_p, vf_p, vb_p, lo_p, lc_p, lh_p = _inproj_prompt(
        x_prompt, mod3, g1, w_in_b, gq, gk, gmat, tabs, cw, cb, wr, br, wi, bi, lam)
    x_tb = x_sample.transpose(1, 0, 2).reshape(nt * nb, D_MODEL)
    st_lru = state_lru_conv[0].transpose(1, 0, 2)
    q_s, k_s, v_s, lo_s, lc_s, lh_s = _inproj_sample(
        x_tb, mod_s, g1, w_in_b, gq, gk, gmat, tabs_s, cw, cb, wr, br, wi, bi, lam, st_lru, state_lru_h[0])
    to_bt = lambda a: a.reshape(nt, nb, -1).transpose(1, 0, 2)
    pad_t = lambda a: jnp.pad(a, ((0, 0), (0, pad_rows), (0, 0)))
    k_bt, v_bt = to_bt(k_s), to_bt(v_s)

    score_bound = QK_NORM_BOUND ** 2 * QK_SCALE_LOG2 * jnp.max(jnp.abs(g_q)) * jnp.max(jnp.abs(g_k))
    cache_kt = cache_k[0].transpose(0, 2, 3, 4, 1).reshape(-1, ATT_WIDTH, page)
    cache_vr = cache_v[0].reshape(-1, page * ATT_HEADS, V_HEAD_DIM)
    at_p, at_s = lax.cond(score_bound <= SCORE_LOG2_LIMIT,
                          functools.partial(_attn, True), functools.partial(_attn, False),
                          page_table, q_p, kb_p, vb_p, pad_t(to_bt(q_s)), pad_t(k_bt), pad_t(v_bt),
                          cache_kt, cache_vr, *lams, gs)

    y_p, fc_p = _ffn_prompt(x_prompt, at_p, lo_p, mod3, g2, w_out_b, w_up_b, cfw, cfb, w_dn_b)
    at_tb = at_s[:, :nt].transpose(1, 0, 2).reshape(nt * nb, ATT_WIDTH)
    y_tb, fc_s = _ffn_sample(x_tb, at_tb, lo_s, mod_s, g2, w_out_b, w_up_b, cfw, cfb, w_dn_b,
                             state_ffn_conv[0].transpose(1, 0, 2))

    hd = (ATT_HEADS, 2, QK_SUB_DIM)
    return (y_p, to_bt(y_tb),
            kf_p.reshape(bsz, *hd, s_len).transpose(0, 4, 1, 2, 3)[None], vf_p[None],
            lc_p[None], lh_p.reshape(1, bsz, LRU_WIDTH), fc_p[None],
            k_bt.reshape(1, nb, nt, *hd), v_bt.reshape(1, nb, nt, ATT_HEADS, V_HEAD_DIM),
            lc_s.transpose(1, 0, 2)[None], lh_s[None], fc_s.transpose(1, 0, 2)[None])
```

```python
import functools
import math

import jax
import jax.numpy as jnp
from jax import lax
from jax.experimental import pallas as pl
from jax.experimental.pallas import tpu as pltpu

F32 = jnp.float32
BF16 = jnp.bfloat16

D_MODEL = 1024
ATT_WIDTH = 512
LRU_WIDTH = 512
ATT_HEADS = 4
V_HEAD_DIM = 128
QK_SUB_DIM = 64
ROPE_DIM = 16
ROPE_THETA = 500000.0
LRU_BLOCKS = 8
LRU_C = 8.0
LRU_CONV = 4
FFN_CONV = 3
EPS = 1e-6
LAM_INIT = 0.8 - 0.6 * math.exp(-0.3 * 0)
QK_SCALE = QK_SUB_DIM ** -0.5
QK_SCALE_LOG2 = QK_SCALE * math.log2(math.e)
SCORE_LOG2_LIMIT = 64.0
QK_NORM_BOUND = 1.01 * math.sqrt(QK_SUB_DIM)

LANES = 128
SUBLANES = 8
VMEM_LIMIT = 56 * 1024 * 1024

TM_IN = 512
TQ = 512
TK = 512
TM_FFN = 256
FF_CHUNK = 256
NEG = -1e30


def _dot(a, b):
    return jnp.dot(a, b, preferred_element_type=F32)


def _dot_nt(a, b):
    return lax.dot_general(a, b, (((1,), (1,)), ((), ())), preferred_element_type=F32)


def _const_spec(shape):
    nd = len(shape)
    return pl.BlockSpec(shape, lambda *_: (0,) * nd, pipeline_mode=pl.Buffered(1))


def _modulated_norm(x, g, sc, sh):
    xn = x * lax.rsqrt(jnp.mean(x * x, axis=-1, keepdims=True) + EPS) * g
    return xn * (1.0 + sc) + sh


def _to_segment_major(val, buf_ref):
    tm, w = val.shape
    n = tm // SUBLANES
    for c in range(w // LANES):
        for s in range(SUBLANES):
            for r0 in range(0, n, SUBLANES):
                t0 = s * n + r0
                buf_ref[c, pl.ds(r0 * SUBLANES + s, SUBLANES, stride=SUBLANES), :] = (
                    val[t0:t0 + SUBLANES, c * LANES:(c + 1) * LANES])
    return jnp.concatenate([buf_ref[c] for c in range(w // LANES)], axis=1)


def _to_time_major(val, buf_ref):
    tm, w = val.shape
    n = tm // SUBLANES
    for c in range(w // LANES):
        buf_ref[c] = val[:, c * LANES:(c + 1) * LANES]
    rows = []
    for s in range(SUBLANES):
        for r0 in range(0, n, SUBLANES):
            rows.append(jnp.concatenate(
                [buf_ref[c, pl.ds(r0 * SUBLANES + s, SUBLANES, stride=SUBLANES), :] for c in range(w // LANES)], axis=1))
    return jnp.concatenate(rows, axis=0)


def _tail_groups(cur, k):
    tm = cur.shape[0]
    return [cur[tm - (k - i) * SUBLANES:tm - (k - i - 1) * SUBLANES, :] for i in range(k)]


def _delayed(cur, prev_tail):
    k = len(prev_tail)
    tm = cur.shape[0]
    first = lax.broadcasted_iota(jnp.int32, prev_tail[0].shape, 0) == 0
    heads = [jnp.where(first, pltpu.roll(p, 1, 0), pltpu.roll(c, 1, 0))
             for p, c in zip(prev_tail, _tail_groups(cur, k))]
    return [jnp.concatenate(heads[k - d:] + [cur[:tm - d * SUBLANES, :]], axis=0) for d in range(k, 0, -1)]


def _group_norm_rope(t, g_tiled, gmat, cos, sin_a, sin_b):
    ms = _dot((t * t).astype(BF16), gmat)
    tn = t * lax.rsqrt(ms + EPS) * g_tiled
    outs = []
    for h in range(ATT_WIDTH // LANES):
        th = tn[:, h * LANES:(h + 1) * LANES]
        outs.append(th * cos + pltpu.roll(th, LANES - 8, 1) * sin_a + pltpu.roll(th, 8, 1) * sin_b)
    return jnp.concatenate(outs, axis=1)


def _lru_gates(xc, wr_ref, br, wi_ref, bi, neg_c_softplus):
    half = LRU_WIDTH // 2
    xb = xc.astype(BF16)
    lo, hi = xb[:, :half], xb[:, half:]
    r = jax.nn.sigmoid(jnp.concatenate([_dot(lo, wr_ref[0]), _dot(hi, wr_ref[1])], axis=1) + br)
    ig = jax.nn.sigmoid(jnp.concatenate([_dot(lo, wi_ref[0]), _dot(hi, wi_ref[1])], axis=1) + bi)
    log_a = neg_c_softplus * r
    a = jnp.exp(log_a)
    one_minus_a2 = -jnp.tanh(log_a) * (a * a + 1.0)
    return a, jnp.sqrt(one_minus_a2) * (ig * xc)


def _neg_c_softplus(lam):
    z = -lam
    return -LRU_C * (jnp.maximum(z, 0.0) + jnp.log1p(jnp.exp(-jnp.abs(z))))


def _diff_lambda(lq1, lk1, lq2, lk2):
    s1 = jnp.sum(lq1 * lk1, axis=-1, keepdims=True)
    s2 = jnp.sum(lq2 * lk2, axis=-1, keepdims=True)
    return jnp.exp(s1) - jnp.exp(s2) + LAM_INIT


def _subln(o, g):
    return o * lax.rsqrt(jnp.mean(o * o, axis=-1, keepdims=True) + EPS) * g * (1.0 - LAM_INIT)


def _ada_kernel(c_ref, w_ref, b_ref, o_ref):
    c = c_ref[...]
    s = (c * jax.nn.sigmoid(c)).astype(BF16)
    o_ref[...] = _dot(s, w_ref[...].astype(BF16)) + b_ref[...]


def _ada(c_all, w_ada, b_ada):
    m = c_all.shape[0]
    n = w_ada.shape[1]
    tn = 1024
    return pl.pallas_call(
        _ada_kernel,
        grid=(n // tn,),
        in_specs=[pl.BlockSpec((m, D_MODEL), lambda j: (0, 0)),
                  pl.BlockSpec((D_MODEL, tn), lambda j: (0, j)),
                  pl.BlockSpec((1, tn), lambda j: (0, j))],
        out_specs=pl.BlockSpec((m, tn), lambda j: (0, j)),
        out_shape=jax.ShapeDtypeStruct((m, n), F32),
        compiler_params=pltpu.CompilerParams(dimension_semantics=("arbitrary",), vmem_limit_bytes=VMEM_LIMIT),
        name="ada_mod",
    )(c_all, w_ada, b_ada)


def _rope_table_kernel(freq_ref, ma_ref, mb_ref, c_ref, sa_ref, sb_ref):
    tm = c_ref.shape[0]
    pos = (pl.program_id(0) * tm + lax.broadcasted_iota(jnp.int32, (tm, LANES), 0)).astype(F32)
    ang = pos * freq_ref[...]
    s = jnp.sin(ang)
    c_ref[...] = jnp.cos(ang)
    sa_ref[...] = -s * ma_ref[...]
    sb_ref[...] = s * mb_ref[...]


def _rope_tables(n_pos):
    half = ROPE_DIM // 2
    freqs = ROPE_THETA ** (-jnp.arange(half, dtype=F32) * 2.0 / ROPE_DIM)
    d = jnp.arange(LANES) % QK_SUB_DIM
    freq_lane = jnp.where(d < ROPE_DIM, freqs[d % half], 0.0).astype(F32)[None, :]
    mask_a = (d < half).astype(F32)[None, :]
    mask_b = ((d >= half) & (d < ROPE_DIM)).astype(F32)[None, :]
    tm = 512
    row = pl.BlockSpec((1, LANES), lambda i: (0, 0))
    tab = pl.BlockSpec((tm, LANES), lambda i: (i, 0))
    shp = jax.ShapeDtypeStruct((n_pos, LANES), F32)
    return pl.pallas_call(
        _rope_table_kernel,
        grid=(n_pos // tm,),
        in_specs=[row, row, row],
        out_specs=[tab, tab, tab],
        out_shape=[shp, shp, shp],
        compiler_params=pltpu.CompilerParams(dimension_semantics=("arbitrary",)),
        name="rope_tables",
    )(freq_lane, mask_a, mask_b)


def _inproj_prompt_kernel(x_ref, sh_ref, sc_ref, g1_ref, win_ref, gq_ref, gk_ref, gmat_ref,
                          cos_ref, sa_ref, sb_ref, cw_ref, cb_ref, wr_ref, br_ref, wi_ref, bi_ref, lam_ref,
                          q_ref, kf_ref, kb_ref, vf_ref, vb_ref, lo_ref, lc_ref, lh_ref,
                          lbuf_ref, hbuf_ref, tail_ref, hcar_ref):
    tm = x_ref.shape[0]

    @pl.when(pl.program_id(1) == 0)
    def _():
        tail_ref[...] = jnp.zeros(tail_ref.shape, F32)
        hcar_ref[...] = jnp.zeros(hcar_ref.shape, F32)

    u = _modulated_norm(x_ref[...], g1_ref[...], sc_ref[...], sh_ref[...]).astype(BF16)
    cos, sa, sb = cos_ref[...], sa_ref[...], sb_ref[...]
    gmat = gmat_ref[...]

    q = _group_norm_rope(_dot(u, win_ref[:, 0:ATT_WIDTH]), gq_ref[...], gmat, cos, sa, sb)
    q_ref[...] = (q * QK_SCALE_LOG2).T.astype(BF16)
    k = _group_norm_rope(_dot(u, win_ref[:, ATT_WIDTH:2 * ATT_WIDTH]), gk_ref[...], gmat, cos, sa, sb)
    kf_ref[...] = k.T
    kb_ref[...] = k.astype(BF16)
    v = _dot(u, win_ref[:, 2 * ATT_WIDTH:3 * ATT_WIDTH])
    for h in range(ATT_HEADS):
        vf_ref[pl.ds(h, tm, stride=ATT_HEADS), :] = v[:, h * V_HEAD_DIM:(h + 1) * V_HEAD_DIM]
    vb_ref[...] = v.T.astype(BF16)

    c0 = 3 * ATT_WIDTH
    n_tail = LRU_CONV - 1
    lx = _to_segment_major(_dot(u, win_ref[:, c0:c0 + LRU_WIDTH]), lbuf_ref)
    cw = cw_ref[...]
    d3, d2, d1 = _delayed(lx, [tail_ref[i] for i in range(n_tail)])
    xc = d3 * cw[0:1] + d2 * cw[1:2] + d1 * cw[2:3] + lx * cw[3:4] + cb_ref[...]
    for i, grp in enumerate(_tail_groups(lx, n_tail)):
        tail_ref[i] = grp
        lc_ref[i:i + 1, :] = grp[SUBLANES - 1:SUBLANES, :]

    a, gx = _lru_gates(xc, wr_ref, br_ref[...], wi_ref, bi_ref[...], _neg_c_softplus(lam_ref[...]))

    n = tm // SUBLANES
    grp = lambda arr, r: arr[r * SUBLANES:(r + 1) * SUBLANES, :]
    ps, hs = [grp(a, 0)], [grp(gx, 0)]
    for r in range(1, n):
        ar = grp(a, r)
        ps.append(ar * ps[-1])
        hs.append(ar * hs[-1] + grp(gx, r))
    p_end, h_end = ps[-1], hs[-1]
    entering = [hcar_ref[...]]
    for s in range(1, SUBLANES):
        entering.append(p_end[s - 1:s, :] * entering[-1] + h_end[s - 1:s, :])
    h_last = p_end[SUBLANES - 1:SUBLANES, :] * entering[-1] + h_end[SUBLANES - 1:SUBLANES, :]
    hcar_ref[...] = h_last
    lh_ref[...] = h_last
    enter = jnp.concatenate(entering, axis=0)
    states = jnp.concatenate([hs[r] + ps[r] * enter for r in range(n)], axis=0)

    lg = _dot(u, win_ref[:, c0 + LRU_WIDTH:c0 + 2 * LRU_WIDTH])
    lo_ref[...] = (_to_time_major(states, hbuf_ref) * jax.nn.gelu(lg, approximate=True)).astype(BF16)


def _inproj_prompt(x, mod3, g1, w_in, gq, gk, gmat, tabs, cw, cb, wr, br, wi, bi, lam):
    bsz, s_len, _ = x.shape
    tm = TM_IN
    row_spec = lambda w: pl.BlockSpec((None, tm, w), lambda b, i: (b, i, 0))
    tab_spec = pl.BlockSpec((tm, LANES), lambda b, i: (i, 0))
    mod_spec = lambda k: pl.BlockSpec((None, 1, D_MODEL), lambda b, i: (b * 6 + k, 0, 0))
    act = lambda dt: jax.ShapeDtypeStruct((bsz, s_len, ATT_WIDTH), dt)
    act_t = lambda dt: jax.ShapeDtypeStruct((bsz, ATT_WIDTH, s_len), dt)
    col_spec = pl.BlockSpec((None, ATT_WIDTH, tm), lambda b, i: (b, 0, i))
    return pl.pallas_call(
        _inproj_prompt_kernel,
        grid=(bsz, s_len // tm),
        in_specs=[row_spec(D_MODEL), mod_spec(0), mod_spec(1), _const_spec(g1.shape), _const_spec(w_in.shape),
                  _const_spec(gq.shape), _const_spec(gk.shape), _const_spec(gmat.shape),
                  tab_spec, tab_spec, tab_spec,
                  _const_spec(cw.shape), _const_spec(cb.shape), _const_spec(wr.shape), _const_spec(br.shape),
                  _const_spec(wi.shape), _const_spec(bi.shape), _const_spec(lam.shape)],
        out_specs=[col_spec, col_spec, row_spec(ATT_WIDTH),
                   pl.BlockSpec((None, tm * ATT_HEADS, V_HEAD_DIM), lambda b, i: (b, i, 0)),
                   col_spec, row_spec(LRU_WIDTH),
                   pl.BlockSpec((None, LRU_CONV - 1, LRU_WIDTH), lambda b, i: (b, 0, 0)),
                   pl.BlockSpec((None, 1, LRU_WIDTH), lambda b, i: (b, 0, 0))],
        out_shape=[act_t(BF16), act_t(F32), act(BF16),
                   jax.ShapeDtypeStruct((bsz, s_len * ATT_HEADS, V_HEAD_DIM), F32), act_t(BF16), act(BF16),
                   jax.ShapeDtypeStruct((bsz, LRU_CONV - 1, LRU_WIDTH), F32),
                   jax.ShapeDtypeStruct((bsz, 1, LRU_WIDTH), F32)],
        scratch_shapes=[pltpu.VMEM((LRU_WIDTH // LANES, tm, LANES), F32),
                        pltpu.VMEM((LRU_WIDTH // LANES, tm, LANES), F32),
                        pltpu.VMEM((LRU_CONV - 1, SUBLANES, LRU_WIDTH), F32),
                        pltpu.VMEM((1, LRU_WIDTH), F32)],
        compiler_params=pltpu.CompilerParams(dimension_semantics=("arbitrary", "arbitrary"),
                                             vmem_limit_bytes=VMEM_LIMIT),
        name="inproj_prompt",
    )(x, mod3, mod3, g1, w_in, gq, gk, gmat, *tabs, cw, cb, wr, br, wi, bi, lam)


def _page_copies(pt_ref, ckt_hbm, cv_hbm, kbuf_ref, vbuf_ref, sem_ref, seq, slot):
    copies = []
    for j in range(kbuf_ref.shape[1]):
        pg = pt_ref[seq, j]
        copies.append(pltpu.make_async_copy(ckt_hbm.at[pg], kbuf_ref.at[slot, j], sem_ref.at[slot]))
        copies.append(pltpu.make_async_copy(cv_hbm.at[pg], vbuf_ref.at[slot, j], sem_ref.at[slot]))
    return copies


def _attn_kernel(bounded, pt_ref, qt_ref, k_ref, vt_ref, lq1_ref, lk1_ref, lq2_ref, lk2_ref, gs_ref,
                 qs_ref, kn_ref, vn_ref, ckt_hbm, cv_hbm, o_ref, os_ref, kbuf_ref, vbuf_ref, sem_ref):
    n_pages = kbuf_ref.shape[1]
    step = (pl.program_id(0) * pl.num_programs(1) + pl.program_id(1)) * pl.num_programs(2) + pl.program_id(2)
    n_steps = pl.num_programs(0) * pl.num_programs(1) * pl.num_programs(2)
    slot = step % 2

    @pl.when(step == 0)
    def _():
        for cp in _page_copies(pt_ref, ckt_hbm, cv_hbm, kbuf_ref, vbuf_ref, sem_ref, step, slot):
            cp.start()

    @pl.when(step + 1 < n_steps)
    def _():
        for cp in _page_copies(pt_ref, ckt_hbm, cv_hbm, kbuf_ref, vbuf_ref, sem_ref, step + 1, 1 - slot):
            cp.start()

    tq = qt_ref.shape[1]
    qi = pl.program_id(2)
    qt = qt_ref[...].astype(F32)
    feat = lax.broadcasted_iota(jnp.int32, (V_HEAD_DIM, tq), 0)
    qc = (jnp.where(feat < QK_SUB_DIM, qt, 0.0).astype(BF16), jnp.where(feat >= QK_SUB_DIM, qt, 0.0).astype(BF16))

    def tile(j, carry, masked):
        k0 = pl.multiple_of(j * TK, TK)
        ks = k_ref[pl.ds(k0, TK), :]
        vts = vt_ref[:, pl.ds(k0, TK)]
        ss = [_dot(ks, qc[c]) for c in range(2)]
        out = []
        for c in range(2):
            m, l, acc = carry[c]
            s = ss[c]
            if masked:
                keys = lax.broadcasted_iota(jnp.int32, (TK, tq), 0)
                qs = lax.broadcasted_iota(jnp.int32, (TK, tq), 1)
                s = jnp.where(keys <= qs, s, NEG)
            if bounded:
                p = jnp.exp2(s)
                l = l + jnp.sum(p, axis=0, keepdims=True)
                acc = acc + _dot(vts, p.astype(BF16))
            else:
                m_new = jnp.maximum(m, jnp.max(s, axis=0, keepdims=True))
                alpha = jnp.exp2(m - m_new)
                p = jnp.exp2(s - m_new)
                l = alpha * l + jnp.sum(p, axis=0, keepdims=True)
                acc = alpha * acc + _dot(vts, p.astype(BF16))
                m = m_new
            out.append((m, l, acc))
        return tuple(out)

    init = tuple((jnp.full((1, tq), NEG, F32), jnp.zeros((1, tq), F32), jnp.zeros((V_HEAD_DIM, tq), F32))
                 for _ in range(2))
    carry = lax.fori_loop(0, qi, lambda j, c: tile(j, c, False), init)
    (_, l0, a0), (_, l1, a1) = tile(qi, carry, True)
    lam = _diff_lambda(lq1_ref[...], lk1_ref[...], lq2_ref[...], lk2_ref[...])
    ot = a0 / l0 - lam * (a1 / l1)
    o_ref[...] = _subln(ot.T, gs_ref[...]).astype(o_ref.dtype)
    for cp in _page_copies(pt_ref, ckt_hbm, cv_hbm, kbuf_ref, vbuf_ref, sem_ref, step, slot):
        cp.wait()
    kt_refs = [kbuf_ref.at[slot, j] for j in range(n_pages)]
    v_refs = [vbuf_ref.at[slot, j] for j in range(n_pages)]
    os_ref[...] = _sample_attention(bounded, qs_ref, kn_ref, vn_ref, lam, gs_ref[...], kt_refs, v_refs)


def _attn(bounded, page_table, qt, k, vt, q8, kn8, vn8, cache_kt, cache_v, lq1, lk1, lq2, lk2, gs):
    bsz, s_len, _ = k.shape
    nb, n_pages = page_table.shape
    nq = q8.shape[1]
    page = cache_kt.shape[2]
    n_qt = s_len // TQ
    assert nb == bsz * ATT_HEADS * n_qt, "one sample sequence per grid step"
    seq = lambda b, h, i: (b * ATT_HEADS + h) * n_qt + i
    qt_spec = pl.BlockSpec((None, V_HEAD_DIM, TQ), lambda b, h, i, pt: (b, h, i))
    k_spec = pl.BlockSpec((None, s_len, V_HEAD_DIM), lambda b, h, i, pt: (b, 0, h))
    vt_spec = pl.BlockSpec((None, V_HEAD_DIM, s_len), lambda b, h, i, pt: (b, h, 0))
    o_spec = pl.BlockSpec((None, TQ, V_HEAD_DIM), lambda b, h, i, pt: (b, i, h))
    small = lambda a: pl.BlockSpec(a.shape, lambda b, h, i, pt: (0, 0))
    new_spec = pl.BlockSpec((None, nq, ATT_WIDTH), lambda b, h, i, pt: (seq(b, h, i), 0, 0))
    hbm = pl.BlockSpec(memory_space=pl.ANY)
    grid_spec = pltpu.PrefetchScalarGridSpec(
        num_scalar_prefetch=1,
        grid=(bsz, ATT_HEADS, n_qt),
        in_specs=[qt_spec, k_spec, vt_spec, small(lq1), small(lk1), small(lq2), small(lk2), small(gs),
                  new_spec, new_spec, new_spec, hbm, hbm],
        out_specs=[o_spec, new_spec],
        scratch_shapes=[pltpu.VMEM((2, n_pages, ATT_WIDTH, page), F32),
                        pltpu.VMEM((2, n_pages, page * ATT_HEADS, V_HEAD_DIM), F32),
                        pltpu.SemaphoreType.DMA((2,))],
    )
    return pl.pallas_call(
        functools.partial(_attn_kernel, bounded),
        grid_spec=grid_spec,
        out_shape=[jax.ShapeDtypeStruct((bsz, s_len, ATT_WIDTH), BF16),
                   jax.ShapeDtypeStruct((nb, nq, ATT_WIDTH), F32)],
        compiler_params=pltpu.CompilerParams(dimension_semantics=("arbitrary", "arbitrary", "arbitrary"),
                                             vmem_limit_bytes=VMEM_LIMIT),
        name="attn_bounded" if bounded else "attn_general",
    )(page_table, qt, k, vt, lq1, lk1, lq2, lk2, gs, q8, kn8, vn8, cache_kt, cache_v)


def _ffn_prompt_kernel(x_ref, at_ref, lo_ref, gt1_ref, sh2_ref, sc2_ref, gt2_ref, g2_ref,
                       wout_ref, wup_ref, cfw_ref, cfb_ref, wdn_ref,
                       y_ref, fc_ref, ubuf_ref, abuf_ref, tail_ref):
    tm = x_ref.shape[0]
    d_ff = wdn_ref.shape[0]
    n_tail = FFN_CONV - 1

    @pl.when(pl.program_id(1) == 0)
    def _():
        tail_ref[...] = jnp.zeros(tail_ref.shape, F32)

    mix = jnp.concatenate([at_ref[...], lo_ref[...]], axis=1)
    x1 = x_ref[...] + gt1_ref[...] * _dot(mix, wout_ref[...])
    u2 = _to_segment_major(_modulated_norm(x1, g2_ref[...], sc2_ref[...], sh2_ref[...]), ubuf_ref).astype(BF16)

    n_chunks = d_ff // FF_CHUNK
    chunk_cols = lambda j: [slice(base + j * FF_CHUNK, base + (j + 1) * FF_CHUNK) for base in (0, d_ff)]
    up_project = lambda j: [_dot(u2, wup_ref[:, cols]) for cols in chunk_cols(j)]

    def hidden(j, ups):
        halves = []
        for cols, up in zip(chunk_cols(j), ups):
            w = cfw_ref[:, cols]
            d2, d1 = _delayed(up, [tail_ref[i, :, cols] for i in range(n_tail)])
            halves.append(d2 * w[0:1] + d1 * w[1:2] + up * w[2:3] + cfb_ref[:, cols])
            for i, grp in enumerate(_tail_groups(up, n_tail)):
                tail_ref[i, :, cols] = grp
                fc_ref[i:i + 1, cols] = grp[SUBLANES - 1:SUBLANES, :]
        g, val = halves
        return (g * jax.nn.sigmoid(g) * val).astype(BF16)

    acc = jnp.zeros((tm, D_MODEL), F32)
    ups = up_project(0)
    for j in range(n_chunks):
        ups_next = up_project(j + 1) if j + 1 < n_chunks else None
        acc = acc + _dot(hidden(j, ups), wdn_ref[j * FF_CHUNK:(j + 1) * FF_CHUNK, :])
        ups = ups_next
    y_ref[...] = x1 + gt2_ref[...] * _to_time_major(acc, abuf_ref)


def _ffn_prompt(x, attn, lru, mod3, g2, w_out, w_up, cfw, cfb, w_dn):
    bsz, s_len, _ = x.shape
    tm = TM_FFN
    d_ff = w_dn.shape[0]
    row_spec = lambda w: pl.BlockSpec((None, tm, w), lambda b, i: (b, i, 0))
    mod_spec = lambda k: pl.BlockSpec((None, 1, D_MODEL), lambda b, i: (b * 6 + k, 0, 0))
    return pl.pallas_call(
        _ffn_prompt_kernel,
        grid=(bsz, s_len // tm),
        in_specs=[row_spec(D_MODEL), row_spec(ATT_WIDTH), row_spec(LRU_WIDTH),
                  mod_spec(2), mod_spec(3), mod_spec(4), mod_spec(5), _const_spec(g2.shape),
                  _const_spec(w_out.shape), _const_spec(w_up.shape), _const_spec(cfw.shape),
                  _const_spec(cfb.shape), _const_spec(w_dn.shape)],
        out_specs=[row_spec(D_MODEL), pl.BlockSpec((None, FFN_CONV - 1, 2 * d_ff), lambda b, i: (b, 0, 0))],
        out_shape=[jax.ShapeDtypeStruct((bsz, s_len, D_MODEL), F32),
                   jax.ShapeDtypeStruct((bsz, FFN_CONV - 1, 2 * d_ff), F32)],
        scratch_shapes=[pltpu.VMEM((D_MODEL // LANES, tm, LANES), F32),
                        pltpu.VMEM((D_MODEL // LANES, tm, LANES), F32),
                        pltpu.VMEM((FFN_CONV - 1, SUBLANES, 2 * d_ff), F32)],
        compiler_params=pltpu.CompilerParams(dimension_semantics=("arbitrary", "arbitrary"),
                                             vmem_limit_bytes=VMEM_LIMIT),
        name="ffn_prompt",
    )(x, attn, lru, mod3, mod3, mod3, mod3, g2, w_out, w_up, cfw, cfb, w_dn)


def _inproj_sample_kernel(x_ref, mod_ref, g1_ref, win_ref, gq_ref, gk_ref, gmat_ref,
                          cos_ref, sa_ref, sb_ref, cw_ref, cb_ref, wr_ref, br_ref, wi_ref, bi_ref, lam_ref,
                          st_ref, h0_ref,
                          q_ref, k_ref, v_ref, lo_ref, lc_ref, lh_ref):
    nb = h0_ref.shape[0]
    nt = x_ref.shape[0] // nb
    rep = lambda a: jnp.concatenate([a] * nt, axis=0)
    sh1 = rep(mod_ref[:, 0:D_MODEL])
    sc1 = rep(mod_ref[:, D_MODEL:2 * D_MODEL])
    u = _modulated_norm(x_ref[...], g1_ref[...], sc1, sh1).astype(BF16)
    per_t = lambda r: jnp.concatenate([jnp.broadcast_to(r[t:t + 1, :], (nb, LANES)) for t in range(nt)], axis=0)
    cos, sa, sb = per_t(cos_ref[...]), per_t(sa_ref[...]), per_t(sb_ref[...])
    gmat = gmat_ref[...]

    q = _group_norm_rope(_dot(u, win_ref[:, 0:ATT_WIDTH]), gq_ref[...], gmat, cos, sa, sb)
    q_ref[...] = q * QK_SCALE_LOG2
    k_ref[...] = _group_norm_rope(_dot(u, win_ref[:, ATT_WIDTH:2 * ATT_WIDTH]), gk_ref[...], gmat, cos, sa, sb)
    v_ref[...] = _dot(u, win_ref[:, 2 * ATT_WIDTH:3 * ATT_WIDTH])

    c0 = 3 * ATT_WIDTH
    lx = _dot(u, win_ref[:, c0:c0 + LRU_WIDTH])
    lg = _dot(u, win_ref[:, c0 + LRU_WIDTH:c0 + 2 * LRU_WIDTH])
    pad = [st_ref[i] for i in range(LRU_CONV - 1)] + [lx[t * nb:(t + 1) * nb, :] for t in range(nt)]
    for i in range(LRU_CONV - 1):
        lc_ref[i] = pad[nt + i]
    cw = cw_ref[...]
    xcs = []
    for t in range(nt):
        xc = pad[t] * cw[0:1]
        for kk in range(1, LRU_CONV):
            xc = xc + pad[t + kk] * cw[kk:kk + 1]
        xcs.append(xc + cb_ref[...])
    a, gx = _lru_gates(jnp.concatenate(xcs, axis=0), wr_ref, br_ref[...], wi_ref, bi_ref[...],
                       _neg_c_softplus(lam_ref[...]))
    h = h0_ref[...]
    hs = []
    for t in range(nt):
        h = a[t * nb:(t + 1) * nb, :] * h + gx[t * nb:(t + 1) * nb, :]
        hs.append(h)
    lh_ref[...] = h
    lo_ref[...] = (jnp.concatenate(hs, axis=0) * jax.nn.gelu(lg, approximate=True)).astype(BF16)


def _inproj_sample(x_tb, mod_s, g1, w_in, gq, gk, gmat, tabs_s, cw, cb, wr, br, wi, bi, lam, st, h0):
    m = x_tb.shape[0]
    nb = h0.shape[0]
    act = jax.ShapeDtypeStruct((m, ATT_WIDTH), F32)
    return pl.pallas_call(
        _inproj_sample_kernel,
        out_shape=[act, act, act, jax.ShapeDtypeStruct((m, LRU_WIDTH), BF16),
                   jax.ShapeDtypeStruct((LRU_CONV - 1, nb, LRU_WIDTH), F32),
                   jax.ShapeDtypeStruct((nb, LRU_WIDTH), F32)],
        compiler_params=pltpu.CompilerParams(vmem_limit_bytes=VMEM_LIMIT),
        name="inproj_sample",
    )(x_tb, mod_s, g1, w_in, gq, gk, gmat, *tabs_s, cw, cb, wr, br, wi, bi, lam, st, h0)


def _sample_attention(bounded, q_ref, kn_ref, vn_ref, lam, gs, kt_refs, v_refs):
    n_pages = len(kt_refs)
    page = kt_refs[0].shape[1]
    nq = q_ref.shape[0]
    hrows = 2 * nq

    lane = lax.broadcasted_iota(jnp.int32, (nq, V_HEAD_DIM), 1)
    qh = []
    for h in range(ATT_HEADS):
        q8 = q_ref[:, h * V_HEAD_DIM:(h + 1) * V_HEAD_DIM]
        qh.append(jnp.concatenate([jnp.where(lane < QK_SUB_DIM, q8, 0.0), jnp.where(lane >= QK_SUB_DIM, q8, 0.0)],
                                  axis=0).astype(BF16))

    zpad = jnp.zeros((page - nq, V_HEAD_DIM), F32)
    rt = lax.broadcasted_iota(jnp.int32, (hrows, page), 0) % nq
    ct = lax.broadcasted_iota(jnp.int32, (hrows, page), 1)
    ps, ls = [], []
    for h in range(ATT_HEADS):
        rows = slice(h * V_HEAD_DIM, (h + 1) * V_HEAD_DIM)
        kt_all = jnp.concatenate([kt_refs[j][rows, :] for j in range(n_pages)], axis=1).astype(BF16)
        k_new = jnp.concatenate([kn_ref[:, rows], zpad], axis=0).astype(BF16)
        s_new = jnp.where(ct <= rt, _dot_nt(qh[h], k_new), NEG)
        s = jnp.concatenate([_dot(qh[h], kt_all), s_new], axis=1)
        if not bounded:
            s = s - jnp.max(s, axis=1, keepdims=True)
        p = jnp.exp2(s)
        ls.append(jnp.sum(p, axis=1, keepdims=True))
        ps.append(p.astype(BF16))
    outs = []
    for h in range(0, ATT_HEADS, 2):
        v_pair = jnp.concatenate([jnp.concatenate(
            [v_refs[j][pl.ds(hh, page, stride=ATT_HEADS), :] for j in range(n_pages)]
            + [vn_ref[:, hh * V_HEAD_DIM:(hh + 1) * V_HEAD_DIM], zpad], axis=0) for hh in (h, h + 1)],
            axis=1).astype(BF16)
        o_pair = _dot(jnp.concatenate([ps[h], ps[h + 1]], axis=0), v_pair)
        for i, hh in enumerate((h, h + 1)):
            o = o_pair[i * hrows:(i + 1) * hrows, i * V_HEAD_DIM:(i + 1) * V_HEAD_DIM] / ls[hh]
            outs.append(_subln(o[0:nq, :] - lam * o[nq:hrows, :], gs))
    return jnp.concatenate(outs, axis=1)


def _ffn_sample_kernel(x_ref, at_ref, lo_ref, mod_ref, g2_ref, wout_ref, wup_ref, cfw_ref, cfb_ref, wdn_ref, st_ref,
                       y_ref, fc_ref):
    nb = st_ref.shape[1]
    nt = x_ref.shape[0] // nb
    d_ff = wdn_ref.shape[0]
    rep = lambda a: jnp.concatenate([a] * nt, axis=0)
    gt1 = rep(mod_ref[:, 2 * D_MODEL:3 * D_MODEL])
    sh2 = rep(mod_ref[:, 3 * D_MODEL:4 * D_MODEL])
    sc2 = rep(mod_ref[:, 4 * D_MODEL:5 * D_MODEL])
    gt2 = rep(mod_ref[:, 5 * D_MODEL:6 * D_MODEL])

    mix = jnp.concatenate([at_ref[...].astype(BF16), lo_ref[...]], axis=1)
    x1 = x_ref[...] + gt1 * _dot(mix, wout_ref[...])
    u2 = _modulated_norm(x1, g2_ref[...], sc2, sh2).astype(BF16)

    acc = jnp.zeros((nt * nb, D_MODEL), F32)
    for j in range(d_ff // FF_CHUNK):
        halves = []
        for base in (0, d_ff):
            c0 = base + j * FF_CHUNK
            cols = slice(c0, c0 + FF_CHUNK)
            up = _dot(u2, wup_ref[:, cols])
            pad = [st_ref[i, :, cols] for i in range(FFN_CONV - 1)] + [up[t * nb:(t + 1) * nb, :] for t in range(nt)]
            for i in range(FFN_CONV - 1):
                fc_ref[i, :, cols] = pad[nt + i]
            w = cfw_ref[:, cols]
            b = cfb_ref[:, cols]
            hcs = []
            for t in range(nt):
                hc = pad[t] * w[0:1]
                for kk in range(1, FFN_CONV):
                    hc = hc + pad[t + kk] * w[kk:kk + 1]
                hcs.append(hc + b)
            halves.append(jnp.concatenate(hcs, axis=0))
        g, val = halves
        hmid = (g * jax.nn.sigmoid(g) * val).astype(BF16)
        acc = acc + _dot(hmid, wdn_ref[j * FF_CHUNK:(j + 1) * FF_CHUNK, :])
    y_ref[...] = x1 + gt2 * acc


def _ffn_sample(x_tb, attn_tb, lru_tb, mod_s, g2, w_out, w_up, cfw, cfb, w_dn, st):
    m = x_tb.shape[0]
    return pl.pallas_call(
        _ffn_sample_kernel,
        out_shape=[jax.ShapeDtypeStruct((m, D_MODEL), F32), jax.ShapeDtypeStruct(st.shape, F32)],
        compiler_params=pltpu.CompilerParams(vmem_limit_bytes=VMEM_LIMIT),
        name="ffn_sample",
    )(x_tb, attn_tb, lru_tb, mod_s, g2, w_out, w_up, cfw, cfb, w_dn, st)


def _block_diag_halves(w):
    n, bd, _ = w.shape
    eye = jnp.eye(n // 2, dtype=w.dtype)
    halves = [jnp.einsum('nij,nm->nimj', w[s * (n // 2):(s + 1) * (n // 2)], eye).reshape(n // 2 * bd, n // 2 * bd)
              for s in range(2)]
    return jnp.stack(halves).astype(BF16)


def kernel(x_prompt, x_sample, cache_k, cache_v, page_table, state_lru_conv, state_lru_h, state_ffn_conv, c_prompt, c_sample, g_norm1, g_norm2, w_ada, b_ada, w_in, g_q, g_k, lam_q1, lam_k1, lam_q2, lam_k2, g_subln, w_out, conv_lru_w, conv_lru_b, w_rgate, b_rgate, w_igate, b_igate, lru_lambda, w_up, conv_ffn_w, conv_ffn_b, w_down):
    depth = w_in.shape[0]
    assert depth == 1, "single-layer step"
    bsz, s_len, _ = x_prompt.shape
    nb, nt, _ = x_sample.shape
    n_pages, page = page_table.shape[1], cache_k.shape[2]
    past_len = n_pages * page
    d_ff = w_down.shape[1]

    w_in_b = w_in[0].astype(BF16)
    w_out_b = w_out[0].astype(BF16)
    w_up_b = w_up[0].astype(BF16)
    w_dn_b = w_down[0].astype(BF16)
    wr = _block_diag_halves(w_rgate[0])
    wi = _block_diag_halves(w_igate[0])
    br = b_rgate[0].reshape(1, LRU_WIDTH)
    bi = b_igate[0].reshape(1, LRU_WIDTH)
    n_grp = ATT_WIDTH // QK_SUB_DIM
    gq = jnp.tile(g_q[0], n_grp)[None, :]
    gk = jnp.tile(g_k[0], n_grp)[None, :]
    grp = jnp.arange(ATT_WIDTH) // QK_SUB_DIM
    gmat = jnp.where(grp[:, None] == grp[None, :], 1.0 / QK_SUB_DIM, 0.0).astype(BF16)
    g1, g2 = g_norm1, g_norm2
    cw, cb = conv_lru_w[0], conv_lru_b
    cfw, cfb = conv_ffn_w[0], conv_ffn_b
    lam = lru_lambda
    lams = (lam_q1, lam_k1, lam_q2, lam_k2)
    gs = g_subln

    n_c = bsz + nb
    n_pad = -n_c % SUBLANES
    c_all = jnp.concatenate([c_prompt, c_sample, jnp.zeros((n_pad, D_MODEL), F32)], axis=0)
    mod = _ada(c_all, w_ada[0], b_ada)
    mod3 = mod[:bsz].reshape(bsz * 6, 1, D_MODEL)
    mod_s = mod[bsz:bsz + nb]

    tabs = _rope_tables(s_len)
    pad_rows = SUBLANES - nt
    tabs_s = tuple(t[past_len:past_len + SUBLANES] for t in tabs)

    q_p, kf_p, kb_p, vf_p, vb_p, lo_p, lc_p, lh_p = _inproj_prompt(
        x_prompt, mod3, g1, w_in_b, gq, gk, gmat, tabs, cw, cb, wr, br, wi, bi, lam)
    x_tb = x_sample.transpose(1, 0, 2).reshape(nt * nb, D_MODEL)
    st_lru = state_lru_conv[0].transpose(1, 0, 2)
    q_s, k_s, v_s, lo_s, lc_s, lh_s = _inproj_sample(
        x_tb, mod_s, g1, w_in_b, gq, gk, gmat, tabs_s, cw, cb, wr, br, wi, bi, lam, st_lru, state_lru_h[0])
    to_bt = lambda a: a.reshape(nt, nb, -1).transpose(1, 0, 2)
    pad_t = lambda a: jnp.pad(a, ((0, 0), (0, pad_rows), (0, 0)))
    k_bt, v_bt = to_bt(k_s), to_bt(v_s)

    score_bound = QK_NORM_BOUND ** 2 * QK_SCALE_LOG2 * jnp.max(jnp.abs(g_q)) * jnp.max(jnp.abs(g_k))
    cache_kt = cache_k[0].transpose(0, 2, 3, 4, 1).reshape(-1, ATT_WIDTH, page)
    cache_vr = cache_v[0].reshape(-1, page * ATT_HEADS, V_HEAD_DIM)
    at_p, at_s = lax.cond(score_bound <= SCORE_LOG2_LIMIT,
                          functools.partial(_attn, True), functools.partial(_attn, False),
                          page_table, q_p, kb_p, vb_p, pad_t(to_bt(q_s)), pad_t(k_bt), pad_t(v_bt),
                          cache_kt, cache_vr, *lams, gs)

    y_p, fc_p = _ffn_prompt(x_prompt, at_p, lo_p, mod3, g2, w_out_b, w_up_b, cfw, cfb, w_dn_b)
    at_tb = at_s[:, :nt].transpose(1, 0, 2).reshape(nt * nb, ATT_WIDTH)
    y_tb, fc_s = _ffn_sample(x_tb, at_tb, lo_s, mod_s, g2, w_out_b, w_up_b, cfw, cfb, w_dn_b,
                             state_ffn_conv[0].transpose(1, 0, 2))

    hd = (ATT_HEADS, 2, QK_SUB_DIM)
    return (y_p, to_bt(y_tb),
            kf_p.reshape(bsz, *hd, s_len).transpose(0, 4, 1, 2, 3)[None],
            vf_p.reshape(1, bsz, s_len, ATT_HEADS, V_HEAD_DIM),
            lc_p[None], lh_p.reshape(1, bsz, LRU_WIDTH), fc_p[None],
            k_bt.reshape(1, nb, nt, *hd), v_bt.reshape(1, nb, nt, ATT_HEADS, V_HEAD_DIM),
            lc_s.transpose(1, 0, 2)[None], lh_s[None], fc_s.transpose(1, 0, 2)[None])
```

```python
import functools
import math

import jax
import jax.numpy as jnp
from jax import lax
from jax.experimental import pallas as pl
from jax.experimental.pallas import tpu as pltpu

F32 = jnp.float32
BF16 = jnp.bfloat16

D_MODEL = 1024
ATT_WIDTH = 512
LRU_WIDTH = 512
ATT_HEADS = 4
V_HEAD_DIM = 128
QK_SUB_DIM = 64
ROPE_DIM = 16
ROPE_THETA = 500000.0
LRU_BLOCKS = 8
LRU_C = 8.0
LRU_CONV = 4
FFN_CONV = 3
EPS = 1e-6
LAM_INIT = 0.8 - 0.6 * math.exp(-0.3 * 0)
QK_SCALE = QK_SUB_DIM ** -0.5
QK_SCALE_LOG2 = QK_SCALE * math.log2(math.e)
SCORE_LOG2_LIMIT = 64.0
QK_NORM_BOUND = 1.01 * math.sqrt(QK_SUB_DIM)

LANES = 128
SUBLANES = 8
VMEM_LIMIT = 56 * 1024 * 1024

TM_IN = 512
TQ = 512
TK = 512
TM_FFN = 256
FF_CHUNK = 256
NEG = -1e30


def _dot(a, b):
    return jnp.dot(a, b, preferred_element_type=F32)


def _dot_nt(a, b):
    return lax.dot_general(a, b, (((1,), (1,)), ((), ())), preferred_element_type=F32)


def _const_spec(shape):
    nd = len(shape)
    return pl.BlockSpec(shape, lambda *_: (0,) * nd, pipeline_mode=pl.Buffered(1))


def _prompt_mod_spec(mod, k):
    return pl.BlockSpec((SUBLANES, D_MODEL), lambda b, i: (mod.shape[0] // SUBLANES - 1, k))


def _seq_row(ref):
    return ref[pl.ds(pl.program_id(0), 1), :]


def _modulated_norm(x, g, sc, sh):
    xn = x * lax.rsqrt(jnp.mean(x * x, axis=-1, keepdims=True) + EPS) * g
    return xn * (1.0 + sc) + sh


def _to_segment_major(val, buf_ref):
    tm, w = val.shape
    n = tm // SUBLANES
    for c in range(w // LANES):
        for s in range(SUBLANES):
            for r0 in range(0, n, SUBLANES):
                t0 = s * n + r0
                buf_ref[c, pl.ds(r0 * SUBLANES + s, SUBLANES, stride=SUBLANES), :] = (
                    val[t0:t0 + SUBLANES, c * LANES:(c + 1) * LANES])
    return jnp.concatenate([buf_ref[c] for c in range(w // LANES)], axis=1)


def _to_time_major(val, buf_ref):
    tm, w = val.shape
    n = tm // SUBLANES
    for c in range(w // LANES):
        buf_ref[c] = val[:, c * LANES:(c + 1) * LANES]
    rows = []
    for s in range(SUBLANES):
        for r0 in range(0, n, SUBLANES):
            rows.append(jnp.concatenate(
                [buf_ref[c, pl.ds(r0 * SUBLANES + s, SUBLANES, stride=SUBLANES), :] for c in range(w // LANES)], axis=1))
    return jnp.concatenate(rows, axis=0)


def _tail_groups(cur, k):
    tm = cur.shape[0]
    return [cur[tm - (k - i) * SUBLANES:tm - (k - i - 1) * SUBLANES, :] for i in range(k)]


def _delayed(cur, prev_tail):
    k = len(prev_tail)
    tm = cur.shape[0]
    first = lax.broadcasted_iota(jnp.int32, prev_tail[0].shape, 0) == 0
    heads = [jnp.where(first, pltpu.roll(p, 1, 0), pltpu.roll(c, 1, 0))
             for p, c in zip(prev_tail, _tail_groups(cur, k))]
    return [jnp.concatenate(heads[k - d:] + [cur[:tm - d * SUBLANES, :]], axis=0) for d in range(k, 0, -1)]


def _group_norm_rope(t, g_tiled, gmat, cos, sin_a, sin_b):
    ms = _dot((t * t).astype(BF16), gmat)
    tn = t * lax.rsqrt(ms + EPS) * g_tiled
    outs = []
    for h in range(ATT_WIDTH // LANES):
        th = tn[:, h * LANES:(h + 1) * LANES]
        outs.append(th * cos + pltpu.roll(th, LANES - 8, 1) * sin_a + pltpu.roll(th, 8, 1) * sin_b)
    return jnp.concatenate(outs, axis=1)


def _lru_gates(xc, wr_ref, br, wi_ref, bi, neg_c_softplus):
    half = LRU_WIDTH // 2
    xb = xc.astype(BF16)
    lo, hi = xb[:, :half], xb[:, half:]
    r = jax.nn.sigmoid(jnp.concatenate([_dot(lo, wr_ref[0]), _dot(hi, wr_ref[1])], axis=1) + br)
    ig = jax.nn.sigmoid(jnp.concatenate([_dot(lo, wi_ref[0]), _dot(hi, wi_ref[1])], axis=1) + bi)
    log_a = neg_c_softplus * r
    a = jnp.exp(log_a)
    one_minus_a2 = -jnp.tanh(log_a) * (a * a + 1.0)
    return a, jnp.sqrt(one_minus_a2) * (ig * xc)


def _neg_c_softplus(lam):
    z = -lam
    return -LRU_C * (jnp.maximum(z, 0.0) + jnp.log1p(jnp.exp(-jnp.abs(z))))


def _diff_lambda(lq1, lk1, lq2, lk2):
    s1 = jnp.sum(lq1 * lk1, axis=-1, keepdims=True)
    s2 = jnp.sum(lq2 * lk2, axis=-1, keepdims=True)
    return jnp.exp(s1) - jnp.exp(s2) + LAM_INIT


def _subln(o, g):
    return o * lax.rsqrt(jnp.mean(o * o, axis=-1, keepdims=True) + EPS) * g * (1.0 - LAM_INIT)


def _ada_kernel(c_ref, w_ref, b_ref, o_ref):
    c = c_ref[...]
    s = (c * jax.nn.sigmoid(c)).astype(BF16)
    o_ref[...] = _dot(s, w_ref[...].astype(BF16)) + b_ref[...]


def _ada(c_all, w_ada, b_ada):
    m = c_all.shape[0]
    n = w_ada.shape[1]
    tn = 1024
    return pl.pallas_call(
        _ada_kernel,
        grid=(n // tn,),
        in_specs=[pl.BlockSpec((m, D_MODEL), lambda j: (0, 0)),
                  pl.BlockSpec((D_MODEL, tn), lambda j: (0, j)),
                  pl.BlockSpec((1, tn), lambda j: (0, j))],
        out_specs=pl.BlockSpec((m, tn), lambda j: (0, j)),
        out_shape=jax.ShapeDtypeStruct((m, n), F32),
        compiler_params=pltpu.CompilerParams(dimension_semantics=("arbitrary",), vmem_limit_bytes=VMEM_LIMIT),
        name="ada_mod",
    )(c_all, w_ada, b_ada)


def _rope_table_kernel(freq_ref, ma_ref, mb_ref, c_ref, sa_ref, sb_ref):
    tm = c_ref.shape[0]
    pos = (pl.program_id(0) * tm + lax.broadcasted_iota(jnp.int32, (tm, LANES), 0)).astype(F32)
    ang = pos * freq_ref[...]
    s = jnp.sin(ang)
    c_ref[...] = jnp.cos(ang)
    sa_ref[...] = -s * ma_ref[...]
    sb_ref[...] = s * mb_ref[...]


def _rope_tables(n_pos):
    half = ROPE_DIM // 2
    freqs = ROPE_THETA ** (-jnp.arange(half, dtype=F32) * 2.0 / ROPE_DIM)
    d = jnp.arange(LANES) % QK_SUB_DIM
    freq_lane = jnp.where(d < ROPE_DIM, freqs[d % half], 0.0).astype(F32)[None, :]
    mask_a = (d < half).astype(F32)[None, :]
    mask_b = ((d >= half) & (d < ROPE_DIM)).astype(F32)[None, :]
    tm = 512
    row = pl.BlockSpec((1, LANES), lambda i: (0, 0))
    tab = pl.BlockSpec((tm, LANES), lambda i: (i, 0))
    shp = jax.ShapeDtypeStruct((n_pos, LANES), F32)
    return pl.pallas_call(
        _rope_table_kernel,
        grid=(n_pos // tm,),
        in_specs=[row, row, row],
        out_specs=[tab, tab, tab],
        out_shape=[shp, shp, shp],
        compiler_params=pltpu.CompilerParams(dimension_semantics=("arbitrary",)),
        name="rope_tables",
    )(freq_lane, mask_a, mask_b)


def _inproj_prompt_kernel(x_ref, sh_ref, sc_ref, g1_ref, win_ref, gq_ref, gk_ref, gmat_ref,
                          cos_ref, sa_ref, sb_ref, cw_ref, cb_ref, wr_ref, br_ref, wi_ref, bi_ref, lam_ref,
                          q_ref, kf_ref, kb_ref, vf_ref, vb_ref, lo_ref, lc_ref, lh_ref,
                          lbuf_ref, hbuf_ref, tail_ref, hcar_ref):
    tm = x_ref.shape[0]

    @pl.when(pl.program_id(1) == 0)
    def _():
        tail_ref[...] = jnp.zeros(tail_ref.shape, F32)
        hcar_ref[...] = jnp.zeros(hcar_ref.shape, F32)

    u = _modulated_norm(x_ref[...], g1_ref[...], _seq_row(sc_ref), _seq_row(sh_ref)).astype(BF16)
    cos, sa, sb = cos_ref[...], sa_ref[...], sb_ref[...]
    gmat = gmat_ref[...]

    q = _group_norm_rope(_dot(u, win_ref[:, 0:ATT_WIDTH]), gq_ref[...], gmat, cos, sa, sb)
    q_ref[...] = (q * QK_SCALE_LOG2).T.astype(BF16)
    k = _group_norm_rope(_dot(u, win_ref[:, ATT_WIDTH:2 * ATT_WIDTH]), gk_ref[...], gmat, cos, sa, sb)
    kf_ref[...] = k.T
    kb_ref[...] = k.astype(BF16)
    v = _dot(u, win_ref[:, 2 * ATT_WIDTH:3 * ATT_WIDTH])
    for h in range(ATT_HEADS):
        vf_ref[pl.ds(h, tm, stride=ATT_HEADS), :] = v[:, h * V_HEAD_DIM:(h + 1) * V_HEAD_DIM]
    vb_ref[...] = v.T.astype(BF16)

    c0 = 3 * ATT_WIDTH
    n_tail = LRU_CONV - 1
    lx = _to_segment_major(_dot(u, win_ref[:, c0:c0 + LRU_WIDTH]), lbuf_ref)
    cw = cw_ref[...]
    d3, d2, d1 = _delayed(lx, [tail_ref[i] for i in range(n_tail)])
    xc = d3 * cw[0:1] + d2 * cw[1:2] + d1 * cw[2:3] + lx * cw[3:4] + cb_ref[...]
    for i, grp in enumerate(_tail_groups(lx, n_tail)):
        tail_ref[i] = grp
        lc_ref[i:i + 1, :] = grp[SUBLANES - 1:SUBLANES, :]

    a, gx = _lru_gates(xc, wr_ref, br_ref[...], wi_ref, bi_ref[...], _neg_c_softplus(lam_ref[...]))

    n = tm // SUBLANES
    grp = lambda arr, r: arr[r * SUBLANES:(r + 1) * SUBLANES, :]
    ps, hs = [grp(a, 0)], [grp(gx, 0)]
    for r in range(1, n):
        ar = grp(a, r)
        ps.append(ar * ps[-1])
        hs.append(ar * hs[-1] + grp(gx, r))
    p_end, h_end = ps[-1], hs[-1]
    entering = [hcar_ref[...]]
    for s in range(1, SUBLANES):
        entering.append(p_end[s - 1:s, :] * entering[-1] + h_end[s - 1:s, :])
    h_last = p_end[SUBLANES - 1:SUBLANES, :] * entering[-1] + h_end[SUBLANES - 1:SUBLANES, :]
    hcar_ref[...] = h_last
    lh_ref[...] = h_last
    enter = jnp.concatenate(entering, axis=0)
    states = jnp.concatenate([hs[r] + ps[r] * enter for r in range(n)], axis=0)

    lg = _dot(u, win_ref[:, c0 + LRU_WIDTH:c0 + 2 * LRU_WIDTH])
    lo_ref[...] = (_to_time_major(states, hbuf_ref) * jax.nn.gelu(lg, approximate=True)).astype(BF16)


def _inproj_prompt(x, mod, g1, w_in, gq, gk, gmat, tabs, cw, cb, wr, br, wi, bi, lam):
    bsz, s_len, _ = x.shape
    tm = TM_IN
    row_spec = lambda w: pl.BlockSpec((None, tm, w), lambda b, i: (b, i, 0))
    tab_spec = pl.BlockSpec((tm, LANES), lambda b, i: (i, 0))
    mod_spec = lambda k: _prompt_mod_spec(mod, k)
    act = lambda dt: jax.ShapeDtypeStruct((bsz, s_len, ATT_WIDTH), dt)
    act_t = lambda dt: jax.ShapeDtypeStruct((bsz, ATT_WIDTH, s_len), dt)
    col_spec = pl.BlockSpec((None, ATT_WIDTH, tm), lambda b, i: (b, 0, i))
    return pl.pallas_call(
        _inproj_prompt_kernel,
        grid=(bsz, s_len // tm),
        in_specs=[row_spec(D_MODEL), mod_spec(0), mod_spec(1), _const_spec(g1.shape), _const_spec(w_in.shape),
                  _const_spec(gq.shape), _const_spec(gk.shape), _const_spec(gmat.shape),
                  tab_spec, tab_spec, tab_spec,
                  _const_spec(cw.shape), _const_spec(cb.shape), _const_spec(wr.shape), _const_spec(br.shape),
                  _const_spec(wi.shape), _const_spec(bi.shape), _const_spec(lam.shape)],
        out_specs=[col_spec, col_spec, row_spec(ATT_WIDTH),
                   pl.BlockSpec((None, tm * ATT_HEADS, V_HEAD_DIM), lambda b, i: (b, i, 0)),
                   col_spec, row_spec(LRU_WIDTH),
                   pl.BlockSpec((None, LRU_CONV - 1, LRU_WIDTH), lambda b, i: (b, 0, 0)),
                   pl.BlockSpec((None, 1, LRU_WIDTH), lambda b, i: (b, 0, 0))],
        out_shape=[act_t(BF16), act_t(F32), act(BF16),
                   jax.ShapeDtypeStruct((bsz, s_len * ATT_HEADS, V_HEAD_DIM), F32), act_t(BF16), act(BF16),
                   jax.ShapeDtypeStruct((bsz, LRU_CONV - 1, LRU_WIDTH), F32),
                   jax.ShapeDtypeStruct((bsz, 1, LRU_WIDTH), F32)],
        scratch_shapes=[pltpu.VMEM((LRU_WIDTH // LANES, tm, LANES), F32),
                        pltpu.VMEM((LRU_WIDTH // LANES, tm, LANES), F32),
                        pltpu.VMEM((LRU_CONV - 1, SUBLANES, LRU_WIDTH), F32),
                        pltpu.VMEM((1, LRU_WIDTH), F32)],
        compiler_params=pltpu.CompilerParams(dimension_semantics=("arbitrary", "arbitrary"),
                                             vmem_limit_bytes=VMEM_LIMIT),
        name="inproj_prompt",
    )(x, mod, mod, g1, w_in, gq, gk, gmat, *tabs, cw, cb, wr, br, wi, bi, lam)


def _page_copies(pt_ref, ckt_hbm, cv_hbm, kbuf_ref, vbuf_ref, sem_ref, seq, slot):
    copies = []
    for j in range(kbuf_ref.shape[1]):
        pg = pt_ref[seq, j]
        copies.append(pltpu.make_async_copy(ckt_hbm.at[pg], kbuf_ref.at[slot, j], sem_ref.at[slot]))
        copies.append(pltpu.make_async_copy(cv_hbm.at[pg], vbuf_ref.at[slot, j], sem_ref.at[slot]))
    return copies


def _attn_kernel(bounded, pt_ref, qt_ref, k_ref, vt_ref, lq1_ref, lk1_ref, lq2_ref, lk2_ref, gs_ref,
                 qs_ref, kn_ref, vn_ref, ckt_hbm, cv_hbm, o_ref, os_ref, kbuf_ref, vbuf_ref, sem_ref):
    n_pages = kbuf_ref.shape[1]
    step = (pl.program_id(0) * pl.num_programs(1) + pl.program_id(1)) * pl.num_programs(2) + pl.program_id(2)
    n_steps = pl.num_programs(0) * pl.num_programs(1) * pl.num_programs(2)
    slot = step % 2

    @pl.when(step == 0)
    def _():
        for cp in _page_copies(pt_ref, ckt_hbm, cv_hbm, kbuf_ref, vbuf_ref, sem_ref, step, slot):
            cp.start()

    @pl.when(step + 1 < n_steps)
    def _():
        for cp in _page_copies(pt_ref, ckt_hbm, cv_hbm, kbuf_ref, vbuf_ref, sem_ref, step + 1, 1 - slot):
            cp.start()

    tq = qt_ref.shape[1]
    qi = pl.program_id(2)
    qt = qt_ref[...].astype(F32)
    feat = lax.broadcasted_iota(jnp.int32, (V_HEAD_DIM, tq), 0)
    qc = (jnp.where(feat < QK_SUB_DIM, qt, 0.0).astype(BF16), jnp.where(feat >= QK_SUB_DIM, qt, 0.0).astype(BF16))

    def tile(k0, width, carry, masked):
        k0 = pl.multiple_of(k0, TK)
        ks = k_ref[pl.ds(k0, width), :]
        vts = vt_ref[:, pl.ds(k0, width)]
        ss = [_dot(ks, qc[c]) for c in range(2)]
        out = []
        for c in range(2):
            m, l, acc = carry[c]
            s = ss[c]
            if masked:
                keys = lax.broadcasted_iota(jnp.int32, (width, tq), 0)
                qs = lax.broadcasted_iota(jnp.int32, (width, tq), 1)
                s = jnp.where(keys <= qs, s, NEG)
            if bounded:
                p = jnp.exp2(s)
                l = l + jnp.sum(p, axis=0, keepdims=True)
                acc = acc + _dot(vts, p.astype(BF16))
            else:
                m_new = jnp.maximum(m, jnp.max(s, axis=0, keepdims=True))
                alpha = jnp.exp2(m - m_new)
                p = jnp.exp2(s - m_new)
                l = alpha * l + jnp.sum(p, axis=0, keepdims=True)
                acc = alpha * acc + _dot(vts, p.astype(BF16))
                m = m_new
            out.append((m, l, acc))
        return tuple(out)

    init = tuple((jnp.full((1, tq), NEG, F32), jnp.zeros((1, tq), F32), jnp.zeros((V_HEAD_DIM, tq), F32))
                 for _ in range(2))
    carry = lax.fori_loop(0, qi // 2, lambda j, c: tile(j * 2 * TK, 2 * TK, c, False), init)
    carry = lax.cond(qi % 2 == 1, lambda c: tile((qi - 1) * TK, TK, c, False), lambda c: c, carry)
    (_, l0, a0), (_, l1, a1) = tile(qi * TK, TK, carry, True)
    lam = _diff_lambda(lq1_ref[...], lk1_ref[...], lq2_ref[...], lk2_ref[...])
    ot = a0 / l0 - lam * (a1 / l1)
    o_ref[...] = _subln(ot.T, gs_ref[...]).astype(o_ref.dtype)
    for cp in _page_copies(pt_ref, ckt_hbm, cv_hbm, kbuf_ref, vbuf_ref, sem_ref, step, slot):
        cp.wait()
    kt_refs = [kbuf_ref.at[slot, j] for j in range(n_pages)]
    v_refs = [vbuf_ref.at[slot, j] for j in range(n_pages)]
    os_ref[...] = _sample_attention(bounded, qs_ref, kn_ref, vn_ref, lam, gs_ref[...], kt_refs, v_refs)


def _attn(bounded, page_table, qt, k, vt, q8, kn8, vn8, cache_kt, cache_v, lq1, lk1, lq2, lk2, gs):
    bsz, s_len, _ = k.shape
    nb, n_pages = page_table.shape
    nq = q8.shape[1]
    page = cache_kt.shape[2]
    n_qt = s_len // TQ
    assert nb == bsz * ATT_HEADS * n_qt, "one sample sequence per grid step"
    seq = lambda b, h, i: (b * ATT_HEADS + h) * n_qt + i
    qt_spec = pl.BlockSpec((None, V_HEAD_DIM, TQ), lambda b, h, i, pt: (b, h, i))
    k_spec = pl.BlockSpec((None, s_len, V_HEAD_DIM), lambda b, h, i, pt: (b, 0, h))
    vt_spec = pl.BlockSpec((None, V_HEAD_DIM, s_len), lambda b, h, i, pt: (b, h, 0))
    o_spec = pl.BlockSpec((None, TQ, V_HEAD_DIM), lambda b, h, i, pt: (b, i, h))
    small = lambda a: pl.BlockSpec(a.shape, lambda b, h, i, pt: (0, 0))
    new_spec = pl.BlockSpec((None, nq, ATT_WIDTH), lambda b, h, i, pt: (seq(b, h, i), 0, 0))
    hbm = pl.BlockSpec(memory_space=pl.ANY)
    grid_spec = pltpu.PrefetchScalarGridSpec(
        num_scalar_prefetch=1,
        grid=(bsz, ATT_HEADS, n_qt),
        in_specs=[qt_spec, k_spec, vt_spec, small(lq1), small(lk1), small(lq2), small(lk2), small(gs),
                  new_spec, new_spec, new_spec, hbm, hbm],
        out_specs=[o_spec, new_spec],
        scratch_shapes=[pltpu.VMEM((2, n_pages, ATT_WIDTH, page), F32),
                        pltpu.VMEM((2, n_pages, page * ATT_HEADS, V_HEAD_DIM), F32),
                        pltpu.SemaphoreType.DMA((2,))],
    )
    return pl.pallas_call(
        functools.partial(_attn_kernel, bounded),
        grid_spec=grid_spec,
        out_shape=[jax.ShapeDtypeStruct((bsz, s_len, ATT_WIDTH), BF16),
                   jax.ShapeDtypeStruct((nb, nq, ATT_WIDTH), F32)],
        compiler_params=pltpu.CompilerParams(dimension_semantics=("arbitrary", "arbitrary", "arbitrary"),
                                             vmem_limit_bytes=VMEM_LIMIT),
        name="attn_bounded" if bounded else "attn_general",
    )(page_table, qt, k, vt, lq1, lk1, lq2, lk2, gs, q8, kn8, vn8, cache_kt, cache_v)


def _ffn_prompt_kernel(x_ref, at_ref, lo_ref, gt1_ref, sh2_ref, sc2_ref, gt2_ref, g2_ref,
                       wout_ref, wup_ref, cfw_ref, cfb_ref, wdn_ref,
                       y_ref, fc_ref, ubuf_ref, abuf_ref, tail_ref):
    tm = x_ref.shape[0]
    d_ff = wdn_ref.shape[0]
    n_tail = FFN_CONV - 1

    @pl.when(pl.program_id(1) == 0)
    def _():
        tail_ref[...] = jnp.zeros(tail_ref.shape, F32)

    mix = jnp.concatenate([at_ref[...], lo_ref[...]], axis=1)
    x1 = x_ref[...] + _seq_row(gt1_ref) * _dot(mix, wout_ref[...])
    u2 = _to_segment_major(_modulated_norm(x1, g2_ref[...], _seq_row(sc2_ref), _seq_row(sh2_ref)),
                           ubuf_ref).astype(BF16)

    n_chunks = d_ff // FF_CHUNK
    chunk_cols = lambda j: [slice(base + j * FF_CHUNK, base + (j + 1) * FF_CHUNK) for base in (0, d_ff)]
    up_project = lambda j: [_dot(u2, wup_ref[:, cols]) for cols in chunk_cols(j)]

    def hidden(j, ups):
        halves = []
        for cols, up in zip(chunk_cols(j), ups):
            w = cfw_ref[:, cols]
            d2, d1 = _delayed(up, [tail_ref[i, :, cols] for i in range(n_tail)])
            halves.append(d2 * w[0:1] + d1 * w[1:2] + up * w[2:3] + cfb_ref[:, cols])
            for i, grp in enumerate(_tail_groups(up, n_tail)):
                tail_ref[i, :, cols] = grp
                fc_ref[i:i + 1, cols] = grp[SUBLANES - 1:SUBLANES, :]
        g, val = halves
        return (g * jax.nn.sigmoid(g) * val).astype(BF16)

    acc = jnp.zeros((tm, D_MODEL), F32)
    ups = up_project(0)
    for j in range(n_chunks):
        ups_next = up_project(j + 1) if j + 1 < n_chunks else None
        acc = acc + _dot(hidden(j, ups), wdn_ref[j * FF_CHUNK:(j + 1) * FF_CHUNK, :])
        ups = ups_next
    y_ref[...] = x1 + _seq_row(gt2_ref) * _to_time_major(acc, abuf_ref)


def _ffn_prompt(x, attn, lru, mod, g2, w_out, w_up, cfw, cfb, w_dn):
    bsz, s_len, _ = x.shape
    tm = TM_FFN
    d_ff = w_dn.shape[0]
    row_spec = lambda w: pl.BlockSpec((None, tm, w), lambda b, i: (b, i, 0))
    mod_spec = lambda k: _prompt_mod_spec(mod, k)
    return pl.pallas_call(
        _ffn_prompt_kernel,
        grid=(bsz, s_len // tm),
        in_specs=[row_spec(D_MODEL), row_spec(ATT_WIDTH), row_spec(LRU_WIDTH),
                  mod_spec(2), mod_spec(3), mod_spec(4), mod_spec(5), _const_spec(g2.shape),
                  _const_spec(w_out.shape), _const_spec(w_up.shape), _const_spec(cfw.shape),
                  _const_spec(cfb.shape), _const_spec(w_dn.shape)],
        out_specs=[row_spec(D_MODEL), pl.BlockSpec((None, FFN_CONV - 1, 2 * d_ff), lambda b, i: (b, 0, 0))],
        out_shape=[jax.ShapeDtypeStruct((bsz, s_len, D_MODEL), F32),
                   jax.ShapeDtypeStruct((bsz, FFN_CONV - 1, 2 * d_ff), F32)],
        scratch_shapes=[pltpu.VMEM((D_MODEL // LANES, tm, LANES), F32),
                        pltpu.VMEM((D_MODEL // LANES, tm, LANES), F32),
                        pltpu.VMEM((FFN_CONV - 1, SUBLANES, 2 * d_ff), F32)],
        compiler_params=pltpu.CompilerParams(dimension_semantics=("arbitrary", "arbitrary"),
                                             vmem_limit_bytes=VMEM_LIMIT),
        name="ffn_prompt",
    )(x, attn, lru, mod, mod, mod, mod, g2, w_out, w_up, cfw, cfb, w_dn)


def _inproj_sample_kernel(x_ref, mod_ref, g1_ref, win_ref, gq_ref, gk_ref, gmat_ref,
                          cos_ref, sa_ref, sb_ref, cw_ref, cb_ref, wr_ref, br_ref, wi_ref, bi_ref, lam_ref,
                          st_ref, h0_ref,
                          q_ref, k_ref, v_ref, lo_ref, lc_ref, lh_ref):
    nb = h0_ref.shape[0]
    nt = x_ref.shape[0] // nb
    rep = lambda a: jnp.concatenate([a] * nt, axis=0)
    sh1 = rep(mod_ref[0:nb, 0:D_MODEL])
    sc1 = rep(mod_ref[0:nb, D_MODEL:2 * D_MODEL])
    u = _modulated_norm(x_ref[...], g1_ref[...], sc1, sh1).astype(BF16)
    per_t = lambda r: jnp.concatenate([jnp.broadcast_to(r[t:t + 1, :], (nb, LANES)) for t in range(nt)], axis=0)
    cos, sa, sb = per_t(cos_ref[...]), per_t(sa_ref[...]), per_t(sb_ref[...])
    gmat = gmat_ref[...]

    q = _group_norm_rope(_dot(u, win_ref[:, 0:ATT_WIDTH]), gq_ref[...], gmat, cos, sa, sb)
    q_ref[...] = q * QK_SCALE_LOG2
    k_ref[...] = _group_norm_rope(_dot(u, win_ref[:, ATT_WIDTH:2 * ATT_WIDTH]), gk_ref[...], gmat, cos, sa, sb)
    v_ref[...] = _dot(u, win_ref[:, 2 * ATT_WIDTH:3 * ATT_WIDTH])

    c0 = 3 * ATT_WIDTH
    lx = _dot(u, win_ref[:, c0:c0 + LRU_WIDTH])
    lg = _dot(u, win_ref[:, c0 + LRU_WIDTH:c0 + 2 * LRU_WIDTH])
    pad = [st_ref[i] for i in range(LRU_CONV - 1)] + [lx[t * nb:(t + 1) * nb, :] for t in range(nt)]
    for i in range(LRU_CONV - 1):
        lc_ref[i] = pad[nt + i]
    cw = cw_ref[...]
    xcs = []
    for t in range(nt):
        xc = pad[t] * cw[0:1]
        for kk in range(1, LRU_CONV):
            xc = xc + pad[t + kk] * cw[kk:kk + 1]
        xcs.append(xc + cb_ref[...])
    a, gx = _lru_gates(jnp.concatenate(xcs, axis=0), wr_ref, br_ref[...], wi_ref, bi_ref[...],
                       _neg_c_softplus(lam_ref[...]))
    h = h0_ref[...]
    hs = []
    for t in range(nt):
        h = a[t * nb:(t + 1) * nb, :] * h + gx[t * nb:(t + 1) * nb, :]
        hs.append(h)
    lh_ref[...] = h
    lo_ref[...] = (jnp.concatenate(hs, axis=0) * jax.nn.gelu(lg, approximate=True)).astype(BF16)


def _inproj_sample(x_tb, mod_s, g1, w_in, gq, gk, gmat, tabs_s, cw, cb, wr, br, wi, bi, lam, st, h0):
    m = x_tb.shape[0]
    nb = h0.shape[0]
    act = jax.ShapeDtypeStruct((m, ATT_WIDTH), F32)
    return pl.pallas_call(
        _inproj_sample_kernel,
        out_shape=[act, act, act, jax.ShapeDtypeStruct((m, LRU_WIDTH), BF16),
                   jax.ShapeDtypeStruct((LRU_CONV - 1, nb, LRU_WIDTH), F32),
                   jax.ShapeDtypeStruct((nb, LRU_WIDTH), F32)],
        compiler_params=pltpu.CompilerParams(vmem_limit_bytes=VMEM_LIMIT),
        name="inproj_sample",
    )(x_tb, mod_s, g1, w_in, gq, gk, gmat, *tabs_s, cw, cb, wr, br, wi, bi, lam, st, h0)


def _sample_attention(bounded, q_ref, kn_ref, vn_ref, lam, gs, kt_refs, v_refs):
    n_pages = len(kt_refs)
    page = kt_refs[0].shape[1]
    nq = q_ref.shape[0]
    hrows = 2 * nq

    lane = lax.broadcasted_iota(jnp.int32, (nq, V_HEAD_DIM), 1)
    qh = []
    for h in range(ATT_HEADS):
        q8 = q_ref[:, h * V_HEAD_DIM:(h + 1) * V_HEAD_DIM]
        qh.append(jnp.concatenate([jnp.where(lane < QK_SUB_DIM, q8, 0.0), jnp.where(lane >= QK_SUB_DIM, q8, 0.0)],
                                  axis=0).astype(BF16))

    zpad = jnp.zeros((page - nq, V_HEAD_DIM), F32)
    rt = lax.broadcasted_iota(jnp.int32, (hrows, page), 0) % nq
    ct = lax.broadcasted_iota(jnp.int32, (hrows, page), 1)
    ps, ls = [], []
    for h in range(ATT_HEADS):
        rows = slice(h * V_HEAD_DIM, (h + 1) * V_HEAD_DIM)
        kt_all = jnp.concatenate([kt_refs[j][rows, :] for j in range(n_pages)], axis=1).astype(BF16)
        k_new = jnp.concatenate([kn_ref[:, rows], zpad], axis=0).astype(BF16)
        s_new = jnp.where(ct <= rt, _dot_nt(qh[h], k_new), NEG)
        s = jnp.concatenate([_dot(qh[h], kt_all), s_new], axis=1)
        if not bounded:
            s = s - jnp.max(s, axis=1, keepdims=True)
        p = jnp.exp2(s)
        ls.append(jnp.sum(p, axis=1, keepdims=True))
        ps.append(p.astype(BF16))
    outs = []
    for h in range(0, ATT_HEADS, 2):
        v_pair = jnp.concatenate([jnp.concatenate(
            [v_refs[j][pl.ds(hh, page, stride=ATT_HEADS), :] for j in range(n_pages)]
            + [vn_ref[:, hh * V_HEAD_DIM:(hh + 1) * V_HEAD_DIM], zpad], axis=0) for hh in (h, h + 1)],
            axis=1).astype(BF16)
        o_pair = _dot(jnp.concatenate([ps[h], ps[h + 1]], axis=0), v_pair)
        for i, hh in enumerate((h, h + 1)):
            o = o_pair[i * hrows:(i + 1) * hrows, i * V_HEAD_DIM:(i + 1) * V_HEAD_DIM] / ls[hh]
            outs.append(_subln(o[0:nq, :] - lam * o[nq:hrows, :], gs))
    return jnp.concatenate(outs, axis=1)


def _ffn_sample_kernel(x_ref, at_ref, lo_ref, mod_ref, g2_ref, wout_ref, wupg_ref, wupv_ref, cfwg_ref, cfwv_ref,
                       cfbg_ref, cfbv_ref, wdn_ref, stg_ref, stv_ref,
                       y_ref, fcg_ref, fcv_ref, x1_ref, u2_ref, acc_ref):
    j = pl.program_id(0)
    nb = stg_ref.shape[1]
    nt = x_ref.shape[0] // nb
    rep = lambda a: jnp.concatenate([a] * nt, axis=0)

    @pl.when(j == 0)
    def _():
        gt1 = rep(mod_ref[0:nb, 2 * D_MODEL:3 * D_MODEL])
        sh2 = rep(mod_ref[0:nb, 3 * D_MODEL:4 * D_MODEL])
        sc2 = rep(mod_ref[0:nb, 4 * D_MODEL:5 * D_MODEL])
        mix = jnp.concatenate([at_ref[...].astype(BF16), lo_ref[...]], axis=1)
        x1 = x_ref[...] + gt1 * _dot(mix, wout_ref[...])
        x1_ref[...] = x1
        u2_ref[...] = _modulated_norm(x1, g2_ref[...], sc2, sh2).astype(BF16)
        acc_ref[...] = jnp.zeros(acc_ref.shape, F32)

    u2 = u2_ref[...]
    halves = []
    for wup_ref, cfw_ref, cfb_ref, st_ref, fc_ref in ((wupg_ref, cfwg_ref, cfbg_ref, stg_ref, fcg_ref),
                                                      (wupv_ref, cfwv_ref, cfbv_ref, stv_ref, fcv_ref)):
        up = _dot(u2, wup_ref[...])
        pad = [st_ref[i] for i in range(FFN_CONV - 1)] + [up[t * nb:(t + 1) * nb, :] for t in range(nt)]
        for i in range(FFN_CONV - 1):
            fc_ref[i] = pad[nt + i]
        w = cfw_ref[...]
        b = cfb_ref[...]
        hcs = []
        for t in range(nt):
            hc = pad[t] * w[0:1]
            for kk in range(1, FFN_CONV):
                hc = hc + pad[t + kk] * w[kk:kk + 1]
            hcs.append(hc + b)
        halves.append(jnp.concatenate(hcs, axis=0))
    g, val = halves
    hmid = (g * jax.nn.sigmoid(g) * val).astype(BF16)
    acc_ref[...] += _dot(hmid, wdn_ref[...])

    @pl.when(j == pl.num_programs(0) - 1)
    def _():
        gt2 = rep(mod_ref[0:nb, 5 * D_MODEL:6 * D_MODEL])
        y_ref[...] = x1_ref[...] + gt2 * acc_ref[...]


def _ffn_sample(x_tb, attn_tb, lru_tb, mod, g2, w_out, w_up, cfw, cfb, w_dn, st):
    m = x_tb.shape[0]
    d_ff = w_dn.shape[0]
    nb = st.shape[1]
    n_chunks = d_ff // FF_CHUNK
    whole = lambda a: pl.BlockSpec(a.shape, lambda j: (0,) * a.ndim)
    col = lambda rows, half: pl.BlockSpec((rows, FF_CHUNK), lambda j: (0, half * n_chunks + j))
    st_spec = lambda half: pl.BlockSpec((FFN_CONV - 1, nb, FF_CHUNK), lambda j: (0, 0, half * n_chunks + j))
    fc_spec = pl.BlockSpec((FFN_CONV - 1, nb, FF_CHUNK), lambda j: (0, 0, j))
    fc_shape = jax.ShapeDtypeStruct((FFN_CONV - 1, nb, d_ff), F32)
    return pl.pallas_call(
        _ffn_sample_kernel,
        grid=(n_chunks,),
        in_specs=[whole(x_tb), whole(attn_tb), whole(lru_tb), whole(mod), whole(g2), whole(w_out),
                  col(D_MODEL, 0), col(D_MODEL, 1), col(FFN_CONV, 0), col(FFN_CONV, 1), col(1, 0), col(1, 1),
                  pl.BlockSpec((FF_CHUNK, D_MODEL), lambda j: (j, 0)), st_spec(0), st_spec(1)],
        out_specs=[pl.BlockSpec((m, D_MODEL), lambda j: (0, 0)), fc_spec, fc_spec],
        out_shape=[jax.ShapeDtypeStruct((m, D_MODEL), F32), fc_shape, fc_shape],
        scratch_shapes=[pltpu.VMEM((m, D_MODEL), F32), pltpu.VMEM((m, D_MODEL), BF16),
                        pltpu.VMEM((m, D_MODEL), F32)],
        compiler_params=pltpu.CompilerParams(dimension_semantics=("arbitrary",), vmem_limit_bytes=VMEM_LIMIT),
        name="ffn_sample",
    )(x_tb, attn_tb, lru_tb, mod, g2, w_out, w_up, w_up, cfw, cfw, cfb, cfb, w_dn, st, st)


def _block_diag_halves(w):
    n, bd, _ = w.shape
    eye = jnp.eye(n // 2, dtype=w.dtype)
    halves = [jnp.einsum('nij,nm->nimj', w[s * (n // 2):(s + 1) * (n // 2)], eye).reshape(n // 2 * bd, n // 2 * bd)
              for s in range(2)]
    return jnp.stack(halves).astype(BF16)


def kernel(x_prompt, x_sample, cache_k, cache_v, page_table, state_lru_conv, state_lru_h, state_ffn_conv, c_prompt, c_sample, g_norm1, g_norm2, w_ada, b_ada, w_in, g_q, g_k, lam_q1, lam_k1, lam_q2, lam_k2, g_subln, w_out, conv_lru_w, conv_lru_b, w_rgate, b_rgate, w_igate, b_igate, lru_lambda, w_up, conv_ffn_w, conv_ffn_b, w_down):
    depth = w_in.shape[0]
    assert depth == 1, "single-layer step"
    bsz, s_len, _ = x_prompt.shape
    nb, nt, _ = x_sample.shape
    n_pages, page = page_table.shape[1], cache_k.shape[2]
    past_len = n_pages * page
    d_ff = w_down.shape[1]

    w_in_b = w_in[0].astype(BF16)
    w_out_b = w_out[0].astype(BF16)
    w_up_b = w_up[0].astype(BF16)
    w_dn_b = w_down[0].astype(BF16)
    wr = _block_diag_halves(w_rgate[0])
    wi = _block_diag_halves(w_igate[0])
    br = b_rgate[0].reshape(1, LRU_WIDTH)
    bi = b_igate[0].reshape(1, LRU_WIDTH)
    n_grp = ATT_WIDTH // QK_SUB_DIM
    gq = jnp.tile(g_q[0], n_grp)[None, :]
    gk = jnp.tile(g_k[0], n_grp)[None, :]
    grp = jnp.arange(ATT_WIDTH) // QK_SUB_DIM
    gmat = jnp.where(grp[:, None] == grp[None, :], 1.0 / QK_SUB_DIM, 0.0).astype(BF16)
    g1, g2 = g_norm1, g_norm2
    cw, cb = conv_lru_w[0], conv_lru_b
    cfw, cfb = conv_ffn_w[0], conv_ffn_b
    lam = lru_lambda
    lams = (lam_q1, lam_k1, lam_q2, lam_k2)
    gs = g_subln

    assert nb % SUBLANES == 0 and bsz <= SUBLANES
    c_all = jnp.concatenate([c_sample, c_prompt, jnp.zeros((SUBLANES - bsz, D_MODEL), F32)], axis=0)
    mod = _ada(c_all, w_ada[0], b_ada)

    tabs = _rope_tables(s_len)
    pad_rows = SUBLANES - nt
    tabs_s = tuple(t[past_len:past_len + SUBLANES] for t in tabs)

    q_p, kf_p, kb_p, vf_p, vb_p, lo_p, lc_p, lh_p = _inproj_prompt(
        x_prompt, mod, g1, w_in_b, gq, gk, gmat, tabs, cw, cb, wr, br, wi, bi, lam)
    x_tb = x_sample.transpose(1, 0, 2).reshape(nt * nb, D_MODEL)
    st_lru = state_lru_conv[0].transpose(1, 0, 2)
    q_s, k_s, v_s, lo_s, lc_s, lh_s = _inproj_sample(
        x_tb, mod, g1, w_in_b, gq, gk, gmat, tabs_s, cw, cb, wr, br, wi, bi, lam, st_lru, state_lru_h[0])
    to_bt = lambda a: a.reshape(nt, nb, -1).transpose(1, 0, 2)
    pad_t = lambda a: jnp.pad(a, ((0, 0), (0, pad_rows), (0, 0)))
    k_bt, v_bt = to_bt(k_s), to_bt(v_s)

    score_bound = QK_NORM_BOUND ** 2 * QK_SCALE_LOG2 * jnp.max(jnp.abs(g_q)) * jnp.max(jnp.abs(g_k))
    cache_kt = cache_k[0].transpose(0, 2, 3, 4, 1).reshape(-1, ATT_WIDTH, page)
    cache_vr = cache_v[0].reshape(-1, page * ATT_HEADS, V_HEAD_DIM)
    at_p, at_s = lax.cond(score_bound <= SCORE_LOG2_LIMIT,
                          functools.partial(_attn, True), functools.partial(_attn, False),
                          page_table, q_p, kb_p, vb_p, pad_t(to_bt(q_s)), pad_t(k_bt), pad_t(v_bt),
                          cache_kt, cache_vr, *lams, gs)

    y_p, fc_p = _ffn_prompt(x_prompt, at_p, lo_p, mod, g2, w_out_b, w_up_b, cfw, cfb, w_dn_b)
    at_tb = at_s[:, :nt].transpose(1, 0, 2).reshape(nt * nb, ATT_WIDTH)
    y_tb, fcg_s, fcv_s = _ffn_sample(x_tb, at_tb, lo_s, mod, g2, w_out_b, w_up_b, cfw, cfb, w_dn_b,
                                     state_ffn_conv[0].transpose(1, 0, 2))

    hd = (ATT_HEADS, 2, QK_SUB_DIM)
    return (y_p, to_bt(y_tb),
            kf_p.reshape(bsz, *hd, s_len).transpose(0, 4, 1, 2, 3)[None],
            vf_p.reshape(1, bsz, s_len, ATT_HEADS, V_HEAD_DIM),
            lc_p[None], lh_p.reshape(1, bsz, LRU_WIDTH), fc_p[None],
            k_bt.reshape(1, nb, nt, *hd), v_bt.reshape(1, nb, nt, ATT_HEADS, V_HEAD_DIM),
            lc_s.transpose(1, 0, 2)[None], lh_s[None],
            jnp.concatenate([fcg_s, fcv_s], axis=-1).transpose(1, 0, 2)[None])
```

```python
import functools
import math

import jax
import jax.numpy as jnp
from jax import lax
from jax.experimental import pallas as pl
from jax.experimental.pallas import tpu as pltpu

F32 = jnp.float32
BF16 = jnp.bfloat16

D_MODEL = 1024
ATT_WIDTH = 512
LRU_WIDTH = 512
ATT_HEADS = 4
V_HEAD_DIM = 128
QK_SUB_DIM = 64
ROPE_DIM = 16
ROPE_THETA = 500000.0
LRU_BLOCKS = 8
LRU_C = 8.0
LRU_CONV = 4
FFN_CONV = 3
EPS = 1e-6
LAM_INIT = 0.8 - 0.6 * math.exp(-0.3 * 0)
QK_SCALE = QK_SUB_DIM ** -0.5
QK_SCALE_LOG2 = QK_SCALE * math.log2(math.e)
SCORE_LOG2_LIMIT = 64.0
QK_NORM_BOUND = 1.01 * math.sqrt(QK_SUB_DIM)

LANES = 128
SUBLANES = 8
VMEM_LIMIT = 56 * 1024 * 1024

TM_IN = 512
TQ = 512
TK = 512
TM_FFN = 512
FFN_SPLIT = 2
FFN_SLOT = 2
FF_CHUNK = 256
NEG = -1e30


def _dot(a, b):
    return jnp.dot(a, b, preferred_element_type=F32)


def _dot_nt(a, b):
    return lax.dot_general(a, b, (((1,), (1,)), ((), ())), preferred_element_type=F32)


def _const_spec(shape):
    nd = len(shape)
    return pl.BlockSpec(shape, lambda *_: (0,) * nd, pipeline_mode=pl.Buffered(1))


def _prompt_mod_spec(mod, k):
    return pl.BlockSpec((SUBLANES, D_MODEL), lambda b, i: (mod.shape[0] // SUBLANES - 1, k))


def _seq_row(ref):
    return ref[pl.ds(pl.program_id(0), 1), :]


def _modulated_norm(x, g, sc, sh):
    xn = x * lax.rsqrt(jnp.mean(x * x, axis=-1, keepdims=True) + EPS) * g
    return xn * (1.0 + sc) + sh


def _to_segment_major(val, buf_ref):
    tm, w = val.shape
    n = tm // SUBLANES
    for c in range(w // LANES):
        for s in range(SUBLANES):
            for r0 in range(0, n, SUBLANES):
                t0 = s * n + r0
                buf_ref[c, pl.ds(r0 * SUBLANES + s, SUBLANES, stride=SUBLANES), :] = (
                    val[t0:t0 + SUBLANES, c * LANES:(c + 1) * LANES])
    return jnp.concatenate([buf_ref[c] for c in range(w // LANES)], axis=1)


def _to_time_major(val, buf_ref):
    tm, w = val.shape
    n = tm // SUBLANES
    for c in range(w // LANES):
        buf_ref[c] = val[:, c * LANES:(c + 1) * LANES]
    rows = []
    for s in range(SUBLANES):
        for r0 in range(0, n, SUBLANES):
            rows.append(jnp.concatenate(
                [buf_ref[c, pl.ds(r0 * SUBLANES + s, SUBLANES, stride=SUBLANES), :] for c in range(w // LANES)], axis=1))
    return jnp.concatenate(rows, axis=0)


def _tail_groups(cur, k):
    tm = cur.shape[0]
    return [cur[tm - (k - i) * SUBLANES:tm - (k - i - 1) * SUBLANES, :] for i in range(k)]


def _delayed(cur, prev_tail):
    k = len(prev_tail)
    tm = cur.shape[0]
    first = lax.broadcasted_iota(jnp.int32, prev_tail[0].shape, 0) == 0
    heads = [jnp.where(first, pltpu.roll(p, 1, 0), pltpu.roll(c, 1, 0))
             for p, c in zip(prev_tail, _tail_groups(cur, k))]
    return [jnp.concatenate(heads[k - d:] + [cur[:tm - d * SUBLANES, :]], axis=0) for d in range(k, 0, -1)]


def _group_norm_rope(t, g_tiled, gmat, cos, sin_a, sin_b):
    ms = _dot((t * t).astype(BF16), gmat)
    tn = t * lax.rsqrt(ms + EPS) * g_tiled
    outs = []
    for h in range(ATT_WIDTH // LANES):
        th = tn[:, h * LANES:(h + 1) * LANES]
        outs.append(th * cos + pltpu.roll(th, LANES - 8, 1) * sin_a + pltpu.roll(th, 8, 1) * sin_b)
    return jnp.concatenate(outs, axis=1)


def _lru_gates(xc, wr_ref, br, wi_ref, bi, neg_c_softplus):
    half = LRU_WIDTH // 2
    xb = xc.astype(BF16)
    lo, hi = xb[:, :half], xb[:, half:]
    r = jax.nn.sigmoid(jnp.concatenate([_dot(lo, wr_ref[0]), _dot(hi, wr_ref[1])], axis=1) + br)
    ig = jax.nn.sigmoid(jnp.concatenate([_dot(lo, wi_ref[0]), _dot(hi, wi_ref[1])], axis=1) + bi)
    log_a = neg_c_softplus * r
    a = jnp.exp(log_a)
    one_minus_a2 = -jnp.tanh(log_a) * (a * a + 1.0)
    return a, jnp.sqrt(one_minus_a2) * (ig * xc)


def _neg_c_softplus(lam):
    z = -lam
    return -LRU_C * (jnp.maximum(z, 0.0) + jnp.log1p(jnp.exp(-jnp.abs(z))))


def _diff_lambda(lq1, lk1, lq2, lk2):
    s1 = jnp.sum(lq1 * lk1, axis=-1, keepdims=True)
    s2 = jnp.sum(lq2 * lk2, axis=-1, keepdims=True)
    return jnp.exp(s1) - jnp.exp(s2) + LAM_INIT


def _subln(o, g):
    return o * lax.rsqrt(jnp.mean(o * o, axis=-1, keepdims=True) + EPS) * g * (1.0 - LAM_INIT)


def _ada_kernel(c_ref, w_ref, b_ref, o_ref):
    c = c_ref[...]
    s = (c * jax.nn.sigmoid(c)).astype(BF16)
    o_ref[...] = _dot(s, w_ref[...].astype(BF16)) + b_ref[...]


def _ada(c_all, w_ada, b_ada):
    m = c_all.shape[0]
    n = w_ada.shape[1]
    tn = 1024
    return pl.pallas_call(
        _ada_kernel,
        grid=(n // tn,),
        in_specs=[pl.BlockSpec((m, D_MODEL), lambda j: (0, 0)),
                  pl.BlockSpec((D_MODEL, tn), lambda j: (0, j)),
                  pl.BlockSpec((1, tn), lambda j: (0, j))],
        out_specs=pl.BlockSpec((m, tn), lambda j: (0, j)),
        out_shape=jax.ShapeDtypeStruct((m, n), F32),
        compiler_params=pltpu.CompilerParams(dimension_semantics=("arbitrary",), vmem_limit_bytes=VMEM_LIMIT),
        name="ada_mod",
    )(c_all, w_ada, b_ada)


def _rope_table_kernel(freq_ref, ma_ref, mb_ref, c_ref, sa_ref, sb_ref):
    tm = c_ref.shape[0]
    pos = (pl.program_id(0) * tm + lax.broadcasted_iota(jnp.int32, (tm, LANES), 0)).astype(F32)
    ang = pos * freq_ref[...]
    s = jnp.sin(ang)
    c_ref[...] = jnp.cos(ang)
    sa_ref[...] = -s * ma_ref[...]
    sb_ref[...] = s * mb_ref[...]


def _rope_tables(n_pos):
    half = ROPE_DIM // 2
    d = jnp.arange(LANES) % QK_SUB_DIM
    freqs = ROPE_THETA ** (-(d % half).astype(F32) * 2.0 / ROPE_DIM)
    freq_lane = jnp.where(d < ROPE_DIM, freqs, 0.0).astype(F32)[None, :]
    mask_a = (d < half).astype(F32)[None, :]
    mask_b = ((d >= half) & (d < ROPE_DIM)).astype(F32)[None, :]
    tm = 512
    row = pl.BlockSpec((1, LANES), lambda i: (0, 0))
    tab = pl.BlockSpec((tm, LANES), lambda i: (i, 0))
    shp = jax.ShapeDtypeStruct((n_pos, LANES), F32)
    return pl.pallas_call(
        _rope_table_kernel,
        grid=(n_pos // tm,),
        in_specs=[row, row, row],
        out_specs=[tab, tab, tab],
        out_shape=[shp, shp, shp],
        compiler_params=pltpu.CompilerParams(dimension_semantics=("arbitrary",)),
        name="rope_tables",
    )(freq_lane, mask_a, mask_b)


def _inproj_prompt_kernel(x_ref, sh_ref, sc_ref, g1_ref, win_ref, gq_ref, gk_ref, gmat_ref,
                          cos_ref, sa_ref, sb_ref, cw_ref, cb_ref, wr_ref, br_ref, wi_ref, bi_ref, lam_ref,
                          q_ref, kf_ref, kb_ref, vf_ref, vb_ref, lo_ref, lc_ref, lh_ref,
                          lbuf_ref, hbuf_ref, tail_ref, hcar_ref):
    tm = x_ref.shape[0]

    @pl.when(pl.program_id(1) == 0)
    def _():
        tail_ref[...] = jnp.zeros(tail_ref.shape, F32)
        hcar_ref[...] = jnp.zeros(hcar_ref.shape, F32)

    u = _modulated_norm(x_ref[...], g1_ref[...], _seq_row(sc_ref), _seq_row(sh_ref)).astype(BF16)
    cos, sa, sb = cos_ref[...], sa_ref[...], sb_ref[...]
    gmat = gmat_ref[...]

    q = _group_norm_rope(_dot(u, win_ref[:, 0:ATT_WIDTH]), gq_ref[...], gmat, cos, sa, sb)
    q_ref[...] = (q * QK_SCALE_LOG2).T.astype(BF16)
    k = _group_norm_rope(_dot(u, win_ref[:, ATT_WIDTH:2 * ATT_WIDTH]), gk_ref[...], gmat, cos, sa, sb)
    kf_ref[...] = k.T
    kb_ref[...] = k.astype(BF16)
    v = _dot(u, win_ref[:, 2 * ATT_WIDTH:3 * ATT_WIDTH])
    for h in range(ATT_HEADS):
        vf_ref[pl.ds(h, tm, stride=ATT_HEADS), :] = v[:, h * V_HEAD_DIM:(h + 1) * V_HEAD_DIM]
    vb_ref[...] = v.T.astype(BF16)

    c0 = 3 * ATT_WIDTH
    n_tail = LRU_CONV - 1
    lx = _to_segment_major(_dot(u, win_ref[:, c0:c0 + LRU_WIDTH]), lbuf_ref)
    cw = cw_ref[...]
    d3, d2, d1 = _delayed(lx, [tail_ref[i] for i in range(n_tail)])
    xc = d3 * cw[0:1] + d2 * cw[1:2] + d1 * cw[2:3] + lx * cw[3:4] + cb_ref[...]
    for i, grp in enumerate(_tail_groups(lx, n_tail)):
        tail_ref[i] = grp
        lc_ref[i:i + 1, :] = grp[SUBLANES - 1:SUBLANES, :]

    a, gx = _lru_gates(xc, wr_ref, br_ref[...], wi_ref, bi_ref[...], _neg_c_softplus(lam_ref[...]))

    n = tm // SUBLANES
    grp = lambda arr, r: arr[r * SUBLANES:(r + 1) * SUBLANES, :]
    ps, hs = [grp(a, 0)], [grp(gx, 0)]
    for r in range(1, n):
        ar = grp(a, r)
        ps.append(ar * ps[-1])
        hs.append(ar * hs[-1] + grp(gx, r))
    p_end, h_end = ps[-1], hs[-1]
    entering = [hcar_ref[...]]
    for s in range(1, SUBLANES):
        entering.append(p_end[s - 1:s, :] * entering[-1] + h_end[s - 1:s, :])
    h_last = p_end[SUBLANES - 1:SUBLANES, :] * entering[-1] + h_end[SUBLANES - 1:SUBLANES, :]
    hcar_ref[...] = h_last
    lh_ref[...] = h_last
    enter = jnp.concatenate(entering, axis=0)
    states = jnp.concatenate([hs[r] + ps[r] * enter for r in range(n)], axis=0)

    lg = _dot(u, win_ref[:, c0 + LRU_WIDTH:c0 + 2 * LRU_WIDTH])
    lo_ref[...] = (_to_time_major(states, hbuf_ref) * jax.nn.gelu(lg, approximate=True)).astype(BF16)


def _inproj_prompt(x, mod, g1, w_in, gq, gk, gmat, tabs, cw, cb, wr, br, wi, bi, lam):
    bsz, s_len, _ = x.shape
    tm = TM_IN
    row_spec = lambda w: pl.BlockSpec((None, tm, w), lambda b, i: (b, i, 0))
    tab_spec = pl.BlockSpec((tm, LANES), lambda b, i: (i, 0))
    mod_spec = lambda k: _prompt_mod_spec(mod, k)
    act = lambda dt: jax.ShapeDtypeStruct((bsz, s_len, ATT_WIDTH), dt)
    act_t = lambda dt: jax.ShapeDtypeStruct((bsz, ATT_WIDTH, s_len), dt)
    col_spec = pl.BlockSpec((None, ATT_WIDTH, tm), lambda b, i: (b, 0, i))
    return pl.pallas_call(
        _inproj_prompt_kernel,
        grid=(bsz, s_len // tm),
        in_specs=[row_spec(D_MODEL), mod_spec(0), mod_spec(1), _const_spec(g1.shape), _const_spec(w_in.shape),
                  _const_spec(gq.shape), _const_spec(gk.shape), _const_spec(gmat.shape),
                  tab_spec, tab_spec, tab_spec,
                  _const_spec(cw.shape), _const_spec(cb.shape), _const_spec(wr.shape), _const_spec(br.shape),
                  _const_spec(wi.shape), _const_spec(bi.shape), _const_spec(lam.shape)],
        out_specs=[col_spec, col_spec, row_spec(ATT_WIDTH),
                   pl.BlockSpec((None, tm * ATT_HEADS, V_HEAD_DIM), lambda b, i: (b, i, 0)),
                   col_spec, row_spec(LRU_WIDTH),
                   pl.BlockSpec((None, LRU_CONV - 1, LRU_WIDTH), lambda b, i: (b, 0, 0)),
                   pl.BlockSpec((None, 1, LRU_WIDTH), lambda b, i: (b, 0, 0))],
        out_shape=[act_t(BF16), act_t(F32), act(BF16),
                   jax.ShapeDtypeStruct((bsz, s_len * ATT_HEADS, V_HEAD_DIM), F32), act_t(BF16), act(BF16),
                   jax.ShapeDtypeStruct((bsz, LRU_CONV - 1, LRU_WIDTH), F32),
                   jax.ShapeDtypeStruct((bsz, 1, LRU_WIDTH), F32)],
        scratch_shapes=[pltpu.VMEM((LRU_WIDTH // LANES, tm, LANES), F32),
                        pltpu.VMEM((LRU_WIDTH // LANES, tm, LANES), F32),
                        pltpu.VMEM((LRU_CONV - 1, SUBLANES, LRU_WIDTH), F32),
                        pltpu.VMEM((1, LRU_WIDTH), F32)],
        compiler_params=pltpu.CompilerParams(dimension_semantics=("arbitrary", "arbitrary"),
                                             vmem_limit_bytes=VMEM_LIMIT),
        name="inproj_prompt",
    )(x, mod, mod, g1, w_in, gq, gk, gmat, *tabs, cw, cb, wr, br, wi, bi, lam)


def _page_copies(pt_ref, ckt_hbm, cv_hbm, kbuf_ref, vbuf_ref, sem_ref, seq, slot):
    copies = []
    for j in range(kbuf_ref.shape[1]):
        pg = pt_ref[seq, j]
        copies.append(pltpu.make_async_copy(ckt_hbm.at[pg], kbuf_ref.at[slot, j], sem_ref.at[slot]))
        copies.append(pltpu.make_async_copy(cv_hbm.at[pg], vbuf_ref.at[slot, j], sem_ref.at[slot]))
    return copies


def _attn_kernel(bounded, pt_ref, qt_ref, k_ref, vt_ref, lq1_ref, lk1_ref, lq2_ref, lk2_ref, gs_ref,
                 qs_ref, kn_ref, vn_ref, ckt_hbm, cv_hbm, o_ref, os_ref, kbuf_ref, vbuf_ref, sem_ref):
    n_pages = kbuf_ref.shape[1]
    step = (pl.program_id(0) * pl.num_programs(1) + pl.program_id(1)) * pl.num_programs(2) + pl.program_id(2)
    n_steps = pl.num_programs(0) * pl.num_programs(1) * pl.num_programs(2)
    slot = step % 2

    @pl.when(step == 0)
    def _():
        for cp in _page_copies(pt_ref, ckt_hbm, cv_hbm, kbuf_ref, vbuf_ref, sem_ref, step, slot):
            cp.start()

    @pl.when(step + 1 < n_steps)
    def _():
        for cp in _page_copies(pt_ref, ckt_hbm, cv_hbm, kbuf_ref, vbuf_ref, sem_ref, step + 1, 1 - slot):
            cp.start()

    tq = qt_ref.shape[1]
    qi = pl.program_id(2)
    qt = qt_ref[...].astype(F32)
    feat = lax.broadcasted_iota(jnp.int32, (V_HEAD_DIM, tq), 0)
    qc = (jnp.where(feat < QK_SUB_DIM, qt, 0.0).astype(BF16), jnp.where(feat >= QK_SUB_DIM, qt, 0.0).astype(BF16))

    def tile(k0, width, carry, masked):
        k0 = pl.multiple_of(k0, TK)
        ks = k_ref[pl.ds(k0, width), :]
        vts = vt_ref[:, pl.ds(k0, width)]
        ss = [_dot(ks, qc[c]) for c in range(2)]
        out = []
        for c in range(2):
            m, l, acc = carry[c]
            s = ss[c]
            if masked:
                keys = lax.broadcasted_iota(jnp.int32, (width, tq), 0)
                qs = lax.broadcasted_iota(jnp.int32, (width, tq), 1)
                s = jnp.where(keys <= qs, s, NEG)
            if bounded:
                p = jnp.exp2(s)
                l = l + jnp.sum(p, axis=0, keepdims=True)
                acc = acc + _dot(vts, p.astype(BF16))
            else:
                m_new = jnp.maximum(m, jnp.max(s, axis=0, keepdims=True))
                alpha = jnp.exp2(m - m_new)
                p = jnp.exp2(s - m_new)
                l = alpha * l + jnp.sum(p, axis=0, keepdims=True)
                acc = alpha * acc + _dot(vts, p.astype(BF16))
                m = m_new
            out.append((m, l, acc))
        return tuple(out)

    init = tuple((jnp.full((1, tq), NEG, F32), jnp.zeros((1, tq), F32), jnp.zeros((V_HEAD_DIM, tq), F32))
                 for _ in range(2))
    carry = lax.fori_loop(0, qi // 2, lambda j, c: tile(j * 2 * TK, 2 * TK, c, False), init)
    carry = lax.cond(qi % 2 == 1, lambda c: tile((qi - 1) * TK, TK, c, False), lambda c: c, carry)
    (_, l0, a0), (_, l1, a1) = tile(qi * TK, TK, carry, True)
    lam = _diff_lambda(lq1_ref[...], lk1_ref[...], lq2_ref[...], lk2_ref[...])
    ot = a0 / l0 - lam * (a1 / l1)
    o_ref[...] = _subln(ot.T, gs_ref[...]).astype(o_ref.dtype)
    for cp in _page_copies(pt_ref, ckt_hbm, cv_hbm, kbuf_ref, vbuf_ref, sem_ref, step, slot):
        cp.wait()
    kt_refs = [kbuf_ref.at[slot, j] for j in range(n_pages)]
    v_refs = [vbuf_ref.at[slot, j] for j in range(n_pages)]
    os_ref[...] = _sample_attention(bounded, qs_ref, kn_ref, vn_ref, lam, gs_ref[...], kt_refs, v_refs)


def _attn(bounded, page_table, qt, k, vt, q8, kn8, vn8, cache_kt, cache_v, lq1, lk1, lq2, lk2, gs):
    bsz, s_len, _ = k.shape
    nb, n_pages = page_table.shape
    nq = q8.shape[1]
    page = cache_kt.shape[2]
    n_qt = s_len // TQ
    assert nb == bsz * ATT_HEADS * n_qt, "one sample sequence per grid step"
    seq = lambda b, h, i: (b * ATT_HEADS + h) * n_qt + i
    qt_spec = pl.BlockSpec((None, V_HEAD_DIM, TQ), lambda b, h, i, pt: (b, h, i))
    k_spec = pl.BlockSpec((None, s_len, V_HEAD_DIM), lambda b, h, i, pt: (b, 0, h))
    vt_spec = pl.BlockSpec((None, V_HEAD_DIM, s_len), lambda b, h, i, pt: (b, h, 0))
    o_spec = pl.BlockSpec((None, TQ, V_HEAD_DIM), lambda b, h, i, pt: (b, i, h))
    small = lambda a: pl.BlockSpec(a.shape, lambda b, h, i, pt: (0, 0))
    new_spec = pl.BlockSpec((None, nq, ATT_WIDTH), lambda b, h, i, pt: (seq(b, h, i), 0, 0))
    hbm = pl.BlockSpec(memory_space=pl.ANY)
    grid_spec = pltpu.PrefetchScalarGridSpec(
        num_scalar_prefetch=1,
        grid=(bsz, ATT_HEADS, n_qt),
        in_specs=[qt_spec, k_spec, vt_spec, small(lq1), small(lk1), small(lq2), small(lk2), small(gs),
                  new_spec, new_spec, new_spec, hbm, hbm],
        out_specs=[o_spec, new_spec],
        scratch_shapes=[pltpu.VMEM((2, n_pages, ATT_WIDTH, page), F32),
                        pltpu.VMEM((2, n_pages, page * ATT_HEADS, V_HEAD_DIM), F32),
                        pltpu.SemaphoreType.DMA((2,))],
    )
    return pl.pallas_call(
        functools.partial(_attn_kernel, bounded),
        grid_spec=grid_spec,
        out_shape=[jax.ShapeDtypeStruct((bsz, s_len, ATT_WIDTH), BF16),
                   jax.ShapeDtypeStruct((nb, nq, ATT_WIDTH), F32)],
        compiler_params=pltpu.CompilerParams(dimension_semantics=("arbitrary", "arbitrary", "arbitrary"),
                                             vmem_limit_bytes=VMEM_LIMIT),
        name="attn_bounded" if bounded else "attn_general",
    )(page_table, qt, k, vt, lq1, lk1, lq2, lk2, gs, q8, kn8, vn8, cache_kt, cache_v)


def _ffn_prompt_kernel(x_ref, at_ref, lo_ref, gt1_ref, sh2_ref, sc2_ref, gt2_ref, g2_ref,
                       wout_ref, wup_ref, cfw_ref, cfb_ref, wdn_ref,
                       y_ref, fc_ref, ubuf_ref, abuf_ref, tail_ref):
    tm = x_ref.shape[0]
    sub = tm // FFN_SPLIT
    d_ff = wdn_ref.shape[0]
    n_tail = FFN_CONV - 1
    n_chunks = d_ff // FF_CHUNK
    chunk_cols = lambda j: [slice(base + j * FF_CHUNK, base + (j + 1) * FF_CHUNK) for base in (0, d_ff)]

    @pl.when(pl.program_id(1) == 0)
    def _():
        tail_ref[...] = jnp.zeros(tail_ref.shape, F32)

    def attention_residual(g):
        rows = slice(g * sub, (g + 1) * sub)
        mix = jnp.concatenate([at_ref[rows, :], lo_ref[rows, :]], axis=1)
        x1 = x_ref[rows, :] + _seq_row(gt1_ref) * _dot(mix, wout_ref[...])
        u2 = _to_segment_major(_modulated_norm(x1, g2_ref[...], _seq_row(sc2_ref), _seq_row(sh2_ref)),
                               ubuf_ref.at[g]).astype(BF16)
        return x1, u2

    def finish(g, x1, acc):
        y_ref[g * sub:(g + 1) * sub, :] = x1 + _seq_row(gt2_ref) * _to_time_major(acc, abuf_ref.at[g])

    def hidden(j, ups):
        halves = []
        for cols, up in zip(chunk_cols(j), ups):
            w = cfw_ref[:, cols]
            d2, d1 = _delayed(up, [tail_ref[i, :, cols] for i in range(n_tail)])
            halves.append(d2 * w[0:1] + d1 * w[1:2] + up * w[2:3] + cfb_ref[:, cols])
            for i, grp in enumerate(_tail_groups(up, n_tail)):
                tail_ref[i, :, cols] = grp
                fc_ref[i:i + 1, cols] = grp[SUBLANES - 1:SUBLANES, :]
        g, val = halves
        return (g * jax.nn.sigmoid(g) * val).astype(BF16)

    state = attention_residual(0)
    done = None
    for g in range(FFN_SPLIT):
        x1, u2 = state
        up_project = lambda j: [_dot(u2, wup_ref[:, cols]) for cols in chunk_cols(j)]
        acc = jnp.zeros((sub, D_MODEL), F32)
        ups = up_project(0)
        for j in range(n_chunks):
            ups_next = up_project(j + 1) if j + 1 < n_chunks else None
            acc = acc + _dot(hidden(j, ups), wdn_ref[j * FF_CHUNK:(j + 1) * FF_CHUNK, :])
            ups = ups_next
            if j == FFN_SLOT and done is not None:
                finish(*done)
                done = None
            if j == FFN_SLOT and g + 1 < FFN_SPLIT:
                state = attention_residual(g + 1)
        done = (g, x1, acc)
    finish(*done)


def _ffn_prompt(x, attn, lru, mod, g2, w_out, w_up, cfw, cfb, w_dn):
    bsz, s_len, _ = x.shape
    tm = TM_FFN
    d_ff = w_dn.shape[0]
    row_spec = lambda w: pl.BlockSpec((None, tm, w), lambda b, i: (b, i, 0))
    mod_spec = lambda k: _prompt_mod_spec(mod, k)
    return pl.pallas_call(
        _ffn_prompt_kernel,
        grid=(bsz, s_len // tm),
        in_specs=[row_spec(D_MODEL), row_spec(ATT_WIDTH), row_spec(LRU_WIDTH),
                  mod_spec(2), mod_spec(3), mod_spec(4), mod_spec(5), _const_spec(g2.shape),
                  _const_spec(w_out.shape), _const_spec(w_up.shape), _const_spec(cfw.shape),
                  _const_spec(cfb.shape), _const_spec(w_dn.shape)],
        out_specs=[row_spec(D_MODEL), pl.BlockSpec((None, FFN_CONV - 1, 2 * d_ff), lambda b, i: (b, 0, 0))],
        out_shape=[jax.ShapeDtypeStruct((bsz, s_len, D_MODEL), F32),
                   jax.ShapeDtypeStruct((bsz, FFN_CONV - 1, 2 * d_ff), F32)],
        scratch_shapes=[pltpu.VMEM((FFN_SPLIT, D_MODEL // LANES, tm // FFN_SPLIT, LANES), F32),
                        pltpu.VMEM((FFN_SPLIT, D_MODEL // LANES, tm // FFN_SPLIT, LANES), F32),
                        pltpu.VMEM((FFN_CONV - 1, SUBLANES, 2 * d_ff), F32)],
        compiler_params=pltpu.CompilerParams(dimension_semantics=("arbitrary", "arbitrary"),
                                             vmem_limit_bytes=VMEM_LIMIT),
        name="ffn_prompt",
    )(x, attn, lru, mod, mod, mod, mod, g2, w_out, w_up, cfw, cfb, w_dn)


def _inproj_sample_kernel(x_ref, mod_ref, g1_ref, win_ref, gq_ref, gk_ref, gmat_ref,
                          cos_ref, sa_ref, sb_ref, cw_ref, cb_ref, wr_ref, br_ref, wi_ref, bi_ref, lam_ref,
                          st_ref, h0_ref,
                          q_ref, k_ref, v_ref, lo_ref, lc_ref, lh_ref):
    nb = h0_ref.shape[0]
    nt = x_ref.shape[0] // nb
    rep = lambda a: jnp.concatenate([a] * nt, axis=0)
    sh1 = rep(mod_ref[0:nb, 0:D_MODEL])
    sc1 = rep(mod_ref[0:nb, D_MODEL:2 * D_MODEL])
    u = _modulated_norm(x_ref[...], g1_ref[...], sc1, sh1).astype(BF16)
    per_t = lambda r: jnp.concatenate([jnp.broadcast_to(r[t:t + 1, :], (nb, LANES)) for t in range(nt)], axis=0)
    cos, sa, sb = per_t(cos_ref[...]), per_t(sa_ref[...]), per_t(sb_ref[...])
    gmat = gmat_ref[...]

    q = _group_norm_rope(_dot(u, win_ref[:, 0:ATT_WIDTH]), gq_ref[...], gmat, cos, sa, sb)
    q_ref[...] = q * QK_SCALE_LOG2
    k_ref[...] = _group_norm_rope(_dot(u, win_ref[:, ATT_WIDTH:2 * ATT_WIDTH]), gk_ref[...], gmat, cos, sa, sb)
    v_ref[...] = _dot(u, win_ref[:, 2 * ATT_WIDTH:3 * ATT_WIDTH])

    c0 = 3 * ATT_WIDTH
    lx = _dot(u, win_ref[:, c0:c0 + LRU_WIDTH])
    lg = _dot(u, win_ref[:, c0 + LRU_WIDTH:c0 + 2 * LRU_WIDTH])
    pad = [st_ref[i] for i in range(LRU_CONV - 1)] + [lx[t * nb:(t + 1) * nb, :] for t in range(nt)]
    for i in range(LRU_CONV - 1):
        lc_ref[i] = pad[nt + i]
    cw = cw_ref[...]
    xcs = []
    for t in range(nt):
        xc = pad[t] * cw[0:1]
        for kk in range(1, LRU_CONV):
            xc = xc + pad[t + kk] * cw[kk:kk + 1]
        xcs.append(xc + cb_ref[...])
    a, gx = _lru_gates(jnp.concatenate(xcs, axis=0), wr_ref, br_ref[...], wi_ref, bi_ref[...],
                       _neg_c_softplus(lam_ref[...]))
    h = h0_ref[...]
    hs = []
    for t in range(nt):
        h = a[t * nb:(t + 1) * nb, :] * h + gx[t * nb:(t + 1) * nb, :]
        hs.append(h)
    lh_ref[...] = h
    lo_ref[...] = (jnp.concatenate(hs, axis=0) * jax.nn.gelu(lg, approximate=True)).astype(BF16)


def _inproj_sample(x_tb, mod_s, g1, w_in, gq, gk, gmat, tabs_s, cw, cb, wr, br, wi, bi, lam, st, h0):
    m = x_tb.shape[0]
    nb = h0.shape[0]
    act = jax.ShapeDtypeStruct((m, ATT_WIDTH), F32)
    return pl.pallas_call(
        _inproj_sample_kernel,
        out_shape=[act, act, act, jax.ShapeDtypeStruct((m, LRU_WIDTH), BF16),
                   jax.ShapeDtypeStruct((LRU_CONV - 1, nb, LRU_WIDTH), F32),
                   jax.ShapeDtypeStruct((nb, LRU_WIDTH), F32)],
        compiler_params=pltpu.CompilerParams(vmem_limit_bytes=VMEM_LIMIT),
        name="inproj_sample",
    )(x_tb, mod_s, g1, w_in, gq, gk, gmat, *tabs_s, cw, cb, wr, br, wi, bi, lam, st, h0)


def _sample_attention(bounded, q_ref, kn_ref, vn_ref, lam, gs, kt_refs, v_refs):
    n_pages = len(kt_refs)
    page = kt_refs[0].shape[1]
    nq = q_ref.shape[0]
    hrows = 2 * nq

    lane = lax.broadcasted_iota(jnp.int32, (nq, V_HEAD_DIM), 1)
    qh = []
    for h in range(ATT_HEADS):
        q8 = q_ref[:, h * V_HEAD_DIM:(h + 1) * V_HEAD_DIM]
        qh.append(jnp.concatenate([jnp.where(lane < QK_SUB_DIM, q8, 0.0), jnp.where(lane >= QK_SUB_DIM, q8, 0.0)],
                                  axis=0).astype(BF16))

    zpad = jnp.zeros((page - nq, V_HEAD_DIM), F32)
    rt = lax.broadcasted_iota(jnp.int32, (hrows, page), 0) % nq
    ct = lax.broadcasted_iota(jnp.int32, (hrows, page), 1)
    ps, ls = [], []
    for h in range(ATT_HEADS):
        rows = slice(h * V_HEAD_DIM, (h + 1) * V_HEAD_DIM)
        kt_all = jnp.concatenate([kt_refs[j][rows, :] for j in range(n_pages)], axis=1).astype(BF16)
        k_new = jnp.concatenate([kn_ref[:, rows], zpad], axis=0).astype(BF16)
        s_new = jnp.where(ct <= rt, _dot_nt(qh[h], k_new), NEG)
        s = jnp.concatenate([_dot(qh[h], kt_all), s_new], axis=1)
        if not bounded:
            s = s - jnp.max(s, axis=1, keepdims=True)
        p = jnp.exp2(s)
        ls.append(jnp.sum(p, axis=1, keepdims=True))
        ps.append(p.astype(BF16))
    outs = []
    for h in range(0, ATT_HEADS, 2):
        v_pair = jnp.concatenate([jnp.concatenate(
            [v_refs[j][pl.ds(hh, page, stride=ATT_HEADS), :] for j in range(n_pages)]
            + [vn_ref[:, hh * V_HEAD_DIM:(hh + 1) * V_HEAD_DIM], zpad], axis=0) for hh in (h, h + 1)],
            axis=1).astype(BF16)
        o_pair = _dot(jnp.concatenate([ps[h], ps[h + 1]], axis=0), v_pair)
        for i, hh in enumerate((h, h + 1)):
            o = o_pair[i * hrows:(i + 1) * hrows, i * V_HEAD_DIM:(i + 1) * V_HEAD_DIM] / ls[hh]
            outs.append(_subln(o[0:nq, :] - lam * o[nq:hrows, :], gs))
    return jnp.concatenate(outs, axis=1)


def _ffn_sample_kernel(x_ref, at_ref, lo_ref, mod_ref, g2_ref, wout_ref, wupg_ref, wupv_ref, cfwg_ref, cfwv_ref,
                       cfbg_ref, cfbv_ref, wdn_ref, stg_ref, stv_ref,
                       y_ref, fcg_ref, fcv_ref, x1_ref, u2_ref, acc_ref):
    j = pl.program_id(0)
    nb = stg_ref.shape[1]
    nt = x_ref.shape[0] // nb
    rep = lambda a: jnp.concatenate([a] * nt, axis=0)

    @pl.when(j == 0)
    def _():
        gt1 = rep(mod_ref[0:nb, 2 * D_MODEL:3 * D_MODEL])
        sh2 = rep(mod_ref[0:nb, 3 * D_MODEL:4 * D_MODEL])
        sc2 = rep(mod_ref[0:nb, 4 * D_MODEL:5 * D_MODEL])
        mix = jnp.concatenate([at_ref[...].astype(BF16), lo_ref[...]], axis=1)
        x1 = x_ref[...] + gt1 * _dot(mix, wout_ref[...])
        x1_ref[...] = x1
        u2_ref[...] = _modulated_norm(x1, g2_ref[...], sc2, sh2).astype(BF16)
        acc_ref[...] = jnp.zeros(acc_ref.shape, F32)

    u2 = u2_ref[...]
    halves = []
    for wup_ref, cfw_ref, cfb_ref, st_ref, fc_ref in ((wupg_ref, cfwg_ref, cfbg_ref, stg_ref, fcg_ref),
                                                      (wupv_ref, cfwv_ref, cfbv_ref, stv_ref, fcv_ref)):
        up = _dot(u2, wup_ref[...])
        pad = [st_ref[i] for i in range(FFN_CONV - 1)] + [up[t * nb:(t + 1) * nb, :] for t in range(nt)]
        for i in range(FFN_CONV - 1):
            fc_ref[i] = pad[nt + i]
        w = cfw_ref[...]
        b = cfb_ref[...]
        hcs = []
        for t in range(nt):
            hc = pad[t] * w[0:1]
            for kk in range(1, FFN_CONV):
                hc = hc + pad[t + kk] * w[kk:kk + 1]
            hcs.append(hc + b)
        halves.append(jnp.concatenate(hcs, axis=0))
    g, val = halves
    hmid = (g * jax.nn.sigmoid(g) * val).astype(BF16)
    acc_ref[...] += _dot(hmid, wdn_ref[...])

    @pl.when(j == pl.num_programs(0) - 1)
    def _():
        gt2 = rep(mod_ref[0:nb, 5 * D_MODEL:6 * D_MODEL])
        y_ref[...] = x1_ref[...] + gt2 * acc_ref[...]


def _ffn_sample(x_tb, attn_tb, lru_tb, mod, g2, w_out, w_up, cfw, cfb, w_dn, st):
    m = x_tb.shape[0]
    d_ff = w_dn.shape[0]
    nb = st.shape[1]
    n_chunks = d_ff // FF_CHUNK
    whole = lambda a: pl.BlockSpec(a.shape, lambda j: (0,) * a.ndim)
    col = lambda rows, half: pl.BlockSpec((rows, FF_CHUNK), lambda j: (0, half * n_chunks + j))
    st_spec = lambda half: pl.BlockSpec((FFN_CONV - 1, nb, FF_CHUNK), lambda j: (0, 0, half * n_chunks + j))
    fc_spec = pl.BlockSpec((FFN_CONV - 1, nb, FF_CHUNK), lambda j: (0, 0, j))
    fc_shape = jax.ShapeDtypeStruct((FFN_CONV - 1, nb, d_ff), F32)
    return pl.pallas_call(
        _ffn_sample_kernel,
        grid=(n_chunks,),
        in_specs=[whole(x_tb), whole(attn_tb), whole(lru_tb), whole(mod), whole(g2), whole(w_out),
                  col(D_MODEL, 0), col(D_MODEL, 1), col(FFN_CONV, 0), col(FFN_CONV, 1), col(1, 0), col(1, 1),
                  pl.BlockSpec((FF_CHUNK, D_MODEL), lambda j: (j, 0)), st_spec(0), st_spec(1)],
        out_specs=[pl.BlockSpec((m, D_MODEL), lambda j: (0, 0)), fc_spec, fc_spec],
        out_shape=[jax.ShapeDtypeStruct((m, D_MODEL), F32), fc_shape, fc_shape],
        scratch_shapes=[pltpu.VMEM((m, D_MODEL), F32), pltpu.VMEM((m, D_MODEL), BF16),
                        pltpu.VMEM((m, D_MODEL), F32)],
        compiler_params=pltpu.CompilerParams(dimension_semantics=("arbitrary",), vmem_limit_bytes=VMEM_LIMIT),
        name="ffn_sample",
    )(x_tb, attn_tb, lru_tb, mod, g2, w_out, w_up, w_up, cfw, cfw, cfb, cfb, w_dn, st, st)


def _block_diag_halves(w):
    n, bd, _ = w.shape
    eye = jnp.eye(n // 2, dtype=w.dtype)
    halves = [jnp.einsum('nij,nm->nimj', w[s * (n // 2):(s + 1) * (n // 2)], eye).reshape(n // 2 * bd, n // 2 * bd)
              for s in range(2)]
    return jnp.stack(halves).astype(BF16)


def kernel(x_prompt, x_sample, cache_k, cache_v, page_table, state_lru_conv, state_lru_h, state_ffn_conv, c_prompt, c_sample, g_norm1, g_norm2, w_ada, b_ada, w_in, g_q, g_k, lam_q1, lam_k1, lam_q2, lam_k2, g_subln, w_out, conv_lru_w, conv_lru_b, w_rgate, b_rgate, w_igate, b_igate, lru_lambda, w_up, conv_ffn_w, conv_ffn_b, w_down):
    depth = w_in.shape[0]
    assert depth == 1, "single-layer step"
    bsz, s_len, _ = x_prompt.shape
    nb, nt, _ = x_sample.shape
    n_pages, page = page_table.shape[1], cache_k.shape[2]
    past_len = n_pages * page
    d_ff = w_down.shape[1]

    w_in_b = w_in[0].astype(BF16)
    w_out_b = w_out[0].astype(BF16)
    w_up_b = w_up[0].astype(BF16)
    w_dn_b = w_down[0].astype(BF16)
    wr = _block_diag_halves(w_rgate[0])
    wi = _block_diag_halves(w_igate[0])
    br = b_rgate[0].reshape(1, LRU_WIDTH)
    bi = b_igate[0].reshape(1, LRU_WIDTH)
    n_grp = ATT_WIDTH // QK_SUB_DIM
    gq = jnp.tile(g_q[0], n_grp)[None, :]
    gk = jnp.tile(g_k[0], n_grp)[None, :]
    grp = jnp.arange(ATT_WIDTH) // QK_SUB_DIM
    gmat = jnp.where(grp[:, None] == grp[None, :], 1.0 / QK_SUB_DIM, 0.0).astype(BF16)
    g1, g2 = g_norm1, g_norm2
    cw, cb = conv_lru_w[0], conv_lru_b
    cfw, cfb = conv_ffn_w[0], conv_ffn_b
    lam = lru_lambda
    lams = (lam_q1, lam_k1, lam_q2, lam_k2)
    gs = g_subln

    assert nb % SUBLANES == 0 and bsz <= SUBLANES
    c_all = jnp.concatenate([c_sample, c_prompt, jnp.zeros((SUBLANES - bsz, D_MODEL), F32)], axis=0)
    mod = _ada(c_all, w_ada[0], b_ada)

    tabs = _rope_tables(s_len)
    pad_rows = SUBLANES - nt
    tabs_s = tuple(t[past_len:past_len + SUBLANES] for t in tabs)

    q_p, kf_p, kb_p, vf_p, vb_p, lo_p, lc_p, lh_p = _inproj_prompt(
        x_prompt, mod, g1, w_in_b, gq, gk, gmat, tabs, cw, cb, wr, br, wi, bi, lam)
    x_tb = x_sample.transpose(1, 0, 2).reshape(nt * nb, D_MODEL)
    st_lru = state_lru_conv[0].transpose(1, 0, 2)
    q_s, k_s, v_s, lo_s, lc_s, lh_s = _inproj_sample(
        x_tb, mod, g1, w_in_b, gq, gk, gmat, tabs_s, cw, cb, wr, br, wi, bi, lam, st_lru, state_lru_h[0])
    to_bt = lambda a: a.reshape(nt, nb, -1).transpose(1, 0, 2)
    pad_t = lambda a: jnp.pad(a, ((0, 0), (0, pad_rows), (0, 0)))
    k_bt, v_bt = to_bt(k_s), to_bt(v_s)

    score_bound = QK_NORM_BOUND ** 2 * QK_SCALE_LOG2 * jnp.max(jnp.abs(g_q)) * jnp.max(jnp.abs(g_k))
    cache_kt = cache_k[0].transpose(0, 2, 3, 4, 1).reshape(-1, ATT_WIDTH, page)
    cache_vr = cache_v[0].reshape(-1, page * ATT_HEADS, V_HEAD_DIM)
    at_p, at_s = lax.cond(score_bound <= SCORE_LOG2_LIMIT,
                          functools.partial(_attn, True), functools.partial(_attn, False),
                          page_table, q_p, kb_p, vb_p, pad_t(to_bt(q_s)), pad_t(k_bt), pad_t(v_bt),
                          cache_kt, cache_vr, *lams, gs)

    y_p, fc_p = _ffn_prompt(x_prompt, at_p, lo_p, mod, g2, w_out_b, w_up_b, cfw, cfb, w_dn_b)
    at_tb = at_s[:, :nt].transpose(1, 0, 2).reshape(nt * nb, ATT_WIDTH)
    y_tb, fcg_s, fcv_s = _ffn_sample(x_tb, at_tb, lo_s, mod, g2, w_out_b, w_up_b, cfw, cfb, w_dn_b,
                                     state_ffn_conv[0].transpose(1, 0, 2))

    hd = (ATT_HEADS, 2, QK_SUB_DIM)
    return (y_p, to_bt(y_tb),
            kf_p.reshape(bsz, *hd, s_len).transpose(0, 4, 1, 2, 3)[None],
            vf_p.reshape(1, bsz, s_len, ATT_HEADS, V_HEAD_DIM),
            lc_p[None], lh_p.reshape(1, bsz, LRU_WIDTH), fc_p[None],
            k_bt.reshape(1, nb, nt, *hd), v_bt.reshape(1, nb, nt, ATT_HEADS, V_HEAD_DIM),
            lc_s.transpose(1, 0, 2)[None], lh_s[None],
            jnp.concatenate([fcg_s, fcv_s], axis=-1).transpose(1, 0, 2)[None])
```

```python
import functools
import math

import jax
import jax.numpy as jnp
from jax import lax
from jax.experimental import pallas as pl
from jax.experimental.pallas import tpu as pltpu

F32 = jnp.float32
BF16 = jnp.bfloat16

D_MODEL = 1024
ATT_WIDTH = 512
LRU_WIDTH = 512
ATT_HEADS = 4
V_HEAD_DIM = 128
QK_SUB_DIM = 64
ROPE_DIM = 16
ROPE_THETA = 500000.0
LRU_BLOCKS = 8
LRU_C = 8.0
LRU_CONV = 4
FFN_CONV = 3
EPS = 1e-6
LAM_INIT = 0.8 - 0.6 * math.exp(-0.3 * 0)
QK_SCALE = QK_SUB_DIM ** -0.5
QK_SCALE_LOG2 = QK_SCALE * math.log2(math.e)
SCORE_LOG2_LIMIT = 64.0
QK_NORM_BOUND = 1.01 * math.sqrt(QK_SUB_DIM)

LANES = 128
SUBLANES = 8
VMEM_LIMIT = 56 * 1024 * 1024

TM_IN = 512
TQ = 512
TK = 512
TM_FFN = 512
FFN_SPLIT = 2
FFN_SLOT = 2
FF_CHUNK = 256
NEG = -1e30


def _dot(a, b):
    return jnp.dot(a, b, preferred_element_type=F32)


def _dot_nt(a, b):
    return lax.dot_general(a, b, (((1,), (1,)), ((), ())), preferred_element_type=F32)


def _const_spec(shape):
    nd = len(shape)
    return pl.BlockSpec(shape, lambda *_: (0,) * nd, pipeline_mode=pl.Buffered(1))


def _prompt_mod_spec(mod, k):
    return pl.BlockSpec((SUBLANES, D_MODEL), lambda b, i: (mod.shape[0] // SUBLANES - 1, k))


def _seq_row(ref):
    return ref[pl.ds(pl.program_id(0), 1), :]


def _modulated_norm(x, g, sc, sh):
    xn = x * lax.rsqrt(jnp.mean(x * x, axis=-1, keepdims=True) + EPS) * g
    return xn * (1.0 + sc) + sh


def _to_segment_major(val, buf_ref):
    tm, w = val.shape
    n = tm // SUBLANES
    for c in range(w // LANES):
        for s in range(SUBLANES):
            for r0 in range(0, n, SUBLANES):
                t0 = s * n + r0
                buf_ref[c, pl.ds(r0 * SUBLANES + s, SUBLANES, stride=SUBLANES), :] = (
                    val[t0:t0 + SUBLANES, c * LANES:(c + 1) * LANES])
    return jnp.concatenate([buf_ref[c] for c in range(w // LANES)], axis=1)


def _to_time_major(val, buf_ref):
    tm, w = val.shape
    n = tm // SUBLANES
    for c in range(w // LANES):
        buf_ref[c] = val[:, c * LANES:(c + 1) * LANES]
    rows = []
    for s in range(SUBLANES):
        for r0 in range(0, n, SUBLANES):
            rows.append(jnp.concatenate(
                [buf_ref[c, pl.ds(r0 * SUBLANES + s, SUBLANES, stride=SUBLANES), :] for c in range(w // LANES)], axis=1))
    return jnp.concatenate(rows, axis=0)


def _tail_groups(cur, k):
    tm = cur.shape[0]
    return [cur[tm - (k - i) * SUBLANES:tm - (k - i - 1) * SUBLANES, :] for i in range(k)]


def _delayed(cur, prev_tail):
    k = len(prev_tail)
    tm = cur.shape[0]
    first = lax.broadcasted_iota(jnp.int32, prev_tail[0].shape, 0) == 0
    heads = [jnp.where(first, pltpu.roll(p, 1, 0), pltpu.roll(c, 1, 0))
             for p, c in zip(prev_tail, _tail_groups(cur, k))]
    return [jnp.concatenate(heads[k - d:] + [cur[:tm - d * SUBLANES, :]], axis=0) for d in range(k, 0, -1)]


def _group_norm_rope(t, g_tiled, gmat, cos, sin_a, sin_b):
    ms = _dot((t * t).astype(BF16), gmat)
    tn = t * lax.rsqrt(ms + EPS) * g_tiled
    outs = []
    for h in range(ATT_WIDTH // LANES):
        th = tn[:, h * LANES:(h + 1) * LANES]
        outs.append(th * cos + pltpu.roll(th, LANES - 8, 1) * sin_a + pltpu.roll(th, 8, 1) * sin_b)
    return jnp.concatenate(outs, axis=1)


def _lru_gates(xc, wr_ref, br, wi_ref, bi, neg_c_softplus):
    half = LRU_WIDTH // 2
    xb = xc.astype(BF16)
    lo, hi = xb[:, :half], xb[:, half:]
    r = jax.nn.sigmoid(jnp.concatenate([_dot(lo, wr_ref[0]), _dot(hi, wr_ref[1])], axis=1) + br)
    ig = jax.nn.sigmoid(jnp.concatenate([_dot(lo, wi_ref[0]), _dot(hi, wi_ref[1])], axis=1) + bi)
    log_a = neg_c_softplus * r
    a = jnp.exp(log_a)
    one_minus_a2 = -jnp.tanh(log_a) * (a * a + 1.0)
    return a, jnp.sqrt(one_minus_a2) * (ig * xc)


def _neg_c_softplus(lam):
    z = -lam
    return -LRU_C * (jnp.maximum(z, 0.0) + jnp.log1p(jnp.exp(-jnp.abs(z))))


def _diff_lambda(lq1, lk1, lq2, lk2):
    s1 = jnp.sum(lq1 * lk1, axis=-1, keepdims=True)
    s2 = jnp.sum(lq2 * lk2, axis=-1, keepdims=True)
    return jnp.exp(s1) - jnp.exp(s2) + LAM_INIT


def _subln(o, g):
    return o * lax.rsqrt(jnp.mean(o * o, axis=-1, keepdims=True) + EPS) * g * (1.0 - LAM_INIT)


def _ada_kernel(c_ref, w_ref, b_ref, o_ref):
    c = c_ref[...]
    s = (c * jax.nn.sigmoid(c)).astype(BF16)
    o_ref[...] = _dot(s, w_ref[...].astype(BF16)) + b_ref[...]


def _ada(c_all, w_ada, b_ada):
    m = c_all.shape[0]
    n = w_ada.shape[1]
    tn = 1024
    return pl.pallas_call(
        _ada_kernel,
        grid=(n // tn,),
        in_specs=[pl.BlockSpec((m, D_MODEL), lambda j: (0, 0)),
                  pl.BlockSpec((D_MODEL, tn), lambda j: (0, j)),
                  pl.BlockSpec((1, tn), lambda j: (0, j))],
        out_specs=pl.BlockSpec((m, tn), lambda j: (0, j)),
        out_shape=jax.ShapeDtypeStruct((m, n), F32),
        compiler_params=pltpu.CompilerParams(dimension_semantics=("arbitrary",), vmem_limit_bytes=VMEM_LIMIT),
        name="ada_mod",
    )(c_all, w_ada, b_ada)


def _rope_table_kernel(freq_ref, ma_ref, mb_ref, c_ref, sa_ref, sb_ref):
    tm = c_ref.shape[0]
    pos = (pl.program_id(0) * tm + lax.broadcasted_iota(jnp.int32, (tm, LANES), 0)).astype(F32)
    ang = pos * freq_ref[...]
    s = jnp.sin(ang)
    c_ref[...] = jnp.cos(ang)
    sa_ref[...] = -s * ma_ref[...]
    sb_ref[...] = s * mb_ref[...]


def _rope_tables(n_pos):
    half = ROPE_DIM // 2
    d = jnp.arange(LANES) % QK_SUB_DIM
    freqs = ROPE_THETA ** (-(d % half).astype(F32) * 2.0 / ROPE_DIM)
    freq_lane = jnp.where(d < ROPE_DIM, freqs, 0.0).astype(F32)[None, :]
    mask_a = (d < half).astype(F32)[None, :]
    mask_b = ((d >= half) & (d < ROPE_DIM)).astype(F32)[None, :]
    tm = 512
    row = pl.BlockSpec((1, LANES), lambda i: (0, 0))
    tab = pl.BlockSpec((tm, LANES), lambda i: (i, 0))
    shp = jax.ShapeDtypeStruct((n_pos, LANES), F32)
    return pl.pallas_call(
        _rope_table_kernel,
        grid=(n_pos // tm,),
        in_specs=[row, row, row],
        out_specs=[tab, tab, tab],
        out_shape=[shp, shp, shp],
        compiler_params=pltpu.CompilerParams(dimension_semantics=("arbitrary",)),
        name="rope_tables",
    )(freq_lane, mask_a, mask_b)


def _inproj_prompt_kernel(x_ref, sh_ref, sc_ref, g1_ref, win_ref, gq_ref, gk_ref, gmat_ref,
                          cos_ref, sa_ref, sb_ref, cw_ref, cb_ref, wr_ref, br_ref, wi_ref, bi_ref, lam_ref,
                          q_ref, kf_ref, kb_ref, vf_ref, vb_ref, lo_ref, lc_ref, lh_ref,
                          lbuf_ref, hbuf_ref, tail_ref, hcar_ref):
    tm = x_ref.shape[0]

    @pl.when(pl.program_id(1) == 0)
    def _():
        tail_ref[...] = jnp.zeros(tail_ref.shape, F32)
        hcar_ref[...] = jnp.zeros(hcar_ref.shape, F32)

    u = _modulated_norm(x_ref[...], g1_ref[...], _seq_row(sc_ref), _seq_row(sh_ref)).astype(BF16)
    cos, sa, sb = cos_ref[...], sa_ref[...], sb_ref[...]
    gmat = gmat_ref[...]

    q = _group_norm_rope(_dot(u, win_ref[:, 0:ATT_WIDTH]), gq_ref[...], gmat, cos, sa, sb)
    q_ref[...] = (q * QK_SCALE_LOG2).T.astype(BF16)
    k = _group_norm_rope(_dot(u, win_ref[:, ATT_WIDTH:2 * ATT_WIDTH]), gk_ref[...], gmat, cos, sa, sb)
    kf_ref[...] = k.T
    kb_ref[...] = k.astype(BF16)
    v = _dot(u, win_ref[:, 2 * ATT_WIDTH:3 * ATT_WIDTH])
    for h in range(ATT_HEADS):
        vf_ref[pl.ds(h, tm, stride=ATT_HEADS), :] = v[:, h * V_HEAD_DIM:(h + 1) * V_HEAD_DIM]
    vb_ref[...] = v.T.astype(BF16)

    c0 = 3 * ATT_WIDTH
    n_tail = LRU_CONV - 1
    lx = _to_segment_major(_dot(u, win_ref[:, c0:c0 + LRU_WIDTH]), lbuf_ref)
    cw = cw_ref[...]
    d3, d2, d1 = _delayed(lx, [tail_ref[i] for i in range(n_tail)])
    xc = d3 * cw[0:1] + d2 * cw[1:2] + d1 * cw[2:3] + lx * cw[3:4] + cb_ref[...]
    for i, grp in enumerate(_tail_groups(lx, n_tail)):
        tail_ref[i] = grp
        lc_ref[i:i + 1, :] = grp[SUBLANES - 1:SUBLANES, :]

    a, gx = _lru_gates(xc, wr_ref, br_ref[...], wi_ref, bi_ref[...], _neg_c_softplus(lam_ref[...]))

    n = tm // SUBLANES
    grp = lambda arr, r: arr[r * SUBLANES:(r + 1) * SUBLANES, :]
    ps, hs = [grp(a, 0)], [grp(gx, 0)]
    for r in range(1, n):
        ar = grp(a, r)
        ps.append(ar * ps[-1])
        hs.append(ar * hs[-1] + grp(gx, r))
    p_end, h_end = ps[-1], hs[-1]
    entering = [hcar_ref[...]]
    for s in range(1, SUBLANES):
        entering.append(p_end[s - 1:s, :] * entering[-1] + h_end[s - 1:s, :])
    h_last = p_end[SUBLANES - 1:SUBLANES, :] * entering[-1] + h_end[SUBLANES - 1:SUBLANES, :]
    hcar_ref[...] = h_last
    lh_ref[...] = h_last
    enter = jnp.concatenate(entering, axis=0)
    states = jnp.concatenate([hs[r] + ps[r] * enter for r in range(n)], axis=0)

    lg = _dot(u, win_ref[:, c0 + LRU_WIDTH:c0 + 2 * LRU_WIDTH])
    lo_ref[...] = (_to_time_major(states, hbuf_ref) * jax.nn.gelu(lg, approximate=True)).astype(BF16)


def _inproj_prompt(x, mod, g1, w_in, gq, gk, gmat, tabs, cw, cb, wr, br, wi, bi, lam):
    bsz, s_len, _ = x.shape
    tm = TM_IN
    row_spec = lambda w: pl.BlockSpec((None, tm, w), lambda b, i: (b, i, 0))
    tab_spec = pl.BlockSpec((tm, LANES), lambda b, i: (i, 0))
    mod_spec = lambda k: _prompt_mod_spec(mod, k)
    act = lambda dt: jax.ShapeDtypeStruct((bsz, s_len, ATT_WIDTH), dt)
    act_t = lambda dt: jax.ShapeDtypeStruct((bsz, ATT_WIDTH, s_len), dt)
    col_spec = pl.BlockSpec((None, ATT_WIDTH, tm), lambda b, i: (b, 0, i))
    return pl.pallas_call(
        _inproj_prompt_kernel,
        grid=(bsz, s_len // tm),
        in_specs=[row_spec(D_MODEL), mod_spec(0), mod_spec(1), _const_spec(g1.shape), _const_spec(w_in.shape),
                  _const_spec(gq.shape), _const_spec(gk.shape), _const_spec(gmat.shape),
                  tab_spec, tab_spec, tab_spec,
                  _const_spec(cw.shape), _const_spec(cb.shape), _const_spec(wr.shape), _const_spec(br.shape),
                  _const_spec(wi.shape), _const_spec(bi.shape), _const_spec(lam.shape)],
        out_specs=[col_spec, col_spec, row_spec(ATT_WIDTH),
                   pl.BlockSpec((None, tm * ATT_HEADS, V_HEAD_DIM), lambda b, i: (b, i, 0)),
                   col_spec, row_spec(LRU_WIDTH),
                   pl.BlockSpec((None, LRU_CONV - 1, LRU_WIDTH), lambda b, i: (b, 0, 0)),
                   pl.BlockSpec((None, 1, LRU_WIDTH), lambda b, i: (b, 0, 0))],
        out_shape=[act_t(BF16), act_t(F32), act(BF16),
                   jax.ShapeDtypeStruct((bsz, s_len * ATT_HEADS, V_HEAD_DIM), F32), act_t(BF16), act(BF16),
                   jax.ShapeDtypeStruct((bsz, LRU_CONV - 1, LRU_WIDTH), F32),
                   jax.ShapeDtypeStruct((bsz, 1, LRU_WIDTH), F32)],
        scratch_shapes=[pltpu.VMEM((LRU_WIDTH // LANES, tm, LANES), F32),
                        pltpu.VMEM((LRU_WIDTH // LANES, tm, LANES), F32),
                        pltpu.VMEM((LRU_CONV - 1, SUBLANES, LRU_WIDTH), F32),
                        pltpu.VMEM((1, LRU_WIDTH), F32)],
        compiler_params=pltpu.CompilerParams(dimension_semantics=("arbitrary", "arbitrary"),
                                             vmem_limit_bytes=VMEM_LIMIT),
        name="inproj_prompt",
    )(x, mod, mod, g1, w_in, gq, gk, gmat, *tabs, cw, cb, wr, br, wi, bi, lam)


def _alternate(*step_generators):
    live = list(step_generators)
    while live:
        for g in list(live):
            try:
                next(g)
            except StopIteration:
                live.remove(g)


def _page_copies(pt_ref, ckt_hbm, cv_hbm, kbuf_ref, vbuf_ref, sem_ref, seq, slot):
    copies = []
    for j in range(kbuf_ref.shape[1]):
        pg = pt_ref[seq, j]
        copies.append(pltpu.make_async_copy(ckt_hbm.at[pg], kbuf_ref.at[slot, j], sem_ref.at[slot]))
        copies.append(pltpu.make_async_copy(cv_hbm.at[pg], vbuf_ref.at[slot, j], sem_ref.at[slot]))
    return copies


def _attn_kernel(bounded, pt_ref, qt_ref, k_ref, vt_ref, lq1_ref, lk1_ref, lq2_ref, lk2_ref, gs_ref,
                 qs_ref, kn_ref, vn_ref, ckt_hbm, cv_hbm, o_ref, os_ref, kbuf_ref, vbuf_ref, sem_ref):
    n_pages = kbuf_ref.shape[1]
    step = (pl.program_id(0) * pl.num_programs(1) + pl.program_id(1)) * pl.num_programs(2) + pl.program_id(2)
    n_steps = pl.num_programs(0) * pl.num_programs(1) * pl.num_programs(2)
    slot = step % 2

    @pl.when(step == 0)
    def _():
        for cp in _page_copies(pt_ref, ckt_hbm, cv_hbm, kbuf_ref, vbuf_ref, sem_ref, step, slot):
            cp.start()

    @pl.when(step + 1 < n_steps)
    def _():
        for cp in _page_copies(pt_ref, ckt_hbm, cv_hbm, kbuf_ref, vbuf_ref, sem_ref, step + 1, 1 - slot):
            cp.start()

    tq = qt_ref.shape[1]
    qi = pl.program_id(2)
    qt = qt_ref[...].astype(F32)
    feat = lax.broadcasted_iota(jnp.int32, (V_HEAD_DIM, tq), 0)
    qc = (jnp.where(feat < QK_SUB_DIM, qt, 0.0).astype(BF16), jnp.where(feat >= QK_SUB_DIM, qt, 0.0).astype(BF16))

    def tile_steps(k0, width, carry, masked, out):
        k0 = pl.multiple_of(k0, TK)
        ks = k_ref[pl.ds(k0, width), :]
        vts = vt_ref[:, pl.ds(k0, width)]
        ss = []
        for c in range(2):
            ss.append(_dot(ks, qc[c]))
            yield
        for c in range(2):
            m, l, acc = carry[c]
            s = ss[c]
            if masked:
                keys = lax.broadcasted_iota(jnp.int32, (width, tq), 0)
                qs = lax.broadcasted_iota(jnp.int32, (width, tq), 1)
                s = jnp.where(keys <= qs, s, NEG)
            if bounded:
                p = jnp.exp2(s)
                l = l + jnp.sum(p, axis=0, keepdims=True)
                acc = acc + _dot(vts, p.astype(BF16))
            else:
                m_new = jnp.maximum(m, jnp.max(s, axis=0, keepdims=True))
                alpha = jnp.exp2(m - m_new)
                p = jnp.exp2(s - m_new)
                l = alpha * l + jnp.sum(p, axis=0, keepdims=True)
                acc = alpha * acc + _dot(vts, p.astype(BF16))
                m = m_new
            out.append((m, l, acc))
            yield

    def tile(k0, width, carry, masked):
        out = []
        for _ in tile_steps(k0, width, carry, masked, out):
            pass
        return tuple(out)

    init = tuple((jnp.full((1, tq), NEG, F32), jnp.zeros((1, tq), F32), jnp.zeros((V_HEAD_DIM, tq), F32))
                 for _ in range(2))
    carry = lax.fori_loop(0, qi // 2, lambda j, c: tile(j * 2 * TK, 2 * TK, c, False), init)
    carry = lax.cond(qi % 2 == 1, lambda c: tile((qi - 1) * TK, TK, c, False), lambda c: c, carry)
    for cp in _page_copies(pt_ref, ckt_hbm, cv_hbm, kbuf_ref, vbuf_ref, sem_ref, step, slot):
        cp.wait()
    kt_refs = [kbuf_ref.at[slot, j] for j in range(n_pages)]
    v_refs = [vbuf_ref.at[slot, j] for j in range(n_pages)]
    lam = _diff_lambda(lq1_ref[...], lk1_ref[...], lq2_ref[...], lk2_ref[...])
    tile_out, sample_out = [], []
    _alternate(tile_steps(qi * TK, TK, carry, True, tile_out),
               _sample_attention_steps(bounded, qs_ref, kn_ref, vn_ref, lam, gs_ref[...], kt_refs, v_refs, sample_out))
    (_, l0, a0), (_, l1, a1) = tile_out
    ot = a0 / l0 - lam * (a1 / l1)
    o_ref[...] = _subln(ot.T, gs_ref[...]).astype(o_ref.dtype)
    os_ref[...] = sample_out[0]


def _attn(bounded, page_table, qt, k, vt, q8, kn8, vn8, cache_kt, cache_v, lq1, lk1, lq2, lk2, gs):
    bsz, s_len, _ = k.shape
    nb, n_pages = page_table.shape
    nq = q8.shape[1]
    page = cache_kt.shape[2]
    n_qt = s_len // TQ
    assert nb == bsz * ATT_HEADS * n_qt, "one sample sequence per grid step"
    seq = lambda b, h, i: (b * ATT_HEADS + h) * n_qt + i
    qt_spec = pl.BlockSpec((None, V_HEAD_DIM, TQ), lambda b, h, i, pt: (b, h, i))
    k_spec = pl.BlockSpec((None, s_len, V_HEAD_DIM), lambda b, h, i, pt: (b, 0, h))
    vt_spec = pl.BlockSpec((None, V_HEAD_DIM, s_len), lambda b, h, i, pt: (b, h, 0))
    o_spec = pl.BlockSpec((None, TQ, V_HEAD_DIM), lambda b, h, i, pt: (b, i, h))
    small = lambda a: pl.BlockSpec(a.shape, lambda b, h, i, pt: (0, 0))
    new_spec = pl.BlockSpec((None, nq, ATT_WIDTH), lambda b, h, i, pt: (seq(b, h, i), 0, 0))
    hbm = pl.BlockSpec(memory_space=pl.ANY)
    grid_spec = pltpu.PrefetchScalarGridSpec(
        num_scalar_prefetch=1,
        grid=(bsz, ATT_HEADS, n_qt),
        in_specs=[qt_spec, k_spec, vt_spec, small(lq1), small(lk1), small(lq2), small(lk2), small(gs),
                  new_spec, new_spec, new_spec, hbm, hbm],
        out_specs=[o_spec, new_spec],
        scratch_shapes=[pltpu.VMEM((2, n_pages, ATT_WIDTH, page), F32),
                        pltpu.VMEM((2, n_pages, page * ATT_HEADS, V_HEAD_DIM), F32),
                        pltpu.SemaphoreType.DMA((2,))],
    )
    return pl.pallas_call(
        functools.partial(_attn_kernel, bounded),
        grid_spec=grid_spec,
        out_shape=[jax.ShapeDtypeStruct((bsz, s_len, ATT_WIDTH), BF16),
                   jax.ShapeDtypeStruct((nb, nq, ATT_WIDTH), F32)],
        compiler_params=pltpu.CompilerParams(dimension_semantics=("arbitrary", "arbitrary", "arbitrary"),
                                             vmem_limit_bytes=VMEM_LIMIT),
        name="attn_bounded" if bounded else "attn_general",
    )(page_table, qt, k, vt, lq1, lk1, lq2, lk2, gs, q8, kn8, vn8, cache_kt, cache_v)


def _ffn_prompt_kernel(x_ref, at_ref, lo_ref, gt1_ref, sh2_ref, sc2_ref, gt2_ref, g2_ref,
                       wout_ref, wup_ref, cfw_ref, cfb_ref, wdn_ref,
                       y_ref, fc_ref, ubuf_ref, abuf_ref, tail_ref):
    tm = x_ref.shape[0]
    sub = tm // FFN_SPLIT
    d_ff = wdn_ref.shape[0]
    n_tail = FFN_CONV - 1
    n_chunks = d_ff // FF_CHUNK
    chunk_cols = lambda j: [slice(base + j * FF_CHUNK, base + (j + 1) * FF_CHUNK) for base in (0, d_ff)]

    @pl.when(pl.program_id(1) == 0)
    def _():
        tail_ref[...] = jnp.zeros(tail_ref.shape, F32)

    def attention_residual(g):
        rows = slice(g * sub, (g + 1) * sub)
        mix = jnp.concatenate([at_ref[rows, :], lo_ref[rows, :]], axis=1)
        x1 = x_ref[rows, :] + _seq_row(gt1_ref) * _dot(mix, wout_ref[...])
        u2 = _to_segment_major(_modulated_norm(x1, g2_ref[...], _seq_row(sc2_ref), _seq_row(sh2_ref)),
                               ubuf_ref.at[g]).astype(BF16)
        return x1, u2

    def finish(g, x1, acc):
        y_ref[g * sub:(g + 1) * sub, :] = x1 + _seq_row(gt2_ref) * _to_time_major(acc, abuf_ref.at[g])

    def hidden(j, ups):
        halves = []
        for cols, up in zip(chunk_cols(j), ups):
            w = cfw_ref[:, cols]
            d2, d1 = _delayed(up, [tail_ref[i, :, cols] for i in range(n_tail)])
            halves.append(d2 * w[0:1] + d1 * w[1:2] + up * w[2:3] + cfb_ref[:, cols])
            for i, grp in enumerate(_tail_groups(up, n_tail)):
                tail_ref[i, :, cols] = grp
                fc_ref[i:i + 1, cols] = grp[SUBLANES - 1:SUBLANES, :]
        g, val = halves
        return (g * jax.nn.sigmoid(g) * val).astype(BF16)

    state = attention_residual(0)
    done = None
    for g in range(FFN_SPLIT):
        x1, u2 = state
        up_project = lambda j: [_dot(u2, wup_ref[:, cols]) for cols in chunk_cols(j)]
        acc = jnp.zeros((sub, D_MODEL), F32)
        ups = up_project(0)
        for j in range(n_chunks):
            ups_next = up_project(j + 1) if j + 1 < n_chunks else None
            acc = acc + _dot(hidden(j, ups), wdn_ref[j * FF_CHUNK:(j + 1) * FF_CHUNK, :])
            ups = ups_next
            if j == FFN_SLOT and done is not None:
                finish(*done)
                done = None
            if j == FFN_SLOT and g + 1 < FFN_SPLIT:
                state = attention_residual(g + 1)
        done = (g, x1, acc)
    finish(*done)


def _ffn_prompt(x, attn, lru, mod, g2, w_out, w_up, cfw, cfb, w_dn):
    bsz, s_len, _ = x.shape
    tm = TM_FFN
    d_ff = w_dn.shape[0]
    row_spec = lambda w: pl.BlockSpec((None, tm, w), lambda b, i: (b, i, 0))
    mod_spec = lambda k: _prompt_mod_spec(mod, k)
    return pl.pallas_call(
        _ffn_prompt_kernel,
        grid=(bsz, s_len // tm),
        in_specs=[row_spec(D_MODEL), row_spec(ATT_WIDTH), row_spec(LRU_WIDTH),
                  mod_spec(2), mod_spec(3), mod_spec(4), mod_spec(5), _const_spec(g2.shape),
                  _const_spec(w_out.shape), _const_spec(w_up.shape), _const_spec(cfw.shape),
                  _const_spec(cfb.shape), _const_spec(w_dn.shape)],
        out_specs=[row_spec(D_MODEL), pl.BlockSpec((None, FFN_CONV - 1, 2 * d_ff), lambda b, i: (b, 0, 0))],
        out_shape=[jax.ShapeDtypeStruct((bsz, s_len, D_MODEL), F32),
                   jax.ShapeDtypeStruct((bsz, FFN_CONV - 1, 2 * d_ff), F32)],
        scratch_shapes=[pltpu.VMEM((FFN_SPLIT, D_MODEL // LANES, tm // FFN_SPLIT, LANES), F32),
                        pltpu.VMEM((FFN_SPLIT, D_MODEL // LANES, tm // FFN_SPLIT, LANES), F32),
                        pltpu.VMEM((FFN_CONV - 1, SUBLANES, 2 * d_ff), F32)],
        compiler_params=pltpu.CompilerParams(dimension_semantics=("arbitrary", "arbitrary"),
                                             vmem_limit_bytes=VMEM_LIMIT),
        name="ffn_prompt",
    )(x, attn, lru, mod, mod, mod, mod, g2, w_out, w_up, cfw, cfb, w_dn)


def _inproj_sample_kernel(x_ref, mod_ref, g1_ref, win_ref, gq_ref, gk_ref, gmat_ref,
                          cos_ref, sa_ref, sb_ref, cw_ref, cb_ref, wr_ref, br_ref, wi_ref, bi_ref, lam_ref,
                          st_ref, h0_ref,
                          q_ref, k_ref, v_ref, lo_ref, lc_ref, lh_ref):
    nb = h0_ref.shape[0]
    nt = x_ref.shape[0] // nb
    rep = lambda a: jnp.concatenate([a] * nt, axis=0)
    sh1 = rep(mod_ref[0:nb, 0:D_MODEL])
    sc1 = rep(mod_ref[0:nb, D_MODEL:2 * D_MODEL])
    u = _modulated_norm(x_ref[...], g1_ref[...], sc1, sh1).astype(BF16)
    per_t = lambda r: jnp.concatenate([jnp.broadcast_to(r[t:t + 1, :], (nb, LANES)) for t in range(nt)], axis=0)
    cos, sa, sb = per_t(cos_ref[...]), per_t(sa_ref[...]), per_t(sb_ref[...])
    gmat = gmat_ref[...]

    q = _group_norm_rope(_dot(u, win_ref[:, 0:ATT_WIDTH]), gq_ref[...], gmat, cos, sa, sb)
    q_ref[...] = q * QK_SCALE_LOG2
    k_ref[...] = _group_norm_rope(_dot(u, win_ref[:, ATT_WIDTH:2 * ATT_WIDTH]), gk_ref[...], gmat, cos, sa, sb)
    v_ref[...] = _dot(u, win_ref[:, 2 * ATT_WIDTH:3 * ATT_WIDTH])

    c0 = 3 * ATT_WIDTH
    lx = _dot(u, win_ref[:, c0:c0 + LRU_WIDTH])
    lg = _dot(u, win_ref[:, c0 + LRU_WIDTH:c0 + 2 * LRU_WIDTH])
    pad = [st_ref[i] for i in range(LRU_CONV - 1)] + [lx[t * nb:(t + 1) * nb, :] for t in range(nt)]
    for i in range(LRU_CONV - 1):
        lc_ref[i] = pad[nt + i]
    cw = cw_ref[...]
    xcs = []
    for t in range(nt):
        xc = pad[t] * cw[0:1]
        for kk in range(1, LRU_CONV):
            xc = xc + pad[t + kk] * cw[kk:kk + 1]
        xcs.append(xc + cb_ref[...])
    a, gx = _lru_gates(jnp.concatenate(xcs, axis=0), wr_ref, br_ref[...], wi_ref, bi_ref[...],
                       _neg_c_softplus(lam_ref[...]))
    h = h0_ref[...]
    hs = []
    for t in range(nt):
        h = a[t * nb:(t + 1) * nb, :] * h + gx[t * nb:(t + 1) * nb, :]
        hs.append(h)
    lh_ref[...] = h
    lo_ref[...] = (jnp.concatenate(hs, axis=0) * jax.nn.gelu(lg, approximate=True)).astype(BF16)


def _inproj_sample(x_tb, mod_s, g1, w_in, gq, gk, gmat, tabs_s, cw, cb, wr, br, wi, bi, lam, st, h0):
    m = x_tb.shape[0]
    nb = h0.shape[0]
    act = jax.ShapeDtypeStruct((m, ATT_WIDTH), F32)
    return pl.pallas_call(
        _inproj_sample_kernel,
        out_shape=[act, act, act, jax.ShapeDtypeStruct((m, LRU_WIDTH), BF16),
                   jax.ShapeDtypeStruct((LRU_CONV - 1, nb, LRU_WIDTH), F32),
                   jax.ShapeDtypeStruct((nb, LRU_WIDTH), F32)],
        compiler_params=pltpu.CompilerParams(vmem_limit_bytes=VMEM_LIMIT),
        name="inproj_sample",
    )(x_tb, mod_s, g1, w_in, gq, gk, gmat, *tabs_s, cw, cb, wr, br, wi, bi, lam, st, h0)


def _sample_attention_steps(bounded, q_ref, kn_ref, vn_ref, lam, gs, kt_refs, v_refs, out):
    n_pages = len(kt_refs)
    page = kt_refs[0].shape[1]
    nq = q_ref.shape[0]
    hrows = 2 * nq

    lane = lax.broadcasted_iota(jnp.int32, (nq, V_HEAD_DIM), 1)
    qh = []
    for h in range(ATT_HEADS):
        q8 = q_ref[:, h * V_HEAD_DIM:(h + 1) * V_HEAD_DIM]
        qh.append(jnp.concatenate([jnp.where(lane < QK_SUB_DIM, q8, 0.0), jnp.where(lane >= QK_SUB_DIM, q8, 0.0)],
                                  axis=0).astype(BF16))

    zpad = jnp.zeros((page - nq, V_HEAD_DIM), F32)
    rt = lax.broadcasted_iota(jnp.int32, (hrows, page), 0) % nq
    ct = lax.broadcasted_iota(jnp.int32, (hrows, page), 1)
    ps, ls = [], []
    for h in range(ATT_HEADS):
        rows = slice(h * V_HEAD_DIM, (h + 1) * V_HEAD_DIM)
        kt_all = jnp.concatenate([kt_refs[j][rows, :] for j in range(n_pages)], axis=1).astype(BF16)
        k_new = jnp.concatenate([kn_ref[:, rows], zpad], axis=0).astype(BF16)
        s_new = jnp.where(ct <= rt, _dot_nt(qh[h], k_new), NEG)
        s = jnp.concatenate([_dot(qh[h], kt_all), s_new], axis=1)
        if not bounded:
            s = s - jnp.max(s, axis=1, keepdims=True)
        p = jnp.exp2(s)
        ls.append(jnp.sum(p, axis=1, keepdims=True))
        ps.append(p.astype(BF16))
        yield
    outs = []
    for h in range(0, ATT_HEADS, 2):
        v_pair = jnp.concatenate([jnp.concatenate(
            [v_refs[j][pl.ds(hh, page, stride=ATT_HEADS), :] for j in range(n_pages)]
            + [vn_ref[:, hh * V_HEAD_DIM:(hh + 1) * V_HEAD_DIM], zpad], axis=0) for hh in (h, h + 1)],
            axis=1).astype(BF16)
        o_pair = _dot(jnp.concatenate([ps[h], ps[h + 1]], axis=0), v_pair)
        for i, hh in enumerate((h, h + 1)):
            o = o_pair[i * hrows:(i + 1) * hrows, i * V_HEAD_DIM:(i + 1) * V_HEAD_DIM] / ls[hh]
            outs.append(_subln(o[0:nq, :] - lam * o[nq:hrows, :], gs))
        yield
    out.append(jnp.concatenate(outs, axis=1))


def _ffn_sample_kernel(x_ref, at_ref, lo_ref, mod_ref, g2_ref, wout_ref, wupg_ref, wupv_ref, cfwg_ref, cfwv_ref,
                       cfbg_ref, cfbv_ref, wdn_ref, stg_ref, stv_ref,
                       y_ref, fcg_ref, fcv_ref, x1_ref, u2_ref, acc_ref):
    j = pl.program_id(0)
    nb = stg_ref.shape[1]
    nt = x_ref.shape[0] // nb
    rep = lambda a: jnp.concatenate([a] * nt, axis=0)

    @pl.when(j == 0)
    def _():
        gt1 = rep(mod_ref[0:nb, 2 * D_MODEL:3 * D_MODEL])
        sh2 = rep(mod_ref[0:nb, 3 * D_MODEL:4 * D_MODEL])
        sc2 = rep(mod_ref[0:nb, 4 * D_MODEL:5 * D_MODEL])
        mix = jnp.concatenate([at_ref[...].astype(BF16), lo_ref[...]], axis=1)
        x1 = x_ref[...] + gt1 * _dot(mix, wout_ref[...])
        x1_ref[...] = x1
        u2_ref[...] = _modulated_norm(x1, g2_ref[...], sc2, sh2).astype(BF16)
        acc_ref[...] = jnp.zeros(acc_ref.shape, F32)

    u2 = u2_ref[...]
    halves = []
    for wup_ref, cfw_ref, cfb_ref, st_ref, fc_ref in ((wupg_ref, cfwg_ref, cfbg_ref, stg_ref, fcg_ref),
                                                      (wupv_ref, cfwv_ref, cfbv_ref, stv_ref, fcv_ref)):
        up = _dot(u2, wup_ref[...])
        pad = [st_ref[i] for i in range(FFN_CONV - 1)] + [up[t * nb:(t + 1) * nb, :] for t in range(nt)]
        for i in range(FFN_CONV - 1):
            fc_ref[i] = pad[nt + i]
        w = cfw_ref[...]
        b = cfb_ref[...]
        hcs = []
        for t in range(nt):
            hc = pad[t] * w[0:1]
            for kk in range(1, FFN_CONV):
                hc = hc + pad[t + kk] * w[kk:kk + 1]
            hcs.append(hc + b)
        halves.append(jnp.concatenate(hcs, axis=0))
    g, val = halves
    hmid = (g * jax.nn.sigmoid(g) * val).astype(BF16)
    acc_ref[...] += _dot(hmid, wdn_ref[...])

    @pl.when(j == pl.num_programs(0) - 1)
    def _():
        gt2 = rep(mod_ref[0:nb, 5 * D_MODEL:6 * D_MODEL])
        y_ref[...] = x1_ref[...] + gt2 * acc_ref[...]


def _ffn_sample(x_tb, attn_tb, lru_tb, mod, g2, w_out, w_up, cfw, cfb, w_dn, st):
    m = x_tb.shape[0]
    d_ff = w_dn.shape[0]
    nb = st.shape[1]
    n_chunks = d_ff // FF_CHUNK
    whole = lambda a: pl.BlockSpec(a.shape, lambda j: (0,) * a.ndim)
    col = lambda rows, half: pl.BlockSpec((rows, FF_CHUNK), lambda j: (0, half * n_chunks + j))
    st_spec = lambda half: pl.BlockSpec((FFN_CONV - 1, nb, FF_CHUNK), lambda j: (0, 0, half * n_chunks + j))
    fc_spec = pl.BlockSpec((FFN_CONV - 1, nb, FF_CHUNK), lambda j: (0, 0, j))
    fc_shape = jax.ShapeDtypeStruct((FFN_CONV - 1, nb, d_ff), F32)
    return pl.pallas_call(
        _ffn_sample_kernel,
        grid=(n_chunks,),
        in_specs=[whole(x_tb), whole(attn_tb), whole(lru_tb), whole(mod), whole(g2), whole(w_out),
                  col(D_MODEL, 0), col(D_MODEL, 1), col(FFN_CONV, 0), col(FFN_CONV, 1), col(1, 0), col(1, 1),
                  pl.BlockSpec((FF_CHUNK, D_MODEL), lambda j: (j, 0)), st_spec(0), st_spec(1)],
        out_specs=[pl.BlockSpec((m, D_MODEL), lambda j: (0, 0)), fc_spec, fc_spec],
        out_shape=[jax.ShapeDtypeStruct((m, D_MODEL), F32), fc_shape, fc_shape],
        scratch_shapes=[pltpu.VMEM((m, D_MODEL), F32), pltpu.VMEM((m, D_MODEL), BF16),
                        pltpu.VMEM((m, D_MODEL), F32)],
        compiler_params=pltpu.CompilerParams(dimension_semantics=("arbitrary",), vmem_limit_bytes=VMEM_LIMIT),
        name="ffn_sample",
    )(x_tb, attn_tb, lru_tb, mod, g2, w_out, w_up, w_up, cfw, cfw, cfb, cfb, w_dn, st, st)


def _block_diag_halves(w):
    n, bd, _ = w.shape
    eye = jnp.eye(n // 2, dtype=w.dtype)
    halves = [jnp.einsum('nij,nm->nimj', w[s * (n // 2):(s + 1) * (n // 2)], eye).reshape(n // 2 * bd, n // 2 * bd)
              for s in range(2)]
    return jnp.stack(halves).astype(BF16)


def kernel(x_prompt, x_sample, cache_k, cache_v, page_table, state_lru_conv, state_lru_h, state_ffn_conv, c_prompt, c_sample, g_norm1, g_norm2, w_ada, b_ada, w_in, g_q, g_k, lam_q1, lam_k1, lam_q2, lam_k2, g_subln, w_out, conv_lru_w, conv_lru_b, w_rgate, b_rgate, w_igate, b_igate, lru_lambda, w_up, conv_ffn_w, conv_ffn_b, w_down):
    depth = w_in.shape[0]
    assert depth == 1, "single-layer step"
    bsz, s_len, _ = x_prompt.shape
    nb, nt, _ = x_sample.shape
    n_pages, page = page_table.shape[1], cache_k.shape[2]
    past_len = n_pages * page
    d_ff = w_down.shape[1]

    w_in_b = w_in[0].astype(BF16)
    w_out_b = w_out[0].astype(BF16)
    w_up_b = w_up[0].astype(BF16)
    w_dn_b = w_down[0].astype(BF16)
    wr = _block_diag_halves(w_rgate[0])
    wi = _block_diag_halves(w_igate[0])
    br = b_rgate[0].reshape(1, LRU_WIDTH)
    bi = b_igate[0].reshape(1, LRU_WIDTH)
    n_grp = ATT_WIDTH // QK_SUB_DIM
    gq = jnp.tile(g_q[0], n_grp)[None, :]
    gk = jnp.tile(g_k[0], n_grp)[None, :]
    grp = jnp.arange(ATT_WIDTH) // QK_SUB_DIM
    gmat = jnp.where(grp[:, None] == grp[None, :], 1.0 / QK_SUB_DIM, 0.0).astype(BF16)
    g1, g2 = g_norm1, g_norm2
    cw, cb = conv_lru_w[0], conv_lru_b
    cfw, cfb = conv_ffn_w[0], conv_ffn_b
    lam = lru_lambda
    lams = (lam_q1, lam_k1, lam_q2, lam_k2)
    gs = g_subln

    assert nb % SUBLANES == 0 and bsz <= SUBLANES
    c_all = jnp.concatenate([c_sample, c_prompt, jnp.zeros((SUBLANES - bsz, D_MODEL), F32)], axis=0)
    mod = _ada(c_all, w_ada[0], b_ada)

    tabs = _rope_tables(s_len)
    pad_rows = SUBLANES - nt
    tabs_s = tuple(t[past_len:past_len + SUBLANES] for t in tabs)

    q_p, kf_p, kb_p, vf_p, vb_p, lo_p, lc_p, lh_p = _inproj_prompt(
        x_prompt, mod, g1, w_in_b, gq, gk, gmat, tabs, cw, cb, wr, br, wi, bi, lam)
    x_tb = x_sample.transpose(1, 0, 2).reshape(nt * nb, D_MODEL)
    st_lru = state_lru_conv[0].transpose(1, 0, 2)
    q_s, k_s, v_s, lo_s, lc_s, lh_s = _inproj_sample(
        x_tb, mod, g1, w_in_b, gq, gk, gmat, tabs_s, cw, cb, wr, br, wi, bi, lam, st_lru, state_lru_h[0])
    to_bt = lambda a: a.reshape(nt, nb, -1).transpose(1, 0, 2)
    pad_t = lambda a: jnp.pad(a, ((0, 0), (0, pad_rows), (0, 0)))
    k_bt, v_bt = to_bt(k_s), to_bt(v_s)

    score_bound = QK_NORM_BOUND ** 2 * QK_SCALE_LOG2 * jnp.max(jnp.abs(g_q)) * jnp.max(jnp.abs(g_k))
    cache_kt = cache_k[0].transpose(0, 2, 3, 4, 1).reshape(-1, ATT_WIDTH, page)
    cache_vr = cache_v[0].reshape(-1, page * ATT_HEADS, V_HEAD_DIM)
    at_p, at_s = lax.cond(score_bound <= SCORE_LOG2_LIMIT,
                          functools.partial(_attn, True), functools.partial(_attn, False),
                          page_table, q_p, kb_p, vb_p, pad_t(to_bt(q_s)), pad_t(k_bt), pad_t(v_bt),
                          cache_kt, cache_vr, *lams, gs)

    y_p, fc_p = _ffn_prompt(x_prompt, at_p, lo_p, mod, g2, w_out_b, w_up_b, cfw, cfb, w_dn_b)
    at_tb = at_s[:, :nt].transpose(1, 0, 2).reshape(nt * nb, ATT_WIDTH)
    y_tb, fcg_s, fcv_s = _ffn_sample(x_tb, at_tb, lo_s, mod, g2, w_out_b, w_up_b, cfw, cfb, w_dn_b,
                                     state_ffn_conv[0].transpose(1, 0, 2))

    hd = (ATT_HEADS, 2, QK_SUB_DIM)
    return (y_p, to_bt(y_tb),
            kf_p.reshape(bsz, *hd, s_len).transpose(0, 4, 1, 2, 3)[None],
            vf_p.reshape(1, bsz, s_len, ATT_HEADS, V_HEAD_DIM),
            lc_p[None], lh_p.reshape(1, bsz, LRU_WIDTH), fc_p[None],
            k_bt.reshape(1, nb, nt, *hd), v_bt.reshape(1, nb, nt, ATT_HEADS, V_HEAD_DIM),
            lc_s.transpose(1, 0, 2)[None], lh_s[None],
            jnp.concatenate([fcg_s, fcv_s], axis=-1).transpose(1, 0, 2)[None])
```

```python
import functools
import math

import jax
import jax.numpy as jnp
from jax import lax
from jax.experimental import pallas as pl
from jax.experimental.pallas import tpu as pltpu

F32 = jnp.float32
BF16 = jnp.bfloat16

D_MODEL = 1024
ATT_WIDTH = 512
LRU_WIDTH = 512
ATT_HEADS = 4
V_HEAD_DIM = 128
QK_SUB_DIM = 64
ROPE_DIM = 16
ROPE_THETA = 500000.0
LRU_BLOCKS = 8
LRU_C = 8.0
LRU_CONV = 4
FFN_CONV = 3
EPS = 1e-6
LAM_INIT = 0.8 - 0.6 * math.exp(-0.3 * 0)
QK_SCALE = QK_SUB_DIM ** -0.5
QK_SCALE_LOG2 = QK_SCALE * math.log2(math.e)
SCORE_LOG2_LIMIT = 64.0
QK_NORM_BOUND = 1.01 * math.sqrt(QK_SUB_DIM)

LANES = 128
SUBLANES = 8
VMEM_LIMIT = 56 * 1024 * 1024

TM_IN = 1024
TQ = 512
TK = 512
TM_FFN = 512
FFN_SPLIT = 2
FFN_SLOT = 2
FF_CHUNK = 256
NEG = -1e30


def _dot(a, b):
    return jnp.dot(a, b, preferred_element_type=F32)


def _dot_nt(a, b):
    return lax.dot_general(a, b, (((1,), (1,)), ((), ())), preferred_element_type=F32)


def _const_spec(shape):
    nd = len(shape)
    return pl.BlockSpec(shape, lambda *_: (0,) * nd, pipeline_mode=pl.Buffered(1))


def _prompt_mod_spec(mod, k):
    return pl.BlockSpec((SUBLANES, D_MODEL), lambda b, i: (mod.shape[0] // SUBLANES - 1, k))


def _seq_row(ref):
    return ref[pl.ds(pl.program_id(0), 1), :]


def _modulated_norm(x, g, sc, sh):
    xn = x * lax.rsqrt(jnp.mean(x * x, axis=-1, keepdims=True) + EPS) * g
    return xn * (1.0 + sc) + sh


def _to_segment_major(val, buf_ref):
    tm, w = val.shape
    n = tm // SUBLANES
    for c in range(w // LANES):
        for s in range(SUBLANES):
            for r0 in range(0, n, SUBLANES):
                t0 = s * n + r0
                buf_ref[c, pl.ds(r0 * SUBLANES + s, SUBLANES, stride=SUBLANES), :] = (
                    val[t0:t0 + SUBLANES, c * LANES:(c + 1) * LANES])
    return jnp.concatenate([buf_ref[c] for c in range(w // LANES)], axis=1)


def _to_time_major(val, buf_ref):
    tm, w = val.shape
    n = tm // SUBLANES
    for c in range(w // LANES):
        buf_ref[c] = val[:, c * LANES:(c + 1) * LANES]
    rows = []
    for s in range(SUBLANES):
        for r0 in range(0, n, SUBLANES):
            rows.append(jnp.concatenate(
                [buf_ref[c, pl.ds(r0 * SUBLANES + s, SUBLANES, stride=SUBLANES), :] for c in range(w // LANES)], axis=1))
    return jnp.concatenate(rows, axis=0)


def _tail_groups(cur, k):
    tm = cur.shape[0]
    return [cur[tm - (k - i) * SUBLANES:tm - (k - i - 1) * SUBLANES, :] for i in range(k)]


def _delayed(cur, prev_tail):
    k = len(prev_tail)
    tm = cur.shape[0]
    first = lax.broadcasted_iota(jnp.int32, prev_tail[0].shape, 0) == 0
    heads = [jnp.where(first, pltpu.roll(p, 1, 0), pltpu.roll(c, 1, 0))
             for p, c in zip(prev_tail, _tail_groups(cur, k))]
    return [jnp.concatenate(heads[k - d:] + [cur[:tm - d * SUBLANES, :]], axis=0) for d in range(k, 0, -1)]


def _group_norm_rope(t, g_tiled, gmat, cos, sin_a, sin_b):
    ms = _dot((t * t).astype(BF16), gmat)
    tn = t * lax.rsqrt(ms + EPS) * g_tiled
    outs = []
    for h in range(ATT_WIDTH // LANES):
        th = tn[:, h * LANES:(h + 1) * LANES]
        outs.append(th * cos + pltpu.roll(th, LANES - 8, 1) * sin_a + pltpu.roll(th, 8, 1) * sin_b)
    return jnp.concatenate(outs, axis=1)


def _lru_gates(xc, wr_ref, br, wi_ref, bi, neg_c_softplus):
    half = LRU_WIDTH // 2
    xb = xc.astype(BF16)
    lo, hi = xb[:, :half], xb[:, half:]
    r = jax.nn.sigmoid(jnp.concatenate([_dot(lo, wr_ref[0]), _dot(hi, wr_ref[1])], axis=1) + br)
    ig = jax.nn.sigmoid(jnp.concatenate([_dot(lo, wi_ref[0]), _dot(hi, wi_ref[1])], axis=1) + bi)
    log_a = neg_c_softplus * r
    a = jnp.exp(log_a)
    one_minus_a2 = -jnp.tanh(log_a) * (a * a + 1.0)
    return a, jnp.sqrt(one_minus_a2) * (ig * xc)


def _neg_c_softplus(lam):
    z = -lam
    return -LRU_C * (jnp.maximum(z, 0.0) + jnp.log1p(jnp.exp(-jnp.abs(z))))


def _diff_lambda(lq1, lk1, lq2, lk2):
    s1 = jnp.sum(lq1 * lk1, axis=-1, keepdims=True)
    s2 = jnp.sum(lq2 * lk2, axis=-1, keepdims=True)
    return jnp.exp(s1) - jnp.exp(s2) + LAM_INIT


def _subln(o, g):
    return o * lax.rsqrt(jnp.mean(o * o, axis=-1, keepdims=True) + EPS) * g * (1.0 - LAM_INIT)


def _ada_kernel(c_ref, w_ref, b_ref, o_ref):
    c = c_ref[...]
    s = (c * jax.nn.sigmoid(c)).astype(BF16)
    o_ref[...] = _dot(s, w_ref[...].astype(BF16)) + b_ref[...]


def _ada(c_all, w_ada, b_ada):
    m = c_all.shape[0]
    n = w_ada.shape[1]
    tn = 1024
    return pl.pallas_call(
        _ada_kernel,
        grid=(n // tn,),
        in_specs=[pl.BlockSpec((m, D_MODEL), lambda j: (0, 0)),
                  pl.BlockSpec((D_MODEL, tn), lambda j: (0, j)),
                  pl.BlockSpec((1, tn), lambda j: (0, j))],
        out_specs=pl.BlockSpec((m, tn), lambda j: (0, j)),
        out_shape=jax.ShapeDtypeStruct((m, n), F32),
        compiler_params=pltpu.CompilerParams(dimension_semantics=("arbitrary",), vmem_limit_bytes=VMEM_LIMIT),
        name="ada_mod",
    )(c_all, w_ada, b_ada)


def _rope_table_kernel(freq_ref, ma_ref, mb_ref, c_ref, sa_ref, sb_ref):
    tm = c_ref.shape[0]
    pos = (pl.program_id(0) * tm + lax.broadcasted_iota(jnp.int32, (tm, LANES), 0)).astype(F32)
    ang = pos * freq_ref[...]
    s = jnp.sin(ang)
    c_ref[...] = jnp.cos(ang)
    sa_ref[...] = -s * ma_ref[...]
    sb_ref[...] = s * mb_ref[...]


def _rope_tables(n_pos):
    half = ROPE_DIM // 2
    d = jnp.arange(LANES) % QK_SUB_DIM
    freqs = ROPE_THETA ** (-(d % half).astype(F32) * 2.0 / ROPE_DIM)
    freq_lane = jnp.where(d < ROPE_DIM, freqs, 0.0).astype(F32)[None, :]
    mask_a = (d < half).astype(F32)[None, :]
    mask_b = ((d >= half) & (d < ROPE_DIM)).astype(F32)[None, :]
    tm = 512
    row = pl.BlockSpec((1, LANES), lambda i: (0, 0))
    tab = pl.BlockSpec((tm, LANES), lambda i: (i, 0))
    shp = jax.ShapeDtypeStruct((n_pos, LANES), F32)
    return pl.pallas_call(
        _rope_table_kernel,
        grid=(n_pos // tm,),
        in_specs=[row, row, row],
        out_specs=[tab, tab, tab],
        out_shape=[shp, shp, shp],
        compiler_params=pltpu.CompilerParams(dimension_semantics=("arbitrary",)),
        name="rope_tables",
    )(freq_lane, mask_a, mask_b)


def _inproj_prompt_kernel(x_ref, sh_ref, sc_ref, g1_ref, win_ref, gq_ref, gk_ref, gmat_ref,
                          cos_ref, sa_ref, sb_ref, cw_ref, cb_ref, wr_ref, br_ref, wi_ref, bi_ref, lam_ref,
                          q_ref, kf_ref, kb_ref, vf_ref, vb_ref, lo_ref, lc_ref, lh_ref,
                          lbuf_ref, hbuf_ref, tail_ref, hcar_ref):
    tm = x_ref.shape[0]

    @pl.when(pl.program_id(1) == 0)
    def _():
        tail_ref[...] = jnp.zeros(tail_ref.shape, F32)
        hcar_ref[...] = jnp.zeros(hcar_ref.shape, F32)

    u = _modulated_norm(x_ref[...], g1_ref[...], _seq_row(sc_ref), _seq_row(sh_ref)).astype(BF16)
    cos, sa, sb = cos_ref[...], sa_ref[...], sb_ref[...]
    gmat = gmat_ref[...]

    q = _group_norm_rope(_dot(u, win_ref[:, 0:ATT_WIDTH]), gq_ref[...], gmat, cos, sa, sb)
    q_ref[...] = (q * QK_SCALE_LOG2).T.astype(BF16)
    k = _group_norm_rope(_dot(u, win_ref[:, ATT_WIDTH:2 * ATT_WIDTH]), gk_ref[...], gmat, cos, sa, sb)
    kf_ref[...] = k.T
    kb_ref[...] = k.astype(BF16)
    v = _dot(u, win_ref[:, 2 * ATT_WIDTH:3 * ATT_WIDTH])
    for h in range(ATT_HEADS):
        vf_ref[pl.ds(h, tm, stride=ATT_HEADS), :] = v[:, h * V_HEAD_DIM:(h + 1) * V_HEAD_DIM]
    vb_ref[...] = v.T.astype(BF16)

    c0 = 3 * ATT_WIDTH
    n_tail = LRU_CONV - 1
    lx = _to_segment_major(_dot(u, win_ref[:, c0:c0 + LRU_WIDTH]), lbuf_ref)
    cw = cw_ref[...]
    d3, d2, d1 = _delayed(lx, [tail_ref[i] for i in range(n_tail)])
    xc = d3 * cw[0:1] + d2 * cw[1:2] + d1 * cw[2:3] + lx * cw[3:4] + cb_ref[...]
    for i, grp in enumerate(_tail_groups(lx, n_tail)):
        tail_ref[i] = grp
        lc_ref[i:i + 1, :] = grp[SUBLANES - 1:SUBLANES, :]

    a, gx = _lru_gates(xc, wr_ref, br_ref[...], wi_ref, bi_ref[...], _neg_c_softplus(lam_ref[...]))

    n = tm // SUBLANES
    grp = lambda arr, r: arr[r * SUBLANES:(r + 1) * SUBLANES, :]
    ps, hs = [grp(a, 0)], [grp(gx, 0)]
    for r in range(1, n):
        ar = grp(a, r)
        ps.append(ar * ps[-1])
        hs.append(ar * hs[-1] + grp(gx, r))
    p_end, h_end = ps[-1], hs[-1]
    entering = [hcar_ref[...]]
    for s in range(1, SUBLANES):
        entering.append(p_end[s - 1:s, :] * entering[-1] + h_end[s - 1:s, :])
    h_last = p_end[SUBLANES - 1:SUBLANES, :] * entering[-1] + h_end[SUBLANES - 1:SUBLANES, :]
    hcar_ref[...] = h_last
    lh_ref[...] = h_last
    enter = jnp.concatenate(entering, axis=0)
    states = jnp.concatenate([hs[r] + ps[r] * enter for r in range(n)], axis=0)

    lg = _dot(u, win_ref[:, c0 + LRU_WIDTH:c0 + 2 * LRU_WIDTH])
    lo_ref[...] = (_to_time_major(states, hbuf_ref) * jax.nn.gelu(lg, approximate=True)).astype(BF16)


def _inproj_prompt(x, mod, g1, w_in, gq, gk, gmat, tabs, cw, cb, wr, br, wi, bi, lam):
    bsz, s_len, _ = x.shape
    tm = TM_IN
    row_spec = lambda w: pl.BlockSpec((None, tm, w), lambda b, i: (b, i, 0))
    tab_spec = pl.BlockSpec((tm, LANES), lambda b, i: (i, 0))
    mod_spec = lambda k: _prompt_mod_spec(mod, k)
    act = lambda dt: jax.ShapeDtypeStruct((bsz, s_len, ATT_WIDTH), dt)
    act_t = lambda dt: jax.ShapeDtypeStruct((bsz, ATT_WIDTH, s_len), dt)
    col_spec = pl.BlockSpec((None, ATT_WIDTH, tm), lambda b, i: (b, 0, i))
    return pl.pallas_call(
        _inproj_prompt_kernel,
        grid=(bsz, s_len // tm),
        in_specs=[row_spec(D_MODEL), mod_spec(0), mod_spec(1), _const_spec(g1.shape), _const_spec(w_in.shape),
                  _const_spec(gq.shape), _const_spec(gk.shape), _const_spec(gmat.shape),
                  tab_spec, tab_spec, tab_spec,
                  _const_spec(cw.shape), _const_spec(cb.shape), _const_spec(wr.shape), _const_spec(br.shape),
                  _const_spec(wi.shape), _const_spec(bi.shape), _const_spec(lam.shape)],
        out_specs=[col_spec, col_spec, row_spec(ATT_WIDTH),
                   pl.BlockSpec((None, tm * ATT_HEADS, V_HEAD_DIM), lambda b, i: (b, i, 0)),
                   col_spec, row_spec(LRU_WIDTH),
                   pl.BlockSpec((None, LRU_CONV - 1, LRU_WIDTH), lambda b, i: (b, 0, 0)),
                   pl.BlockSpec((None, 1, LRU_WIDTH), lambda b, i: (b, 0, 0))],
        out_shape=[act_t(BF16), act_t(F32), act(BF16),
                   jax.ShapeDtypeStruct((bsz, s_len * ATT_HEADS, V_HEAD_DIM), F32), act_t(BF16), act(BF16),
                   jax.ShapeDtypeStruct((bsz, LRU_CONV - 1, LRU_WIDTH), F32),
                   jax.ShapeDtypeStruct((bsz, 1, LRU_WIDTH), F32)],
        scratch_shapes=[pltpu.VMEM((LRU_WIDTH // LANES, tm, LANES), F32),
                        pltpu.VMEM((LRU_WIDTH // LANES, tm, LANES), F32),
                        pltpu.VMEM((LRU_CONV - 1, SUBLANES, LRU_WIDTH), F32),
                        pltpu.VMEM((1, LRU_WIDTH), F32)],
        compiler_params=pltpu.CompilerParams(dimension_semantics=("arbitrary", "arbitrary"),
                                             vmem_limit_bytes=VMEM_LIMIT),
        name="inproj_prompt",
    )(x, mod, mod, g1, w_in, gq, gk, gmat, *tabs, cw, cb, wr, br, wi, bi, lam)


def _alternate(*step_generators):
    live = list(step_generators)
    while live:
        for g in list(live):
            try:
                next(g)
            except StopIteration:
                live.remove(g)


def _page_copies(pt_ref, ckt_hbm, cv_hbm, kbuf_ref, vbuf_ref, sem_ref, seq, slot):
    copies = []
    for j in range(kbuf_ref.shape[1]):
        pg = pt_ref[seq, j]
        copies.append(pltpu.make_async_copy(ckt_hbm.at[pg], kbuf_ref.at[slot, j], sem_ref.at[slot]))
        copies.append(pltpu.make_async_copy(cv_hbm.at[pg], vbuf_ref.at[slot, j], sem_ref.at[slot]))
    return copies


def _attn_kernel(bounded, pt_ref, qt_ref, k_ref, vt_ref, lq1_ref, lk1_ref, lq2_ref, lk2_ref, gs_ref,
                 qs_ref, kn_ref, vn_ref, ckt_hbm, cv_hbm, o_ref, os_ref, kbuf_ref, vbuf_ref, sem_ref):
    n_pages = kbuf_ref.shape[1]
    step = (pl.program_id(0) * pl.num_programs(1) + pl.program_id(1)) * pl.num_programs(2) + pl.program_id(2)
    n_steps = pl.num_programs(0) * pl.num_programs(1) * pl.num_programs(2)
    slot = step % 2

    @pl.when(step == 0)
    def _():
        for cp in _page_copies(pt_ref, ckt_hbm, cv_hbm, kbuf_ref, vbuf_ref, sem_ref, step, slot):
            cp.start()

    @pl.when(step + 1 < n_steps)
    def _():
        for cp in _page_copies(pt_ref, ckt_hbm, cv_hbm, kbuf_ref, vbuf_ref, sem_ref, step + 1, 1 - slot):
            cp.start()

    tq = qt_ref.shape[1]
    qi = pl.program_id(2)
    qt = qt_ref[...].astype(F32)
    feat = lax.broadcasted_iota(jnp.int32, (V_HEAD_DIM, tq), 0)
    qc = (jnp.where(feat < QK_SUB_DIM, qt, 0.0).astype(BF16), jnp.where(feat >= QK_SUB_DIM, qt, 0.0).astype(BF16))

    def tile_steps(k0, width, carry, masked, out):
        k0 = pl.multiple_of(k0, TK)
        ks = k_ref[pl.ds(k0, width), :]
        vts = vt_ref[:, pl.ds(k0, width)]
        ss = []
        for c in range(2):
            ss.append(_dot(ks, qc[c]))
            yield
        for c in range(2):
            m, l, acc = carry[c]
            s = ss[c]
            if masked:
                keys = lax.broadcasted_iota(jnp.int32, (width, tq), 0)
                qs = lax.broadcasted_iota(jnp.int32, (width, tq), 1)
                s = jnp.where(keys <= qs, s, NEG)
            if bounded:
                p = jnp.exp2(s)
                l = l + jnp.sum(p, axis=0, keepdims=True)
                acc = acc + _dot(vts, p.astype(BF16))
            else:
                m_new = jnp.maximum(m, jnp.max(s, axis=0, keepdims=True))
                alpha = jnp.exp2(m - m_new)
                p = jnp.exp2(s - m_new)
                l = alpha * l + jnp.sum(p, axis=0, keepdims=True)
                acc = alpha * acc + _dot(vts, p.astype(BF16))
                m = m_new
            out.append((m, l, acc))
            yield

    def tile(k0, width, carry, masked):
        out = []
        for _ in tile_steps(k0, width, carry, masked, out):
            pass
        return tuple(out)

    init = tuple((jnp.full((1, tq), NEG, F32), jnp.zeros((1, tq), F32), jnp.zeros((V_HEAD_DIM, tq), F32))
                 for _ in range(2))
    carry = lax.fori_loop(0, qi // 2, lambda j, c: tile(j * 2 * TK, 2 * TK, c, False), init)
    carry = lax.cond(qi % 2 == 1, lambda c: tile((qi - 1) * TK, TK, c, False), lambda c: c, carry)
    for cp in _page_copies(pt_ref, ckt_hbm, cv_hbm, kbuf_ref, vbuf_ref, sem_ref, step, slot):
        cp.wait()
    kt_refs = [kbuf_ref.at[slot, j] for j in range(n_pages)]
    v_refs = [vbuf_ref.at[slot, j] for j in range(n_pages)]
    lam = _diff_lambda(lq1_ref[...], lk1_ref[...], lq2_ref[...], lk2_ref[...])
    tile_out, sample_out = [], []
    _alternate(tile_steps(qi * TK, TK, carry, True, tile_out),
               _sample_attention_steps(bounded, qs_ref, kn_ref, vn_ref, lam, gs_ref[...], kt_refs, v_refs, sample_out))
    (_, l0, a0), (_, l1, a1) = tile_out
    ot = a0 / l0 - lam * (a1 / l1)
    o_ref[...] = _subln(ot.T, gs_ref[...]).astype(o_ref.dtype)
    os_ref[...] = sample_out[0]


def _attn(bounded, page_table, qt, k, vt, q8, kn8, vn8, cache_kt, cache_v, lq1, lk1, lq2, lk2, gs):
    bsz, s_len, _ = k.shape
    nb, n_pages = page_table.shape
    nq = q8.shape[1]
    page = cache_kt.shape[2]
    n_qt = s_len // TQ
    assert nb == bsz * ATT_HEADS * n_qt, "one sample sequence per grid step"
    seq = lambda b, h, i: (b * ATT_HEADS + h) * n_qt + i
    qt_spec = pl.BlockSpec((None, V_HEAD_DIM, TQ), lambda b, h, i, pt: (b, h, i))
    k_spec = pl.BlockSpec((None, s_len, V_HEAD_DIM), lambda b, h, i, pt: (b, 0, h))
    vt_spec = pl.BlockSpec((None, V_HEAD_DIM, s_len), lambda b, h, i, pt: (b, h, 0))
    o_spec = pl.BlockSpec((None, TQ, V_HEAD_DIM), lambda b, h, i, pt: (b, i, h))
    small = lambda a: pl.BlockSpec(a.shape, lambda b, h, i, pt: (0, 0))
    new_spec = pl.BlockSpec((None, nq, ATT_WIDTH), lambda b, h, i, pt: (seq(b, h, i), 0, 0))
    hbm = pl.BlockSpec(memory_space=pl.ANY)
    grid_spec = pltpu.PrefetchScalarGridSpec(
        num_scalar_prefetch=1,
        grid=(bsz, ATT_HEADS, n_qt),
        in_specs=[qt_spec, k_spec, vt_spec, small(lq1), small(lk1), small(lq2), small(lk2), small(gs),
                  new_spec, new_spec, new_spec, hbm, hbm],
        out_specs=[o_spec, new_spec],
        scratch_shapes=[pltpu.VMEM((2, n_pages, ATT_WIDTH, page), F32),
                        pltpu.VMEM((2, n_pages, page * ATT_HEADS, V_HEAD_DIM), F32),
                        pltpu.SemaphoreType.DMA((2,))],
    )
    return pl.pallas_call(
        functools.partial(_attn_kernel, bounded),
        grid_spec=grid_spec,
        out_shape=[jax.ShapeDtypeStruct((bsz, s_len, ATT_WIDTH), BF16),
                   jax.ShapeDtypeStruct((nb, nq, ATT_WIDTH), F32)],
        compiler_params=pltpu.CompilerParams(dimension_semantics=("arbitrary", "arbitrary", "arbitrary"),
                                             vmem_limit_bytes=VMEM_LIMIT),
        name="attn_bounded" if bounded else "attn_general",
    )(page_table, qt, k, vt, lq1, lk1, lq2, lk2, gs, q8, kn8, vn8, cache_kt, cache_v)


def _ffn_prompt_kernel(x_ref, at_ref, lo_ref, gt1_ref, sh2_ref, sc2_ref, gt2_ref, g2_ref,
                       wout_ref, wup_ref, cfw_ref, cfb_ref, wdn_ref,
                       y_ref, fc_ref, ubuf_ref, abuf_ref, tail_ref):
    tm = x_ref.shape[0]
    sub = tm // FFN_SPLIT
    d_ff = wdn_ref.shape[0]
    n_tail = FFN_CONV - 1
    n_chunks = d_ff // FF_CHUNK
    chunk_cols = lambda j: [slice(base + j * FF_CHUNK, base + (j + 1) * FF_CHUNK) for base in (0, d_ff)]

    @pl.when(pl.program_id(1) == 0)
    def _():
        tail_ref[...] = jnp.zeros(tail_ref.shape, F32)

    def attention_residual(g):
        rows = slice(g * sub, (g + 1) * sub)
        mix = jnp.concatenate([at_ref[rows, :], lo_ref[rows, :]], axis=1)
        x1 = x_ref[rows, :] + _seq_row(gt1_ref) * _dot(mix, wout_ref[...])
        u2 = _to_segment_major(_modulated_norm(x1, g2_ref[...], _seq_row(sc2_ref), _seq_row(sh2_ref)),
                               ubuf_ref.at[g]).astype(BF16)
        return x1, u2

    def finish(g, x1, acc):
        y_ref[g * sub:(g + 1) * sub, :] = x1 + _seq_row(gt2_ref) * _to_time_major(acc, abuf_ref.at[g])

    def hidden(j, ups):
        halves = []
        for cols, up in zip(chunk_cols(j), ups):
            w = cfw_ref[:, cols]
            d2, d1 = _delayed(up, [tail_ref[i, :, cols] for i in range(n_tail)])
            halves.append(d2 * w[0:1] + d1 * w[1:2] + up * w[2:3] + cfb_ref[:, cols])
            for i, grp in enumerate(_tail_groups(up, n_tail)):
                tail_ref[i, :, cols] = grp
                fc_ref[i:i + 1, cols] = grp[SUBLANES - 1:SUBLANES, :]
        g, val = halves
        return (g * jax.nn.sigmoid(g) * val).astype(BF16)

    state = attention_residual(0)
    done = None
    for g in range(FFN_SPLIT):
        x1, u2 = state
        up_project = lambda j: [_dot(u2, wup_ref[:, cols]) for cols in chunk_cols(j)]
        acc = jnp.zeros((sub, D_MODEL), F32)
        ups = up_project(0)
        for j in range(n_chunks):
            ups_next = up_project(j + 1) if j + 1 < n_chunks else None
            acc = acc + _dot(hidden(j, ups), wdn_ref[j * FF_CHUNK:(j + 1) * FF_CHUNK, :])
            ups = ups_next
            if j == FFN_SLOT and done is not None:
                finish(*done)
                done = None
            if j == FFN_SLOT and g + 1 < FFN_SPLIT:
                state = attention_residual(g + 1)
        done = (g, x1, acc)
    finish(*done)


def _ffn_prompt(x, attn, lru, mod, g2, w_out, w_up, cfw, cfb, w_dn):
    bsz, s_len, _ = x.shape
    tm = TM_FFN
    d_ff = w_dn.shape[0]
    row_spec = lambda w: pl.BlockSpec((None, tm, w), lambda b, i: (b, i, 0))
    mod_spec = lambda k: _prompt_mod_spec(mod, k)
    return pl.pallas_call(
        _ffn_prompt_kernel,
        grid=(bsz, s_len // tm),
        in_specs=[row_spec(D_MODEL), row_spec(ATT_WIDTH), row_spec(LRU_WIDTH),
                  mod_spec(2), mod_spec(3), mod_spec(4), mod_spec(5), _const_spec(g2.shape),
                  _const_spec(w_out.shape), _const_spec(w_up.shape), _const_spec(cfw.shape),
                  _const_spec(cfb.shape), _const_spec(w_dn.shape)],
        out_specs=[row_spec(D_MODEL), pl.BlockSpec((None, FFN_CONV - 1, 2 * d_ff), lambda b, i: (b, 0, 0))],
        out_shape=[jax.ShapeDtypeStruct((bsz, s_len, D_MODEL), F32),
                   jax.ShapeDtypeStruct((bsz, FFN_CONV - 1, 2 * d_ff), F32)],
        scratch_shapes=[pltpu.VMEM((FFN_SPLIT, D_MODEL // LANES, tm // FFN_SPLIT, LANES), F32),
                        pltpu.VMEM((FFN_SPLIT, D_MODEL // LANES, tm // FFN_SPLIT, LANES), F32),
                        pltpu.VMEM((FFN_CONV - 1, SUBLANES, 2 * d_ff), F32)],
        compiler_params=pltpu.CompilerParams(dimension_semantics=("arbitrary", "arbitrary"),
                                             vmem_limit_bytes=VMEM_LIMIT),
        name="ffn_prompt",
    )(x, attn, lru, mod, mod, mod, mod, g2, w_out, w_up, cfw, cfb, w_dn)


def _inproj_sample_kernel(x_ref, mod_ref, g1_ref, win_ref, gq_ref, gk_ref, gmat_ref,
                          cos_ref, sa_ref, sb_ref, cw_ref, cb_ref, wr_ref, br_ref, wi_ref, bi_ref, lam_ref,
                          st_ref, h0_ref,
                          q_ref, k_ref, v_ref, lo_ref, lc_ref, lh_ref):
    nb = h0_ref.shape[0]
    nt = x_ref.shape[0] // nb
    rep = lambda a: jnp.concatenate([a] * nt, axis=0)
    sh1 = rep(mod_ref[0:nb, 0:D_MODEL])
    sc1 = rep(mod_ref[0:nb, D_MODEL:2 * D_MODEL])
    u = _modulated_norm(x_ref[...], g1_ref[...], sc1, sh1).astype(BF16)
    per_t = lambda r: jnp.concatenate([jnp.broadcast_to(r[t:t + 1, :], (nb, LANES)) for t in range(nt)], axis=0)
    cos, sa, sb = per_t(cos_ref[...]), per_t(sa_ref[...]), per_t(sb_ref[...])
    gmat = gmat_ref[...]

    q = _group_norm_rope(_dot(u, win_ref[:, 0:ATT_WIDTH]), gq_ref[...], gmat, cos, sa, sb)
    q_ref[...] = q * QK_SCALE_LOG2
    k_ref[...] = _group_norm_rope(_dot(u, win_ref[:, ATT_WIDTH:2 * ATT_WIDTH]), gk_ref[...], gmat, cos, sa, sb)
    v_ref[...] = _dot(u, win_ref[:, 2 * ATT_WIDTH:3 * ATT_WIDTH])

    c0 = 3 * ATT_WIDTH
    lx = _dot(u, win_ref[:, c0:c0 + LRU_WIDTH])
    lg = _dot(u, win_ref[:, c0 + LRU_WIDTH:c0 + 2 * LRU_WIDTH])
    pad = [st_ref[i] for i in range(LRU_CONV - 1)] + [lx[t * nb:(t + 1) * nb, :] for t in range(nt)]
    for i in range(LRU_CONV - 1):
        lc_ref[i] = pad[nt + i]
    cw = cw_ref[...]
    xcs = []
    for t in range(nt):
        xc = pad[t] * cw[0:1]
        for kk in range(1, LRU_CONV):
            xc = xc + pad[t + kk] * cw[kk:kk + 1]
        xcs.append(xc + cb_ref[...])
    a, gx = _lru_gates(jnp.concatenate(xcs, axis=0), wr_ref, br_ref[...], wi_ref, bi_ref[...],
                       _neg_c_softplus(lam_ref[...]))
    h = h0_ref[...]
    hs = []
    for t in range(nt):
        h = a[t * nb:(t + 1) * nb, :] * h + gx[t * nb:(t + 1) * nb, :]
        hs.append(h)
    lh_ref[...] = h
    lo_ref[...] = (jnp.concatenate(hs, axis=0) * jax.nn.gelu(lg, approximate=True)).astype(BF16)


def _inproj_sample(x_tb, mod_s, g1, w_in, gq, gk, gmat, tabs_s, cw, cb, wr, br, wi, bi, lam, st, h0):
    m = x_tb.shape[0]
    nb = h0.shape[0]
    act = jax.ShapeDtypeStruct((m, ATT_WIDTH), F32)
    return pl.pallas_call(
        _inproj_sample_kernel,
        out_shape=[act, act, act, jax.ShapeDtypeStruct((m, LRU_WIDTH), BF16),
                   jax.ShapeDtypeStruct((LRU_CONV - 1, nb, LRU_WIDTH), F32),
                   jax.ShapeDtypeStruct((nb, LRU_WIDTH), F32)],
        compiler_params=pltpu.CompilerParams(vmem_limit_bytes=VMEM_LIMIT),
        name="inproj_sample",
    )(x_tb, mod_s, g1, w_in, gq, gk, gmat, *tabs_s, cw, cb, wr, br, wi, bi, lam, st, h0)


def _sample_attention_steps(bounded, q_ref, kn_ref, vn_ref, lam, gs, kt_refs, v_refs, out):
    n_pages = len(kt_refs)
    page = kt_refs[0].shape[1]
    nq = q_ref.shape[0]
    hrows = 2 * nq

    lane = lax.broadcasted_iota(jnp.int32, (nq, V_HEAD_DIM), 1)
    qh = []
    for h in range(ATT_HEADS):
        q8 = q_ref[:, h * V_HEAD_DIM:(h + 1) * V_HEAD_DIM]
        qh.append(jnp.concatenate([jnp.where(lane < QK_SUB_DIM, q8, 0.0), jnp.where(lane >= QK_SUB_DIM, q8, 0.0)],
                                  axis=0).astype(BF16))

    zpad = jnp.zeros((page - nq, V_HEAD_DIM), F32)
    rt = lax.broadcasted_iota(jnp.int32, (hrows, page), 0) % nq
    ct = lax.broadcasted_iota(jnp.int32, (hrows, page), 1)
    ps, ls = [], []
    for h in range(ATT_HEADS):
        rows = slice(h * V_HEAD_DIM, (h + 1) * V_HEAD_DIM)
        kt_all = jnp.concatenate([kt_refs[j][rows, :] for j in range(n_pages)], axis=1).astype(BF16)
        k_new = jnp.concatenate([kn_ref[:, rows], zpad], axis=0).astype(BF16)
        s_new = jnp.where(ct <= rt, _dot_nt(qh[h], k_new), NEG)
        s = jnp.concatenate([_dot(qh[h], kt_all), s_new], axis=1)
        if not bounded:
            s = s - jnp.max(s, axis=1, keepdims=True)
        p = jnp.exp2(s)
        ls.append(jnp.sum(p, axis=1, keepdims=True))
        ps.append(p.astype(BF16))
        yield
    outs = []
    for h in range(0, ATT_HEADS, 2):
        v_pair = jnp.concatenate([jnp.concatenate(
            [v_refs[j][pl.ds(hh, page, stride=ATT_HEADS), :] for j in range(n_pages)]
            + [vn_ref[:, hh * V_HEAD_DIM:(hh + 1) * V_HEAD_DIM], zpad], axis=0) for hh in (h, h + 1)],
            axis=1).astype(BF16)
        o_pair = _dot(jnp.concatenate([ps[h], ps[h + 1]], axis=0), v_pair)
        for i, hh in enumerate((h, h + 1)):
            o = o_pair[i * hrows:(i + 1) * hrows, i * V_HEAD_DIM:(i + 1) * V_HEAD_DIM] / ls[hh]
            outs.append(_subln(o[0:nq, :] - lam * o[nq:hrows, :], gs))
        yield
    out.append(jnp.concatenate(outs, axis=1))


def _ffn_sample_kernel(x_ref, at_ref, lo_ref, mod_ref, g2_ref, wout_ref, wupg_ref, wupv_ref, cfwg_ref, cfwv_ref,
                       cfbg_ref, cfbv_ref, wdn_ref, stg_ref, stv_ref,
                       y_ref, fcg_ref, fcv_ref, x1_ref, u2_ref, acc_ref):
    j = pl.program_id(0)
    nb = stg_ref.shape[1]
    nt = x_ref.shape[0] // nb
    rep = lambda a: jnp.concatenate([a] * nt, axis=0)

    @pl.when(j == 0)
    def _():
        gt1 = rep(mod_ref[0:nb, 2 * D_MODEL:3 * D_MODEL])
        sh2 = rep(mod_ref[0:nb, 3 * D_MODEL:4 * D_MODEL])
        sc2 = rep(mod_ref[0:nb, 4 * D_MODEL:5 * D_MODEL])
        mix = jnp.concatenate([at_ref[...].astype(BF16), lo_ref[...]], axis=1)
        x1 = x_ref[...] + gt1 * _dot(mix, wout_ref[...])
        x1_ref[...] = x1
        u2_ref[...] = _modulated_norm(x1, g2_ref[...], sc2, sh2).astype(BF16)
        acc_ref[...] = jnp.zeros(acc_ref.shape, F32)

    u2 = u2_ref[...]
    halves = []
    for wup_ref, cfw_ref, cfb_ref, st_ref, fc_ref in ((wupg_ref, cfwg_ref, cfbg_ref, stg_ref, fcg_ref),
                                                      (wupv_ref, cfwv_ref, cfbv_ref, stv_ref, fcv_ref)):
        up = _dot(u2, wup_ref[...])
        pad = [st_ref[i] for i in range(FFN_CONV - 1)] + [up[t * nb:(t + 1) * nb, :] for t in range(nt)]
        for i in range(FFN_CONV - 1):
            fc_ref[i] = pad[nt + i]
        w = cfw_ref[...]
        b = cfb_ref[...]
        hcs = []
        for t in range(nt):
            hc = pad[t] * w[0:1]
            for kk in range(1, FFN_CONV):
                hc = hc + pad[t + kk] * w[kk:kk + 1]
            hcs.append(hc + b)
        halves.append(jnp.concatenate(hcs, axis=0))
    g, val = halves
    hmid = (g * jax.nn.sigmoid(g) * val).astype(BF16)
    acc_ref[...] += _dot(hmid, wdn_ref[...])

    @pl.when(j == pl.num_programs(0) - 1)
    def _():
        gt2 = rep(mod_ref[0:nb, 5 * D_MODEL:6 * D_MODEL])
        y_ref[...] = x1_ref[...] + gt2 * acc_ref[...]


def _ffn_sample(x_tb, attn_tb, lru_tb, mod, g2, w_out, w_up, cfw, cfb, w_dn, st):
    m = x_tb.shape[0]
    d_ff = w_dn.shape[0]
    nb = st.shape[1]
    n_chunks = d_ff // FF_CHUNK
    whole = lambda a: pl.BlockSpec(a.shape, lambda j: (0,) * a.ndim)
    col = lambda rows, half: pl.BlockSpec((rows, FF_CHUNK), lambda j: (0, half * n_chunks + j))
    st_spec = lambda half: pl.BlockSpec((FFN_CONV - 1, nb, FF_CHUNK), lambda j: (0, 0, half * n_chunks + j))
    fc_spec = pl.BlockSpec((FFN_CONV - 1, nb, FF_CHUNK), lambda j: (0, 0, j))
    fc_shape = jax.ShapeDtypeStruct((FFN_CONV - 1, nb, d_ff), F32)
    return pl.pallas_call(
        _ffn_sample_kernel,
        grid=(n_chunks,),
        in_specs=[whole(x_tb), whole(attn_tb), whole(lru_tb), whole(mod), whole(g2), whole(w_out),
                  col(D_MODEL, 0), col(D_MODEL, 1), col(FFN_CONV, 0), col(FFN_CONV, 1), col(1, 0), col(1, 1),
                  pl.BlockSpec((FF_CHUNK, D_MODEL), lambda j: (j, 0)), st_spec(0), st_spec(1)],
        out_specs=[pl.BlockSpec((m, D_MODEL), lambda j: (0, 0)), fc_spec, fc_spec],
        out_shape=[jax.ShapeDtypeStruct((m, D_MODEL), F32), fc_shape, fc_shape],
        scratch_shapes=[pltpu.VMEM((m, D_MODEL), F32), pltpu.VMEM((m, D_MODEL), BF16),
                        pltpu.VMEM((m, D_MODEL), F32)],
        compiler_params=pltpu.CompilerParams(dimension_semantics=("arbitrary",), vmem_limit_bytes=VMEM_LIMIT),
        name="ffn_sample",
    )(x_tb, attn_tb, lru_tb, mod, g2, w_out, w_up, w_up, cfw, cfw, cfb, cfb, w_dn, st, st)


def _block_diag_halves(w):
    n, bd, _ = w.shape
    eye = jnp.eye(n // 2, dtype=w.dtype)
    halves = [jnp.einsum('nij,nm->nimj', w[s * (n // 2):(s + 1) * (n // 2)], eye).reshape(n // 2 * bd, n // 2 * bd)
              for s in range(2)]
    return jnp.stack(halves).astype(BF16)


def kernel(x_prompt, x_sample, cache_k, cache_v, page_table, state_lru_conv, state_lru_h, state_ffn_conv, c_prompt, c_sample, g_norm1, g_norm2, w_ada, b_ada, w_in, g_q, g_k, lam_q1, lam_k1, lam_q2, lam_k2, g_subln, w_out, conv_lru_w, conv_lru_b, w_rgate, b_rgate, w_igate, b_igate, lru_lambda, w_up, conv_ffn_w, conv_ffn_b, w_down):
    depth = w_in.shape[0]
    assert depth == 1, "single-layer step"
    bsz, s_len, _ = x_prompt.shape
    nb, nt, _ = x_sample.shape
    n_pages, page = page_table.shape[1], cache_k.shape[2]
    past_len = n_pages * page
    d_ff = w_down.shape[1]

    w_in_b = w_in[0].astype(BF16)
    w_out_b = w_out[0].astype(BF16)
    w_up_b = w_up[0].astype(BF16)
    w_dn_b = w_down[0].astype(BF16)
    wr = _block_diag_halves(w_rgate[0])
    wi = _block_diag_halves(w_igate[0])
    br = b_rgate[0].reshape(1, LRU_WIDTH)
    bi = b_igate[0].reshape(1, LRU_WIDTH)
    n_grp = ATT_WIDTH // QK_SUB_DIM
    gq = jnp.tile(g_q[0], n_grp)[None, :]
    gk = jnp.tile(g_k[0], n_grp)[None, :]
    grp = jnp.arange(ATT_WIDTH) // QK_SUB_DIM
    gmat = jnp.where(grp[:, None] == grp[None, :], 1.0 / QK_SUB_DIM, 0.0).astype(BF16)
    g1, g2 = g_norm1, g_norm2
    cw, cb = conv_lru_w[0], conv_lru_b
    cfw, cfb = conv_ffn_w[0], conv_ffn_b
    lam = lru_lambda
    lams = (lam_q1, lam_k1, lam_q2, lam_k2)
    gs = g_subln

    assert nb % SUBLANES == 0 and bsz <= SUBLANES
    c_all = jnp.concatenate([c_sample, c_prompt, jnp.zeros((SUBLANES - bsz, D_MODEL), F32)], axis=0)
    mod = _ada(c_all, w_ada[0], b_ada)

    tabs = _rope_tables(s_len)
    pad_rows = SUBLANES - nt
    tabs_s = tuple(t[past_len:past_len + SUBLANES] for t in tabs)

    q_p, kf_p, kb_p, vf_p, vb_p, lo_p, lc_p, lh_p = _inproj_prompt(
        x_prompt, mod, g1, w_in_b, gq, gk, gmat, tabs, cw, cb, wr, br, wi, bi, lam)
    x_tb = x_sample.transpose(1, 0, 2).reshape(nt * nb, D_MODEL)
    st_lru = state_lru_conv[0].transpose(1, 0, 2)
    q_s, k_s, v_s, lo_s, lc_s, lh_s = _inproj_sample(
        x_tb, mod, g1, w_in_b, gq, gk, gmat, tabs_s, cw, cb, wr, br, wi, bi, lam, st_lru, state_lru_h[0])
    to_bt = lambda a: a.reshape(nt, nb, -1).transpose(1, 0, 2)
    pad_t = lambda a: jnp.pad(a, ((0, 0), (0, pad_rows), (0, 0)))
    k_bt, v_bt = to_bt(k_s), to_bt(v_s)

    score_bound = QK_NORM_BOUND ** 2 * QK_SCALE_LOG2 * jnp.max(jnp.abs(g_q)) * jnp.max(jnp.abs(g_k))
    cache_kt = cache_k[0].transpose(0, 2, 3, 4, 1).reshape(-1, ATT_WIDTH, page)
    cache_vr = cache_v[0].reshape(-1, page * ATT_HEADS, V_HEAD_DIM)
    at_p, at_s = lax.cond(score_bound <= SCORE_LOG2_LIMIT,
                          functools.partial(_attn, True), functools.partial(_attn, False),
                          page_table, q_p, kb_p, vb_p, pad_t(to_bt(q_s)), pad_t(k_bt), pad_t(v_bt),
                          cache_kt, cache_vr, *lams, gs)

    y_p, fc_p = _ffn_prompt(x_prompt, at_p, lo_p, mod, g2, w_out_b, w_up_b, cfw, cfb, w_dn_b)
    at_tb = at_s[:, :nt].transpose(1, 0, 2).reshape(nt * nb, ATT_WIDTH)
    y_tb, fcg_s, fcv_s = _ffn_sample(x_tb, at_tb, lo_s, mod, g2, w_out_b, w_up_b, cfw, cfb, w_dn_b,
                                     state_ffn_conv[0].transpose(1, 0, 2))

    hd = (ATT_HEADS, 2, QK_SUB_DIM)
    return (y_p, to_bt(y_tb),
            kf_p.reshape(bsz, *hd, s_len).transpose(0, 4, 1, 2, 3)[None],
            vf_p.reshape(1, bsz, s_len, ATT_HEADS, V_HEAD_DIM),
            lc_p[None], lh_p.reshape(1, bsz, LRU_WIDTH), fc_p[None],
            k_bt.reshape(1, nb, nt, *hd), v_bt.reshape(1, nb, nt, ATT_HEADS, V_HEAD_DIM),
            lc_s.transpose(1, 0, 2)[None], lh_s[None],
            jnp.concatenate([fcg_s, fcv_s], axis=-1).transpose(1, 0, 2)[None])
```

```python
import functools
import math

import jax
import jax.numpy as jnp
from jax import lax
from jax.experimental import pallas as pl
from jax.experimental.pallas import tpu as pltpu

F32 = jnp.float32
BF16 = jnp.bfloat16

D_MODEL = 1024
ATT_WIDTH = 512
LRU_WIDTH = 512
ATT_HEADS = 4
V_HEAD_DIM = 128
QK_SUB_DIM = 64
ROPE_DIM = 16
ROPE_THETA = 500000.0
LRU_BLOCKS = 8
LRU_C = 8.0
LRU_CONV = 4
FFN_CONV = 3
EPS = 1e-6
LAM_INIT = 0.8 - 0.6 * math.exp(-0.3 * 0)
QK_SCALE = QK_SUB_DIM ** -0.5
QK_SCALE_LOG2 = QK_SCALE * math.log2(math.e)
SCORE_LOG2_LIMIT = 64.0
QK_NORM_BOUND = 1.01 * math.sqrt(QK_SUB_DIM)

LANES = 128
SUBLANES = 8
VMEM_LIMIT = 56 * 1024 * 1024

TM_IN = 1024
TQ = 512
TK = 512
TM_FFN = 1024
FFN_SPLIT = 4
FFN_SLOT = 2
FF_CHUNK = 256
NEG = -1e30


def _dot(a, b):
    return jnp.dot(a, b, preferred_element_type=F32)


def _dot_nt(a, b):
    return lax.dot_general(a, b, (((1,), (1,)), ((), ())), preferred_element_type=F32)


def _const_spec(shape):
    nd = len(shape)
    return pl.BlockSpec(shape, lambda *_: (0,) * nd, pipeline_mode=pl.Buffered(1))


def _prompt_mod_spec(mod, k):
    return pl.BlockSpec((SUBLANES, D_MODEL), lambda b, i: (mod.shape[0] // SUBLANES - 1, k))


def _seq_row(ref):
    return ref[pl.ds(pl.program_id(0), 1), :]


def _modulated_norm(x, g, sc, sh):
    xn = x * lax.rsqrt(jnp.mean(x * x, axis=-1, keepdims=True) + EPS) * g
    return xn * (1.0 + sc) + sh


def _to_segment_major(val, buf_ref):
    tm, w = val.shape
    n = tm // SUBLANES
    for c in range(w // LANES):
        for s in range(SUBLANES):
            for r0 in range(0, n, SUBLANES):
                t0 = s * n + r0
                buf_ref[c, pl.ds(r0 * SUBLANES + s, SUBLANES, stride=SUBLANES), :] = (
                    val[t0:t0 + SUBLANES, c * LANES:(c + 1) * LANES])
    return jnp.concatenate([buf_ref[c] for c in range(w // LANES)], axis=1)


def _to_time_major(val, buf_ref):
    tm, w = val.shape
    n = tm // SUBLANES
    for c in range(w // LANES):
        buf_ref[c] = val[:, c * LANES:(c + 1) * LANES]
    rows = []
    for s in range(SUBLANES):
        for r0 in range(0, n, SUBLANES):
            rows.append(jnp.concatenate(
                [buf_ref[c, pl.ds(r0 * SUBLANES + s, SUBLANES, stride=SUBLANES), :] for c in range(w // LANES)], axis=1))
    return jnp.concatenate(rows, axis=0)


def _tail_groups(cur, k):
    tm = cur.shape[0]
    return [cur[tm - (k - i) * SUBLANES:tm - (k - i - 1) * SUBLANES, :] for i in range(k)]


def _delayed(cur, prev_tail):
    k = len(prev_tail)
    tm = cur.shape[0]
    first = lax.broadcasted_iota(jnp.int32, prev_tail[0].shape, 0) == 0
    heads = [jnp.where(first, pltpu.roll(p, 1, 0), pltpu.roll(c, 1, 0))
             for p, c in zip(prev_tail, _tail_groups(cur, k))]
    return [jnp.concatenate(heads[k - d:] + [cur[:tm - d * SUBLANES, :]], axis=0) for d in range(k, 0, -1)]


def _group_norm_rope(t, g_tiled, gmat, cos, sin_a, sin_b):
    ms = _dot((t * t).astype(BF16), gmat)
    tn = t * lax.rsqrt(ms + EPS) * g_tiled
    outs = []
    for h in range(ATT_WIDTH // LANES):
        th = tn[:, h * LANES:(h + 1) * LANES]
        outs.append(th * cos + pltpu.roll(th, LANES - 8, 1) * sin_a + pltpu.roll(th, 8, 1) * sin_b)
    return jnp.concatenate(outs, axis=1)


def _lru_gates(xc, wr_ref, br, wi_ref, bi, neg_c_softplus):
    half = LRU_WIDTH // 2
    xb = xc.astype(BF16)
    lo, hi = xb[:, :half], xb[:, half:]
    r = jax.nn.sigmoid(jnp.concatenate([_dot(lo, wr_ref[0]), _dot(hi, wr_ref[1])], axis=1) + br)
    ig = jax.nn.sigmoid(jnp.concatenate([_dot(lo, wi_ref[0]), _dot(hi, wi_ref[1])], axis=1) + bi)
    log_a = neg_c_softplus * r
    a = jnp.exp(log_a)
    one_minus_a2 = -jnp.tanh(log_a) * (a * a + 1.0)
    return a, jnp.sqrt(one_minus_a2) * (ig * xc)


def _neg_c_softplus(lam):
    z = -lam
    return -LRU_C * (jnp.maximum(z, 0.0) + jnp.log1p(jnp.exp(-jnp.abs(z))))


def _diff_lambda(lq1, lk1, lq2, lk2):
    s1 = jnp.sum(lq1 * lk1, axis=-1, keepdims=True)
    s2 = jnp.sum(lq2 * lk2, axis=-1, keepdims=True)
    return jnp.exp(s1) - jnp.exp(s2) + LAM_INIT


def _subln(o, g):
    return o * lax.rsqrt(jnp.mean(o * o, axis=-1, keepdims=True) + EPS) * g * (1.0 - LAM_INIT)


def _ada_kernel(c_ref, w_ref, b_ref, o_ref):
    c = c_ref[...]
    s = (c * jax.nn.sigmoid(c)).astype(BF16)
    o_ref[...] = _dot(s, w_ref[...].astype(BF16)) + b_ref[...]


def _ada(c_all, w_ada, b_ada):
    m = c_all.shape[0]
    n = w_ada.shape[1]
    tn = 1024
    return pl.pallas_call(
        _ada_kernel,
        grid=(n // tn,),
        in_specs=[pl.BlockSpec((m, D_MODEL), lambda j: (0, 0)),
                  pl.BlockSpec((D_MODEL, tn), lambda j: (0, j)),
                  pl.BlockSpec((1, tn), lambda j: (0, j))],
        out_specs=pl.BlockSpec((m, tn), lambda j: (0, j)),
        out_shape=jax.ShapeDtypeStruct((m, n), F32),
        compiler_params=pltpu.CompilerParams(dimension_semantics=("arbitrary",), vmem_limit_bytes=VMEM_LIMIT),
        name="ada_mod",
    )(c_all, w_ada, b_ada)


def _rope_table_kernel(freq_ref, ma_ref, mb_ref, c_ref, sa_ref, sb_ref):
    tm = c_ref.shape[0]
    pos = (pl.program_id(0) * tm + lax.broadcasted_iota(jnp.int32, (tm, LANES), 0)).astype(F32)
    ang = pos * freq_ref[...]
    s = jnp.sin(ang)
    c_ref[...] = jnp.cos(ang)
    sa_ref[...] = -s * ma_ref[...]
    sb_ref[...] = s * mb_ref[...]


def _rope_tables(n_pos):
    half = ROPE_DIM // 2
    d = jnp.arange(LANES) % QK_SUB_DIM
    freqs = ROPE_THETA ** (-(d % half).astype(F32) * 2.0 / ROPE_DIM)
    freq_lane = jnp.where(d < ROPE_DIM, freqs, 0.0).astype(F32)[None, :]
    mask_a = (d < half).astype(F32)[None, :]
    mask_b = ((d >= half) & (d < ROPE_DIM)).astype(F32)[None, :]
    tm = 512
    row = pl.BlockSpec((1, LANES), lambda i: (0, 0))
    tab = pl.BlockSpec((tm, LANES), lambda i: (i, 0))
    shp = jax.ShapeDtypeStruct((n_pos, LANES), F32)
    return pl.pallas_call(
        _rope_table_kernel,
        grid=(n_pos // tm,),
        in_specs=[row, row, row],
        out_specs=[tab, tab, tab],
        out_shape=[shp, shp, shp],
        compiler_params=pltpu.CompilerParams(dimension_semantics=("arbitrary",)),
        name="rope_tables",
    )(freq_lane, mask_a, mask_b)


def _inproj_prompt_kernel(x_ref, sh_ref, sc_ref, g1_ref, win_ref, gq_ref, gk_ref, gmat_ref,
                          cos_ref, sa_ref, sb_ref, cw_ref, cb_ref, wr_ref, br_ref, wi_ref, bi_ref, lam_ref,
                          q_ref, kf_ref, kb_ref, vf_ref, vb_ref, lo_ref, lc_ref, lh_ref,
                          lbuf_ref, hbuf_ref, tail_ref, hcar_ref):
    tm = x_ref.shape[0]

    @pl.when(pl.program_id(1) == 0)
    def _():
        tail_ref[...] = jnp.zeros(tail_ref.shape, F32)
        hcar_ref[...] = jnp.zeros(hcar_ref.shape, F32)

    u = _modulated_norm(x_ref[...], g1_ref[...], _seq_row(sc_ref), _seq_row(sh_ref)).astype(BF16)
    cos, sa, sb = cos_ref[...], sa_ref[...], sb_ref[...]
    gmat = gmat_ref[...]

    q = _group_norm_rope(_dot(u, win_ref[:, 0:ATT_WIDTH]), gq_ref[...], gmat, cos, sa, sb)
    q_ref[...] = (q * QK_SCALE_LOG2).T.astype(BF16)
    k = _group_norm_rope(_dot(u, win_ref[:, ATT_WIDTH:2 * ATT_WIDTH]), gk_ref[...], gmat, cos, sa, sb)
    kf_ref[...] = k.T
    kb_ref[...] = k.astype(BF16)
    v = _dot(u, win_ref[:, 2 * ATT_WIDTH:3 * ATT_WIDTH])
    for h in range(ATT_HEADS):
        vf_ref[pl.ds(h, tm, stride=ATT_HEADS), :] = v[:, h * V_HEAD_DIM:(h + 1) * V_HEAD_DIM]
    vb_ref[...] = v.T.astype(BF16)

    c0 = 3 * ATT_WIDTH
    n_tail = LRU_CONV - 1
    lx = _to_segment_major(_dot(u, win_ref[:, c0:c0 + LRU_WIDTH]), lbuf_ref)
    cw = cw_ref[...]
    d3, d2, d1 = _delayed(lx, [tail_ref[i] for i in range(n_tail)])
    xc = d3 * cw[0:1] + d2 * cw[1:2] + d1 * cw[2:3] + lx * cw[3:4] + cb_ref[...]
    for i, grp in enumerate(_tail_groups(lx, n_tail)):
        tail_ref[i] = grp
        lc_ref[i:i + 1, :] = grp[SUBLANES - 1:SUBLANES, :]

    a, gx = _lru_gates(xc, wr_ref, br_ref[...], wi_ref, bi_ref[...], _neg_c_softplus(lam_ref[...]))

    n = tm // SUBLANES
    grp = lambda arr, r: arr[r * SUBLANES:(r + 1) * SUBLANES, :]
    ps, hs = [grp(a, 0)], [grp(gx, 0)]
    for r in range(1, n):
        ar = grp(a, r)
        ps.append(ar * ps[-1])
        hs.append(ar * hs[-1] + grp(gx, r))
    p_end, h_end = ps[-1], hs[-1]
    entering = [hcar_ref[...]]
    for s in range(1, SUBLANES):
        entering.append(p_end[s - 1:s, :] * entering[-1] + h_end[s - 1:s, :])
    h_last = p_end[SUBLANES - 1:SUBLANES, :] * entering[-1] + h_end[SUBLANES - 1:SUBLANES, :]
    hcar_ref[...] = h_last
    lh_ref[...] = h_last
    enter = jnp.concatenate(entering, axis=0)
    states = jnp.concatenate([hs[r] + ps[r] * enter for r in range(n)], axis=0)

    lg = _dot(u, win_ref[:, c0 + LRU_WIDTH:c0 + 2 * LRU_WIDTH])
    lo_ref[...] = (_to_time_major(states, hbuf_ref) * jax.nn.gelu(lg, approximate=True)).astype(BF16)


def _inproj_prompt(x, mod, g1, w_in, gq, gk, gmat, tabs, cw, cb, wr, br, wi, bi, lam):
    bsz, s_len, _ = x.shape
    tm = TM_IN
    row_spec = lambda w: pl.BlockSpec((None, tm, w), lambda b, i: (b, i, 0))
    tab_spec = pl.BlockSpec((tm, LANES), lambda b, i: (i, 0))
    mod_spec = lambda k: _prompt_mod_spec(mod, k)
    act = lambda dt: jax.ShapeDtypeStruct((bsz, s_len, ATT_WIDTH), dt)
    act_t = lambda dt: jax.ShapeDtypeStruct((bsz, ATT_WIDTH, s_len), dt)
    col_spec = pl.BlockSpec((None, ATT_WIDTH, tm), lambda b, i: (b, 0, i))
    return pl.pallas_call(
        _inproj_prompt_kernel,
        grid=(bsz, s_len // tm),
        in_specs=[row_spec(D_MODEL), mod_spec(0), mod_spec(1), _const_spec(g1.shape), _const_spec(w_in.shape),
                  _const_spec(gq.shape), _const_spec(gk.shape), _const_spec(gmat.shape),
                  tab_spec, tab_spec, tab_spec,
                  _const_spec(cw.shape), _const_spec(cb.shape), _const_spec(wr.shape), _const_spec(br.shape),
                  _const_spec(wi.shape), _const_spec(bi.shape), _const_spec(lam.shape)],
        out_specs=[col_spec, col_spec, row_spec(ATT_WIDTH),
                   pl.BlockSpec((None, tm * ATT_HEADS, V_HEAD_DIM), lambda b, i: (b, i, 0)),
                   col_spec, row_spec(LRU_WIDTH),
                   pl.BlockSpec((None, LRU_CONV - 1, LRU_WIDTH), lambda b, i: (b, 0, 0)),
                   pl.BlockSpec((None, 1, LRU_WIDTH), lambda b, i: (b, 0, 0))],
        out_shape=[act_t(BF16), act_t(F32), act(BF16),
                   jax.ShapeDtypeStruct((bsz, s_len * ATT_HEADS, V_HEAD_DIM), F32), act_t(BF16), act(BF16),
                   jax.ShapeDtypeStruct((bsz, LRU_CONV - 1, LRU_WIDTH), F32),
                   jax.ShapeDtypeStruct((bsz, 1, LRU_WIDTH), F32)],
        scratch_shapes=[pltpu.VMEM((LRU_WIDTH // LANES, tm, LANES), F32),
                        pltpu.VMEM((LRU_WIDTH // LANES, tm, LANES), F32),
                        pltpu.VMEM((LRU_CONV - 1, SUBLANES, LRU_WIDTH), F32),
                        pltpu.VMEM((1, LRU_WIDTH), F32)],
        compiler_params=pltpu.CompilerParams(dimension_semantics=("arbitrary", "arbitrary"),
                                             vmem_limit_bytes=VMEM_LIMIT),
        name="inproj_prompt",
    )(x, mod, mod, g1, w_in, gq, gk, gmat, *tabs, cw, cb, wr, br, wi, bi, lam)


def _alternate(*step_generators):
    live = list(step_generators)
    while live:
        for g in list(live):
            try:
                next(g)
            except StopIteration:
                live.remove(g)


def _page_copies(pt_ref, ckt_hbm, cv_hbm, kbuf_ref, vbuf_ref, sem_ref, seq, slot):
    copies = []
    for j in range(kbuf_ref.shape[1]):
        pg = pt_ref[seq, j]
        copies.append(pltpu.make_async_copy(ckt_hbm.at[pg], kbuf_ref.at[slot, j], sem_ref.at[slot]))
        copies.append(pltpu.make_async_copy(cv_hbm.at[pg], vbuf_ref.at[slot, j], sem_ref.at[slot]))
    return copies


def _attn_kernel(bounded, pt_ref, qt_ref, k_ref, vt_ref, lq1_ref, lk1_ref, lq2_ref, lk2_ref, gs_ref,
                 qs_ref, kn_ref, vn_ref, ckt_hbm, cv_hbm, o_ref, os_ref, kbuf_ref, vbuf_ref, sem_ref):
    n_pages = kbuf_ref.shape[1]
    step = (pl.program_id(0) * pl.num_programs(1) + pl.program_id(1)) * pl.num_programs(2) + pl.program_id(2)
    n_steps = pl.num_programs(0) * pl.num_programs(1) * pl.num_programs(2)
    slot = step % 2

    @pl.when(step == 0)
    def _():
        for cp in _page_copies(pt_ref, ckt_hbm, cv_hbm, kbuf_ref, vbuf_ref, sem_ref, step, slot):
            cp.start()

    @pl.when(step + 1 < n_steps)
    def _():
        for cp in _page_copies(pt_ref, ckt_hbm, cv_hbm, kbuf_ref, vbuf_ref, sem_ref, step + 1, 1 - slot):
            cp.start()

    tq = qt_ref.shape[1]
    qi = pl.program_id(2)
    qt = qt_ref[...].astype(F32)
    feat = lax.broadcasted_iota(jnp.int32, (V_HEAD_DIM, tq), 0)
    qc = (jnp.where(feat < QK_SUB_DIM, qt, 0.0).astype(BF16), jnp.where(feat >= QK_SUB_DIM, qt, 0.0).astype(BF16))

    def tile_steps(k0, width, carry, masked, out):
        k0 = pl.multiple_of(k0, TK)
        ks = k_ref[pl.ds(k0, width), :]
        vts = vt_ref[:, pl.ds(k0, width)]
        ss = []
        for c in range(2):
            ss.append(_dot(ks, qc[c]))
            yield
        for c in range(2):
            m, l, acc = carry[c]
            s = ss[c]
            if masked:
                keys = lax.broadcasted_iota(jnp.int32, (width, tq), 0)
                qs = lax.broadcasted_iota(jnp.int32, (width, tq), 1)
                s = jnp.where(keys <= qs, s, NEG)
            if bounded:
                p = jnp.exp2(s)
                l = l + jnp.sum(p, axis=0, keepdims=True)
                acc = acc + _dot(vts, p.astype(BF16))
            else:
                m_new = jnp.maximum(m, jnp.max(s, axis=0, keepdims=True))
                alpha = jnp.exp2(m - m_new)
                p = jnp.exp2(s - m_new)
                l = alpha * l + jnp.sum(p, axis=0, keepdims=True)
                acc = alpha * acc + _dot(vts, p.astype(BF16))
                m = m_new
            out.append((m, l, acc))
            yield

    def tile(k0, width, carry, masked):
        out = []
        for _ in tile_steps(k0, width, carry, masked, out):
            pass
        return tuple(out)

    init = tuple((jnp.full((1, tq), NEG, F32), jnp.zeros((1, tq), F32), jnp.zeros((V_HEAD_DIM, tq), F32))
                 for _ in range(2))
    carry = lax.fori_loop(0, qi // 2, lambda j, c: tile(j * 2 * TK, 2 * TK, c, False), init)
    carry = lax.cond(qi % 2 == 1, lambda c: tile((qi - 1) * TK, TK, c, False), lambda c: c, carry)
    for cp in _page_copies(pt_ref, ckt_hbm, cv_hbm, kbuf_ref, vbuf_ref, sem_ref, step, slot):
        cp.wait()
    kt_refs = [kbuf_ref.at[slot, j] for j in range(n_pages)]
    v_refs = [vbuf_ref.at[slot, j] for j in range(n_pages)]
    lam = _diff_lambda(lq1_ref[...], lk1_ref[...], lq2_ref[...], lk2_ref[...])
    tile_out, sample_out = [], []
    _alternate(tile_steps(qi * TK, TK, carry, True, tile_out),
               _sample_attention_steps(bounded, qs_ref, kn_ref, vn_ref, lam, gs_ref[...], kt_refs, v_refs, sample_out))
    (_, l0, a0), (_, l1, a1) = tile_out
    ot = a0 / l0 - lam * (a1 / l1)
    o_ref[...] = _subln(ot.T, gs_ref[...]).astype(o_ref.dtype)
    os_ref[...] = sample_out[0]


def _attn(bounded, page_table, qt, k, vt, q8, kn8, vn8, cache_kt, cache_v, lq1, lk1, lq2, lk2, gs):
    bsz, s_len, _ = k.shape
    nb, n_pages = page_table.shape
    nq = q8.shape[1]
    page = cache_kt.shape[2]
    n_qt = s_len // TQ
    assert nb == bsz * ATT_HEADS * n_qt, "one sample sequence per grid step"
    seq = lambda b, h, i: (b * ATT_HEADS + h) * n_qt + i
    qt_spec = pl.BlockSpec((None, V_HEAD_DIM, TQ), lambda b, h, i, pt: (b, h, i))
    k_spec = pl.BlockSpec((None, s_len, V_HEAD_DIM), lambda b, h, i, pt: (b, 0, h))
    vt_spec = pl.BlockSpec((None, V_HEAD_DIM, s_len), lambda b, h, i, pt: (b, h, 0))
    o_spec = pl.BlockSpec((None, TQ, V_HEAD_DIM), lambda b, h, i, pt: (b, i, h))
    small = lambda a: pl.BlockSpec(a.shape, lambda b, h, i, pt: (0, 0))
    new_spec = pl.BlockSpec((None, nq, ATT_WIDTH), lambda b, h, i, pt: (seq(b, h, i), 0, 0))
    hbm = pl.BlockSpec(memory_space=pl.ANY)
    grid_spec = pltpu.PrefetchScalarGridSpec(
        num_scalar_prefetch=1,
        grid=(bsz, ATT_HEADS, n_qt),
        in_specs=[qt_spec, k_spec, vt_spec, small(lq1), small(lk1), small(lq2), small(lk2), small(gs),
                  new_spec, new_spec, new_spec, hbm, hbm],
        out_specs=[o_spec, new_spec],
        scratch_shapes=[pltpu.VMEM((2, n_pages, ATT_WIDTH, page), F32),
                        pltpu.VMEM((2, n_pages, page * ATT_HEADS, V_HEAD_DIM), F32),
                        pltpu.SemaphoreType.DMA((2,))],
    )
    return pl.pallas_call(
        functools.partial(_attn_kernel, bounded),
        grid_spec=grid_spec,
        out_shape=[jax.ShapeDtypeStruct((bsz, s_len, ATT_WIDTH), BF16),
                   jax.ShapeDtypeStruct((nb, nq, ATT_WIDTH), F32)],
        compiler_params=pltpu.CompilerParams(dimension_semantics=("arbitrary", "arbitrary", "arbitrary"),
                                             vmem_limit_bytes=VMEM_LIMIT),
        name="attn_bounded" if bounded else "attn_general",
    )(page_table, qt, k, vt, lq1, lk1, lq2, lk2, gs, q8, kn8, vn8, cache_kt, cache_v)


def _ffn_prompt_kernel(x_ref, at_ref, lo_ref, gt1_ref, sh2_ref, sc2_ref, gt2_ref, g2_ref,
                       wout_ref, wup_ref, cfw_ref, cfb_ref, wdn_ref,
                       y_ref, fc_ref, ubuf_ref, abuf_ref, tail_ref):
    tm = x_ref.shape[0]
    sub = tm // FFN_SPLIT
    d_ff = wdn_ref.shape[0]
    n_tail = FFN_CONV - 1
    n_chunks = d_ff // FF_CHUNK
    chunk_cols = lambda j: [slice(base + j * FF_CHUNK, base + (j + 1) * FF_CHUNK) for base in (0, d_ff)]

    @pl.when(pl.program_id(1) == 0)
    def _():
        tail_ref[...] = jnp.zeros(tail_ref.shape, F32)

    def attention_residual(g):
        rows = slice(g * sub, (g + 1) * sub)
        mix = jnp.concatenate([at_ref[rows, :], lo_ref[rows, :]], axis=1)
        x1 = x_ref[rows, :] + _seq_row(gt1_ref) * _dot(mix, wout_ref[...])
        u2 = _to_segment_major(_modulated_norm(x1, g2_ref[...], _seq_row(sc2_ref), _seq_row(sh2_ref)),
                               ubuf_ref.at[g]).astype(BF16)
        return x1, u2

    def finish(g, x1, acc):
        y_ref[g * sub:(g + 1) * sub, :] = x1 + _seq_row(gt2_ref) * _to_time_major(acc, abuf_ref.at[g])

    def hidden(j, ups):
        halves = []
        for cols, up in zip(chunk_cols(j), ups):
            w = cfw_ref[:, cols]
            d2, d1 = _delayed(up, [tail_ref[i, :, cols] for i in range(n_tail)])
            halves.append(d2 * w[0:1] + d1 * w[1:2] + up * w[2:3] + cfb_ref[:, cols])
            for i, grp in enumerate(_tail_groups(up, n_tail)):
                tail_ref[i, :, cols] = grp
                fc_ref[i:i + 1, cols] = grp[SUBLANES - 1:SUBLANES, :]
        g, val = halves
        return (g * jax.nn.sigmoid(g) * val).astype(BF16)

    state = attention_residual(0)
    done = None
    for g in range(FFN_SPLIT):
        x1, u2 = state
        up_project = lambda j: [_dot(u2, wup_ref[:, cols]) for cols in chunk_cols(j)]
        acc = jnp.zeros((sub, D_MODEL), F32)
        ups = up_project(0)
        for j in range(n_chunks):
            ups_next = up_project(j + 1) if j + 1 < n_chunks else None
            acc = acc + _dot(hidden(j, ups), wdn_ref[j * FF_CHUNK:(j + 1) * FF_CHUNK, :])
            ups = ups_next
            if j == FFN_SLOT and done is not None:
                finish(*done)
                done = None
            if j == FFN_SLOT and g + 1 < FFN_SPLIT:
                state = attention_residual(g + 1)
        done = (g, x1, acc)
    finish(*done)


def _ffn_prompt(x, attn, lru, mod, g2, w_out, w_up, cfw, cfb, w_dn):
    bsz, s_len, _ = x.shape
    tm = TM_FFN
    d_ff = w_dn.shape[0]
    row_spec = lambda w: pl.BlockSpec((None, tm, w), lambda b, i: (b, i, 0))
    mod_spec = lambda k: _prompt_mod_spec(mod, k)
    return pl.pallas_call(
        _ffn_prompt_kernel,
        grid=(bsz, s_len // tm),
        in_specs=[row_spec(D_MODEL), row_spec(ATT_WIDTH), row_spec(LRU_WIDTH),
                  mod_spec(2), mod_spec(3), mod_spec(4), mod_spec(5), _const_spec(g2.shape),
                  _const_spec(w_out.shape), _const_spec(w_up.shape), _const_spec(cfw.shape),
                  _const_spec(cfb.shape), _const_spec(w_dn.shape)],
        out_specs=[row_spec(D_MODEL), pl.BlockSpec((None, FFN_CONV - 1, 2 * d_ff), lambda b, i: (b, 0, 0))],
        out_shape=[jax.ShapeDtypeStruct((bsz, s_len, D_MODEL), F32),
                   jax.ShapeDtypeStruct((bsz, FFN_CONV - 1, 2 * d_ff), F32)],
        scratch_shapes=[pltpu.VMEM((FFN_SPLIT, D_MODEL // LANES, tm // FFN_SPLIT, LANES), F32),
                        pltpu.VMEM((FFN_SPLIT, D_MODEL // LANES, tm // FFN_SPLIT, LANES), F32),
                        pltpu.VMEM((FFN_CONV - 1, SUBLANES, 2 * d_ff), F32)],
        compiler_params=pltpu.CompilerParams(dimension_semantics=("arbitrary", "arbitrary"),
                                             vmem_limit_bytes=VMEM_LIMIT),
        name="ffn_prompt",
    )(x, attn, lru, mod, mod, mod, mod, g2, w_out, w_up, cfw, cfb, w_dn)


def _inproj_sample_kernel(x_ref, mod_ref, g1_ref, win_ref, gq_ref, gk_ref, gmat_ref,
                          cos_ref, sa_ref, sb_ref, cw_ref, cb_ref, wr_ref, br_ref, wi_ref, bi_ref, lam_ref,
                          st_ref, h0_ref,
                          q_ref, k_ref, v_ref, lo_ref, lc_ref, lh_ref):
    nb = h0_ref.shape[0]
    nt = x_ref.shape[0] // nb
    rep = lambda a: jnp.concatenate([a] * nt, axis=0)
    sh1 = rep(mod_ref[0:nb, 0:D_MODEL])
    sc1 = rep(mod_ref[0:nb, D_MODEL:2 * D_MODEL])
    u = _modulated_norm(x_ref[...], g1_ref[...], sc1, sh1).astype(BF16)
    per_t = lambda r: jnp.concatenate([jnp.broadcast_to(r[t:t + 1, :], (nb, LANES)) for t in range(nt)], axis=0)
    cos, sa, sb = per_t(cos_ref[...]), per_t(sa_ref[...]), per_t(sb_ref[...])
    gmat = gmat_ref[...]

    q = _group_norm_rope(_dot(u, win_ref[:, 0:ATT_WIDTH]), gq_ref[...], gmat, cos, sa, sb)
    q_ref[...] = q * QK_SCALE_LOG2
    k_ref[...] = _group_norm_rope(_dot(u, win_ref[:, ATT_WIDTH:2 * ATT_WIDTH]), gk_ref[...], gmat, cos, sa, sb)
    v_ref[...] = _dot(u, win_ref[:, 2 * ATT_WIDTH:3 * ATT_WIDTH])

    c0 = 3 * ATT_WIDTH
    lx = _dot(u, win_ref[:, c0:c0 + LRU_WIDTH])
    lg = _dot(u, win_ref[:, c0 + LRU_WIDTH:c0 + 2 * LRU_WIDTH])
    pad = [st_ref[i] for i in range(LRU_CONV - 1)] + [lx[t * nb:(t + 1) * nb, :] for t in range(nt)]
    for i in range(LRU_CONV - 1):
        lc_ref[i] = pad[nt + i]
    cw = cw_ref[...]
    xcs = []
    for t in range(nt):
        xc = pad[t] * cw[0:1]
        for kk in range(1, LRU_CONV):
            xc = xc + pad[t + kk] * cw[kk:kk + 1]
        xcs.append(xc + cb_ref[...])
    a, gx = _lru_gates(jnp.concatenate(xcs, axis=0), wr_ref, br_ref[...], wi_ref, bi_ref[...],
                       _neg_c_softplus(lam_ref[...]))
    h = h0_ref[...]
    hs = []
    for t in range(nt):
        h = a[t * nb:(t + 1) * nb, :] * h + gx[t * nb:(t + 1) * nb, :]
        hs.append(h)
    lh_ref[...] = h
    lo_ref[...] = (jnp.concatenate(hs, axis=0) * jax.nn.gelu(lg, approximate=True)).astype(BF16)


def _inproj_sample(x_tb, mod_s, g1, w_in, gq, gk, gmat, tabs_s, cw, cb, wr, br, wi, bi, lam, st, h0):
    m = x_tb.shape[0]
    nb = h0.shape[0]
    act = jax.ShapeDtypeStruct((m, ATT_WIDTH), F32)
    return pl.pallas_call(
        _inproj_sample_kernel,
        out_shape=[act, act, act, jax.ShapeDtypeStruct((m, LRU_WIDTH), BF16),
                   jax.ShapeDtypeStruct((LRU_CONV - 1, nb, LRU_WIDTH), F32),
                   jax.ShapeDtypeStruct((nb, LRU_WIDTH), F32)],
        compiler_params=pltpu.CompilerParams(vmem_limit_bytes=VMEM_LIMIT),
        name="inproj_sample",
    )(x_tb, mod_s, g1, w_in, gq, gk, gmat, *tabs_s, cw, cb, wr, br, wi, bi, lam, st, h0)


def _sample_attention_steps(bounded, q_ref, kn_ref, vn_ref, lam, gs, kt_refs, v_refs, out):
    n_pages = len(kt_refs)
    page = kt_refs[0].shape[1]
    nq = q_ref.shape[0]
    hrows = 2 * nq

    lane = lax.broadcasted_iota(jnp.int32, (nq, V_HEAD_DIM), 1)
    qh = []
    for h in range(ATT_HEADS):
        q8 = q_ref[:, h * V_HEAD_DIM:(h + 1) * V_HEAD_DIM]
        qh.append(jnp.concatenate([jnp.where(lane < QK_SUB_DIM, q8, 0.0), jnp.where(lane >= QK_SUB_DIM, q8, 0.0)],
                                  axis=0).astype(BF16))

    zpad = jnp.zeros((page - nq, V_HEAD_DIM), F32)
    rt = lax.broadcasted_iota(jnp.int32, (hrows, page), 0) % nq
    ct = lax.broadcasted_iota(jnp.int32, (hrows, page), 1)
    ps, ls = [], []
    for h in range(ATT_HEADS):
        rows = slice(h * V_HEAD_DIM, (h + 1) * V_HEAD_DIM)
        kt_all = jnp.concatenate([kt_refs[j][rows, :] for j in range(n_pages)], axis=1).astype(BF16)
        k_new = jnp.concatenate([kn_ref[:, rows], zpad], axis=0).astype(BF16)
        s_new = jnp.where(ct <= rt, _dot_nt(qh[h], k_new), NEG)
        s = jnp.concatenate([_dot(qh[h], kt_all), s_new], axis=1)
        if not bounded:
            s = s - jnp.max(s, axis=1, keepdims=True)
        p = jnp.exp2(s)
        ls.append(jnp.sum(p, axis=1, keepdims=True))
        ps.append(p.astype(BF16))
        yield
    outs = []
    for h in range(0, ATT_HEADS, 2):
        v_pair = jnp.concatenate([jnp.concatenate(
            [v_refs[j][pl.ds(hh, page, stride=ATT_HEADS), :] for j in range(n_pages)]
            + [vn_ref[:, hh * V_HEAD_DIM:(hh + 1) * V_HEAD_DIM], zpad], axis=0) for hh in (h, h + 1)],
            axis=1).astype(BF16)
        o_pair = _dot(jnp.concatenate([ps[h], ps[h + 1]], axis=0), v_pair)
        for i, hh in enumerate((h, h + 1)):
            o = o_pair[i * hrows:(i + 1) * hrows, i * V_HEAD_DIM:(i + 1) * V_HEAD_DIM] / ls[hh]
            outs.append(_subln(o[0:nq, :] - lam * o[nq:hrows, :], gs))
        yield
    out.append(jnp.concatenate(outs, axis=1))


def _ffn_sample_kernel(x_ref, at_ref, lo_ref, mod_ref, g2_ref, wout_ref, wupg_ref, wupv_ref, cfwg_ref, cfwv_ref,
                       cfbg_ref, cfbv_ref, wdn_ref, stg_ref, stv_ref,
                       y_ref, fcg_ref, fcv_ref, x1_ref, u2_ref, acc_ref):
    j = pl.program_id(0)
    nb = stg_ref.shape[1]
    nt = x_ref.shape[0] // nb
    rep = lambda a: jnp.concatenate([a] * nt, axis=0)

    @pl.when(j == 0)
    def _():
        gt1 = rep(mod_ref[0:nb, 2 * D_MODEL:3 * D_MODEL])
        sh2 = rep(mod_ref[0:nb, 3 * D_MODEL:4 * D_MODEL])
        sc2 = rep(mod_ref[0:nb, 4 * D_MODEL:5 * D_MODEL])
        mix = jnp.concatenate([at_ref[...].astype(BF16), lo_ref[...]], axis=1)
        x1 = x_ref[...] + gt1 * _dot(mix, wout_ref[...])
        x1_ref[...] = x1
        u2_ref[...] = _modulated_norm(x1, g2_ref[...], sc2, sh2).astype(BF16)
        acc_ref[...] = jnp.zeros(acc_ref.shape, F32)

    u2 = u2_ref[...]
    halves = []
    for wup_ref, cfw_ref, cfb_ref, st_ref, fc_ref in ((wupg_ref, cfwg_ref, cfbg_ref, stg_ref, fcg_ref),
                                                      (wupv_ref, cfwv_ref, cfbv_ref, stv_ref, fcv_ref)):
        up = _dot(u2, wup_ref[...])
        pad = [st_ref[i] for i in range(FFN_CONV - 1)] + [up[t * nb:(t + 1) * nb, :] for t in range(nt)]
        for i in range(FFN_CONV - 1):
            fc_ref[i] = pad[nt + i]
        w = cfw_ref[...]
        b = cfb_ref[...]
        hcs = []
        for t in range(nt):
            hc = pad[t] * w[0:1]
            for kk in range(1, FFN_CONV):
                hc = hc + pad[t + kk] * w[kk:kk + 1]
            hcs.append(hc + b)
        halves.append(jnp.concatenate(hcs, axis=0))
    g, val = halves
    hmid = (g * jax.nn.sigmoid(g) * val).astype(BF16)
    acc_ref[...] += _dot(hmid, wdn_ref[...])

    @pl.when(j == pl.num_programs(0) - 1)
    def _():
        gt2 = rep(mod_ref[0:nb, 5 * D_MODEL:6 * D_MODEL])
        y_ref[...] = x1_ref[...] + gt2 * acc_ref[...]


def _ffn_sample(x_tb, attn_tb, lru_tb, mod, g2, w_out, w_up, cfw, cfb, w_dn, st):
    m = x_tb.shape[0]
    d_ff = w_dn.shape[0]
    nb = st.shape[1]
    n_chunks = d_ff // FF_CHUNK
    whole = lambda a: pl.BlockSpec(a.shape, lambda j: (0,) * a.ndim)
    col = lambda rows, half: pl.BlockSpec((rows, FF_CHUNK), lambda j: (0, half * n_chunks + j))
    st_spec = lambda half: pl.BlockSpec((FFN_CONV - 1, nb, FF_CHUNK), lambda j: (0, 0, half * n_chunks + j))
    fc_spec = pl.BlockSpec((FFN_CONV - 1, nb, FF_CHUNK), lambda j: (0, 0, j))
    fc_shape = jax.ShapeDtypeStruct((FFN_CONV - 1, nb, d_ff), F32)
    return pl.pallas_call(
        _ffn_sample_kernel,
        grid=(n_chunks,),
        in_specs=[whole(x_tb), whole(attn_tb), whole(lru_tb), whole(mod), whole(g2), whole(w_out),
                  col(D_MODEL, 0), col(D_MODEL, 1), col(FFN_CONV, 0), col(FFN_CONV, 1), col(1, 0), col(1, 1),
                  pl.BlockSpec((FF_CHUNK, D_MODEL), lambda j: (j, 0)), st_spec(0), st_spec(1)],
        out_specs=[pl.BlockSpec((m, D_MODEL), lambda j: (0, 0)), fc_spec, fc_spec],
        out_shape=[jax.ShapeDtypeStruct((m, D_MODEL), F32), fc_shape, fc_shape],
        scratch_shapes=[pltpu.VMEM((m, D_MODEL), F32), pltpu.VMEM((m, D_MODEL), BF16),
                        pltpu.VMEM((m, D_MODEL), F32)],
        compiler_params=pltpu.CompilerParams(dimension_semantics=("arbitrary",), vmem_limit_bytes=VMEM_LIMIT),
        name="ffn_sample",
    )(x_tb, attn_tb, lru_tb, mod, g2, w_out, w_up, w_up, cfw, cfw, cfb, cfb, w_dn, st, st)


def _block_diag_halves(w):
    n, bd, _ = w.shape
    eye = jnp.eye(n // 2, dtype=w.dtype)
    halves = [jnp.einsum('nij,nm->nimj', w[s * (n // 2):(s + 1) * (n // 2)], eye).reshape(n // 2 * bd, n // 2 * bd)
              for s in range(2)]
    return jnp.stack(halves).astype(BF16)


def kernel(x_prompt, x_sample, cache_k, cache_v, page_table, state_lru_conv, state_lru_h, state_ffn_conv, c_prompt, c_sample, g_norm1, g_norm2, w_ada, b_ada, w_in, g_q, g_k, lam_q1, lam_k1, lam_q2, lam_k2, g_subln, w_out, conv_lru_w, conv_lru_b, w_rgate, b_rgate, w_igate, b_igate, lru_lambda, w_up, conv_ffn_w, conv_ffn_b, w_down):
    depth = w_in.shape[0]
    assert depth == 1, "single-layer step"
    bsz, s_len, _ = x_prompt.shape
    nb, nt, _ = x_sample.shape
    n_pages, page = page_table.shape[1], cache_k.shape[2]
    past_len = n_pages * page
    d_ff = w_down.shape[1]

    w_in_b = w_in[0].astype(BF16)
    w_out_b = w_out[0].astype(BF16)
    w_up_b = w_up[0].astype(BF16)
    w_dn_b = w_down[0].astype(BF16)
    wr = _block_diag_halves(w_rgate[0])
    wi = _block_diag_halves(w_igate[0])
    br = b_rgate[0].reshape(1, LRU_WIDTH)
    bi = b_igate[0].reshape(1, LRU_WIDTH)
    n_grp = ATT_WIDTH // QK_SUB_DIM
    gq = jnp.tile(g_q[0], n_grp)[None, :]
    gk = jnp.tile(g_k[0], n_grp)[None, :]
    grp = jnp.arange(ATT_WIDTH) // QK_SUB_DIM
    gmat = jnp.where(grp[:, None] == grp[None, :], 1.0 / QK_SUB_DIM, 0.0).astype(BF16)
    g1, g2 = g_norm1, g_norm2
    cw, cb = conv_lru_w[0], conv_lru_b
    cfw, cfb = conv_ffn_w[0], conv_ffn_b
    lam = lru_lambda
    lams = (lam_q1, lam_k1, lam_q2, lam_k2)
    gs = g_subln

    assert nb % SUBLANES == 0 and bsz <= SUBLANES
    c_all = jnp.concatenate([c_sample, c_prompt, jnp.zeros((SUBLANES - bsz, D_MODEL), F32)], axis=0)
    mod = _ada(c_all, w_ada[0], b_ada)

    tabs = _rope_tables(s_len)
    pad_rows = SUBLANES - nt
    tabs_s = tuple(t[past_len:past_len + SUBLANES] for t in tabs)

    q_p, kf_p, kb_p, vf_p, vb_p, lo_p, lc_p, lh_p = _inproj_prompt(
        x_prompt, mod, g1, w_in_b, gq, gk, gmat, tabs, cw, cb, wr, br, wi, bi, lam)
    x_tb = x_sample.transpose(1, 0, 2).reshape(nt * nb, D_MODEL)
    st_lru = state_lru_conv[0].transpose(1, 0, 2)
    q_s, k_s, v_s, lo_s, lc_s, lh_s = _inproj_sample(
        x_tb, mod, g1, w_in_b, gq, gk, gmat, tabs_s, cw, cb, wr, br, wi, bi, lam, st_lru, state_lru_h[0])
    to_bt = lambda a: a.reshape(nt, nb, -1).transpose(1, 0, 2)
    pad_t = lambda a: jnp.pad(a, ((0, 0), (0, pad_rows), (0, 0)))
    k_bt, v_bt = to_bt(k_s), to_bt(v_s)

    score_bound = QK_NORM_BOUND ** 2 * QK_SCALE_LOG2 * jnp.max(jnp.abs(g_q)) * jnp.max(jnp.abs(g_k))
    cache_kt = cache_k[0].transpose(0, 2, 3, 4, 1).reshape(-1, ATT_WIDTH, page)
    cache_vr = cache_v[0].reshape(-1, page * ATT_HEADS, V_HEAD_DIM)
    at_p, at_s = lax.cond(score_bound <= SCORE_LOG2_LIMIT,
                          functools.partial(_attn, True), functools.partial(_attn, False),
                          page_table, q_p, kb_p, vb_p, pad_t(to_bt(q_s)), pad_t(k_bt), pad_t(v_bt),
                          cache_kt, cache_vr, *lams, gs)

    y_p, fc_p = _ffn_prompt(x_prompt, at_p, lo_p, mod, g2, w_out_b, w_up_b, cfw, cfb, w_dn_b)
    at_tb = at_s[:, :nt].transpose(1, 0, 2).reshape(nt * nb, ATT_WIDTH)
    y_tb, fcg_s, fcv_s = _ffn_sample(x_tb, at_tb, lo_s, mod, g2, w_out_b, w_up_b, cfw, cfb, w_dn_b,
                                     state_ffn_conv[0].transpose(1, 0, 2))

    hd = (ATT_HEADS, 2, QK_SUB_DIM)
    return (y_p, to_bt(y_tb),
            kf_p.reshape(bsz, *hd, s_len).transpose(0, 4, 1, 2, 3)[None],
            vf_p.reshape(1, bsz, s_len, ATT_HEADS, V_HEAD_DIM),
            lc_p[None], lh_p.reshape(1, bsz, LRU_WIDTH), fc_p[None],
            k_bt.reshape(1, nb, nt, *hd), v_bt.reshape(1, nb, nt, ATT_HEADS, V_HEAD_DIM),
            lc_s.transpose(1, 0, 2)[None], lh_s[None],
            jnp.concatenate([fcg_s, fcv_s], axis=-1).transpose(1, 0, 2)[None])
```

```python
import functools
import math

import jax
import jax.numpy as jnp
from jax import lax
from jax.experimental import pallas as pl
from jax.experimental.pallas import tpu as pltpu

F32 = jnp.float32
BF16 = jnp.bfloat16

D_MODEL = 1024
ATT_WIDTH = 512
LRU_WIDTH = 512
ATT_HEADS = 4
V_HEAD_DIM = 128
QK_SUB_DIM = 64
ROPE_DIM = 16
ROPE_THETA = 500000.0
LRU_BLOCKS = 8
LRU_C = 8.0
LRU_CONV = 4
FFN_CONV = 3
EPS = 1e-6
LAM_INIT = 0.8 - 0.6 * math.exp(-0.3 * 0)
QK_SCALE = QK_SUB_DIM ** -0.5
QK_SCALE_LOG2 = QK_SCALE * math.log2(math.e)
SCORE_LOG2_LIMIT = 64.0
QK_NORM_BOUND = 1.01 * math.sqrt(QK_SUB_DIM)

LANES = 128
SUBLANES = 8
VMEM_LIMIT = 56 * 1024 * 1024

TM_IN = 1024
TQ = 512
TK = 512
TM_FFN = 512
FFN_SPLIT = 2
FFN_SLOT = 2
FF_CHUNK = 256
ADA_TN = 1024
ROPE_TM = 512
NEG = -1e30


def _dot(a, b):
    return jnp.dot(a, b, preferred_element_type=F32)


def _dot_nt(a, b):
    return lax.dot_general(a, b, (((1,), (1,)), ((), ())), preferred_element_type=F32)


def _const_spec(shape):
    nd = len(shape)
    return pl.BlockSpec(shape, lambda *_: (0,) * nd, pipeline_mode=pl.Buffered(1))


def _prompt_mod_spec(mod, k):
    return pl.BlockSpec((SUBLANES, D_MODEL), lambda b, i: (mod.shape[0] // SUBLANES - 1, k))


def _seq_row(ref):
    return ref[pl.ds(pl.program_id(0), 1), :]


def _modulated_norm(x, g, sc, sh):
    xn = x * lax.rsqrt(jnp.mean(x * x, axis=-1, keepdims=True) + EPS) * g
    return xn * (1.0 + sc) + sh


def _to_segment_major(val, buf_ref):
    tm, w = val.shape
    n = tm // SUBLANES
    for c in range(w // LANES):
        for s in range(SUBLANES):
            for r0 in range(0, n, SUBLANES):
                t0 = s * n + r0
                buf_ref[c, pl.ds(r0 * SUBLANES + s, SUBLANES, stride=SUBLANES), :] = (
                    val[t0:t0 + SUBLANES, c * LANES:(c + 1) * LANES])
    return jnp.concatenate([buf_ref[c] for c in range(w // LANES)], axis=1)


def _to_time_major(val, buf_ref):
    tm, w = val.shape
    n = tm // SUBLANES
    for c in range(w // LANES):
        buf_ref[c] = val[:, c * LANES:(c + 1) * LANES]
    rows = []
    for s in range(SUBLANES):
        for r0 in range(0, n, SUBLANES):
            rows.append(jnp.concatenate(
                [buf_ref[c, pl.ds(r0 * SUBLANES + s, SUBLANES, stride=SUBLANES), :] for c in range(w // LANES)], axis=1))
    return jnp.concatenate(rows, axis=0)


def _tail_groups(cur, k):
    tm = cur.shape[0]
    return [cur[tm - (k - i) * SUBLANES:tm - (k - i - 1) * SUBLANES, :] for i in range(k)]


def _delayed(cur, prev_tail):
    k = len(prev_tail)
    tm = cur.shape[0]
    first = lax.broadcasted_iota(jnp.int32, prev_tail[0].shape, 0) == 0
    heads = [jnp.where(first, pltpu.roll(p, 1, 0), pltpu.roll(c, 1, 0))
             for p, c in zip(prev_tail, _tail_groups(cur, k))]
    return [jnp.concatenate(heads[k - d:] + [cur[:tm - d * SUBLANES, :]], axis=0) for d in range(k, 0, -1)]


def _group_norm_rope(t, g_tiled, gmat, cos, sin_a, sin_b):
    ms = _dot((t * t).astype(BF16), gmat)
    tn = t * lax.rsqrt(ms + EPS) * g_tiled
    outs = []
    for h in range(ATT_WIDTH // LANES):
        th = tn[:, h * LANES:(h + 1) * LANES]
        outs.append(th * cos + pltpu.roll(th, LANES - 8, 1) * sin_a + pltpu.roll(th, 8, 1) * sin_b)
    return jnp.concatenate(outs, axis=1)


def _lru_gates(xc, wr_ref, br, wi_ref, bi, neg_c_softplus):
    half = LRU_WIDTH // 2
    xb = xc.astype(BF16)
    lo, hi = xb[:, :half], xb[:, half:]
    r = jax.nn.sigmoid(jnp.concatenate([_dot(lo, wr_ref[0]), _dot(hi, wr_ref[1])], axis=1) + br)
    ig = jax.nn.sigmoid(jnp.concatenate([_dot(lo, wi_ref[0]), _dot(hi, wi_ref[1])], axis=1) + bi)
    log_a = neg_c_softplus * r
    a = jnp.exp(log_a)
    one_minus_a2 = -jnp.tanh(log_a) * (a * a + 1.0)
    return a, jnp.sqrt(one_minus_a2) * (ig * xc)


def _neg_c_softplus(lam):
    z = -lam
    return -LRU_C * (jnp.maximum(z, 0.0) + jnp.log1p(jnp.exp(-jnp.abs(z))))


def _diff_lambda(lq1, lk1, lq2, lk2):
    s1 = jnp.sum(lq1 * lk1, axis=-1, keepdims=True)
    s2 = jnp.sum(lq2 * lk2, axis=-1, keepdims=True)
    return jnp.exp(s1) - jnp.exp(s2) + LAM_INIT


def _subln(o, g):
    return o * lax.rsqrt(jnp.mean(o * o, axis=-1, keepdims=True) + EPS) * g * (1.0 - LAM_INIT)


def _ada_kernel(c_ref, w_ref, b_ref, o_ref):
    c = c_ref[...]
    s = (c * jax.nn.sigmoid(c)).astype(BF16)
    o_ref[...] = _dot(s, w_ref[...].astype(BF16)) + b_ref[...]


def _ada(c_all, w_ada, b_ada):
    m = c_all.shape[0]
    n = w_ada.shape[1]
    tn = ADA_TN
    return pl.pallas_call(
        _ada_kernel,
        grid=(n // tn,),
        in_specs=[pl.BlockSpec((m, D_MODEL), lambda j: (0, 0)),
                  pl.BlockSpec((D_MODEL, tn), lambda j: (0, j)),
                  pl.BlockSpec((1, tn), lambda j: (0, j))],
        out_specs=pl.BlockSpec((m, tn), lambda j: (0, j)),
        out_shape=jax.ShapeDtypeStruct((m, n), F32),
        compiler_params=pltpu.CompilerParams(dimension_semantics=("arbitrary",), vmem_limit_bytes=VMEM_LIMIT),
        name="ada_mod",
    )(c_all, w_ada, b_ada)


def _rope_table_kernel(freq_ref, ma_ref, mb_ref, c_ref, sa_ref, sb_ref):
    tm = c_ref.shape[0]
    pos = (pl.program_id(0) * tm + lax.broadcasted_iota(jnp.int32, (tm, LANES), 0)).astype(F32)
    ang = pos * freq_ref[...]
    s = jnp.sin(ang)
    c_ref[...] = jnp.cos(ang)
    sa_ref[...] = -s * ma_ref[...]
    sb_ref[...] = s * mb_ref[...]


def _rope_tables(n_pos):
    half = ROPE_DIM // 2
    d = jnp.arange(LANES) % QK_SUB_DIM
    freqs = ROPE_THETA ** (-(d % half).astype(F32) * 2.0 / ROPE_DIM)
    freq_lane = jnp.where(d < ROPE_DIM, freqs, 0.0).astype(F32)[None, :]
    mask_a = (d < half).astype(F32)[None, :]
    mask_b = ((d >= half) & (d < ROPE_DIM)).astype(F32)[None, :]
    tm = ROPE_TM
    row = pl.BlockSpec((1, LANES), lambda i: (0, 0))
    tab = pl.BlockSpec((tm, LANES), lambda i: (i, 0))
    shp = jax.ShapeDtypeStruct((n_pos, LANES), F32)
    return pl.pallas_call(
        _rope_table_kernel,
        grid=(n_pos // tm,),
        in_specs=[row, row, row],
        out_specs=[tab, tab, tab],
        out_shape=[shp, shp, shp],
        compiler_params=pltpu.CompilerParams(dimension_semantics=("arbitrary",)),
        name="rope_tables",
    )(freq_lane, mask_a, mask_b)


def _inproj_prompt_kernel(x_ref, sh_ref, sc_ref, g1_ref, win_ref, gq_ref, gk_ref, gmat_ref,
                          cos_ref, sa_ref, sb_ref, cw_ref, cb_ref, wr_ref, br_ref, wi_ref, bi_ref, lam_ref,
                          q_ref, kf_ref, kb_ref, vf_ref, vb_ref, lo_ref, lc_ref, lh_ref,
                          lbuf_ref, hbuf_ref, tail_ref, hcar_ref):
    tm = x_ref.shape[0]

    @pl.when(pl.program_id(1) == 0)
    def _():
        tail_ref[...] = jnp.zeros(tail_ref.shape, F32)
        hcar_ref[...] = jnp.zeros(hcar_ref.shape, F32)

    u = _modulated_norm(x_ref[...], g1_ref[...], _seq_row(sc_ref), _seq_row(sh_ref)).astype(BF16)
    cos, sa, sb = cos_ref[...], sa_ref[...], sb_ref[...]
    gmat = gmat_ref[...]

    q = _group_norm_rope(_dot(u, win_ref[:, 0:ATT_WIDTH]), gq_ref[...], gmat, cos, sa, sb)
    q_ref[...] = (q * QK_SCALE_LOG2).T.astype(BF16)
    k = _group_norm_rope(_dot(u, win_ref[:, ATT_WIDTH:2 * ATT_WIDTH]), gk_ref[...], gmat, cos, sa, sb)
    kf_ref[...] = k.T
    kb_ref[...] = k.astype(BF16)
    v = _dot(u, win_ref[:, 2 * ATT_WIDTH:3 * ATT_WIDTH])
    for h in range(ATT_HEADS):
        vf_ref[pl.ds(h, tm, stride=ATT_HEADS), :] = v[:, h * V_HEAD_DIM:(h + 1) * V_HEAD_DIM]
    vb_ref[...] = v.T.astype(BF16)

    c0 = 3 * ATT_WIDTH
    n_tail = LRU_CONV - 1
    lx = _to_segment_major(_dot(u, win_ref[:, c0:c0 + LRU_WIDTH]), lbuf_ref)
    cw = cw_ref[...]
    d3, d2, d1 = _delayed(lx, [tail_ref[i] for i in range(n_tail)])
    xc = d3 * cw[0:1] + d2 * cw[1:2] + d1 * cw[2:3] + lx * cw[3:4] + cb_ref[...]
    for i, grp in enumerate(_tail_groups(lx, n_tail)):
        tail_ref[i] = grp
        lc_ref[i:i + 1, :] = grp[SUBLANES - 1:SUBLANES, :]

    a, gx = _lru_gates(xc, wr_ref, br_ref[...], wi_ref, bi_ref[...], _neg_c_softplus(lam_ref[...]))

    n = tm // SUBLANES
    grp = lambda arr, r: arr[r * SUBLANES:(r + 1) * SUBLANES, :]
    ps, hs = [grp(a, 0)], [grp(gx, 0)]
    for r in range(1, n):
        ar = grp(a, r)
        ps.append(ar * ps[-1])
        hs.append(ar * hs[-1] + grp(gx, r))
    p_end, h_end = ps[-1], hs[-1]
    entering = [hcar_ref[...]]
    for s in range(1, SUBLANES):
        entering.append(p_end[s - 1:s, :] * entering[-1] + h_end[s - 1:s, :])
    h_last = p_end[SUBLANES - 1:SUBLANES, :] * entering[-1] + h_end[SUBLANES - 1:SUBLANES, :]
    hcar_ref[...] = h_last
    lh_ref[...] = h_last
    enter = jnp.concatenate(entering, axis=0)
    states = jnp.concatenate([hs[r] + ps[r] * enter for r in range(n)], axis=0)

    lg = _dot(u, win_ref[:, c0 + LRU_WIDTH:c0 + 2 * LRU_WIDTH])
    lo_ref[...] = (_to_time_major(states, hbuf_ref) * jax.nn.gelu(lg, approximate=True)).astype(BF16)


def _inproj_prompt(x, mod, g1, w_in, gq, gk, gmat, tabs, cw, cb, wr, br, wi, bi, lam):
    bsz, s_len, _ = x.shape
    tm = TM_IN
    row_spec = lambda w: pl.BlockSpec((None, tm, w), lambda b, i: (b, i, 0))
    tab_spec = pl.BlockSpec((tm, LANES), lambda b, i: (i, 0))
    mod_spec = lambda k: _prompt_mod_spec(mod, k)
    act = lambda dt: jax.ShapeDtypeStruct((bsz, s_len, ATT_WIDTH), dt)
    act_t = lambda dt: jax.ShapeDtypeStruct((bsz, ATT_WIDTH, s_len), dt)
    col_spec = pl.BlockSpec((None, ATT_WIDTH, tm), lambda b, i: (b, 0, i))
    return pl.pallas_call(
        _inproj_prompt_kernel,
        grid=(bsz, s_len // tm),
        in_specs=[row_spec(D_MODEL), mod_spec(0), mod_spec(1), _const_spec(g1.shape), _const_spec(w_in.shape),
                  _const_spec(gq.shape), _const_spec(gk.shape), _const_spec(gmat.shape),
                  tab_spec, tab_spec, tab_spec,
                  _const_spec(cw.shape), _const_spec(cb.shape), _const_spec(wr.shape), _const_spec(br.shape),
                  _const_spec(wi.shape), _const_spec(bi.shape), _const_spec(lam.shape)],
        out_specs=[col_spec, col_spec, row_spec(ATT_WIDTH),
                   pl.BlockSpec((None, tm * ATT_HEADS, V_HEAD_DIM), lambda b, i: (b, i, 0)),
                   col_spec, row_spec(LRU_WIDTH),
                   pl.BlockSpec((None, LRU_CONV - 1, LRU_WIDTH), lambda b, i: (b, 0, 0)),
                   pl.BlockSpec((None, 1, LRU_WIDTH), lambda b, i: (b, 0, 0))],
        out_shape=[act_t(BF16), act_t(F32), act(BF16),
                   jax.ShapeDtypeStruct((bsz, s_len * ATT_HEADS, V_HEAD_DIM), F32), act_t(BF16), act(BF16),
                   jax.ShapeDtypeStruct((bsz, LRU_CONV - 1, LRU_WIDTH), F32),
                   jax.ShapeDtypeStruct((bsz, 1, LRU_WIDTH), F32)],
        scratch_shapes=[pltpu.VMEM((LRU_WIDTH // LANES, tm, LANES), F32),
                        pltpu.VMEM((LRU_WIDTH // LANES, tm, LANES), F32),
                        pltpu.VMEM((LRU_CONV - 1, SUBLANES, LRU_WIDTH), F32),
                        pltpu.VMEM((1, LRU_WIDTH), F32)],
        compiler_params=pltpu.CompilerParams(dimension_semantics=("arbitrary", "arbitrary"),
                                             vmem_limit_bytes=VMEM_LIMIT),
        name="inproj_prompt",
    )(x, mod, mod, g1, w_in, gq, gk, gmat, *tabs, cw, cb, wr, br, wi, bi, lam)


def _alternate(*step_generators):
    live = list(step_generators)
    while live:
        for g in list(live):
            try:
                next(g)
            except StopIteration:
                live.remove(g)


def _page_copies(pt_ref, ckt_hbm, cv_hbm, kbuf_ref, vbuf_ref, sem_ref, seq, slot):
    copies = []
    for j in range(kbuf_ref.shape[1]):
        pg = pt_ref[seq, j]
        copies.append(pltpu.make_async_copy(ckt_hbm.at[pg], kbuf_ref.at[slot, j], sem_ref.at[slot]))
        copies.append(pltpu.make_async_copy(cv_hbm.at[pg], vbuf_ref.at[slot, j], sem_ref.at[slot]))
    return copies


def _attn_kernel(bounded, pt_ref, qt_ref, k_ref, vt_ref, lq1_ref, lk1_ref, lq2_ref, lk2_ref, gs_ref,
                 qs_ref, kn_ref, vn_ref, ckt_hbm, cv_hbm, o_ref, os_ref, kbuf_ref, vbuf_ref, sem_ref):
    n_pages = kbuf_ref.shape[1]
    step = (pl.program_id(0) * pl.num_programs(1) + pl.program_id(1)) * pl.num_programs(2) + pl.program_id(2)
    n_steps = pl.num_programs(0) * pl.num_programs(1) * pl.num_programs(2)
    slot = step % 2

    @pl.when(step == 0)
    def _():
        for cp in _page_copies(pt_ref, ckt_hbm, cv_hbm, kbuf_ref, vbuf_ref, sem_ref, step, slot):
            cp.start()

    @pl.when(step + 1 < n_steps)
    def _():
        for cp in _page_copies(pt_ref, ckt_hbm, cv_hbm, kbuf_ref, vbuf_ref, sem_ref, step + 1, 1 - slot):
            cp.start()

    tq = qt_ref.shape[1]
    qi = pl.program_id(2)
    qt = qt_ref[...].astype(F32)
    feat = lax.broadcasted_iota(jnp.int32, (V_HEAD_DIM, tq), 0)
    qc = (jnp.where(feat < QK_SUB_DIM, qt, 0.0).astype(BF16), jnp.where(feat >= QK_SUB_DIM, qt, 0.0).astype(BF16))

    def tile_steps(k0, width, carry, masked, out):
        k0 = pl.multiple_of(k0, TK)
        ks = k_ref[pl.ds(k0, width), :]
        vts = vt_ref[:, pl.ds(k0, width)]
        ss = []
        for c in range(2):
            ss.append(_dot(ks, qc[c]))
            yield
        for c in range(2):
            m, l, acc = carry[c]
            s = ss[c]
            if masked:
                keys = lax.broadcasted_iota(jnp.int32, (width, tq), 0)
                qs = lax.broadcasted_iota(jnp.int32, (width, tq), 1)
                s = jnp.where(keys <= qs, s, NEG)
            if bounded:
                p = jnp.exp2(s)
                l = l + jnp.sum(p, axis=0, keepdims=True)
                acc = acc + _dot(vts, p.astype(BF16))
            else:
                m_new = jnp.maximum(m, jnp.max(s, axis=0, keepdims=True))
                alpha = jnp.exp2(m - m_new)
                p = jnp.exp2(s - m_new)
                l = alpha * l + jnp.sum(p, axis=0, keepdims=True)
                acc = alpha * acc + _dot(vts, p.astype(BF16))
                m = m_new
            out.append((m, l, acc))
            yield

    def tile(k0, width, carry, masked):
        out = []
        for _ in tile_steps(k0, width, carry, masked, out):
            pass
        return tuple(out)

    init = tuple((jnp.full((1, tq), NEG, F32), jnp.zeros((1, tq), F32), jnp.zeros((V_HEAD_DIM, tq), F32))
                 for _ in range(2))
    carry = lax.fori_loop(0, qi // 2, lambda j, c: tile(j * 2 * TK, 2 * TK, c, False), init)
    carry = lax.cond(qi % 2 == 1, lambda c: tile((qi - 1) * TK, TK, c, False), lambda c: c, carry)
    for cp in _page_copies(pt_ref, ckt_hbm, cv_hbm, kbuf_ref, vbuf_ref, sem_ref, step, slot):
        cp.wait()
    kt_refs = [kbuf_ref.at[slot, j] for j in range(n_pages)]
    v_refs = [vbuf_ref.at[slot, j] for j in range(n_pages)]
    lam = _diff_lambda(lq1_ref[...], lk1_ref[...], lq2_ref[...], lk2_ref[...])
    tile_out, sample_out = [], []
    _alternate(tile_steps(qi * TK, TK, carry, True, tile_out),
               _sample_attention_steps(bounded, qs_ref, kn_ref, vn_ref, lam, gs_ref[...], kt_refs, v_refs, sample_out))
    (_, l0, a0), (_, l1, a1) = tile_out
    ot = a0 / l0 - lam * (a1 / l1)
    o_ref[...] = _subln(ot.T, gs_ref[...]).astype(o_ref.dtype)
    os_ref[...] = sample_out[0]


def _attn(bounded, page_table, qt, k, vt, q8, kn8, vn8, cache_kt, cache_v, lq1, lk1, lq2, lk2, gs):
    bsz, s_len, _ = k.shape
    nb, n_pages = page_table.shape
    nq = q8.shape[1]
    page = cache_kt.shape[2]
    n_qt = s_len // TQ
    assert nb == bsz * ATT_HEADS * n_qt, "one sample sequence per grid step"
    seq = lambda b, h, i: (b * ATT_HEADS + h) * n_qt + i
    qt_spec = pl.BlockSpec((None, V_HEAD_DIM, TQ), lambda b, h, i, pt: (b, h, i))
    k_spec = pl.BlockSpec((None, s_len, V_HEAD_DIM), lambda b, h, i, pt: (b, 0, h))
    vt_spec = pl.BlockSpec((None, V_HEAD_DIM, s_len), lambda b, h, i, pt: (b, h, 0))
    o_spec = pl.BlockSpec((None, TQ, V_HEAD_DIM), lambda b, h, i, pt: (b, i, h))
    small = lambda a: pl.BlockSpec(a.shape, lambda b, h, i, pt: (0, 0))
    new_spec = pl.BlockSpec((None, nq, ATT_WIDTH), lambda b, h, i, pt: (seq(b, h, i), 0, 0))
    hbm = pl.BlockSpec(memory_space=pl.ANY)
    grid_spec = pltpu.PrefetchScalarGridSpec(
        num_scalar_prefetch=1,
        grid=(bsz, ATT_HEADS, n_qt),
        in_specs=[qt_spec, k_spec, vt_spec, small(lq1), small(lk1), small(lq2), small(lk2), small(gs),
                  new_spec, new_spec, new_spec, hbm, hbm],
        out_specs=[o_spec, new_spec],
        scratch_shapes=[pltpu.VMEM((2, n_pages, ATT_WIDTH, page), F32),
                        pltpu.VMEM((2, n_pages, page * ATT_HEADS, V_HEAD_DIM), F32),
                        pltpu.SemaphoreType.DMA((2,))],
    )
    return pl.pallas_call(
        functools.partial(_attn_kernel, bounded),
        grid_spec=grid_spec,
        out_shape=[jax.ShapeDtypeStruct((bsz, s_len, ATT_WIDTH), BF16),
                   jax.ShapeDtypeStruct((nb, nq, ATT_WIDTH), F32)],
        compiler_params=pltpu.CompilerParams(dimension_semantics=("arbitrary", "arbitrary", "arbitrary"),
                                             vmem_limit_bytes=VMEM_LIMIT),
        name="attn_bounded" if bounded else "attn_general",
    )(page_table, qt, k, vt, lq1, lk1, lq2, lk2, gs, q8, kn8, vn8, cache_kt, cache_v)


def _ffn_prompt_kernel(x_ref, at_ref, lo_ref, gt1_ref, sh2_ref, sc2_ref, gt2_ref, g2_ref,
                       wout_ref, wup_ref, cfw_ref, cfb_ref, wdn_ref,
                       y_ref, fc_ref, ubuf_ref, abuf_ref, tail_ref):
    tm = x_ref.shape[0]
    sub = tm // FFN_SPLIT
    d_ff = wdn_ref.shape[0]
    n_tail = FFN_CONV - 1
    n_chunks = d_ff // FF_CHUNK
    chunk_cols = lambda j: [slice(base + j * FF_CHUNK, base + (j + 1) * FF_CHUNK) for base in (0, d_ff)]

    @pl.when(pl.program_id(1) == 0)
    def _():
        tail_ref[...] = jnp.zeros(tail_ref.shape, F32)

    def attention_residual(g):
        rows = slice(g * sub, (g + 1) * sub)
        mix = jnp.concatenate([at_ref[rows, :], lo_ref[rows, :]], axis=1)
        x1 = x_ref[rows, :] + _seq_row(gt1_ref) * _dot(mix, wout_ref[...])
        u2 = _to_segment_major(_modulated_norm(x1, g2_ref[...], _seq_row(sc2_ref), _seq_row(sh2_ref)),
                               ubuf_ref.at[g]).astype(BF16)
        return x1, u2

    def finish(g, x1, acc):
        y_ref[g * sub:(g + 1) * sub, :] = x1 + _seq_row(gt2_ref) * _to_time_major(acc, abuf_ref.at[g])

    def hidden(j, ups):
        halves = []
        for cols, up in zip(chunk_cols(j), ups):
            w = cfw_ref[:, cols]
            d2, d1 = _delayed(up, [tail_ref[i, :, cols] for i in range(n_tail)])
            halves.append(d2 * w[0:1] + d1 * w[1:2] + up * w[2:3] + cfb_ref[:, cols])
            for i, grp in enumerate(_tail_groups(up, n_tail)):
                tail_ref[i, :, cols] = grp
                fc_ref[i:i + 1, cols] = grp[SUBLANES - 1:SUBLANES, :]
        g, val = halves
        return (g * jax.nn.sigmoid(g) * val).astype(BF16)

    state = attention_residual(0)
    done = None
    for g in range(FFN_SPLIT):
        x1, u2 = state
        up_project = lambda j: [_dot(u2, wup_ref[:, cols]) for cols in chunk_cols(j)]
        acc = jnp.zeros((sub, D_MODEL), F32)
        ups = up_project(0)
        for j in range(n_chunks):
            ups_next = up_project(j + 1) if j + 1 < n_chunks else None
            acc = acc + _dot(hidden(j, ups), wdn_ref[j * FF_CHUNK:(j + 1) * FF_CHUNK, :])
            ups = ups_next
            if j == FFN_SLOT and done is not None:
                finish(*done)
                done = None
            if j == FFN_SLOT and g + 1 < FFN_SPLIT:
                state = attention_residual(g + 1)
        done = (g, x1, acc)
    finish(*done)


def _ffn_prompt(x, attn, lru, mod, g2, w_out, w_up, cfw, cfb, w_dn):
    bsz, s_len, _ = x.shape
    tm = TM_FFN
    d_ff = w_dn.shape[0]
    row_spec = lambda w: pl.BlockSpec((None, tm, w), lambda b, i: (b, i, 0))
    mod_spec = lambda k: _prompt_mod_spec(mod, k)
    return pl.pallas_call(
        _ffn_prompt_kernel,
        grid=(bsz, s_len // tm),
        in_specs=[row_spec(D_MODEL), row_spec(ATT_WIDTH), row_spec(LRU_WIDTH),
                  mod_spec(2), mod_spec(3), mod_spec(4), mod_spec(5), _const_spec(g2.shape),
                  _const_spec(w_out.shape), _const_spec(w_up.shape), _const_spec(cfw.shape),
                  _const_spec(cfb.shape), _const_spec(w_dn.shape)],
        out_specs=[row_spec(D_MODEL), pl.BlockSpec((None, FFN_CONV - 1, 2 * d_ff), lambda b, i: (b, 0, 0))],
        out_shape=[jax.ShapeDtypeStruct((bsz, s_len, D_MODEL), F32),
                   jax.ShapeDtypeStruct((bsz, FFN_CONV - 1, 2 * d_ff), F32)],
        scratch_shapes=[pltpu.VMEM((FFN_SPLIT, D_MODEL // LANES, tm // FFN_SPLIT, LANES), F32),
                        pltpu.VMEM((FFN_SPLIT, D_MODEL // LANES, tm // FFN_SPLIT, LANES), F32),
                        pltpu.VMEM((FFN_CONV - 1, SUBLANES, 2 * d_ff), F32)],
        compiler_params=pltpu.CompilerParams(dimension_semantics=("arbitrary", "arbitrary"),
                                             vmem_limit_bytes=VMEM_LIMIT),
        name="ffn_prompt",
    )(x, attn, lru, mod, mod, mod, mod, g2, w_out, w_up, cfw, cfb, w_dn)


def _inproj_sample_kernel(x_ref, mod_ref, g1_ref, win_ref, gq_ref, gk_ref, gmat_ref,
                          cos_ref, sa_ref, sb_ref, cw_ref, cb_ref, wr_ref, br_ref, wi_ref, bi_ref, lam_ref,
                          st_ref, h0_ref,
                          q_ref, k_ref, v_ref, lo_ref, lc_ref, lh_ref):
    nb = h0_ref.shape[0]
    nt = x_ref.shape[0] // nb
    rep = lambda a: jnp.concatenate([a] * nt, axis=0)
    sh1 = rep(mod_ref[0:nb, 0:D_MODEL])
    sc1 = rep(mod_ref[0:nb, D_MODEL:2 * D_MODEL])
    u = _modulated_norm(x_ref[...], g1_ref[...], sc1, sh1).astype(BF16)
    per_t = lambda r: jnp.concatenate([jnp.broadcast_to(r[t:t + 1, :], (nb, LANES)) for t in range(nt)], axis=0)
    cos, sa, sb = per_t(cos_ref[...]), per_t(sa_ref[...]), per_t(sb_ref[...])
    gmat = gmat_ref[...]

    q = _group_norm_rope(_dot(u, win_ref[:, 0:ATT_WIDTH]), gq_ref[...], gmat, cos, sa, sb)
    q_ref[...] = q * QK_SCALE_LOG2
    k_ref[...] = _group_norm_rope(_dot(u, win_ref[:, ATT_WIDTH:2 * ATT_WIDTH]), gk_ref[...], gmat, cos, sa, sb)
    v_ref[...] = _dot(u, win_ref[:, 2 * ATT_WIDTH:3 * ATT_WIDTH])

    c0 = 3 * ATT_WIDTH
    lx = _dot(u, win_ref[:, c0:c0 + LRU_WIDTH])
    lg = _dot(u, win_ref[:, c0 + LRU_WIDTH:c0 + 2 * LRU_WIDTH])
    pad = [st_ref[i] for i in range(LRU_CONV - 1)] + [lx[t * nb:(t + 1) * nb, :] for t in range(nt)]
    for i in range(LRU_CONV - 1):
        lc_ref[i] = pad[nt + i]
    cw = cw_ref[...]
    xcs = []
    for t in range(nt):
        xc = pad[t] * cw[0:1]
        for kk in range(1, LRU_CONV):
            xc = xc + pad[t + kk] * cw[kk:kk + 1]
        xcs.append(xc + cb_ref[...])
    a, gx = _lru_gates(jnp.concatenate(xcs, axis=0), wr_ref, br_ref[...], wi_ref, bi_ref[...],
                       _neg_c_softplus(lam_ref[...]))
    h = h0_ref[...]
    hs = []
    for t in range(nt):
        h = a[t * nb:(t + 1) * nb, :] * h + gx[t * nb:(t + 1) * nb, :]
        hs.append(h)
    lh_ref[...] = h
    lo_ref[...] = (jnp.concatenate(hs, axis=0) * jax.nn.gelu(lg, approximate=True)).astype(BF16)


def _inproj_sample(x_tb, mod_s, g1, w_in, gq, gk, gmat, tabs_s, cw, cb, wr, br, wi, bi, lam, st, h0):
    m = x_tb.shape[0]
    nb = h0.shape[0]
    act = jax.ShapeDtypeStruct((m, ATT_WIDTH), F32)
    return pl.pallas_call(
        _inproj_sample_kernel,
        out_shape=[act, act, act, jax.ShapeDtypeStruct((m, LRU_WIDTH), BF16),
                   jax.ShapeDtypeStruct((LRU_CONV - 1, nb, LRU_WIDTH), F32),
                   jax.ShapeDtypeStruct((nb, LRU_WIDTH), F32)],
        compiler_params=pltpu.CompilerParams(vmem_limit_bytes=VMEM_LIMIT),
        name="inproj_sample",
    )(x_tb, mod_s, g1, w_in, gq, gk, gmat, *tabs_s, cw, cb, wr, br, wi, bi, lam, st, h0)


def _sample_attention_steps(bounded, q_ref, kn_ref, vn_ref, lam, gs, kt_refs, v_refs, out):
    n_pages = len(kt_refs)
    page = kt_refs[0].shape[1]
    nq = q_ref.shape[0]
    hrows = 2 * nq

    lane = lax.broadcasted_iota(jnp.int32, (nq, V_HEAD_DIM), 1)
    qh = []
    for h in range(ATT_HEADS):
        q8 = q_ref[:, h * V_HEAD_DIM:(h + 1) * V_HEAD_DIM]
        qh.append(jnp.concatenate([jnp.where(lane < QK_SUB_DIM, q8, 0.0), jnp.where(lane >= QK_SUB_DIM, q8, 0.0)],
                                  axis=0).astype(BF16))

    zpad = jnp.zeros((page - nq, V_HEAD_DIM), F32)
    rt = lax.broadcasted_iota(jnp.int32, (hrows, page), 0) % nq
    ct = lax.broadcasted_iota(jnp.int32, (hrows, page), 1)
    ps, ls = [], []
    for h in range(ATT_HEADS):
        rows = slice(h * V_HEAD_DIM, (h + 1) * V_HEAD_DIM)
        kt_all = jnp.concatenate([kt_refs[j][rows, :] for j in range(n_pages)], axis=1).astype(BF16)
        k_new = jnp.concatenate([kn_ref[:, rows], zpad], axis=0).astype(BF16)
        s_new = jnp.where(ct <= rt, _dot_nt(qh[h], k_new), NEG)
        s = jnp.concatenate([_dot(qh[h], kt_all), s_new], axis=1)
        if not bounded:
            s = s - jnp.max(s, axis=1, keepdims=True)
        p = jnp.exp2(s)
        ls.append(jnp.sum(p, axis=1, keepdims=True))
        ps.append(p.astype(BF16))
        yield
    outs = []
    for h in range(0, ATT_HEADS, 2):
        v_pair = jnp.concatenate([jnp.concatenate(
            [v_refs[j][pl.ds(hh, page, stride=ATT_HEADS), :] for j in range(n_pages)]
            + [vn_ref[:, hh * V_HEAD_DIM:(hh + 1) * V_HEAD_DIM], zpad], axis=0) for hh in (h, h + 1)],
            axis=1).astype(BF16)
        o_pair = _dot(jnp.concatenate([ps[h], ps[h + 1]], axis=0), v_pair)
        for i, hh in enumerate((h, h + 1)):
            o = o_pair[i * hrows:(i + 1) * hrows, i * V_HEAD_DIM:(i + 1) * V_HEAD_DIM] / ls[hh]
            outs.append(_subln(o[0:nq, :] - lam * o[nq:hrows, :], gs))
        yield
    out.append(jnp.concatenate(outs, axis=1))


def _ffn_sample_kernel(x_ref, at_ref, lo_ref, mod_ref, g2_ref, wout_ref, wupg_ref, wupv_ref, cfwg_ref, cfwv_ref,
                       cfbg_ref, cfbv_ref, wdn_ref, stg_ref, stv_ref,
                       y_ref, fcg_ref, fcv_ref, x1_ref, u2_ref, acc_ref):
    j = pl.program_id(0)
    nb = stg_ref.shape[1]
    nt = x_ref.shape[0] // nb
    rep = lambda a: jnp.concatenate([a] * nt, axis=0)

    @pl.when(j == 0)
    def _():
        gt1 = rep(mod_ref[0:nb, 2 * D_MODEL:3 * D_MODEL])
        sh2 = rep(mod_ref[0:nb, 3 * D_MODEL:4 * D_MODEL])
        sc2 = rep(mod_ref[0:nb, 4 * D_MODEL:5 * D_MODEL])
        mix = jnp.concatenate([at_ref[...].astype(BF16), lo_ref[...]], axis=1)
        x1 = x_ref[...] + gt1 * _dot(mix, wout_ref[...])
        x1_ref[...] = x1
        u2_ref[...] = _modulated_norm(x1, g2_ref[...], sc2, sh2).astype(BF16)
        acc_ref[...] = jnp.zeros(acc_ref.shape, F32)

    u2 = u2_ref[...]
    halves = []
    for wup_ref, cfw_ref, cfb_ref, st_ref, fc_ref in ((wupg_ref, cfwg_ref, cfbg_ref, stg_ref, fcg_ref),
                                                      (wupv_ref, cfwv_ref, cfbv_ref, stv_ref, fcv_ref)):
        up = _dot(u2, wup_ref[...])
        pad = [st_ref[i] for i in range(FFN_CONV - 1)] + [up[t * nb:(t + 1) * nb, :] for t in range(nt)]
        for i in range(FFN_CONV - 1):
            fc_ref[i] = pad[nt + i]
        w = cfw_ref[...]
        b = cfb_ref[...]
        hcs = []
        for t in range(nt):
            hc = pad[t] * w[0:1]
            for kk in range(1, FFN_CONV):
                hc = hc + pad[t + kk] * w[kk:kk + 1]
            hcs.append(hc + b)
        halves.append(jnp.concatenate(hcs, axis=0))
    g, val = halves
    hmid = (g * jax.nn.sigmoid(g) * val).astype(BF16)
    acc_ref[...] += _dot(hmid, wdn_ref[...])

    @pl.when(j == pl.num_programs(0) - 1)
    def _():
        gt2 = rep(mod_ref[0:nb, 5 * D_MODEL:6 * D_MODEL])
        y_ref[...] = x1_ref[...] + gt2 * acc_ref[...]


def _ffn_sample(x_tb, attn_tb, lru_tb, mod, g2, w_out, w_up, cfw, cfb, w_dn, st):
    m = x_tb.shape[0]
    d_ff = w_dn.shape[0]
    nb = st.shape[1]
    n_chunks = d_ff // FF_CHUNK
    whole = lambda a: pl.BlockSpec(a.shape, lambda j: (0,) * a.ndim)
    col = lambda rows, half: pl.BlockSpec((rows, FF_CHUNK), lambda j: (0, half * n_chunks + j))
    st_spec = lambda half: pl.BlockSpec((FFN_CONV - 1, nb, FF_CHUNK), lambda j: (0, 0, half * n_chunks + j))
    fc_spec = pl.BlockSpec((FFN_CONV - 1, nb, FF_CHUNK), lambda j: (0, 0, j))
    fc_shape = jax.ShapeDtypeStruct((FFN_CONV - 1, nb, d_ff), F32)
    return pl.pallas_call(
        _ffn_sample_kernel,
        grid=(n_chunks,),
        in_specs=[whole(x_tb), whole(attn_tb), whole(lru_tb), whole(mod), whole(g2), whole(w_out),
                  col(D_MODEL, 0), col(D_MODEL, 1), col(FFN_CONV, 0), col(FFN_CONV, 1), col(1, 0), col(1, 1),
                  pl.BlockSpec((FF_CHUNK, D_MODEL), lambda j: (j, 0)), st_spec(0), st_spec(1)],
        out_specs=[pl.BlockSpec((m, D_MODEL), lambda j: (0, 0)), fc_spec, fc_spec],
        out_shape=[jax.ShapeDtypeStruct((m, D_MODEL), F32), fc_shape, fc_shape],
        scratch_shapes=[pltpu.VMEM((m, D_MODEL), F32), pltpu.VMEM((m, D_MODEL), BF16),
                        pltpu.VMEM((m, D_MODEL), F32)],
        compiler_params=pltpu.CompilerParams(dimension_semantics=("arbitrary",), vmem_limit_bytes=VMEM_LIMIT),
        name="ffn_sample",
    )(x_tb, attn_tb, lru_tb, mod, g2, w_out, w_up, w_up, cfw, cfw, cfb, cfb, w_dn, st, st)


def _block_diag_halves(w):
    n, bd, _ = w.shape
    eye = jnp.eye(n // 2, dtype=w.dtype)
    halves = [jnp.einsum('nij,nm->nimj', w[s * (n // 2):(s + 1) * (n // 2)], eye).reshape(n // 2 * bd, n // 2 * bd)
              for s in range(2)]
    return jnp.stack(halves).astype(BF16)


def kernel(x_prompt, x_sample, cache_k, cache_v, page_table, state_lru_conv, state_lru_h, state_ffn_conv, c_prompt, c_sample, g_norm1, g_norm2, w_ada, b_ada, w_in, g_q, g_k, lam_q1, lam_k1, lam_q2, lam_k2, g_subln, w_out, conv_lru_w, conv_lru_b, w_rgate, b_rgate, w_igate, b_igate, lru_lambda, w_up, conv_ffn_w, conv_ffn_b, w_down):
    depth = w_in.shape[0]
    assert depth == 1, "single-layer step"
    bsz, s_len, _ = x_prompt.shape
    nb, nt, _ = x_sample.shape
    n_pages, page = page_table.shape[1], cache_k.shape[2]
    past_len = n_pages * page
    d_ff = w_down.shape[1]

    w_in_b = w_in[0].astype(BF16)
    w_out_b = w_out[0].astype(BF16)
    w_up_b = w_up[0].astype(BF16)
    w_dn_b = w_down[0].astype(BF16)
    wr = _block_diag_halves(w_rgate[0])
    wi = _block_diag_halves(w_igate[0])
    br = b_rgate[0].reshape(1, LRU_WIDTH)
    bi = b_igate[0].reshape(1, LRU_WIDTH)
    n_grp = ATT_WIDTH // QK_SUB_DIM
    gq = jnp.tile(g_q[0], n_grp)[None, :]
    gk = jnp.tile(g_k[0], n_grp)[None, :]
    grp = jnp.arange(ATT_WIDTH) // QK_SUB_DIM
    gmat = jnp.where(grp[:, None] == grp[None, :], 1.0 / QK_SUB_DIM, 0.0).astype(BF16)
    g1, g2 = g_norm1, g_norm2
    cw, cb = conv_lru_w[0], conv_lru_b
    cfw, cfb = conv_ffn_w[0], conv_ffn_b
    lam = lru_lambda
    lams = (lam_q1, lam_k1, lam_q2, lam_k2)
    gs = g_subln

    assert nb % SUBLANES == 0 and bsz <= SUBLANES
    c_all = jnp.concatenate([c_sample, c_prompt, jnp.zeros((SUBLANES - bsz, D_MODEL), F32)], axis=0)
    mod = _ada(c_all, w_ada[0], b_ada)

    tabs = _rope_tables(s_len)
    pad_rows = SUBLANES - nt
    tabs_s = tuple(t[past_len:past_len + SUBLANES] for t in tabs)

    q_p, kf_p, kb_p, vf_p, vb_p, lo_p, lc_p, lh_p = _inproj_prompt(
        x_prompt, mod, g1, w_in_b, gq, gk, gmat, tabs, cw, cb, wr, br, wi, bi, lam)
    x_tb = x_sample.transpose(1, 0, 2).reshape(nt * nb, D_MODEL)
    st_lru = state_lru_conv[0].transpose(1, 0, 2)
    q_s, k_s, v_s, lo_s, lc_s, lh_s = _inproj_sample(
        x_tb, mod, g1, w_in_b, gq, gk, gmat, tabs_s, cw, cb, wr, br, wi, bi, lam, st_lru, state_lru_h[0])
    to_bt = lambda a: a.reshape(nt, nb, -1).transpose(1, 0, 2)
    pad_t = lambda a: jnp.pad(a, ((0, 0), (0, pad_rows), (0, 0)))
    k_bt, v_bt = to_bt(k_s), to_bt(v_s)

    score_bound = QK_NORM_BOUND ** 2 * QK_SCALE_LOG2 * jnp.max(jnp.abs(g_q)) * jnp.max(jnp.abs(g_k))
    cache_kt = cache_k[0].transpose(0, 2, 3, 4, 1).reshape(-1, ATT_WIDTH, page)
    cache_vr = cache_v[0].reshape(-1, page * ATT_HEADS, V_HEAD_DIM)
    at_p, at_s = lax.cond(score_bound <= SCORE_LOG2_LIMIT,
                          functools.partial(_attn, True), functools.partial(_attn, False),
                          page_table, q_p, kb_p, vb_p, pad_t(to_bt(q_s)), pad_t(k_bt), pad_t(v_bt),
                          cache_kt, cache_vr, *lams, gs)

    y_p, fc_p = _ffn_prompt(x_prompt, at_p, lo_p, mod, g2, w_out_b, w_up_b, cfw, cfb, w_dn_b)
    at_tb = at_s[:, :nt].transpose(1, 0, 2).reshape(nt * nb, ATT_WIDTH)
    y_tb, fcg_s, fcv_s = _ffn_sample(x_tb, at_tb, lo_s, mod, g2, w_out_b, w_up_b, cfw, cfb, w_dn_b,
                                     state_ffn_conv[0].transpose(1, 0, 2))

    hd = (ATT_HEADS, 2, QK_SUB_DIM)
    return (y_p, to_bt(y_tb),
            kf_p.reshape(bsz, *hd, s_len).transpose(0, 4, 1, 2, 3)[None],
            vf_p.reshape(1, bsz, s_len, ATT_HEADS, V_HEAD_DIM),
            lc_p[None], lh_p.reshape(1, bsz, LRU_WIDTH), fc_p[None],
            k_bt.reshape(1, nb, nt, *hd), v_bt.reshape(1, nb, nt, ATT_HEADS, V_HEAD_DIM),
            lc_s.transpose(1, 0, 2)[None], lh_s[None],
            jnp.concatenate([fcg_s, fcv_s], axis=-1).transpose(1, 0, 2)[None])
```

```python
import functools
import math

import jax
import jax.numpy as jnp
from jax import lax
from jax.experimental import pallas as pl
from jax.experimental.pallas import tpu as pltpu

F32 = jnp.float32
BF16 = jnp.bfloat16

D_MODEL = 1024
ATT_WIDTH = 512
LRU_WIDTH = 512
ATT_HEADS = 4
V_HEAD_DIM = 128
QK_SUB_DIM = 64
ROPE_DIM = 16
ROPE_THETA = 500000.0
LRU_BLOCKS = 8
LRU_C = 8.0
LRU_CONV = 4
FFN_CONV = 3
EPS = 1e-6
LAM_INIT = 0.8 - 0.6 * math.exp(-0.3 * 0)
QK_SCALE = QK_SUB_DIM ** -0.5
QK_SCALE_LOG2 = QK_SCALE * math.log2(math.e)
SCORE_LOG2_LIMIT = 64.0
QK_NORM_BOUND = 1.01 * math.sqrt(QK_SUB_DIM)

LANES = 128
SUBLANES = 8
VMEM_LIMIT = 56 * 1024 * 1024

TM_IN = 1024
TQ = 512
TK = 512
TM_FFN = 512
FFN_SPLIT = 2
FFN_SLOT = 2
FF_CHUNK = 256
ADA_TN = 1024
ROPE_TM = 512
NEG = -1e30


def _dot(a, b):
    return jnp.dot(a, b, preferred_element_type=F32)


def _dot_nt(a, b):
    return lax.dot_general(a, b, (((1,), (1,)), ((), ())), preferred_element_type=F32)


def _const_spec(shape):
    nd = len(shape)
    return pl.BlockSpec(shape, lambda *_: (0,) * nd, pipeline_mode=pl.Buffered(1))


def _prompt_mod_spec(mod, k):
    return pl.BlockSpec((SUBLANES, D_MODEL), lambda b, i: (mod.shape[0] // SUBLANES - 1, k))


def _seq_row(ref):
    return ref[pl.ds(pl.program_id(0), 1), :]


def _modulated_norm(x, g, sc, sh):
    xn = x * lax.rsqrt(jnp.mean(x * x, axis=-1, keepdims=True) + EPS) * g
    return xn * (1.0 + sc) + sh


def _to_segment_major(val, buf_ref):
    tm, w = val.shape
    n = tm // SUBLANES
    for c in range(w // LANES):
        for s in range(SUBLANES):
            for r0 in range(0, n, SUBLANES):
                t0 = s * n + r0
                buf_ref[c, pl.ds(r0 * SUBLANES + s, SUBLANES, stride=SUBLANES), :] = (
                    val[t0:t0 + SUBLANES, c * LANES:(c + 1) * LANES])
    return jnp.concatenate([buf_ref[c] for c in range(w // LANES)], axis=1)


def _to_time_major(val, buf_ref):
    tm, w = val.shape
    n = tm // SUBLANES
    for c in range(w // LANES):
        buf_ref[c] = val[:, c * LANES:(c + 1) * LANES]
    rows = []
    for s in range(SUBLANES):
        for r0 in range(0, n, SUBLANES):
            rows.append(jnp.concatenate(
                [buf_ref[c, pl.ds(r0 * SUBLANES + s, SUBLANES, stride=SUBLANES), :] for c in range(w // LANES)], axis=1))
    return jnp.concatenate(rows, axis=0)


def _tail_groups(cur, k):
    tm = cur.shape[0]
    return [cur[tm - (k - i) * SUBLANES:tm - (k - i - 1) * SUBLANES, :] for i in range(k)]


def _delayed(cur, prev_tail):
    k = len(prev_tail)
    tm = cur.shape[0]
    first = lax.broadcasted_iota(jnp.int32, prev_tail[0].shape, 0) == 0
    heads = [jnp.where(first, pltpu.roll(p, 1, 0), pltpu.roll(c, 1, 0))
             for p, c in zip(prev_tail, _tail_groups(cur, k))]
    return [jnp.concatenate(heads[k - d:] + [cur[:tm - d * SUBLANES, :]], axis=0) for d in range(k, 0, -1)]


def _group_norm_rope(t, g_tiled, gmat, cos, sin_a, sin_b):
    ms = _dot((t * t).astype(BF16), gmat)
    tn = t * lax.rsqrt(ms + EPS) * g_tiled
    outs = []
    for h in range(ATT_WIDTH // LANES):
        th = tn[:, h * LANES:(h + 1) * LANES]
        outs.append(th * cos + pltpu.roll(th, LANES - 8, 1) * sin_a + pltpu.roll(th, 8, 1) * sin_b)
    return jnp.concatenate(outs, axis=1)


def _lru_gates(xc, wr_ref, br, wi_ref, bi, neg_c_softplus):
    half = LRU_WIDTH // 2
    xb = xc.astype(BF16)
    lo, hi = xb[:, :half], xb[:, half:]
    r = jax.nn.sigmoid(jnp.concatenate([_dot(lo, wr_ref[0]), _dot(hi, wr_ref[1])], axis=1) + br)
    ig = jax.nn.sigmoid(jnp.concatenate([_dot(lo, wi_ref[0]), _dot(hi, wi_ref[1])], axis=1) + bi)
    log_a = neg_c_softplus * r
    a = jnp.exp(log_a)
    one_minus_a2 = -jnp.tanh(log_a) * (a * a + 1.0)
    return a, jnp.sqrt(one_minus_a2) * (ig * xc)


def _neg_c_softplus(lam):
    z = -lam
    return -LRU_C * (jnp.maximum(z, 0.0) + jnp.log1p(jnp.exp(-jnp.abs(z))))


def _diff_lambda(lq1, lk1, lq2, lk2):
    s1 = jnp.sum(lq1 * lk1, axis=-1, keepdims=True)
    s2 = jnp.sum(lq2 * lk2, axis=-1, keepdims=True)
    return jnp.exp(s1) - jnp.exp(s2) + LAM_INIT


def _subln(o, g):
    return o * lax.rsqrt(jnp.mean(o * o, axis=-1, keepdims=True) + EPS) * g * (1.0 - LAM_INIT)


def _ada_kernel(c_ref, w_ref, b_ref, o_ref):
    c = c_ref[...]
    s = (c * jax.nn.sigmoid(c)).astype(BF16)
    o_ref[...] = _dot(s, w_ref[...].astype(BF16)) + b_ref[...]


def _ada(c_all, w_ada, b_ada):
    m = c_all.shape[0]
    n = w_ada.shape[1]
    tn = ADA_TN
    return pl.pallas_call(
        _ada_kernel,
        grid=(n // tn,),
        in_specs=[pl.BlockSpec((m, D_MODEL), lambda j: (0, 0)),
                  pl.BlockSpec((D_MODEL, tn), lambda j: (0, j)),
                  pl.BlockSpec((1, tn), lambda j: (0, j))],
        out_specs=pl.BlockSpec((m, tn), lambda j: (0, j)),
        out_shape=jax.ShapeDtypeStruct((m, n), F32),
        compiler_params=pltpu.CompilerParams(dimension_semantics=("arbitrary",), vmem_limit_bytes=VMEM_LIMIT),
        name="ada_mod",
    )(c_all, w_ada, b_ada)


def _rope_table_kernel(freq_ref, ma_ref, mb_ref, c_ref, sa_ref, sb_ref):
    tm = c_ref.shape[0]
    pos = (pl.program_id(0) * tm + lax.broadcasted_iota(jnp.int32, (tm, LANES), 0)).astype(F32)
    ang = pos * freq_ref[...]
    s = jnp.sin(ang)
    c_ref[...] = jnp.cos(ang)
    sa_ref[...] = -s * ma_ref[...]
    sb_ref[...] = s * mb_ref[...]


def _rope_tables(n_pos):
    half = ROPE_DIM // 2
    d = jnp.arange(LANES) % QK_SUB_DIM
    freqs = ROPE_THETA ** (-(d % half).astype(F32) * 2.0 / ROPE_DIM)
    freq_lane = jnp.where(d < ROPE_DIM, freqs, 0.0).astype(F32)[None, :]
    mask_a = (d < half).astype(F32)[None, :]
    mask_b = ((d >= half) & (d < ROPE_DIM)).astype(F32)[None, :]
    tm = ROPE_TM
    row = pl.BlockSpec((1, LANES), lambda i: (0, 0))
    tab = pl.BlockSpec((tm, LANES), lambda i: (i, 0))
    shp = jax.ShapeDtypeStruct((n_pos, LANES), F32)
    return pl.pallas_call(
        _rope_table_kernel,
        grid=(n_pos // tm,),
        in_specs=[row, row, row],
        out_specs=[tab, tab, tab],
        out_shape=[shp, shp, shp],
        compiler_params=pltpu.CompilerParams(dimension_semantics=("arbitrary",)),
        name="rope_tables",
    )(freq_lane, mask_a, mask_b)


def _inproj_prompt_kernel(x_ref, sh_ref, sc_ref, g1_ref, win_ref, gq_ref, gk_ref, gmat_ref,
                          cos_ref, sa_ref, sb_ref, cw_ref, cb_ref, wr_ref, br_ref, wi_ref, bi_ref, lam_ref,
                          q_ref, kf_ref, kb_ref, vf_ref, vb_ref, lo_ref, lc_ref, lh_ref,
                          lbuf_ref, hbuf_ref, tail_ref, hcar_ref):
    tm = x_ref.shape[0]

    @pl.when(pl.program_id(1) == 0)
    def _():
        tail_ref[...] = jnp.zeros(tail_ref.shape, F32)
        hcar_ref[...] = jnp.zeros(hcar_ref.shape, F32)

    u = _modulated_norm(x_ref[...], g1_ref[...], _seq_row(sc_ref), _seq_row(sh_ref)).astype(BF16)
    cos, sa, sb = cos_ref[...], sa_ref[...], sb_ref[...]
    gmat = gmat_ref[...]

    q = _group_norm_rope(_dot(u, win_ref[:, 0:ATT_WIDTH]), gq_ref[...], gmat, cos, sa, sb)
    q_ref[...] = (q * QK_SCALE_LOG2).T.astype(BF16)
    k = _group_norm_rope(_dot(u, win_ref[:, ATT_WIDTH:2 * ATT_WIDTH]), gk_ref[...], gmat, cos, sa, sb)
    kf_ref[...] = k.T
    kb_ref[...] = k.astype(BF16)
    v = _dot(u, win_ref[:, 2 * ATT_WIDTH:3 * ATT_WIDTH])
    for h in range(ATT_HEADS):
        vf_ref[pl.ds(h, tm, stride=ATT_HEADS), :] = v[:, h * V_HEAD_DIM:(h + 1) * V_HEAD_DIM]
    vb_ref[...] = v.T.astype(BF16)

    c0 = 3 * ATT_WIDTH
    n_tail = LRU_CONV - 1
    lx = _to_segment_major(_dot(u, win_ref[:, c0:c0 + LRU_WIDTH]), lbuf_ref)
    cw = cw_ref[...]
    d3, d2, d1 = _delayed(lx, [tail_ref[i] for i in range(n_tail)])
    xc = d3 * cw[0:1] + d2 * cw[1:2] + d1 * cw[2:3] + lx * cw[3:4] + cb_ref[...]
    for i, grp in enumerate(_tail_groups(lx, n_tail)):
        tail_ref[i] = grp
        lc_ref[i:i + 1, :] = grp[SUBLANES - 1:SUBLANES, :]

    a, gx = _lru_gates(xc, wr_ref, br_ref[...], wi_ref, bi_ref[...], _neg_c_softplus(lam_ref[...]))

    n = tm // SUBLANES
    grp = lambda arr, r: arr[r * SUBLANES:(r + 1) * SUBLANES, :]
    ps, hs = [grp(a, 0)], [grp(gx, 0)]
    for r in range(1, n):
        ar = grp(a, r)
        ps.append(ar * ps[-1])
        hs.append(ar * hs[-1] + grp(gx, r))
    p_end, h_end = ps[-1], hs[-1]
    entering = [hcar_ref[...]]
    for s in range(1, SUBLANES):
        entering.append(p_end[s - 1:s, :] * entering[-1] + h_end[s - 1:s, :])
    h_last = p_end[SUBLANES - 1:SUBLANES, :] * entering[-1] + h_end[SUBLANES - 1:SUBLANES, :]
    hcar_ref[...] = h_last
    lh_ref[...] = h_last
    enter = jnp.concatenate(entering, axis=0)
    states = jnp.concatenate([hs[r] + ps[r] * enter for r in range(n)], axis=0)

    lg = _dot(u, win_ref[:, c0 + LRU_WIDTH:c0 + 2 * LRU_WIDTH])
    lo_ref[...] = (_to_time_major(states, hbuf_ref) * jax.nn.gelu(lg, approximate=True)).astype(BF16)


def _inproj_prompt(x, mod, g1, w_in, gq, gk, gmat, tabs, cw, cb, wr, br, wi, bi, lam):
    bsz, s_len, _ = x.shape
    tm = TM_IN
    row_spec = lambda w: pl.BlockSpec((None, tm, w), lambda b, i: (b, i, 0))
    tab_spec = pl.BlockSpec((tm, LANES), lambda b, i: (i, 0))
    mod_spec = lambda k: _prompt_mod_spec(mod, k)
    act = lambda dt: jax.ShapeDtypeStruct((bsz, s_len, ATT_WIDTH), dt)
    act_t = lambda dt: jax.ShapeDtypeStruct((bsz, ATT_WIDTH, s_len), dt)
    col_spec = pl.BlockSpec((None, ATT_WIDTH, tm), lambda b, i: (b, 0, i))
    return pl.pallas_call(
        _inproj_prompt_kernel,
        grid=(bsz, s_len // tm),
        in_specs=[row_spec(D_MODEL), mod_spec(0), mod_spec(1), _const_spec(g1.shape), _const_spec(w_in.shape),
                  _const_spec(gq.shape), _const_spec(gk.shape), _const_spec(gmat.shape),
                  tab_spec, tab_spec, tab_spec,
                  _const_spec(cw.shape), _const_spec(cb.shape), _const_spec(wr.shape), _const_spec(br.shape),
                  _const_spec(wi.shape), _const_spec(bi.shape), _const_spec(lam.shape)],
        out_specs=[col_spec, col_spec, row_spec(ATT_WIDTH),
                   pl.BlockSpec((None, tm * ATT_HEADS, V_HEAD_DIM), lambda b, i: (b, i, 0)),
                   col_spec, row_spec(LRU_WIDTH),
                   pl.BlockSpec((None, LRU_CONV - 1, LRU_WIDTH), lambda b, i: (b, 0, 0)),
                   pl.BlockSpec((None, 1, LRU_WIDTH), lambda b, i: (b, 0, 0))],
        out_shape=[act_t(BF16), act_t(F32), act(BF16),
                   jax.ShapeDtypeStruct((bsz, s_len * ATT_HEADS, V_HEAD_DIM), F32), act_t(BF16), act(BF16),
                   jax.ShapeDtypeStruct((bsz, LRU_CONV - 1, LRU_WIDTH), F32),
                   jax.ShapeDtypeStruct((bsz, 1, LRU_WIDTH), F32)],
        scratch_shapes=[pltpu.VMEM((LRU_WIDTH // LANES, tm, LANES), F32),
                        pltpu.VMEM((LRU_WIDTH // LANES, tm, LANES), F32),
                        pltpu.VMEM((LRU_CONV - 1, SUBLANES, LRU_WIDTH), F32),
                        pltpu.VMEM((1, LRU_WIDTH), F32)],
        compiler_params=pltpu.CompilerParams(dimension_semantics=("arbitrary", "arbitrary"),
                                             vmem_limit_bytes=VMEM_LIMIT),
        name="inproj_prompt",
    )(x, mod, mod, g1, w_in, gq, gk, gmat, *tabs, cw, cb, wr, br, wi, bi, lam)


def _alternate(*step_generators):
    live = list(step_generators)
    while live:
        for g in list(live):
            try:
                next(g)
            except StopIteration:
                live.remove(g)


def _query_tile(i, n):
    return jnp.where(i % 2 == 0, i // 2, n - 1 - i // 2)


def _page_copies(pt_ref, ckt_hbm, cv_hbm, kbuf_ref, vbuf_ref, sem_ref, seq, slot):
    copies = []
    for j in range(kbuf_ref.shape[1]):
        pg = pt_ref[seq, j]
        copies.append(pltpu.make_async_copy(ckt_hbm.at[pg], kbuf_ref.at[slot, j], sem_ref.at[slot]))
        copies.append(pltpu.make_async_copy(cv_hbm.at[pg], vbuf_ref.at[slot, j], sem_ref.at[slot]))
    return copies


def _attn_kernel(bounded, pt_ref, qt_ref, k_ref, vt_ref, lq1_ref, lk1_ref, lq2_ref, lk2_ref, gs_ref,
                 qs_ref, kn_ref, vn_ref, ckt_hbm, cv_hbm, o_ref, os_ref, kbuf_ref, vbuf_ref, sem_ref):
    n_pages = kbuf_ref.shape[1]
    step = (pl.program_id(0) * pl.num_programs(1) + pl.program_id(1)) * pl.num_programs(2) + pl.program_id(2)
    n_steps = pl.num_programs(0) * pl.num_programs(1) * pl.num_programs(2)
    slot = step % 2

    @pl.when(step == 0)
    def _():
        for cp in _page_copies(pt_ref, ckt_hbm, cv_hbm, kbuf_ref, vbuf_ref, sem_ref, step, slot):
            cp.start()

    @pl.when(step + 1 < n_steps)
    def _():
        for cp in _page_copies(pt_ref, ckt_hbm, cv_hbm, kbuf_ref, vbuf_ref, sem_ref, step + 1, 1 - slot):
            cp.start()

    tq = qt_ref.shape[1]
    qi = _query_tile(pl.program_id(2), pl.num_programs(2))
    qt = qt_ref[...].astype(F32)
    feat = lax.broadcasted_iota(jnp.int32, (V_HEAD_DIM, tq), 0)
    qc = (jnp.where(feat < QK_SUB_DIM, qt, 0.0).astype(BF16), jnp.where(feat >= QK_SUB_DIM, qt, 0.0).astype(BF16))

    def tile_steps(k0, width, carry, masked, out):
        k0 = pl.multiple_of(k0, TK)
        ks = k_ref[pl.ds(k0, width), :]
        vts = vt_ref[:, pl.ds(k0, width)]
        ss = []
        for c in range(2):
            ss.append(_dot(ks, qc[c]))
            yield
        for c in range(2):
            m, l, acc = carry[c]
            s = ss[c]
            if masked:
                keys = lax.broadcasted_iota(jnp.int32, (width, tq), 0)
                qs = lax.broadcasted_iota(jnp.int32, (width, tq), 1)
                s = jnp.where(keys <= qs, s, NEG)
            if bounded:
                p = jnp.exp2(s)
                l = l + jnp.sum(p, axis=0, keepdims=True)
                acc = acc + _dot(vts, p.astype(BF16))
            else:
                m_new = jnp.maximum(m, jnp.max(s, axis=0, keepdims=True))
                alpha = jnp.exp2(m - m_new)
                p = jnp.exp2(s - m_new)
                l = alpha * l + jnp.sum(p, axis=0, keepdims=True)
                acc = alpha * acc + _dot(vts, p.astype(BF16))
                m = m_new
            out.append((m, l, acc))
            yield

    def tile(k0, width, carry, masked):
        out = []
        for _ in tile_steps(k0, width, carry, masked, out):
            pass
        return tuple(out)

    init = tuple((jnp.full((1, tq), NEG, F32), jnp.zeros((1, tq), F32), jnp.zeros((V_HEAD_DIM, tq), F32))
                 for _ in range(2))
    carry = lax.fori_loop(0, qi // 2, lambda j, c: tile(j * 2 * TK, 2 * TK, c, False), init)
    carry = lax.cond(qi % 2 == 1, lambda c: tile((qi - 1) * TK, TK, c, False), lambda c: c, carry)
    for cp in _page_copies(pt_ref, ckt_hbm, cv_hbm, kbuf_ref, vbuf_ref, sem_ref, step, slot):
        cp.wait()
    kt_refs = [kbuf_ref.at[slot, j] for j in range(n_pages)]
    v_refs = [vbuf_ref.at[slot, j] for j in range(n_pages)]
    lam = _diff_lambda(lq1_ref[...], lk1_ref[...], lq2_ref[...], lk2_ref[...])
    tile_out, sample_out = [], []
    _alternate(tile_steps(qi * TK, TK, carry, True, tile_out),
               _sample_attention_steps(bounded, qs_ref, kn_ref, vn_ref, lam, gs_ref[...], kt_refs, v_refs, sample_out))
    (_, l0, a0), (_, l1, a1) = tile_out
    ot = a0 / l0 - lam * (a1 / l1)
    o_ref[...] = _subln(ot.T, gs_ref[...]).astype(o_ref.dtype)
    os_ref[...] = sample_out[0]


def _attn(bounded, page_table, qt, k, vt, q8, kn8, vn8, cache_kt, cache_v, lq1, lk1, lq2, lk2, gs):
    bsz, s_len, _ = k.shape
    nb, n_pages = page_table.shape
    nq = q8.shape[1]
    page = cache_kt.shape[2]
    n_qt = s_len // TQ
    assert nb == bsz * ATT_HEADS * n_qt, "one sample sequence per grid step"
    seq = lambda b, h, i: (b * ATT_HEADS + h) * n_qt + i
    qt_spec = pl.BlockSpec((None, V_HEAD_DIM, TQ), lambda b, h, i, pt: (b, h, _query_tile(i, n_qt)))
    k_spec = pl.BlockSpec((None, s_len, V_HEAD_DIM), lambda b, h, i, pt: (b, 0, h))
    vt_spec = pl.BlockSpec((None, V_HEAD_DIM, s_len), lambda b, h, i, pt: (b, h, 0))
    o_spec = pl.BlockSpec((None, TQ, V_HEAD_DIM), lambda b, h, i, pt: (b, _query_tile(i, n_qt), h))
    small = lambda a: pl.BlockSpec(a.shape, lambda b, h, i, pt: (0, 0))
    new_spec = pl.BlockSpec((None, nq, ATT_WIDTH), lambda b, h, i, pt: (seq(b, h, i), 0, 0))
    hbm = pl.BlockSpec(memory_space=pl.ANY)
    grid_spec = pltpu.PrefetchScalarGridSpec(
        num_scalar_prefetch=1,
        grid=(bsz, ATT_HEADS, n_qt),
        in_specs=[qt_spec, k_spec, vt_spec, small(lq1), small(lk1), small(lq2), small(lk2), small(gs),
                  new_spec, new_spec, new_spec, hbm, hbm],
        out_specs=[o_spec, new_spec],
        scratch_shapes=[pltpu.VMEM((2, n_pages, ATT_WIDTH, page), F32),
                        pltpu.VMEM((2, n_pages, page * ATT_HEADS, V_HEAD_DIM), F32),
                        pltpu.SemaphoreType.DMA((2,))],
    )
    return pl.pallas_call(
        functools.partial(_attn_kernel, bounded),
        grid_spec=grid_spec,
        out_shape=[jax.ShapeDtypeStruct((bsz, s_len, ATT_WIDTH), BF16),
                   jax.ShapeDtypeStruct((nb, nq, ATT_WIDTH), F32)],
        compiler_params=pltpu.CompilerParams(dimension_semantics=("arbitrary", "arbitrary", "arbitrary"),
                                             vmem_limit_bytes=VMEM_LIMIT),
        name="attn_bounded" if bounded else "attn_general",
    )(page_table, qt, k, vt, lq1, lk1, lq2, lk2, gs, q8, kn8, vn8, cache_kt, cache_v)


def _ffn_prompt_kernel(x_ref, at_ref, lo_ref, gt1_ref, sh2_ref, sc2_ref, gt2_ref, g2_ref,
                       wout_ref, wup_ref, cfw_ref, cfb_ref, wdn_ref,
                       y_ref, fc_ref, ubuf_ref, abuf_ref, tail_ref):
    tm = x_ref.shape[0]
    sub = tm // FFN_SPLIT
    d_ff = wdn_ref.shape[0]
    n_tail = FFN_CONV - 1
    n_chunks = d_ff // FF_CHUNK
    chunk_cols = lambda j: [slice(base + j * FF_CHUNK, base + (j + 1) * FF_CHUNK) for base in (0, d_ff)]

    @pl.when(pl.program_id(1) == 0)
    def _():
        tail_ref[...] = jnp.zeros(tail_ref.shape, F32)

    def attention_residual(g):
        rows = slice(g * sub, (g + 1) * sub)
        mix = jnp.concatenate([at_ref[rows, :], lo_ref[rows, :]], axis=1)
        x1 = x_ref[rows, :] + _seq_row(gt1_ref) * _dot(mix, wout_ref[...])
        u2 = _to_segment_major(_modulated_norm(x1, g2_ref[...], _seq_row(sc2_ref), _seq_row(sh2_ref)),
                               ubuf_ref.at[g]).astype(BF16)
        return x1, u2

    def finish(g, x1, acc):
        y_ref[g * sub:(g + 1) * sub, :] = x1 + _seq_row(gt2_ref) * _to_time_major(acc, abuf_ref.at[g])

    def hidden(j, ups):
        halves = []
        for cols, up in zip(chunk_cols(j), ups):
            w = cfw_ref[:, cols]
            d2, d1 = _delayed(up, [tail_ref[i, :, cols] for i in range(n_tail)])
            halves.append(d2 * w[0:1] + d1 * w[1:2] + up * w[2:3] + cfb_ref[:, cols])
            for i, grp in enumerate(_tail_groups(up, n_tail)):
                tail_ref[i, :, cols] = grp
                fc_ref[i:i + 1, cols] = grp[SUBLANES - 1:SUBLANES, :]
        g, val = halves
        return (g * jax.nn.sigmoid(g) * val).astype(BF16)

    state = attention_residual(0)
    done = None
    for g in range(FFN_SPLIT):
        x1, u2 = state
        up_project = lambda j: [_dot(u2, wup_ref[:, cols]) for cols in chunk_cols(j)]
        acc = jnp.zeros((sub, D_MODEL), F32)
        ups = up_project(0)
        for j in range(n_chunks):
            ups_next = up_project(j + 1) if j + 1 < n_chunks else None
            acc = acc + _dot(hidden(j, ups), wdn_ref[j * FF_CHUNK:(j + 1) * FF_CHUNK, :])
            ups = ups_next
            if j == FFN_SLOT and done is not None:
                finish(*done)
                done = None
            if j == FFN_SLOT and g + 1 < FFN_SPLIT:
                state = attention_residual(g + 1)
        done = (g, x1, acc)
    finish(*done)


def _ffn_prompt(x, attn, lru, mod, g2, w_out, w_up, cfw, cfb, w_dn):
    bsz, s_len, _ = x.shape
    tm = TM_FFN
    d_ff = w_dn.shape[0]
    row_spec = lambda w: pl.BlockSpec((None, tm, w), lambda b, i: (b, i, 0))
    mod_spec = lambda k: _prompt_mod_spec(mod, k)
    return pl.pallas_call(
        _ffn_prompt_kernel,
        grid=(bsz, s_len // tm),
        in_specs=[row_spec(D_MODEL), row_spec(ATT_WIDTH), row_spec(LRU_WIDTH),
                  mod_spec(2), mod_spec(3), mod_spec(4), mod_spec(5), _const_spec(g2.shape),
                  _const_spec(w_out.shape), _const_spec(w_up.shape), _const_spec(cfw.shape),
                  _const_spec(cfb.shape), _const_spec(w_dn.shape)],
        out_specs=[row_spec(D_MODEL), pl.BlockSpec((None, FFN_CONV - 1, 2 * d_ff), lambda b, i: (b, 0, 0))],
        out_shape=[jax.ShapeDtypeStruct((bsz, s_len, D_MODEL), F32),
                   jax.ShapeDtypeStruct((bsz, FFN_CONV - 1, 2 * d_ff), F32)],
        scratch_shapes=[pltpu.VMEM((FFN_SPLIT, D_MODEL // LANES, tm // FFN_SPLIT, LANES), F32),
                        pltpu.VMEM((FFN_SPLIT, D_MODEL // LANES, tm // FFN_SPLIT, LANES), F32),
                        pltpu.VMEM((FFN_CONV - 1, SUBLANES, 2 * d_ff), F32)],
        compiler_params=pltpu.CompilerParams(dimension_semantics=("arbitrary", "arbitrary"),
                                             vmem_limit_bytes=VMEM_LIMIT),
        name="ffn_prompt",
    )(x, attn, lru, mod, mod, mod, mod, g2, w_out, w_up, cfw, cfb, w_dn)


def _inproj_sample_kernel(x_ref, mod_ref, g1_ref, win_ref, gq_ref, gk_ref, gmat_ref,
                          cos_ref, sa_ref, sb_ref, cw_ref, cb_ref, wr_ref, br_ref, wi_ref, bi_ref, lam_ref,
                          st_ref, h0_ref,
                          q_ref, k_ref, v_ref, lo_ref, lc_ref, lh_ref):
    nb = h0_ref.shape[0]
    nt = x_ref.shape[0] // nb
    rep = lambda a: jnp.concatenate([a] * nt, axis=0)
    sh1 = rep(mod_ref[0:nb, 0:D_MODEL])
    sc1 = rep(mod_ref[0:nb, D_MODEL:2 * D_MODEL])
    u = _modulated_norm(x_ref[...], g1_ref[...], sc1, sh1).astype(BF16)
    per_t = lambda r: jnp.concatenate([jnp.broadcast_to(r[t:t + 1, :], (nb, LANES)) for t in range(nt)], axis=0)
    cos, sa, sb = per_t(cos_ref[...]), per_t(sa_ref[...]), per_t(sb_ref[...])
    gmat = gmat_ref[...]

    q = _group_norm_rope(_dot(u, win_ref[:, 0:ATT_WIDTH]), gq_ref[...], gmat, cos, sa, sb)
    q_ref[...] = q * QK_SCALE_LOG2
    k_ref[...] = _group_norm_rope(_dot(u, win_ref[:, ATT_WIDTH:2 * ATT_WIDTH]), gk_ref[...], gmat, cos, sa, sb)
    v_ref[...] = _dot(u, win_ref[:, 2 * ATT_WIDTH:3 * ATT_WIDTH])

    c0 = 3 * ATT_WIDTH
    lx = _dot(u, win_ref[:, c0:c0 + LRU_WIDTH])
    lg = _dot(u, win_ref[:, c0 + LRU_WIDTH:c0 + 2 * LRU_WIDTH])
    pad = [st_ref[i] for i in range(LRU_CONV - 1)] + [lx[t * nb:(t + 1) * nb, :] for t in range(nt)]
    for i in range(LRU_CONV - 1):
        lc_ref[i] = pad[nt + i]
    cw = cw_ref[...]
    xcs = []
    for t in range(nt):
        xc = pad[t] * cw[0:1]
        for kk in range(1, LRU_CONV):
            xc = xc + pad[t + kk] * cw[kk:kk + 1]
        xcs.append(xc + cb_ref[...])
    a, gx = _lru_gates(jnp.concatenate(xcs, axis=0), wr_ref, br_ref[...], wi_ref, bi_ref[...],
                       _neg_c_softplus(lam_ref[...]))
    h = h0_ref[...]
    hs = []
    for t in range(nt):
        h = a[t * nb:(t + 1) * nb, :] * h + gx[t * nb:(t + 1) * nb, :]
        hs.append(h)
    lh_ref[...] = h
    lo_ref[...] = (jnp.concatenate(hs, axis=0) * jax.nn.gelu(lg, approximate=True)).astype(BF16)


def _inproj_sample(x_tb, mod_s, g1, w_in, gq, gk, gmat, tabs_s, cw, cb, wr, br, wi, bi, lam, st, h0):
    m = x_tb.shape[0]
    nb = h0.shape[0]
    act = jax.ShapeDtypeStruct((m, ATT_WIDTH), F32)
    return pl.pallas_call(
        _inproj_sample_kernel,
        out_shape=[act, act, act, jax.ShapeDtypeStruct((m, LRU_WIDTH), BF16),
                   jax.ShapeDtypeStruct((LRU_CONV - 1, nb, LRU_WIDTH), F32),
                   jax.ShapeDtypeStruct((nb, LRU_WIDTH), F32)],
        compiler_params=pltpu.CompilerParams(vmem_limit_bytes=VMEM_LIMIT),
        name="inproj_sample",
    )(x_tb, mod_s, g1, w_in, gq, gk, gmat, *tabs_s, cw, cb, wr, br, wi, bi, lam, st, h0)


def _sample_attention_steps(bounded, q_ref, kn_ref, vn_ref, lam, gs, kt_refs, v_refs, out):
    n_pages = len(kt_refs)
    page = kt_refs[0].shape[1]
    nq = q_ref.shape[0]
    hrows = 2 * nq

    lane = lax.broadcasted_iota(jnp.int32, (nq, V_HEAD_DIM), 1)
    qh = []
    for h in range(ATT_HEADS):
        q8 = q_ref[:, h * V_HEAD_DIM:(h + 1) * V_HEAD_DIM]
        qh.append(jnp.concatenate([jnp.where(lane < QK_SUB_DIM, q8, 0.0), jnp.where(lane >= QK_SUB_DIM, q8, 0.0)],
                                  axis=0).astype(BF16))

    zpad = jnp.zeros((page - nq, V_HEAD_DIM), F32)
    rt = lax.broadcasted_iota(jnp.int32, (hrows, page), 0) % nq
    ct = lax.broadcasted_iota(jnp.int32, (hrows, page), 1)
    ps, ls = [], []
    for h in range(ATT_HEADS):
        rows = slice(h * V_HEAD_DIM, (h + 1) * V_HEAD_DIM)
        kt_all = jnp.concatenate([kt_refs[j][rows, :] for j in range(n_pages)], axis=1).astype(BF16)
        k_new = jnp.concatenate([kn_ref[:, rows], zpad], axis=0).astype(BF16)
        s_new = jnp.where(ct <= rt, _dot_nt(qh[h], k_new), NEG)
        s = jnp.concatenate([_dot(qh[h], kt_all), s_new], axis=1)
        if not bounded:
            s = s - jnp.max(s, axis=1, keepdims=True)
        p = jnp.exp2(s)
        ls.append(jnp.sum(p, axis=1, keepdims=True))
        ps.append(p.astype(BF16))
        yield
    outs = []
    for h in range(0, ATT_HEADS, 2):
        v_pair = jnp.concatenate([jnp.concatenate(
            [v_refs[j][pl.ds(hh, page, stride=ATT_HEADS), :] for j in range(n_pages)]
            + [vn_ref[:, hh * V_HEAD_DIM:(hh + 1) * V_HEAD_DIM], zpad], axis=0) for hh in (h, h + 1)],
            axis=1).astype(BF16)
        o_pair = _dot(jnp.concatenate([ps[h], ps[h + 1]], axis=0), v_pair)
        for i, hh in enumerate((h, h + 1)):
            o = o_pair[i * hrows:(i + 1) * hrows, i * V_HEAD_DIM:(i + 1) * V_HEAD_DIM] / ls[hh]
            outs.append(_subln(o[0:nq, :] - lam * o[nq:hrows, :], gs))
        yield
    out.append(jnp.concatenate(outs, axis=1))


def _ffn_sample_kernel(x_ref, at_ref, lo_ref, mod_ref, g2_ref, wout_ref, wupg_ref, wupv_ref, cfwg_ref, cfwv_ref,
                       cfbg_ref, cfbv_ref, wdn_ref, stg_ref, stv_ref,
                       y_ref, fcg_ref, fcv_ref, x1_ref, u2_ref, acc_ref):
    j = pl.program_id(0)
    nb = stg_ref.shape[1]
    nt = x_ref.shape[0] // nb
    rep = lambda a: jnp.concatenate([a] * nt, axis=0)

    @pl.when(j == 0)
    def _():
        gt1 = rep(mod_ref[0:nb, 2 * D_MODEL:3 * D_MODEL])
        sh2 = rep(mod_ref[0:nb, 3 * D_MODEL:4 * D_MODEL])
        sc2 = rep(mod_ref[0:nb, 4 * D_MODEL:5 * D_MODEL])
        mix = jnp.concatenate([at_ref[...].astype(BF16), lo_ref[...]], axis=1)
        x1 = x_ref[...] + gt1 * _dot(mix, wout_ref[...])
        x1_ref[...] = x1
        u2_ref[...] = _modulated_norm(x1, g2_ref[...], sc2, sh2).astype(BF16)
        acc_ref[...] = jnp.zeros(acc_ref.shape, F32)

    u2 = u2_ref[...]
    halves = []
    for wup_ref, cfw_ref, cfb_ref, st_ref, fc_ref in ((wupg_ref, cfwg_ref, cfbg_ref, stg_ref, fcg_ref),
                                                      (wupv_ref, cfwv_ref, cfbv_ref, stv_ref, fcv_ref)):
        up = _dot(u2, wup_ref[...])
        pad = [st_ref[i] for i in range(FFN_CONV - 1)] + [up[t * nb:(t + 1) * nb, :] for t in range(nt)]
        for i in range(FFN_CONV - 1):
            fc_ref[i] = pad[nt + i]
        w = cfw_ref[...]
        b = cfb_ref[...]
        hcs = []
        for t in range(nt):
            hc = pad[t] * w[0:1]
            for kk in range(1, FFN_CONV):
                hc = hc + pad[t + kk] * w[kk:kk + 1]
            hcs.append(hc + b)
        halves.append(jnp.concatenate(hcs, axis=0))
    g, val = halves
    hmid = (g * jax.nn.sigmoid(g) * val).astype(BF16)
    acc_ref[...] += _dot(hmid, wdn_ref[...])

    @pl.when(j == pl.num_programs(0) - 1)
    def _():
        gt2 = rep(mod_ref[0:nb, 5 * D_MODEL:6 * D_MODEL])
        y_ref[...] = x1_ref[...] + gt2 * acc_ref[...]


def _ffn_sample(x_tb, attn_tb, lru_tb, mod, g2, w_out, w_up, cfw, cfb, w_dn, st):
    m = x_tb.shape[0]
    d_ff = w_dn.shape[0]
    nb = st.shape[1]
    n_chunks = d_ff // FF_CHUNK
    whole = lambda a: pl.BlockSpec(a.shape, lambda j: (0,) * a.ndim)
    col = lambda rows, half: pl.BlockSpec((rows, FF_CHUNK), lambda j: (0, half * n_chunks + j))
    st_spec = lambda half: pl.BlockSpec((FFN_CONV - 1, nb, FF_CHUNK), lambda j: (0, 0, half * n_chunks + j))
    fc_spec = pl.BlockSpec((FFN_CONV - 1, nb, FF_CHUNK), lambda j: (0, 0, j))
    fc_shape = jax.ShapeDtypeStruct((FFN_CONV - 1, nb, d_ff), F32)
    return pl.pallas_call(
        _ffn_sample_kernel,
        grid=(n_chunks,),
        in_specs=[whole(x_tb), whole(attn_tb), whole(lru_tb), whole(mod), whole(g2), whole(w_out),
                  col(D_MODEL, 0), col(D_MODEL, 1), col(FFN_CONV, 0), col(FFN_CONV, 1), col(1, 0), col(1, 1),
                  pl.BlockSpec((FF_CHUNK, D_MODEL), lambda j: (j, 0)), st_spec(0), st_spec(1)],
        out_specs=[pl.BlockSpec((m, D_MODEL), lambda j: (0, 0)), fc_spec, fc_spec],
        out_shape=[jax.ShapeDtypeStruct((m, D_MODEL), F32), fc_shape, fc_shape],
        scratch_shapes=[pltpu.VMEM((m, D_MODEL), F32), pltpu.VMEM((m, D_MODEL), BF16),
                        pltpu.VMEM((m, D_MODEL), F32)],
        compiler_params=pltpu.CompilerParams(dimension_semantics=("arbitrary",), vmem_limit_bytes=VMEM_LIMIT),
        name="ffn_sample",
    )(x_tb, attn_tb, lru_tb, mod, g2, w_out, w_up, w_up, cfw, cfw, cfb, cfb, w_dn, st, st)


def _block_diag_halves(w):
    n, bd, _ = w.shape
    eye = jnp.eye(n // 2, dtype=w.dtype)
    halves = [jnp.einsum('nij,nm->nimj', w[s * (n // 2):(s + 1) * (n // 2)], eye).reshape(n // 2 * bd, n // 2 * bd)
              for s in range(2)]
    return jnp.stack(halves).astype(BF16)


def kernel(x_prompt, x_sample, cache_k, cache_v, page_table, state_lru_conv, state_lru_h, state_ffn_conv, c_prompt, c_sample, g_norm1, g_norm2, w_ada, b_ada, w_in, g_q, g_k, lam_q1, lam_k1, lam_q2, lam_k2, g_subln, w_out, conv_lru_w, conv_lru_b, w_rgate, b_rgate, w_igate, b_igate, lru_lambda, w_up, conv_ffn_w, conv_ffn_b, w_down):
    depth = w_in.shape[0]
    assert depth == 1, "single-layer step"
    bsz, s_len, _ = x_prompt.shape
    nb, nt, _ = x_sample.shape
    n_pages, page = page_table.shape[1], cache_k.shape[2]
    past_len = n_pages * page
    d_ff = w_down.shape[1]

    w_in_b = w_in[0].astype(BF16)
    w_out_b = w_out[0].astype(BF16)
    w_up_b = w_up[0].astype(BF16)
    w_dn_b = w_down[0].astype(BF16)
    wr = _block_diag_halves(w_rgate[0])
    wi = _block_diag_halves(w_igate[0])
    br = b_rgate[0].reshape(1, LRU_WIDTH)
    bi = b_igate[0].reshape(1, LRU_WIDTH)
    n_grp = ATT_WIDTH // QK_SUB_DIM
    gq = jnp.tile(g_q[0], n_grp)[None, :]
    gk = jnp.tile(g_k[0], n_grp)[None, :]
    grp = jnp.arange(ATT_WIDTH) // QK_SUB_DIM
    gmat = jnp.where(grp[:, None] == grp[None, :], 1.0 / QK_SUB_DIM, 0.0).astype(BF16)
    g1, g2 = g_norm1, g_norm2
    cw, cb = conv_lru_w[0], conv_lru_b
    cfw, cfb = conv_ffn_w[0], conv_ffn_b
    lam = lru_lambda
    lams = (lam_q1, lam_k1, lam_q2, lam_k2)
    gs = g_subln

    assert nb % SUBLANES == 0 and bsz <= SUBLANES
    c_all = jnp.concatenate([c_sample, c_prompt, jnp.zeros((SUBLANES - bsz, D_MODEL), F32)], axis=0)
    mod = _ada(c_all, w_ada[0], b_ada)

    tabs = _rope_tables(s_len)
    pad_rows = SUBLANES - nt
    tabs_s = tuple(t[past_len:past_len + SUBLANES] for t in tabs)

    q_p, kf_p, kb_p, vf_p, vb_p, lo_p, lc_p, lh_p = _inproj_prompt(
        x_prompt, mod, g1, w_in_b, gq, gk, gmat, tabs, cw, cb, wr, br, wi, bi, lam)
    x_tb = x_sample.transpose(1, 0, 2).reshape(nt * nb, D_MODEL)
    st_lru = state_lru_conv[0].transpose(1, 0, 2)
    q_s, k_s, v_s, lo_s, lc_s, lh_s = _inproj_sample(
        x_tb, mod, g1, w_in_b, gq, gk, gmat, tabs_s, cw, cb, wr, br, wi, bi, lam, st_lru, state_lru_h[0])
    to_bt = lambda a: a.reshape(nt, nb, -1).transpose(1, 0, 2)
    pad_t = lambda a: jnp.pad(a, ((0, 0), (0, pad_rows), (0, 0)))
    k_bt, v_bt = to_bt(k_s), to_bt(v_s)

    score_bound = QK_NORM_BOUND ** 2 * QK_SCALE_LOG2 * jnp.max(jnp.abs(g_q)) * jnp.max(jnp.abs(g_k))
    cache_kt = cache_k[0].transpose(0, 2, 3, 4, 1).reshape(-1, ATT_WIDTH, page)
    cache_vr = cache_v[0].reshape(-1, page * ATT_HEADS, V_HEAD_DIM)
    at_p, at_s = lax.cond(score_bound <= SCORE_LOG2_LIMIT,
                          functools.partial(_attn, True), functools.partial(_attn, False),
                          page_table, q_p, kb_p, vb_p, pad_t(to_bt(q_s)), pad_t(k_bt), pad_t(v_bt),
                          cache_kt, cache_vr, *lams, gs)

    y_p, fc_p = _ffn_prompt(x_prompt, at_p, lo_p, mod, g2, w_out_b, w_up_b, cfw, cfb, w_dn_b)
    at_tb = at_s[:, :nt].transpose(1, 0, 2).reshape(nt * nb, ATT_WIDTH)
    y_tb, fcg_s, fcv_s = _ffn_sample(x_tb, at_tb, lo_s, mod, g2, w_out_b, w_up_b, cfw, cfb, w_dn_b,
                                     state_ffn_conv[0].transpose(1, 0, 2))

    hd = (ATT_HEADS, 2, QK_SUB_DIM)
    return (y_p, to_bt(y_tb),
            kf_p.reshape(bsz, *hd, s_len).transpose(0, 4, 1, 2, 3)[None],
            vf_p.reshape(1, bsz, s_len, ATT_HEADS, V_HEAD_DIM),
            lc_p[None], lh_p.reshape(1, bsz, LRU_WIDTH), fc_p[None],
            k_bt.reshape(1, nb, nt, *hd), v_bt.reshape(1, nb, nt, ATT_HEADS, V_HEAD_DIM),
            lc_s.transpose(1, 0, 2)[None], lh_s[None],
            jnp.concatenate([fcg_s, fcv_s], axis=-1).transpose(1, 0, 2)[None])
```

```python
import functools
import math

import jax
import jax.numpy as jnp
from jax import lax
from jax.experimental import pallas as pl
from jax.experimental.pallas import tpu as pltpu

F32 = jnp.float32
BF16 = jnp.bfloat16

D_MODEL = 1024
ATT_WIDTH = 512
LRU_WIDTH = 512
ATT_HEADS = 4
V_HEAD_DIM = 128
QK_SUB_DIM = 64
ROPE_DIM = 16
ROPE_THETA = 500000.0
LRU_BLOCKS = 8
LRU_C = 8.0
LRU_CONV = 4
FFN_CONV = 3
EPS = 1e-6
LAM_INIT = 0.8 - 0.6 * math.exp(-0.3 * 0)
QK_SCALE = QK_SUB_DIM ** -0.5
QK_SCALE_LOG2 = QK_SCALE * math.log2(math.e)
SCORE_LOG2_LIMIT = 64.0
QK_NORM_BOUND = 1.01 * math.sqrt(QK_SUB_DIM)

LANES = 128
SUBLANES = 8
VMEM_LIMIT = 56 * 1024 * 1024

TM_IN = 1024
TQ = 512
TK = 512
TM_FFN = 512
FFN_SPLIT = 2
FFN_SLOT = 2
FF_CHUNK = 256
ADA_TN = 1024
ROPE_TM = 512
NEG = -1e30


def _dot(a, b):
    return jnp.dot(a, b, preferred_element_type=F32)


def _dot_nt(a, b):
    return lax.dot_general(a, b, (((1,), (1,)), ((), ())), preferred_element_type=F32)


def _const_spec(shape):
    nd = len(shape)
    return pl.BlockSpec(shape, lambda *_: (0,) * nd, pipeline_mode=pl.Buffered(1))


def _prompt_mod_spec(mod, k):
    return pl.BlockSpec((SUBLANES, D_MODEL), lambda b, i: (mod.shape[0] // SUBLANES - 1, k))


def _seq_row(ref):
    return ref[pl.ds(pl.program_id(0), 1), :]


def _modulated_norm(x, g, sc, sh):
    xn = x * lax.rsqrt(jnp.mean(x * x, axis=-1, keepdims=True) + EPS) * g
    return xn * (1.0 + sc) + sh


def _to_segment_major(val, buf_ref):
    tm, w = val.shape
    n = tm // SUBLANES
    for c in range(w // LANES):
        for s in range(SUBLANES):
            for r0 in range(0, n, SUBLANES):
                t0 = s * n + r0
                buf_ref[c, pl.ds(r0 * SUBLANES + s, SUBLANES, stride=SUBLANES), :] = (
                    val[t0:t0 + SUBLANES, c * LANES:(c + 1) * LANES])
    return jnp.concatenate([buf_ref[c] for c in range(w // LANES)], axis=1)


def _to_time_major(val, buf_ref):
    tm, w = val.shape
    n = tm // SUBLANES
    for c in range(w // LANES):
        buf_ref[c] = val[:, c * LANES:(c + 1) * LANES]
    rows = []
    for s in range(SUBLANES):
        for r0 in range(0, n, SUBLANES):
            rows.append(jnp.concatenate(
                [buf_ref[c, pl.ds(r0 * SUBLANES + s, SUBLANES, stride=SUBLANES), :] for c in range(w // LANES)], axis=1))
    return jnp.concatenate(rows, axis=0)


def _tail_groups(cur, k):
    tm = cur.shape[0]
    return [cur[tm - (k - i) * SUBLANES:tm - (k - i - 1) * SUBLANES, :] for i in range(k)]


def _delayed(cur, prev_tail):
    k = len(prev_tail)
    tm = cur.shape[0]
    first = lax.broadcasted_iota(jnp.int32, prev_tail[0].shape, 0) == 0
    heads = [jnp.where(first, pltpu.roll(p, 1, 0), pltpu.roll(c, 1, 0))
             for p, c in zip(prev_tail, _tail_groups(cur, k))]
    return [jnp.concatenate(heads[k - d:] + [cur[:tm - d * SUBLANES, :]], axis=0) for d in range(k, 0, -1)]


def _group_norm_rope(t, g_tiled, gmat, cos, sin_a, sin_b):
    ms = _dot((t * t).astype(BF16), gmat)
    tn = t * lax.rsqrt(ms + EPS) * g_tiled
    outs = []
    for h in range(ATT_WIDTH // LANES):
        th = tn[:, h * LANES:(h + 1) * LANES]
        outs.append(th * cos + pltpu.roll(th, LANES - 8, 1) * sin_a + pltpu.roll(th, 8, 1) * sin_b)
    return jnp.concatenate(outs, axis=1)


def _lru_gates(xc, wr_ref, br, wi_ref, bi, neg_c_softplus):
    half = LRU_WIDTH // 2
    xb = xc.astype(BF16)
    lo, hi = xb[:, :half], xb[:, half:]
    r = jax.nn.sigmoid(jnp.concatenate([_dot(lo, wr_ref[0]), _dot(hi, wr_ref[1])], axis=1) + br)
    ig = jax.nn.sigmoid(jnp.concatenate([_dot(lo, wi_ref[0]), _dot(hi, wi_ref[1])], axis=1) + bi)
    log_a = neg_c_softplus * r
    a = jnp.exp(log_a)
    one_minus_a2 = -jnp.tanh(log_a) * (a * a + 1.0)
    return a, jnp.sqrt(one_minus_a2) * (ig * xc)


def _neg_c_softplus(lam):
    z = -lam
    return -LRU_C * (jnp.maximum(z, 0.0) + jnp.log1p(jnp.exp(-jnp.abs(z))))


def _diff_lambda(lq1, lk1, lq2, lk2):
    s1 = jnp.sum(lq1 * lk1, axis=-1, keepdims=True)
    s2 = jnp.sum(lq2 * lk2, axis=-1, keepdims=True)
    return jnp.exp(s1) - jnp.exp(s2) + LAM_INIT


def _subln(o, g):
    return o * lax.rsqrt(jnp.mean(o * o, axis=-1, keepdims=True) + EPS) * g * (1.0 - LAM_INIT)


def _ada_kernel(c_ref, w_ref, b_ref, o_ref):
    c = c_ref[...]
    s = (c * jax.nn.sigmoid(c)).astype(BF16)
    o_ref[...] = _dot(s, w_ref[...].astype(BF16)) + b_ref[...]


def _ada(c_all, w_ada, b_ada):
    m = c_all.shape[0]
    n = w_ada.shape[1]
    tn = ADA_TN
    return pl.pallas_call(
        _ada_kernel,
        grid=(n // tn,),
        in_specs=[pl.BlockSpec((m, D_MODEL), lambda j: (0, 0)),
                  pl.BlockSpec((D_MODEL, tn), lambda j: (0, j)),
                  pl.BlockSpec((1, tn), lambda j: (0, j))],
        out_specs=pl.BlockSpec((m, tn), lambda j: (0, j)),
        out_shape=jax.ShapeDtypeStruct((m, n), F32),
        compiler_params=pltpu.CompilerParams(dimension_semantics=("arbitrary",), vmem_limit_bytes=VMEM_LIMIT),
        name="ada_mod",
    )(c_all, w_ada, b_ada)


def _rope_table_kernel(freq_ref, ma_ref, mb_ref, c_ref, sa_ref, sb_ref):
    tm = c_ref.shape[0]
    pos = (pl.program_id(0) * tm + lax.broadcasted_iota(jnp.int32, (tm, LANES), 0)).astype(F32)
    ang = pos * freq_ref[...]
    s = jnp.sin(ang)
    c_ref[...] = jnp.cos(ang)
    sa_ref[...] = -s * ma_ref[...]
    sb_ref[...] = s * mb_ref[...]


def _rope_tables(n_pos):
    half = ROPE_DIM // 2
    d = jnp.arange(LANES) % QK_SUB_DIM
    freqs = ROPE_THETA ** (-(d % half).astype(F32) * 2.0 / ROPE_DIM)
    freq_lane = jnp.where(d < ROPE_DIM, freqs, 0.0).astype(F32)[None, :]
    mask_a = (d < half).astype(F32)[None, :]
    mask_b = ((d >= half) & (d < ROPE_DIM)).astype(F32)[None, :]
    tm = ROPE_TM
    row = pl.BlockSpec((1, LANES), lambda i: (0, 0))
    tab = pl.BlockSpec((tm, LANES), lambda i: (i, 0))
    shp = jax.ShapeDtypeStruct((n_pos, LANES), F32)
    return pl.pallas_call(
        _rope_table_kernel,
        grid=(n_pos // tm,),
        in_specs=[row, row, row],
        out_specs=[tab, tab, tab],
        out_shape=[shp, shp, shp],
        compiler_params=pltpu.CompilerParams(dimension_semantics=("arbitrary",)),
        name="rope_tables",
    )(freq_lane, mask_a, mask_b)


def _inproj_prompt_kernel(x_ref, sh_ref, sc_ref, g1_ref, win_ref, gq_ref, gk_ref, gmat_ref,
                          cos_ref, sa_ref, sb_ref, cw_ref, cb_ref, wr_ref, br_ref, wi_ref, bi_ref, lam_ref,
                          q_ref, kf_ref, kb_ref, vf_ref, vb_ref, lo_ref, lc_ref, lh_ref,
                          lbuf_ref, hbuf_ref, tail_ref, hcar_ref):
    tm = x_ref.shape[0]

    @pl.when(pl.program_id(1) == 0)
    def _():
        tail_ref[...] = jnp.zeros(tail_ref.shape, F32)
        hcar_ref[...] = jnp.zeros(hcar_ref.shape, F32)

    u = _modulated_norm(x_ref[...], g1_ref[...], _seq_row(sc_ref), _seq_row(sh_ref)).astype(BF16)
    cos, sa, sb = cos_ref[...], sa_ref[...], sb_ref[...]
    gmat = gmat_ref[...]

    q = _group_norm_rope(_dot(u, win_ref[:, 0:ATT_WIDTH]), gq_ref[...], gmat, cos, sa, sb)
    q_ref[...] = (q * QK_SCALE_LOG2).T.astype(BF16)
    k = _group_norm_rope(_dot(u, win_ref[:, ATT_WIDTH:2 * ATT_WIDTH]), gk_ref[...], gmat, cos, sa, sb)
    kf_ref[...] = k.T
    kb_ref[...] = k.astype(BF16)
    v = _dot(u, win_ref[:, 2 * ATT_WIDTH:3 * ATT_WIDTH])
    for h in range(ATT_HEADS):
        vf_ref[pl.ds(h, tm, stride=ATT_HEADS), :] = v[:, h * V_HEAD_DIM:(h + 1) * V_HEAD_DIM]
    vb_ref[...] = v.T.astype(BF16)

    c0 = 3 * ATT_WIDTH
    n_tail = LRU_CONV - 1
    lx = _to_segment_major(_dot(u, win_ref[:, c0:c0 + LRU_WIDTH]), lbuf_ref)
    cw = cw_ref[...]
    d3, d2, d1 = _delayed(lx, [tail_ref[i] for i in range(n_tail)])
    xc = d3 * cw[0:1] + d2 * cw[1:2] + d1 * cw[2:3] + lx * cw[3:4] + cb_ref[...]
    for i, grp in enumerate(_tail_groups(lx, n_tail)):
        tail_ref[i] = grp
        lc_ref[i:i + 1, :] = grp[SUBLANES - 1:SUBLANES, :]

    a, gx = _lru_gates(xc, wr_ref, br_ref[...], wi_ref, bi_ref[...], _neg_c_softplus(lam_ref[...]))

    n = tm // SUBLANES
    grp = lambda arr, r: arr[r * SUBLANES:(r + 1) * SUBLANES, :]
    ps, hs = [grp(a, 0)], [grp(gx, 0)]
    for r in range(1, n):
        ar = grp(a, r)
        ps.append(ar * ps[-1])
        hs.append(ar * hs[-1] + grp(gx, r))
    p_end, h_end = ps[-1], hs[-1]
    entering = [hcar_ref[...]]
    for s in range(1, SUBLANES):
        entering.append(p_end[s - 1:s, :] * entering[-1] + h_end[s - 1:s, :])
    h_last = p_end[SUBLANES - 1:SUBLANES, :] * entering[-1] + h_end[SUBLANES - 1:SUBLANES, :]
    hcar_ref[...] = h_last
    lh_ref[...] = h_last
    enter = jnp.concatenate(entering, axis=0)
    states = jnp.concatenate([hs[r] + ps[r] * enter for r in range(n)], axis=0)

    lg = _dot(u, win_ref[:, c0 + LRU_WIDTH:c0 + 2 * LRU_WIDTH])
    lo_ref[...] = (_to_time_major(states, hbuf_ref) * jax.nn.gelu(lg, approximate=True)).astype(BF16)


def _inproj_prompt(x, mod, g1, w_in, gq, gk, gmat, tabs, cw, cb, wr, br, wi, bi, lam):
    bsz, s_len, _ = x.shape
    tm = TM_IN
    row_spec = lambda w: pl.BlockSpec((None, tm, w), lambda b, i: (b, i, 0))
    tab_spec = pl.BlockSpec((tm, LANES), lambda b, i: (i, 0))
    mod_spec = lambda k: _prompt_mod_spec(mod, k)
    act = lambda dt: jax.ShapeDtypeStruct((bsz, s_len, ATT_WIDTH), dt)
    act_t = lambda dt: jax.ShapeDtypeStruct((bsz, ATT_WIDTH, s_len), dt)
    col_spec = pl.BlockSpec((None, ATT_WIDTH, tm), lambda b, i: (b, 0, i))
    return pl.pallas_call(
        _inproj_prompt_kernel,
        grid=(bsz, s_len // tm),
        in_specs=[row_spec(D_MODEL), mod_spec(0), mod_spec(1), _const_spec(g1.shape), _const_spec(w_in.shape),
                  _const_spec(gq.shape), _const_spec(gk.shape), _const_spec(gmat.shape),
                  tab_spec, tab_spec, tab_spec,
                  _const_spec(cw.shape), _const_spec(cb.shape), _const_spec(wr.shape), _const_spec(br.shape),
                  _const_spec(wi.shape), _const_spec(bi.shape), _const_spec(lam.shape)],
        out_specs=[col_spec, col_spec, row_spec(ATT_WIDTH),
                   pl.BlockSpec((None, tm * ATT_HEADS, V_HEAD_DIM), lambda b, i: (b, i, 0)),
                   col_spec, row_spec(LRU_WIDTH),
                   pl.BlockSpec((None, LRU_CONV - 1, LRU_WIDTH), lambda b, i: (b, 0, 0)),
                   pl.BlockSpec((None, 1, LRU_WIDTH), lambda b, i: (b, 0, 0))],
        out_shape=[act_t(BF16), act_t(F32), act(BF16),
                   jax.ShapeDtypeStruct((bsz, s_len * ATT_HEADS, V_HEAD_DIM), F32), act_t(BF16), act(BF16),
                   jax.ShapeDtypeStruct((bsz, LRU_CONV - 1, LRU_WIDTH), F32),
                   jax.ShapeDtypeStruct((bsz, 1, LRU_WIDTH), F32)],
        scratch_shapes=[pltpu.VMEM((LRU_WIDTH // LANES, tm, LANES), F32),
                        pltpu.VMEM((LRU_WIDTH // LANES, tm, LANES), F32),
                        pltpu.VMEM((LRU_CONV - 1, SUBLANES, LRU_WIDTH), F32),
                        pltpu.VMEM((1, LRU_WIDTH), F32)],
        compiler_params=pltpu.CompilerParams(dimension_semantics=("arbitrary", "arbitrary"),
                                             vmem_limit_bytes=VMEM_LIMIT),
        name="inproj_prompt",
    )(x, mod, mod, g1, w_in, gq, gk, gmat, *tabs, cw, cb, wr, br, wi, bi, lam)


def _alternate(*step_generators):
    live = list(step_generators)
    while live:
        for g in list(live):
            try:
                next(g)
            except StopIteration:
                live.remove(g)


def _query_tile(i, n):
    return jnp.where(i % 2 == 0, i // 2, n - 1 - i // 2)


def _page_copies(pt_ref, ckt_hbm, cv_hbm, kbuf_ref, vbuf_ref, sem_ref, seq, slot):
    copies = []
    for j in range(kbuf_ref.shape[1]):
        pg = pt_ref[seq, j]
        copies.append(pltpu.make_async_copy(ckt_hbm.at[pg], kbuf_ref.at[slot, j], sem_ref.at[slot]))
        copies.append(pltpu.make_async_copy(cv_hbm.at[pg], vbuf_ref.at[slot, j], sem_ref.at[slot]))
    return copies


def _attn_kernel(bounded, pt_ref, qt_ref, k_ref, vt_ref, lq1_ref, lk1_ref, lq2_ref, lk2_ref, gs_ref,
                 qs_ref, kn_ref, vn_ref, ckt_hbm, cv_hbm, o_ref, os_ref, kbuf_ref, vbuf_ref, sem_ref):
    n_pages = kbuf_ref.shape[1]
    step = (pl.program_id(0) * pl.num_programs(1) + pl.program_id(1)) * pl.num_programs(2) + pl.program_id(2)
    n_steps = pl.num_programs(0) * pl.num_programs(1) * pl.num_programs(2)
    slot = step % 2

    @pl.when(step == 0)
    def _():
        for cp in _page_copies(pt_ref, ckt_hbm, cv_hbm, kbuf_ref, vbuf_ref, sem_ref, step, slot):
            cp.start()

    @pl.when(step + 1 < n_steps)
    def _():
        for i, cp in enumerate(_page_copies(pt_ref, ckt_hbm, cv_hbm, kbuf_ref, vbuf_ref, sem_ref, step + 1, 1 - slot)):
            cp.start(priority=i % 2)

    tq = qt_ref.shape[1]
    qi = _query_tile(pl.program_id(2), pl.num_programs(2))
    qt = qt_ref[...].astype(F32)
    feat = lax.broadcasted_iota(jnp.int32, (V_HEAD_DIM, tq), 0)
    qc = (jnp.where(feat < QK_SUB_DIM, qt, 0.0).astype(BF16), jnp.where(feat >= QK_SUB_DIM, qt, 0.0).astype(BF16))

    def tile_steps(k0, width, carry, masked, out):
        k0 = pl.multiple_of(k0, TK)
        ks = k_ref[pl.ds(k0, width), :]
        vts = vt_ref[:, pl.ds(k0, width)]
        ss = []
        for c in range(2):
            ss.append(_dot(ks, qc[c]))
            yield
        for c in range(2):
            m, l, acc = carry[c]
            s = ss[c]
            if masked:
                keys = lax.broadcasted_iota(jnp.int32, (width, tq), 0)
                qs = lax.broadcasted_iota(jnp.int32, (width, tq), 1)
                s = jnp.where(keys <= qs, s, NEG)
            if bounded:
                p = jnp.exp2(s)
                l = l + jnp.sum(p, axis=0, keepdims=True)
                acc = acc + _dot(vts, p.astype(BF16))
            else:
                m_new = jnp.maximum(m, jnp.max(s, axis=0, keepdims=True))
                alpha = jnp.exp2(m - m_new)
                p = jnp.exp2(s - m_new)
                l = alpha * l + jnp.sum(p, axis=0, keepdims=True)
                acc = alpha * acc + _dot(vts, p.astype(BF16))
                m = m_new
            out.append((m, l, acc))
            yield

    def tile(k0, width, carry, masked):
        out = []
        for _ in tile_steps(k0, width, carry, masked, out):
            pass
        return tuple(out)

    init = tuple((jnp.full((1, tq), NEG, F32), jnp.zeros((1, tq), F32), jnp.zeros((V_HEAD_DIM, tq), F32))
                 for _ in range(2))
    carry = lax.fori_loop(0, qi // 2, lambda j, c: tile(j * 2 * TK, 2 * TK, c, False), init)
    carry = lax.cond(qi % 2 == 1, lambda c: tile((qi - 1) * TK, TK, c, False), lambda c: c, carry)
    for cp in _page_copies(pt_ref, ckt_hbm, cv_hbm, kbuf_ref, vbuf_ref, sem_ref, step, slot):
        cp.wait()
    kt_refs = [kbuf_ref.at[slot, j] for j in range(n_pages)]
    v_refs = [vbuf_ref.at[slot, j] for j in range(n_pages)]
    lam = _diff_lambda(lq1_ref[...], lk1_ref[...], lq2_ref[...], lk2_ref[...])
    tile_out, sample_out = [], []
    _alternate(tile_steps(qi * TK, TK, carry, True, tile_out),
               _sample_attention_steps(bounded, qs_ref, kn_ref, vn_ref, lam, gs_ref[...], kt_refs, v_refs, sample_out))
    (_, l0, a0), (_, l1, a1) = tile_out
    ot = a0 / l0 - lam * (a1 / l1)
    o_ref[...] = _subln(ot.T, gs_ref[...]).astype(o_ref.dtype)
    os_ref[...] = sample_out[0]


def _attn(bounded, page_table, qt, k, vt, q8, kn8, vn8, cache_kt, cache_v, lq1, lk1, lq2, lk2, gs):
    bsz, s_len, _ = k.shape
    nb, n_pages = page_table.shape
    nq = q8.shape[1]
    page = cache_kt.shape[2]
    n_qt = s_len // TQ
    assert nb == bsz * ATT_HEADS * n_qt, "one sample sequence per grid step"
    seq = lambda b, h, i: (b * ATT_HEADS + h) * n_qt + i
    qt_spec = pl.BlockSpec((None, V_HEAD_DIM, TQ), lambda b, h, i, pt: (b, h, _query_tile(i, n_qt)))
    k_spec = pl.BlockSpec((None, s_len, V_HEAD_DIM), lambda b, h, i, pt: (b, 0, h))
    vt_spec = pl.BlockSpec((None, V_HEAD_DIM, s_len), lambda b, h, i, pt: (b, h, 0))
    o_spec = pl.BlockSpec((None, TQ, V_HEAD_DIM), lambda b, h, i, pt: (b, _query_tile(i, n_qt), h))
    small = lambda a: pl.BlockSpec(a.shape, lambda b, h, i, pt: (0, 0))
    new_spec = pl.BlockSpec((None, nq, ATT_WIDTH), lambda b, h, i, pt: (seq(b, h, i), 0, 0))
    hbm = pl.BlockSpec(memory_space=pl.ANY)
    grid_spec = pltpu.PrefetchScalarGridSpec(
        num_scalar_prefetch=1,
        grid=(bsz, ATT_HEADS, n_qt),
        in_specs=[qt_spec, k_spec, vt_spec, small(lq1), small(lk1), small(lq2), small(lk2), small(gs),
                  new_spec, new_spec, new_spec, hbm, hbm],
        out_specs=[o_spec, new_spec],
        scratch_shapes=[pltpu.VMEM((2, n_pages, ATT_WIDTH, page), F32),
                        pltpu.VMEM((2, n_pages, page * ATT_HEADS, V_HEAD_DIM), F32),
                        pltpu.SemaphoreType.DMA((2,))],
    )
    return pl.pallas_call(
        functools.partial(_attn_kernel, bounded),
        grid_spec=grid_spec,
        out_shape=[jax.ShapeDtypeStruct((bsz, s_len, ATT_WIDTH), BF16),
                   jax.ShapeDtypeStruct((nb, nq, ATT_WIDTH), F32)],
        compiler_params=pltpu.CompilerParams(dimension_semantics=("arbitrary", "arbitrary", "arbitrary"),
                                             vmem_limit_bytes=VMEM_LIMIT),
        name="attn_bounded" if bounded else "attn_general",
    )(page_table, qt, k, vt, lq1, lk1, lq2, lk2, gs, q8, kn8, vn8, cache_kt, cache_v)


def _ffn_prompt_kernel(x_ref, at_ref, lo_ref, gt1_ref, sh2_ref, sc2_ref, gt2_ref, g2_ref,
                       wout_ref, wup_ref, cfw_ref, cfb_ref, wdn_ref,
                       y_ref, fc_ref, ubuf_ref, abuf_ref, tail_ref):
    tm = x_ref.shape[0]
    sub = tm // FFN_SPLIT
    d_ff = wdn_ref.shape[0]
    n_tail = FFN_CONV - 1
    n_chunks = d_ff // FF_CHUNK
    chunk_cols = lambda j: [slice(base + j * FF_CHUNK, base + (j + 1) * FF_CHUNK) for base in (0, d_ff)]

    @pl.when(pl.program_id(1) == 0)
    def _():
        tail_ref[...] = jnp.zeros(tail_ref.shape, F32)

    def attention_residual(g):
        rows = slice(g * sub, (g + 1) * sub)
        mix = jnp.concatenate([at_ref[rows, :], lo_ref[rows, :]], axis=1)
        x1 = x_ref[rows, :] + _seq_row(gt1_ref) * _dot(mix, wout_ref[...])
        u2 = _to_segment_major(_modulated_norm(x1, g2_ref[...], _seq_row(sc2_ref), _seq_row(sh2_ref)),
                               ubuf_ref.at[g]).astype(BF16)
        return x1, u2

    def finish(g, x1, acc):
        y_ref[g * sub:(g + 1) * sub, :] = x1 + _seq_row(gt2_ref) * _to_time_major(acc, abuf_ref.at[g])

    def hidden(j, ups):
        halves = []
        for cols, up in zip(chunk_cols(j), ups):
            w = cfw_ref[:, cols]
            d2, d1 = _delayed(up, [tail_ref[i, :, cols] for i in range(n_tail)])
            halves.append(d2 * w[0:1] + d1 * w[1:2] + up * w[2:3] + cfb_ref[:, cols])
            for i, grp in enumerate(_tail_groups(up, n_tail)):
                tail_ref[i, :, cols] = grp
                fc_ref[i:i + 1, cols] = grp[SUBLANES - 1:SUBLANES, :]
        g, val = halves
        return (g * jax.nn.sigmoid(g) * val).astype(BF16)

    state = attention_residual(0)
    done = None
    for g in range(FFN_SPLIT):
        x1, u2 = state
        up_project = lambda j: [_dot(u2, wup_ref[:, cols]) for cols in chunk_cols(j)]
        acc = jnp.zeros((sub, D_MODEL), F32)
        ups = up_project(0)
        for j in range(n_chunks):
            ups_next = up_project(j + 1) if j + 1 < n_chunks else None
            acc = acc + _dot(hidden(j, ups), wdn_ref[j * FF_CHUNK:(j + 1) * FF_CHUNK, :])
            ups = ups_next
            if j == FFN_SLOT and done is not None:
                finish(*done)
                done = None
            if j == FFN_SLOT and g + 1 < FFN_SPLIT:
                state = attention_residual(g + 1)
        done = (g, x1, acc)
    finish(*done)


def _ffn_prompt(x, attn, lru, mod, g2, w_out, w_up, cfw, cfb, w_dn):
    bsz, s_len, _ = x.shape
    tm = TM_FFN
    d_ff = w_dn.shape[0]
    row_spec = lambda w: pl.BlockSpec((None, tm, w), lambda b, i: (b, i, 0))
    mod_spec = lambda k: _prompt_mod_spec(mod, k)
    return pl.pallas_call(
        _ffn_prompt_kernel,
        grid=(bsz, s_len // tm),
        in_specs=[row_spec(D_MODEL), row_spec(ATT_WIDTH), row_spec(LRU_WIDTH),
                  mod_spec(2), mod_spec(3), mod_spec(4), mod_spec(5), _const_spec(g2.shape),
                  _const_spec(w_out.shape), _const_spec(w_up.shape), _const_spec(cfw.shape),
                  _const_spec(cfb.shape), _const_spec(w_dn.shape)],
        out_specs=[row_spec(D_MODEL), pl.BlockSpec((None, FFN_CONV - 1, 2 * d_ff), lambda b, i: (b, 0, 0))],
        out_shape=[jax.ShapeDtypeStruct((bsz, s_len, D_MODEL), F32),
                   jax.ShapeDtypeStruct((bsz, FFN_CONV - 1, 2 * d_ff), F32)],
        scratch_shapes=[pltpu.VMEM((FFN_SPLIT, D_MODEL // LANES, tm // FFN_SPLIT, LANES), F32),
                        pltpu.VMEM((FFN_SPLIT, D_MODEL // LANES, tm // FFN_SPLIT, LANES), F32),
                        pltpu.VMEM((FFN_CONV - 1, SUBLANES, 2 * d_ff), F32)],
        compiler_params=pltpu.CompilerParams(dimension_semantics=("arbitrary", "arbitrary"),
                                             vmem_limit_bytes=VMEM_LIMIT),
        name="ffn_prompt",
    )(x, attn, lru, mod, mod, mod, mod, g2, w_out, w_up, cfw, cfb, w_dn)


def _inproj_sample_kernel(x_ref, mod_ref, g1_ref, win_ref, gq_ref, gk_ref, gmat_ref,
                          cos_ref, sa_ref, sb_ref, cw_ref, cb_ref, wr_ref, br_ref, wi_ref, bi_ref, lam_ref,
                          st_ref, h0_ref,
                          q_ref, k_ref, v_ref, lo_ref, lc_ref, lh_ref):
    nb = h0_ref.shape[0]
    nt = x_ref.shape[0] // nb
    rep = lambda a: jnp.concatenate([a] * nt, axis=0)
    sh1 = rep(mod_ref[0:nb, 0:D_MODEL])
    sc1 = rep(mod_ref[0:nb, D_MODEL:2 * D_MODEL])
    u = _modulated_norm(x_ref[...], g1_ref[...], sc1, sh1).astype(BF16)
    per_t = lambda r: jnp.concatenate([jnp.broadcast_to(r[t:t + 1, :], (nb, LANES)) for t in range(nt)], axis=0)
    cos, sa, sb = per_t(cos_ref[...]), per_t(sa_ref[...]), per_t(sb_ref[...])
    gmat = gmat_ref[...]

    q = _group_norm_rope(_dot(u, win_ref[:, 0:ATT_WIDTH]), gq_ref[...], gmat, cos, sa, sb)
    q_ref[...] = q * QK_SCALE_LOG2
    k_ref[...] = _group_norm_rope(_dot(u, win_ref[:, ATT_WIDTH:2 * ATT_WIDTH]), gk_ref[...], gmat, cos, sa, sb)
    v_ref[...] = _dot(u, win_ref[:, 2 * ATT_WIDTH:3 * ATT_WIDTH])

    c0 = 3 * ATT_WIDTH
    lx = _dot(u, win_ref[:, c0:c0 + LRU_WIDTH])
    lg = _dot(u, win_ref[:, c0 + LRU_WIDTH:c0 + 2 * LRU_WIDTH])
    pad = [st_ref[i] for i in range(LRU_CONV - 1)] + [lx[t * nb:(t + 1) * nb, :] for t in range(nt)]
    for i in range(LRU_CONV - 1):
        lc_ref[i] = pad[nt + i]
    cw = cw_ref[...]
    xcs = []
    for t in range(nt):
        xc = pad[t] * cw[0:1]
        for kk in range(1, LRU_CONV):
            xc = xc + pad[t + kk] * cw[kk:kk + 1]
        xcs.append(xc + cb_ref[...])
    a, gx = _lru_gates(jnp.concatenate(xcs, axis=0), wr_ref, br_ref[...], wi_ref, bi_ref[...],
                       _neg_c_softplus(lam_ref[...]))
    h = h0_ref[...]
    hs = []
    for t in range(nt):
        h = a[t * nb:(t + 1) * nb, :] * h + gx[t * nb:(t + 1) * nb, :]
        hs.append(h)
    lh_ref[...] = h
    lo_ref[...] = (jnp.concatenate(hs, axis=0) * jax.nn.gelu(lg, approximate=True)).astype(BF16)


def _inproj_sample(x_tb, mod_s, g1, w_in, gq, gk, gmat, tabs_s, cw, cb, wr, br, wi, bi, lam, st, h0):
    m = x_tb.shape[0]
    nb = h0.shape[0]
    act = jax.ShapeDtypeStruct((m, ATT_WIDTH), F32)
    return pl.pallas_call(
        _inproj_sample_kernel,
        out_shape=[act, act, act, jax.ShapeDtypeStruct((m, LRU_WIDTH), BF16),
                   jax.ShapeDtypeStruct((LRU_CONV - 1, nb, LRU_WIDTH), F32),
                   jax.ShapeDtypeStruct((nb, LRU_WIDTH), F32)],
        compiler_params=pltpu.CompilerParams(vmem_limit_bytes=VMEM_LIMIT),
        name="inproj_sample",
    )(x_tb, mod_s, g1, w_in, gq, gk, gmat, *tabs_s, cw, cb, wr, br, wi, bi, lam, st, h0)


def _sample_attention_steps(bounded, q_ref, kn_ref, vn_ref, lam, gs, kt_refs, v_refs, out):
    n_pages = len(kt_refs)
    page = kt_refs[0].shape[1]
    nq = q_ref.shape[0]
    hrows = 2 * nq

    lane = lax.broadcasted_iota(jnp.int32, (nq, V_HEAD_DIM), 1)
    qh = []
    for h in range(ATT_HEADS):
        q8 = q_ref[:, h * V_HEAD_DIM:(h + 1) * V_HEAD_DIM]
        qh.append(jnp.concatenate([jnp.where(lane < QK_SUB_DIM, q8, 0.0), jnp.where(lane >= QK_SUB_DIM, q8, 0.0)],
                                  axis=0).astype(BF16))

    zpad = jnp.zeros((page - nq, V_HEAD_DIM), F32)
    rt = lax.broadcasted_iota(jnp.int32, (hrows, page), 0) % nq
    ct = lax.broadcasted_iota(jnp.int32, (hrows, page), 1)
    ps, ls = [], []
    for h in range(ATT_HEADS):
        rows = slice(h * V_HEAD_DIM, (h + 1) * V_HEAD_DIM)
        kt_all = jnp.concatenate([kt_refs[j][rows, :] for j in range(n_pages)], axis=1).astype(BF16)
        k_new = jnp.concatenate([kn_ref[:, rows], zpad], axis=0).astype(BF16)
        s_new = jnp.where(ct <= rt, _dot_nt(qh[h], k_new), NEG)
        s = jnp.concatenate([_dot(qh[h], kt_all), s_new], axis=1)
        if not bounded:
            s = s - jnp.max(s, axis=1, keepdims=True)
        p = jnp.exp2(s)
        ls.append(jnp.sum(p, axis=1, keepdims=True))
        ps.append(p.astype(BF16))
        yield
    outs = []
    for h in range(0, ATT_HEADS, 2):
        v_pair = jnp.concatenate([jnp.concatenate(
            [v_refs[j][pl.ds(hh, page, stride=ATT_HEADS), :] for j in range(n_pages)]
            + [vn_ref[:, hh * V_HEAD_DIM:(hh + 1) * V_HEAD_DIM], zpad], axis=0) for hh in (h, h + 1)],
            axis=1).astype(BF16)
        o_pair = _dot(jnp.concatenate([ps[h], ps[h + 1]], axis=0), v_pair)
        for i, hh in enumerate((h, h + 1)):
            o = o_pair[i * hrows:(i + 1) * hrows, i * V_HEAD_DIM:(i + 1) * V_HEAD_DIM] / ls[hh]
            outs.append(_subln(o[0:nq, :] - lam * o[nq:hrows, :], gs))
        yield
    out.append(jnp.concatenate(outs, axis=1))


def _ffn_sample_kernel(x_ref, at_ref, lo_ref, mod_ref, g2_ref, wout_ref, wupg_ref, wupv_ref, cfwg_ref, cfwv_ref,
                       cfbg_ref, cfbv_ref, wdn_ref, stg_ref, stv_ref,
                       y_ref, fcg_ref, fcv_ref, x1_ref, u2_ref, acc_ref):
    j = pl.program_id(0)
    nb = stg_ref.shape[1]
    nt = x_ref.shape[0] // nb
    rep = lambda a: jnp.concatenate([a] * nt, axis=0)

    @pl.when(j == 0)
    def _():
        gt1 = rep(mod_ref[0:nb, 2 * D_MODEL:3 * D_MODEL])
        sh2 = rep(mod_ref[0:nb, 3 * D_MODEL:4 * D_MODEL])
        sc2 = rep(mod_ref[0:nb, 4 * D_MODEL:5 * D_MODEL])
        mix = jnp.concatenate([at_ref[...].astype(BF16), lo_ref[...]], axis=1)
        x1 = x_ref[...] + gt1 * _dot(mix, wout_ref[...])
        x1_ref[...] = x1
        u2_ref[...] = _modulated_norm(x1, g2_ref[...], sc2, sh2).astype(BF16)
        acc_ref[...] = jnp.zeros(acc_ref.shape, F32)

    u2 = u2_ref[...]
    halves = []
    for wup_ref, cfw_ref, cfb_ref, st_ref, fc_ref in ((wupg_ref, cfwg_ref, cfbg_ref, stg_ref, fcg_ref),
                                                      (wupv_ref, cfwv_ref, cfbv_ref, stv_ref, fcv_ref)):
        up = _dot(u2, wup_ref[...])
        pad = [st_ref[i] for i in range(FFN_CONV - 1)] + [up[t * nb:(t + 1) * nb, :] for t in range(nt)]
        for i in range(FFN_CONV - 1):
            fc_ref[i] = pad[nt + i]
        w = cfw_ref[...]
        b = cfb_ref[...]
        hcs = []
        for t in range(nt):
            hc = pad[t] * w[0:1]
            for kk in range(1, FFN_CONV):
                hc = hc + pad[t + kk] * w[kk:kk + 1]
            hcs.append(hc + b)
        halves.append(jnp.concatenate(hcs, axis=0))
    g, val = halves
    hmid = (g * jax.nn.sigmoid(g) * val).astype(BF16)
    acc_ref[...] += _dot(hmid, wdn_ref[...])

    @pl.when(j == pl.num_programs(0) - 1)
    def _():
        gt2 = rep(mod_ref[0:nb, 5 * D_MODEL:6 * D_MODEL])
        y_ref[...] = x1_ref[...] + gt2 * acc_ref[...]


def _ffn_sample(x_tb, attn_tb, lru_tb, mod, g2, w_out, w_up, cfw, cfb, w_dn, st):
    m = x_tb.shape[0]
    d_ff = w_dn.shape[0]
    nb = st.shape[1]
    n_chunks = d_ff // FF_CHUNK
    whole = lambda a: pl.BlockSpec(a.shape, lambda j: (0,) * a.ndim)
    col = lambda rows, half: pl.BlockSpec((rows, FF_CHUNK), lambda j: (0, half * n_chunks + j))
    st_spec = lambda half: pl.BlockSpec((FFN_CONV - 1, nb, FF_CHUNK), lambda j: (0, 0, half * n_chunks + j))
    fc_spec = pl.BlockSpec((FFN_CONV - 1, nb, FF_CHUNK), lambda j: (0, 0, j))
    fc_shape = jax.ShapeDtypeStruct((FFN_CONV - 1, nb, d_ff), F32)
    return pl.pallas_call(
        _ffn_sample_kernel,
        grid=(n_chunks,),
        in_specs=[whole(x_tb), whole(attn_tb), whole(lru_tb), whole(mod), whole(g2), whole(w_out),
                  col(D_MODEL, 0), col(D_MODEL, 1), col(FFN_CONV, 0), col(FFN_CONV, 1), col(1, 0), col(1, 1),
                  pl.BlockSpec((FF_CHUNK, D_MODEL), lambda j: (j, 0)), st_spec(0), st_spec(1)],
        out_specs=[pl.BlockSpec((m, D_MODEL), lambda j: (0, 0)), fc_spec, fc_spec],
        out_shape=[jax.ShapeDtypeStruct((m, D_MODEL), F32), fc_shape, fc_shape],
        scratch_shapes=[pltpu.VMEM((m, D_MODEL), F32), pltpu.VMEM((m, D_MODEL), BF16),
                        pltpu.VMEM((m, D_MODEL), F32)],
        compiler_params=pltpu.CompilerParams(dimension_semantics=("arbitrary",), vmem_limit_bytes=VMEM_LIMIT),
        name="ffn_sample",
    )(x_tb, attn_tb, lru_tb, mod, g2, w_out, w_up, w_up, cfw, cfw, cfb, cfb, w_dn, st, st)


def _block_diag_halves(w):
    n, bd, _ = w.shape
    eye = jnp.eye(n // 2, dtype=w.dtype)
    halves = [jnp.einsum('nij,nm->nimj', w[s * (n // 2):(s + 1) * (n // 2)], eye).reshape(n // 2 * bd, n // 2 * bd)
              for s in range(2)]
    return jnp.stack(halves).astype(BF16)


def kernel(x_prompt, x_sample, cache_k, cache_v, page_table, state_lru_conv, state_lru_h, state_ffn_conv, c_prompt, c_sample, g_norm1, g_norm2, w_ada, b_ada, w_in, g_q, g_k, lam_q1, lam_k1, lam_q2, lam_k2, g_subln, w_out, conv_lru_w, conv_lru_b, w_rgate, b_rgate, w_igate, b_igate, lru_lambda, w_up, conv_ffn_w, conv_ffn_b, w_down):
    depth = w_in.shape[0]
    assert depth == 1, "single-layer step"
    bsz, s_len, _ = x_prompt.shape
    nb, nt, _ = x_sample.shape
    n_pages, page = page_table.shape[1], cache_k.shape[2]
    past_len = n_pages * page
    d_ff = w_down.shape[1]

    w_in_b = w_in[0].astype(BF16)
    w_out_b = w_out[0].astype(BF16)
    w_up_b = w_up[0].astype(BF16)
    w_dn_b = w_down[0].astype(BF16)
    wr = _block_diag_halves(w_rgate[0])
    wi = _block_diag_halves(w_igate[0])
    br = b_rgate[0].reshape(1, LRU_WIDTH)
    bi = b_igate[0].reshape(1, LRU_WIDTH)
    n_grp = ATT_WIDTH // QK_SUB_DIM
    gq = jnp.tile(g_q[0], n_grp)[None, :]
    gk = jnp.tile(g_k[0], n_grp)[None, :]
    grp = jnp.arange(ATT_WIDTH) // QK_SUB_DIM
    gmat = jnp.where(grp[:, None] == grp[None, :], 1.0 / QK_SUB_DIM, 0.0).astype(BF16)
    g1, g2 = g_norm1, g_norm2
    cw, cb = conv_lru_w[0], conv_lru_b
    cfw, cfb = conv_ffn_w[0], conv_ffn_b
    lam = lru_lambda
    lams = (lam_q1, lam_k1, lam_q2, lam_k2)
    gs = g_subln

    assert nb % SUBLANES == 0 and bsz <= SUBLANES
    c_all = jnp.concatenate([c_sample, c_prompt, jnp.zeros((SUBLANES - bsz, D_MODEL), F32)], axis=0)
    mod = _ada(c_all, w_ada[0], b_ada)

    tabs = _rope_tables(s_len)
    pad_rows = SUBLANES - nt
    tabs_s = tuple(t[past_len:past_len + SUBLANES] for t in tabs)

    q_p, kf_p, kb_p, vf_p, vb_p, lo_p, lc_p, lh_p = _inproj_prompt(
        x_prompt, mod, g1, w_in_b, gq, gk, gmat, tabs, cw, cb, wr, br, wi, bi, lam)
    x_tb = x_sample.transpose(1, 0, 2).reshape(nt * nb, D_MODEL)
    st_lru = state_lru_conv[0].transpose(1, 0, 2)
    q_s, k_s, v_s, lo_s, lc_s, lh_s = _inproj_sample(
        x_tb, mod, g1, w_in_b, gq, gk, gmat, tabs_s, cw, cb, wr, br, wi, bi, lam, st_lru, state_lru_h[0])
    to_bt = lambda a: a.reshape(nt, nb, -1).transpose(1, 0, 2)
    pad_t = lambda a: jnp.pad(a, ((0, 0), (0, pad_rows), (0, 0)))
    k_bt, v_bt = to_bt(k_s), to_bt(v_s)

    score_bound = QK_NORM_BOUND ** 2 * QK_SCALE_LOG2 * jnp.max(jnp.abs(g_q)) * jnp.max(jnp.abs(g_k))
    cache_kt = cache_k[0].transpose(0, 2, 3, 4, 1).reshape(-1, ATT_WIDTH, page)
    cache_vr = cache_v[0].reshape(-1, page * ATT_HEADS, V_HEAD_DIM)
    at_p, at_s = lax.cond(score_bound <= SCORE_LOG2_LIMIT,
                          functools.partial(_attn, True), functools.partial(_attn, False),
                          page_table, q_p, kb_p, vb_p, pad_t(to_bt(q_s)), pad_t(k_bt), pad_t(v_bt),
                          cache_kt, cache_vr, *lams, gs)

    y_p, fc_p = _ffn_prompt(x_prompt, at_p, lo_p, mod, g2, w_out_b, w_up_b, cfw, cfb, w_dn_b)
    at_tb = at_s[:, :nt].transpose(1, 0, 2).reshape(nt * nb, ATT_WIDTH)
    y_tb, fcg_s, fcv_s = _ffn_sample(x_tb, at_tb, lo_s, mod, g2, w_out_b, w_up_b, cfw, cfb, w_dn_b,
                                     state_ffn_conv[0].transpose(1, 0, 2))

    hd = (ATT_HEADS, 2, QK_SUB_DIM)
    return (y_p, to_bt(y_tb),
            kf_p.reshape(bsz, *hd, s_len).transpose(0, 4, 1, 2, 3)[None],
            vf_p.reshape(1, bsz, s_len, ATT_HEADS, V_HEAD_DIM),
            lc_p[None], lh_p.reshape(1, bsz, LRU_WIDTH), fc_p[None],
            k_bt.reshape(1, nb, nt, *hd), v_bt.reshape(1, nb, nt, ATT_HEADS, V_HEAD_DIM),
            lc_s.transpose(1, 0, 2)[None], lh_s[None],
            jnp.concatenate([fcg_s, fcv_s], axis=-1).transpose(1, 0, 2)[None])
```
